```python
import jax, jax.numpy as jnp
from jax import lax
import numpy as np

D_MODEL = 1024
BATCH = 2
SEQ = 8192
DEPTH = 1

D_INNER = 2 * D_MODEL
SSD_HEAD_DIM = 64
SSD_HEADS = D_INNER // SSD_HEAD_DIM
SSD_GROUPS = 4
SSD_HEADS_PER_GROUP = SSD_HEADS // SSD_GROUPS
SSD_STATE = 128
CONV_WIDTH = 4
CHUNK = 128
XBC_WIDTH = D_INNER + 2 * SSD_GROUPS * SSD_STATE
DT_MIN = 0.001
DT_MAX = 0.1
POOL_WINDOWS = (2, 4, 8, 16)
POOL_GROUPS = len(POOL_WINDOWS)
POOL_WIDTH = D_MODEL
POOL_GROUP_WIDTH = POOL_WIDTH // POOL_GROUPS
N_BRANCHES = 2
D_FF = 4 * D_MODEL
N_MOD = 6
NORM_EPS = 1e-5
IN_SPLITS = (D_INNER,
             D_INNER + XBC_WIDTH,
             D_INNER + XBC_WIDTH + SSD_HEADS,
             D_INNER + XBC_WIDTH + SSD_HEADS + POOL_WIDTH)
IN_COLS = IN_SPLITS[-1] + N_BRANCHES * D_MODEL

kernel_name = "hybrid_ssd_pool_gated_block"


def rmsnorm(x, w, eps=NORM_EPS):
    x32 = x.astype(jnp.float32)
    y = x32 * lax.rsqrt(jnp.mean(x32 * x32, axis=-1, keepdims=True) + eps)
    return y.astype(x.dtype) * w


def causal_depthwise_conv(u, w, b):
    k_width = w.shape[0]
    seq = u.shape[1]
    up = jnp.pad(u, ((0, 0), (k_width - 1, 0), (0, 0)))
    out = b
    for k in range(k_width):
        out = out + up[:, k:k + seq] * w[k]
    return out


def segsum_decay(a_cs):
    q = a_cs.shape[-1]
    seg = a_cs[..., :, None] - a_cs[..., None, :]
    mask = jnp.tril(jnp.ones((q, q), dtype=bool))
    return jnp.exp(jnp.where(mask, seg, -jnp.inf))


def ssd_chunked_scan(xdt, dtA, bmat, cmat):
    b, seq, _, p = xdt.shape
    g, hg, n, q = SSD_GROUPS, SSD_HEADS_PER_GROUP, SSD_STATE, CHUNK
    nc = seq // q
    x = xdt.astype(jnp.float32).reshape(b, nc, q, g, hg, p)
    a = dtA.astype(jnp.float32).reshape(b, nc, q, g, hg).transpose(0, 3, 4, 1, 2)
    bc = bmat.astype(jnp.float32).reshape(b, nc, q, g, n)
    cc = cmat.astype(jnp.float32).reshape(b, nc, q, g, n)
    a_cs = jnp.cumsum(a, axis=-1)
    scores = jnp.einsum('bclgn,bcsgn->bgcls', cc, bc)
    mmat = scores[:, :, None] * segsum_decay(a_cs)
    y_diag = jnp.einsum('bghcls,bcsghp->bclghp', mmat, x)
    decay_states = jnp.exp(a_cs[..., -1:] - a_cs)
    states = jnp.einsum('bcsgn,bghcs,bcsghp->bcghpn', bc, decay_states, x)
    chunk_decay = jnp.exp(a_cs[..., -1])

    def step(h, inp):
        s_c, d_c = inp
        return h * d_c[..., None, None] + s_c, h

    h0 = jnp.zeros((b, g, hg, p, n), jnp.float32)
    _, prev = lax.scan(step, h0, (jnp.moveaxis(states, 1, 0), jnp.moveaxis(chunk_decay, 3, 0)))
    prev = jnp.moveaxis(prev, 0, 1)
    y_off = jnp.einsum('bclgn,bcghpn,bghcl->bclghp', cc, prev, jnp.exp(a_cs))
    return (y_diag + y_off).reshape(b, seq, SSD_HEADS, p)


def causal_multiscale_pool(u, pool_w, pool_scale):
    b, seq, _ = u.shape
    gw = POOL_GROUP_WIDTH
    u32 = u.astype(jnp.float32)
    cs = jnp.concatenate([jnp.zeros((b, 1, POOL_WIDTH), jnp.float32), jnp.cumsum(u32, axis=1)], axis=1)
    count = jnp.arange(1, seq + 1, dtype=jnp.float32)[:, None]
    outs = []
    for gi, win in enumerate(POOL_WINDOWS):
        csg = cs[:, :, gi * gw:(gi + 1) * gw]
        start = jnp.concatenate([jnp.zeros((b, win - 1, gw), jnp.float32), csg[:, :seq - win + 1]], axis=1)
        mean = (csg[:, 1:] - start) / jnp.minimum(count, float(win))
        outs.append(mean - u32[..., gi * gw:(gi + 1) * gw])
    pooled = jnp.stack(outs, axis=2).astype(u.dtype)
    y = jnp.einsum('blgc,gcd->blgd', pooled, pool_w).reshape(b, seq, POOL_WIDTH)
    return y * pool_scale


def hybrid_mixer(h, w_in, conv_w, conv_b, dt_bias, a_log, d_skip, ssd_norm_w,
                 w_branch_ssd, pool_w, pool_scale, w_branch_pool, w_out):
    b, seq, _ = h.shape
    proj = h @ w_in
    z, xbc, dt_raw, u_pool, gate_logits = jnp.split(proj, IN_SPLITS, axis=-1)
    xbc = jax.nn.silu(causal_depthwise_conv(xbc, conv_w, conv_b))
    xs, bmat, cmat = jnp.split(xbc, (D_INNER, D_INNER + SSD_GROUPS * SSD_STATE), axis=-1)
    xs = xs.reshape(b, seq, SSD_HEADS, SSD_HEAD_DIM)
    bmat = bmat.reshape(b, seq, SSD_GROUPS, SSD_STATE)
    cmat = cmat.reshape(b, seq, SSD_GROUPS, SSD_STATE)
    dt = jax.nn.softplus(dt_raw.astype(jnp.float32) + dt_bias.astype(jnp.float32))
    a_cont = -jnp.exp(a_log.astype(jnp.float32))
    x32 = xs.astype(jnp.float32)
    y = ssd_chunked_scan(x32 * dt[..., None], dt * a_cont, bmat, cmat)
    y = y + d_skip.astype(jnp.float32)[:, None] * x32
    y = y.reshape(b, seq, D_INNER).astype(h.dtype) * jax.nn.silu(z)
    y = rmsnorm(y.reshape(b, seq, SSD_GROUPS, D_INNER // SSD_GROUPS),
                ssd_norm_w.reshape(SSD_GROUPS, D_INNER // SSD_GROUPS))
    y_ssd = y.reshape(b, seq, D_INNER) @ w_branch_ssd
    y_pool = causal_multiscale_pool(u_pool, pool_w, pool_scale) @ w_branch_pool
    g_ssd, g_pool = jnp.split(jax.nn.sigmoid(gate_logits), N_BRANCHES, axis=-1)
    return (g_ssd * y_ssd + g_pool * y_pool) @ w_out


def setup_inputs(seed: int = 0) -> dict:
    key = jax.random.key(seed)
    ks = jax.random.split(key, 24)
    f32 = jnp.float32
    nrm = lambda k, shape, s: jax.random.normal(k, shape, f32) * s
    dt0 = jnp.exp(jax.random.uniform(ks[6], (DEPTH, SSD_HEADS), f32)
                  * (np.log(DT_MAX) - np.log(DT_MIN)) + np.log(DT_MIN))
    return {
        "x": nrm(ks[0], (BATCH, SEQ, D_MODEL), 1.0),
        "c": nrm(ks[1], (BATCH, D_MODEL), 1.0),
        "w_ada": nrm(ks[2], (DEPTH, D_MODEL, N_MOD * D_MODEL), D_MODEL ** -0.5),
        "b_ada": nrm(ks[3], (DEPTH, N_MOD * D_MODEL), 0.01),
        "norm_mix_w": 1.0 + nrm(ks[4], (DEPTH, D_MODEL), 0.05),
        "w_in": nrm(ks[5], (DEPTH, D_MODEL, IN_COLS), D_MODEL ** -0.5),
        "conv_w": nrm(ks[7], (DEPTH, CONV_WIDTH, XBC_WIDTH), CONV_WIDTH ** -0.5),
        "conv_b": nrm(ks[8], (DEPTH, XBC_WIDTH), 0.01),
        "dt_bias": dt0 + jnp.log(-jnp.expm1(-dt0)),
        "a_log": jnp.log(jax.random.uniform(ks[9], (DEPTH, SSD_HEADS), f32, 1.0, 16.0)),
        "d_skip": 1.0 + nrm(ks[10], (DEPTH, SSD_HEADS), 0.1),
        "ssd_norm_w": 1.0 + nrm(ks[11], (DEPTH, D_INNER), 0.05),
        "w_branch_ssd": nrm(ks[12], (DEPTH, D_INNER, D_MODEL), D_INNER ** -0.5),
        "pool_w": nrm(ks[13], (DEPTH, POOL_GROUPS, POOL_GROUP_WIDTH, POOL_GROUP_WIDTH), POOL_GROUP_WIDTH ** -0.5),
        "pool_scale": 1.0 + nrm(ks[14], (DEPTH, POOL_WIDTH), 0.1),
        "w_branch_pool": nrm(ks[15], (DEPTH, POOL_WIDTH, D_MODEL), POOL_WIDTH ** -0.5),
        "w_out": nrm(ks[16], (DEPTH, D_MODEL, D_MODEL), D_MODEL ** -0.5),
        "norm_mlp_w": 1.0 + nrm(ks[17], (DEPTH, D_MODEL), 0.05),
        "w_up": nrm(ks[18], (DEPTH, D_MODEL, D_FF), D_MODEL ** -0.5),
        "w_down": nrm(ks[19], (DEPTH, D_FF, D_MODEL), D_FF ** -0.5),
        "norm_final_w": 1.0 + nrm(ks[20], (D_MODEL,), 0.05),
    }


def reference(x, c, w_ada, b_ada, norm_mix_w, w_in, conv_w, conv_b, dt_bias, a_log, d_skip,
              ssd_norm_w, w_branch_ssd, pool_w, pool_scale, w_branch_pool, w_out,
              norm_mlp_w, w_up, w_down, norm_final_w):
    for i in range(DEPTH):
        mod = (jax.nn.silu(c) @ w_ada[i] + b_ada[i])[:, None, :]
        shift_m, scale_m, gate_m, shift_f, scale_f, gate_f = jnp.split(mod, N_MOD, axis=-1)
        h = rmsnorm(x, norm_mix_w[i]) * (1.0 + scale_m) + shift_m
        x = x + gate_m * hybrid_mixer(h, w_in[i], conv_w[i], conv_b[i], dt_bias[i], a_log[i],
                                      d_skip[i], ssd_norm_w[i], w_branch_ssd[i], pool_w[i],
                                      pool_scale[i], w_branch_pool[i], w_out[i])
        h = rmsnorm(x, norm_mlp_w[i]) * (1.0 + scale_f) + shift_f
        x = x + gate_f * (jnp.square(jax.nn.relu(h @ w_up[i])) @ w_down[i])
    return rmsnorm(x, norm_final_w)
```

```python
import functools

import jax
import jax.numpy as jnp
import numpy as np
from jax import lax
from jax.experimental import pallas as pl
from jax.experimental.pallas import tpu as pltpu

F32 = jnp.float32
BF16 = jnp.bfloat16

D_MODEL = 1024
D_INNER = 2 * D_MODEL
HEAD_DIM = 64
N_HEADS = D_INNER // HEAD_DIM
N_GROUPS = 4
GROUP_CH = D_INNER // N_GROUPS
D_STATE = 128
CONV_K = 4
CHUNK = 128
BC_WIDTH = 2 * N_GROUPS * D_STATE
POOL_WINDOWS = (2, 4, 8, 16)
POOL_GW = D_MODEL // len(POOL_WINDOWS)
POOL_HALO = 16
D_FF = 4 * D_MODEL
N_MOD = 6
EPS = 1e-5
LANES = 128
SUBLANES = 8
DT_REP = 3

COL_Z = 0
COL_XS = D_INNER
COL_BC = 2 * D_INNER
COL_POOL = 2 * D_INNER + BC_WIDTH
COL_GATE = COL_POOL + D_MODEL
PROJ_COLS = COL_GATE + 2 * D_MODEL

VMEM_LIMIT = 56 * 1024 * 1024


def _rms(x):
    return x * lax.rsqrt(jnp.mean(x * x, axis=-1, keepdims=True) + EPS)


def _mod_kernel(c_ref, w_ref, b_ref, o_ref):
    s = jax.nn.silu(c_ref[...])
    o_ref[...] = jnp.dot(s.astype(BF16), w_ref[...].astype(BF16),
                         preferred_element_type=F32) + b_ref[...]


def _modulation(c_pad, w_ada, b_ada):
    n = w_ada.shape[1]
    tn = 1024
    return pl.pallas_call(
        _mod_kernel,
        grid=(n // tn,),
        in_specs=[pl.BlockSpec((SUBLANES, D_MODEL), lambda j: (0, 0)),
                  pl.BlockSpec((D_MODEL, tn), lambda j: (0, j)),
                  pl.BlockSpec((1, tn), lambda j: (0, j))],
        out_specs=pl.BlockSpec((SUBLANES, tn), lambda j: (0, j)),
        out_shape=jax.ShapeDtypeStruct((SUBLANES, n), F32),
        name="adaln_mod",
    )(c_pad, w_ada, b_ada)


def _inproj_kernel(x_ref, nw_ref, mod_ref, w_ref, wdt_ref, o_ref, dt_ref, h_scr):
    @pl.when(pl.program_id(1) == 0)
    def _():
        mod = mod_ref[0]
        shift = mod[:, 0:D_MODEL]
        scale = mod[:, D_MODEL:2 * D_MODEL]
        h = (_rms(x_ref[...]) * nw_ref[...]) * (1.0 + scale) + shift
        hb = h.astype(BF16)
        h_scr[...] = hb
        dt_ref[...] = jnp.dot(hb, wdt_ref[...], preferred_element_type=F32)

    o_ref[...] = jnp.dot(h_scr[...], w_ref[...], preferred_element_type=F32).astype(BF16)


def _in_projection(x2d, norm_w, mod3, w_main, w_dt, seq):
    t = x2d.shape[0]
    tm, tn = 1024, 2048
    tiles_per_batch = seq // tm
    return pl.pallas_call(
        _inproj_kernel,
        grid=(t // tm, PROJ_COLS // tn),
        in_specs=[pl.BlockSpec((tm, D_MODEL), lambda i, j: (i, 0)),
                  pl.BlockSpec((1, D_MODEL), lambda i, j: (0, 0)),
                  pl.BlockSpec((1, 1, N_MOD * D_MODEL), lambda i, j: (i // tiles_per_batch, 0, 0)),
                  pl.BlockSpec((D_MODEL, tn), lambda i, j: (0, j)),
                  pl.BlockSpec((D_MODEL, LANES), lambda i, j: (0, 0))],
        out_specs=[pl.BlockSpec((tm, tn), lambda i, j: (i, j)),
                   pl.BlockSpec((tm, LANES), lambda i, j: (i, 0))],
        out_shape=[jax.ShapeDtypeStruct((t, PROJ_COLS), BF16),
                   jax.ShapeDtypeStruct((t, LANES), F32)],
        scratch_shapes=[pltpu.VMEM((tm, D_MODEL), BF16)],
        compiler_params=pltpu.CompilerParams(
            dimension_semantics=("arbitrary", "arbitrary"), vmem_limit_bytes=VMEM_LIMIT),
        name="norm_in_proj",
    )(x2d, norm_w, mod3, w_main, w_dt)


def _split3(v):
    hi = v.astype(BF16).astype(F32)
    r1 = v - hi
    mid = r1.astype(BF16).astype(F32)
    return hi, r1, r1 - mid


def _lane_pieces(v, lane):
    hi, r1, r2 = _split3(v)
    return jnp.where(lane < N_HEADS, hi, jnp.where(lane < 2 * N_HEADS, r1, r2)).astype(BF16)


def _ssd_kernel(z_ref, xs_ref, bc_ref, dt_ref, cwx_ref, cwb_ref, cbx_ref, cbb_ref,
                dtb_ref, alog_ref, dskip_ref, nw_ref, tri_ref, exp_ref,
                o_ref, xpad, bpad, state, xact, xdtb, xdb, expd):
    c = pl.program_id(1)
    q = CHUNK
    halo = SUBLANES

    @pl.when(c == 0)
    def _():
        xpad[0:halo, :] = jnp.zeros((halo, D_INNER), F32)
        bpad[0:halo, :] = jnp.zeros((halo, BC_WIDTH), F32)
        state[...] = jnp.zeros_like(state)

    xpad[halo:halo + q, :] = xs_ref[...].astype(F32)
    bpad[halo:halo + q, :] = bc_ref[...].astype(F32)
    accx = cbx_ref[...]
    accb = cbb_ref[...]
    for k in range(CONV_K):
        lo = halo - (CONV_K - 1) + k
        accx = accx + xpad[lo:lo + q, :] * cwx_ref[k:k + 1, :]
        accb = accb + bpad[lo:lo + q, :] * cwb_ref[k:k + 1, :]
    xact[...] = jax.nn.silu(accx)
    bc_act = jax.nn.silu(accb).astype(BF16)
    xpad[0:halo, :] = xpad[q:q + halo, :]
    bpad[0:halo, :] = bpad[q:q + halo, :]

    dt = jax.nn.softplus(dt_ref[...] + dtb_ref[...])
    dta = dt * (-jnp.exp(alog_ref[...]))
    hi, r1, r2 = _split3(dta)
    stacked = jnp.concatenate([hi, r1, r2], axis=0).astype(BF16)
    acs = jnp.dot(tri_ref[...], stacked, preferred_element_type=F32)
    acs_t = acs.T
    acs_last = acs[q - 1:q, :]
    decay_to_end = jnp.exp(acs_last - acs)
    decay_from_start = jnp.exp(acs)

    lane = lax.broadcasted_iota(jnp.int32, (q, LANES), 1)
    pieces = jnp.concatenate([_lane_pieces(dt, lane), _lane_pieces(decay_to_end, lane),
                              _lane_pieces(decay_from_start, lane)], axis=0)
    expd[...] = jnp.dot(pieces, exp_ref[...], preferred_element_type=F32)
    xdt = xact[...] * expd[0:q, :]
    xdtb[...] = xdt.astype(BF16)
    xdb[...] = (xdt * expd[q:2 * q, :]).astype(BF16)

    row = lax.broadcasted_iota(jnp.int32, (q, q), 0)
    col = lax.broadcasted_iota(jnp.int32, (q, q), 1)
    causal = row >= col
    half = lax.broadcasted_iota(jnp.int32, (q, LANES), 1) < HEAD_DIM
    zero_b = jnp.zeros((q, LANES), BF16)

    for g in range(N_GROUPS):
        gs = slice(g * GROUP_CH, (g + 1) * GROUP_CH)
        bg = bc_act[:, g * D_STATE:(g + 1) * D_STATE]
        cg = bc_act[:, (N_GROUPS + g) * D_STATE:(N_GROUPS + g + 1) * D_STATE]
        scores = lax.dot_general(cg, bg, (((1,), (1,)), ((), ())), preferred_element_type=F32)
        st_old = state[:, gs]
        y_off = jnp.dot(cg, st_old.astype(BF16), preferred_element_type=F32)
        new_t = lax.dot_general(bg, xdb[:, gs], (((0,), (0,)), ((), ())),
                                preferred_element_type=F32)
        state[:, gs] = st_old * expd[3 * q - 1:3 * q, gs] + new_t
        ys = []
        for pr in range(GROUP_CH // LANES):
            h0 = g * (GROUP_CH // HEAD_DIM) + 2 * pr
            ms = []
            for h in (h0, h0 + 1):
                a_col = jnp.broadcast_to(acs[:, h:h + 1], (q, q))
                a_row = jnp.broadcast_to(acs_t[h:h + 1, :], (q, q))
                decay = jnp.exp(jnp.where(causal, a_col - a_row, -jnp.inf))
                ms.append((scores * decay).astype(BF16))
            xp = xdtb[:, g * GROUP_CH + pr * LANES:g * GROUP_CH + (pr + 1) * LANES]
            rhs = jnp.concatenate([jnp.where(half, xp, zero_b), jnp.where(half, zero_b, xp)], axis=0)
            ys.append(jnp.dot(jnp.concatenate(ms, axis=1), rhs, preferred_element_type=F32))
        y = jnp.concatenate(ys, axis=1) + y_off * expd[2 * q:3 * q, gs]
        xa = xact[:, gs]
        y = (y + dskip_ref[:, gs] * xa) * jax.nn.silu(z_ref[:, gs].astype(F32))
        o_ref[:, gs] = (_rms(y) * nw_ref[:, gs]).astype(BF16)


def _ssd_scan(proj, dt_raw, conv_w, conv_b, dt_bias3, a_log3, d_skip_ch, ssd_norm_w, tri3, expand3,
              batch, seq):
    q = CHUNK
    nc = seq // q
    t = batch * seq
    rowmap = lambda b, c: b * nc + c
    const = lambda b, c: (0, 0)
    return pl.pallas_call(
        _ssd_kernel,
        grid=(batch, nc),
        in_specs=[pl.BlockSpec((q, D_INNER), lambda b, c: (rowmap(b, c), COL_Z // D_INNER)),
                  pl.BlockSpec((q, D_INNER), lambda b, c: (rowmap(b, c), COL_XS // D_INNER)),
                  pl.BlockSpec((q, BC_WIDTH), lambda b, c: (rowmap(b, c), COL_BC // BC_WIDTH)),
                  pl.BlockSpec((q, LANES), lambda b, c: (rowmap(b, c), 0)),
                  pl.BlockSpec((CONV_K, D_INNER), const),
                  pl.BlockSpec((CONV_K, BC_WIDTH), const),
                  pl.BlockSpec((1, D_INNER), const),
                  pl.BlockSpec((1, BC_WIDTH), const),
                  pl.BlockSpec((1, LANES), const),
                  pl.BlockSpec((1, LANES), const),
                  pl.BlockSpec((1, D_INNER), const),
                  pl.BlockSpec((1, D_INNER), const),
                  pl.BlockSpec((q, DT_REP * q), const),
                  pl.BlockSpec((LANES, D_INNER), const)],
        out_specs=pl.BlockSpec((q, D_INNER), lambda b, c: (rowmap(b, c), 0)),
        out_shape=jax.ShapeDtypeStruct((t, D_INNER), BF16),
        scratch_shapes=[pltpu.VMEM((q + SUBLANES, D_INNER), F32),
                        pltpu.VMEM((q + SUBLANES, BC_WIDTH), F32),
                        pltpu.VMEM((D_STATE, D_INNER), F32),
                        pltpu.VMEM((q, D_INNER), F32),
                        pltpu.VMEM((q, D_INNER), BF16),
                        pltpu.VMEM((q, D_INNER), BF16),
                        pltpu.VMEM((DT_REP * q, D_INNER), F32)],
        compiler_params=pltpu.CompilerParams(
            dimension_semantics=("arbitrary", "arbitrary"), vmem_limit_bytes=VMEM_LIMIT),
        name="ssd_scan",
    )(proj, proj, proj, dt_raw, conv_w[:, :D_INNER], conv_w[:, D_INNER:],
      conv_b[:, :D_INNER], conv_b[:, D_INNER:], dt_bias3, a_log3, d_skip_ch, ssd_norm_w,
      tri3, expand3)


def _tail_kernel(x_ref, y_ref, u_ref, g_ref, mod_ref, pw_ref, ps_ref, wbs_ref, wbp_ref, wo_ref,
                 nmlp_ref, wup_ref, wdn_ref, nfin_ref, o_ref, upad, *, tm, tiles_per_batch):
    i = pl.program_id(0)
    halo = POOL_HALO

    @pl.when(i % tiles_per_batch == 0)
    def _():
        upad[0:halo, :] = jnp.zeros((halo, D_MODEL), F32)

    upad[halo:halo + tm, :] = u_ref[...].astype(F32)
    pos = (i % tiles_per_batch) * tm + lax.broadcasted_iota(jnp.int32, (tm, POOL_GW), 0)
    pooled = []
    for gi, win in enumerate(POOL_WINDOWS):
        cs = slice(gi * POOL_GW, (gi + 1) * POOL_GW)
        cur = upad[halo:halo + tm, cs]
        tot = cur
        for k in range(1, win):
            tot = tot + upad[halo - k:halo - k + tm, cs]
        count = jnp.minimum(pos + 1, win).astype(F32)
        pooled.append((tot / count - cur).astype(BF16))
    upad[0:halo, :] = upad[tm:tm + halo, :]
    y_pool_in = jnp.concatenate(
        [jnp.dot(pooled[gi], pw_ref[gi], preferred_element_type=F32) for gi in range(len(POOL_WINDOWS))],
        axis=1) * ps_ref[...]
    y_pool = jnp.dot(y_pool_in.astype(BF16), wbp_ref[...], preferred_element_type=F32)
    y_ssd = jnp.dot(y_ref[...], wbs_ref[...], preferred_element_type=F32)

    gates = jax.nn.sigmoid(g_ref[...].astype(F32))
    merged = gates[:, :D_MODEL] * y_ssd + gates[:, D_MODEL:] * y_pool
    mix = jnp.dot(merged.astype(BF16), wo_ref[...], preferred_element_type=F32)

    mod = mod_ref[0]
    gate_m = mod[:, 2 * D_MODEL:3 * D_MODEL]
    shift_f = mod[:, 3 * D_MODEL:4 * D_MODEL]
    scale_f = mod[:, 4 * D_MODEL:5 * D_MODEL]
    gate_f = mod[:, 5 * D_MODEL:6 * D_MODEL]
    x1 = x_ref[...] + gate_m * mix
    h = (_rms(x1) * nmlp_ref[...]) * (1.0 + scale_f) + shift_f
    up = jnp.dot(h.astype(BF16), wup_ref[...], preferred_element_type=F32)
    act = jnp.square(jnp.maximum(up, 0.0)).astype(BF16)
    x2 = x1 + gate_f * jnp.dot(act, wdn_ref[...], preferred_element_type=F32)
    o_ref[...] = _rms(x2) * nfin_ref[...]


def _tail(x2d, y_norm, proj, mod3, pool_w, pool_scale, w_bs, w_bp, w_out, norm_mlp_w, w_up, w_dn,
          norm_final_w, seq):
    t = x2d.shape[0]
    tm = 512
    tiles_per_batch = seq // tm
    const2 = lambda i: (0, 0)
    resident = functools.partial(pl.BlockSpec, pipeline_mode=pl.Buffered(1))
    return pl.pallas_call(
        functools.partial(_tail_kernel, tm=tm, tiles_per_batch=tiles_per_batch),
        grid=(t // tm,),
        in_specs=[pl.BlockSpec((tm, D_MODEL), lambda i: (i, 0)),
                  pl.BlockSpec((tm, D_INNER), lambda i: (i, 0)),
                  pl.BlockSpec((tm, D_MODEL), lambda i: (i, COL_POOL // D_MODEL)),
                  pl.BlockSpec((tm, 2 * D_MODEL), lambda i: (i, COL_GATE // (2 * D_MODEL))),
                  pl.BlockSpec((1, 1, N_MOD * D_MODEL), lambda i: (i // tiles_per_batch, 0, 0)),
                  resident((len(POOL_WINDOWS), POOL_GW, POOL_GW), lambda i: (0, 0, 0)),
                  resident((1, D_MODEL), const2),
                  resident((D_INNER, D_MODEL), const2),
                  resident((D_MODEL, D_MODEL), const2),
                  resident((D_MODEL, D_MODEL), const2),
                  resident((1, D_MODEL), const2),
                  resident((D_MODEL, D_FF), const2),
                  resident((D_FF, D_MODEL), const2),
                  resident((1, D_MODEL), const2)],
        out_specs=pl.BlockSpec((tm, D_MODEL), lambda i: (i, 0)),
        out_shape=jax.ShapeDtypeStruct((t, D_MODEL), F32),
        scratch_shapes=[pltpu.VMEM((tm + POOL_HALO, D_MODEL), F32)],
        compiler_params=pltpu.CompilerParams(
            dimension_semantics=("arbitrary",), vmem_limit_bytes=VMEM_LIMIT),
        name="pool_merge_mlp",
    )(x2d, y_norm, proj, proj, mod3, pool_w, pool_scale, w_bs, w_bp, w_out, norm_mlp_w, w_up, w_dn,
      norm_final_w)


def _cumsum_matrix():
    tri = np.tril(np.ones((CHUNK, CHUNK), np.float32))
    return jnp.asarray(np.concatenate([tri] * DT_REP, axis=1), BF16)


def _head_expand_matrix():
    m = np.zeros((LANES, D_INNER), np.float32)
    for k in range(DT_REP * N_HEADS):
        h = k % N_HEADS
        m[k, h * HEAD_DIM:(h + 1) * HEAD_DIM] = 1.0
    return jnp.asarray(m, BF16)


def _rep_heads(v):
    pad = jnp.zeros((LANES - DT_REP * N_HEADS,), v.dtype)
    return jnp.concatenate([v] * DT_REP + [pad])[None, :]


def kernel(x, c, w_ada, b_ada, norm_mix_w, w_in, conv_w, conv_b, dt_bias, a_log, d_skip, ssd_norm_w,
           w_branch_ssd, pool_w, pool_scale, w_branch_pool, w_out, norm_mlp_w, w_up, w_down,
           norm_final_w):
    batch, seq, d = x.shape
    depth = w_ada.shape[0]
    assert depth == 1, "the final norm is fused into the last kernel of a single layer"
    t = batch * seq
    tri3 = _cumsum_matrix()
    expand3 = _head_expand_matrix()
    c_pad = jnp.pad(c, ((0, SUBLANES - batch), (0, 0)))
    dt_lo = 2 * D_INNER + BC_WIDTH
    x2d = x.reshape(t, d)
    for i in range(depth):
        mod = _modulation(c_pad, w_ada[i], b_ada[i][None, :])[:batch]
        mod3 = mod[:, None, :]
        wi = w_in[i]
        w_main = jnp.concatenate([wi[:, :dt_lo], wi[:, dt_lo + N_HEADS:]], axis=1).astype(BF16)
        w_dt = wi[:, dt_lo:dt_lo + N_HEADS]
        w_dt3 = jnp.concatenate([w_dt] * DT_REP + [jnp.zeros((d, LANES - DT_REP * N_HEADS), F32)],
                                axis=1).astype(BF16)
        proj, dt_raw = _in_projection(x2d, norm_mix_w[i][None, :], mod3, w_main, w_dt3, seq)
        y_norm = _ssd_scan(proj, dt_raw, conv_w[i], conv_b[i][None, :], _rep_heads(dt_bias[i]),
                           _rep_heads(a_log[i]), jnp.repeat(d_skip[i], HEAD_DIM)[None, :],
                           ssd_norm_w[i][None, :], tri3, expand3, batch, seq)
        x2d = _tail(x2d, y_norm, proj, mod3, pool_w[i].astype(BF16), pool_scale[i][None, :],
                    w_branch_ssd[i].astype(BF16), w_branch_pool[i].astype(BF16), w_out[i].astype(BF16),
                    norm_mlp_w[i][None, :], w_up[i].astype(BF16), w_down[i].astype(BF16),
                    norm_final_w[None, :], seq)
    return x2d.reshape(batch, seq, d)
```

```python
import functools

import jax
import jax.numpy as jnp
import numpy as np
from jax import lax
from jax.experimental import pallas as pl
from jax.experimental.pallas import tpu as pltpu

F32 = jnp.float32
BF16 = jnp.bfloat16

D_MODEL = 1024
D_INNER = 2 * D_MODEL
HEAD_DIM = 64
N_HEADS = D_INNER // HEAD_DIM
N_GROUPS = 4
GROUP_CH = D_INNER // N_GROUPS
D_STATE = 128
CONV_K = 4
CHUNK = 128
BC_WIDTH = 2 * N_GROUPS * D_STATE
POOL_WINDOWS = (2, 4, 8, 16)
POOL_GW = D_MODEL // len(POOL_WINDOWS)
POOL_HALO = 16
D_FF = 4 * D_MODEL
N_MOD = 6
EPS = 1e-5
LANES = 128
SUBLANES = 8
DT_REP = 3

PROJ_TN = 2048
COL_Z = 0
COL_XS = D_INNER
COL_BC = 2 * D_INNER
COL_POOL = 2 * D_INNER + BC_WIDTH
COL_GATE = COL_POOL + D_MODEL
PROJ_COLS = COL_GATE + 2 * D_MODEL
TILE_Z, TILE_XS, TILE_BC_POOL, TILE_GATE = 0, 1, 2, 3

VMEM_LIMIT = 56 * 1024 * 1024
SSD_CHUNKS_PER_STEP = 2


def _rms(x):
    return x * lax.rsqrt(jnp.mean(x * x, axis=-1, keepdims=True) + EPS)


def _mod_kernel(c_ref, w_ref, b_ref, o_ref):
    s = jax.nn.silu(c_ref[...])
    o_ref[...] = jnp.dot(s.astype(BF16), w_ref[...].astype(BF16),
                         preferred_element_type=F32) + b_ref[...]


def _modulation(c_pad, w_ada, b_ada):
    n = w_ada.shape[1]
    tn = 1024
    return pl.pallas_call(
        _mod_kernel,
        grid=(n // tn,),
        in_specs=[pl.BlockSpec((SUBLANES, D_MODEL), lambda j: (0, 0)),
                  pl.BlockSpec((D_MODEL, tn), lambda j: (0, j)),
                  pl.BlockSpec((1, tn), lambda j: (0, j))],
        out_specs=pl.BlockSpec((SUBLANES, tn), lambda j: (0, j)),
        out_shape=jax.ShapeDtypeStruct((SUBLANES, n), F32),
        name="adaln_mod",
    )(c_pad, w_ada, b_ada)


def _conv_silu(r, cw_ref, cb_ref, halo_ref, width):
    tm = r.shape[0]
    halo = halo_ref[:, 0:width]
    halo_ref[:, 0:width] = r[tm - SUBLANES:tm, :]
    row = lax.broadcasted_iota(jnp.int32, (SUBLANES, width), 0)
    acc = r * cw_ref[CONV_K - 1:CONV_K, 0:width] + cb_ref[:, 0:width]
    for shift in range(1, CONV_K):
        sh = pltpu.roll(r, shift, 0)
        top = jnp.where(row < shift, pltpu.roll(halo, shift, 0), sh[0:SUBLANES, :])
        sh = jnp.concatenate([top, sh[SUBLANES:, :]], axis=0)
        acc = acc + sh * cw_ref[CONV_K - 1 - shift:CONV_K - shift, 0:width]
    return jax.nn.silu(acc)


def _inproj_kernel(x_ref, nw_ref, mod_ref, w_ref, wdt_ref, cw_ref, cb_ref, o_ref, dt_ref,
                   h_scr, halo_scr, *, tiles_per_batch):
    i = pl.program_id(0)
    j = pl.program_id(1)

    @pl.when(j == 0)
    def _():
        mod = mod_ref[0]
        shift = mod[:, 0:D_MODEL]
        scale = mod[:, D_MODEL:2 * D_MODEL]
        h = (_rms(x_ref[...]) * nw_ref[...]) * (1.0 + scale) + shift
        hb = h.astype(BF16)
        h_scr[...] = hb
        dt_ref[...] = jnp.dot(hb, wdt_ref[...], preferred_element_type=F32)

    @pl.when(jnp.logical_and(i % tiles_per_batch == 0, j == 0))
    def _():
        halo_scr[...] = jnp.zeros_like(halo_scr)

    def proj():
        return jnp.dot(h_scr[...], w_ref[...], preferred_element_type=F32)

    @pl.when(j == TILE_Z)
    def _():
        o_ref[...] = jax.nn.silu(proj()).astype(BF16)

    @pl.when(j == TILE_XS)
    def _():
        o_ref[...] = _conv_silu(proj(), cw_ref, cb_ref, halo_scr.at[0], PROJ_TN).astype(BF16)

    @pl.when(j == TILE_BC_POOL)
    def _():
        r = proj()
        o_ref[:, 0:BC_WIDTH] = _conv_silu(r[:, 0:BC_WIDTH], cw_ref, cb_ref, halo_scr.at[1],
                                          BC_WIDTH).astype(BF16)
        o_ref[:, BC_WIDTH:] = r[:, BC_WIDTH:].astype(BF16)

    @pl.when(j == TILE_GATE)
    def _():
        o_ref[...] = jax.nn.sigmoid(proj()).astype(BF16)


def _in_projection(x2d, norm_w, mod3, w_main, w_dt, conv_w_cols, conv_b_cols, seq):
    t = x2d.shape[0]
    tm, tn = 1024, PROJ_TN
    tiles_per_batch = seq // tm
    return pl.pallas_call(
        functools.partial(_inproj_kernel, tiles_per_batch=tiles_per_batch),
        grid=(t // tm, PROJ_COLS // tn),
        in_specs=[pl.BlockSpec((tm, D_MODEL), lambda i, j: (i, 0)),
                  pl.BlockSpec((1, D_MODEL), lambda i, j: (0, 0)),
                  pl.BlockSpec((1, 1, N_MOD * D_MODEL), lambda i, j: (i // tiles_per_batch, 0, 0)),
                  pl.BlockSpec((D_MODEL, tn), lambda i, j: (0, j)),
                  pl.BlockSpec((D_MODEL, LANES), lambda i, j: (0, 0)),
                  pl.BlockSpec((CONV_K, tn), lambda i, j: (0, j)),
                  pl.BlockSpec((1, tn), lambda i, j: (0, j))],
        out_specs=[pl.BlockSpec((tm, tn), lambda i, j: (i, j)),
                   pl.BlockSpec((tm, LANES), lambda i, j: (i, 0))],
        out_shape=[jax.ShapeDtypeStruct((t, PROJ_COLS), BF16),
                   jax.ShapeDtypeStruct((t, LANES), F32)],
        scratch_shapes=[pltpu.VMEM((tm, D_MODEL), BF16),
                        pltpu.VMEM((2, SUBLANES, tn), F32)],
        compiler_params=pltpu.CompilerParams(
            dimension_semantics=("arbitrary", "arbitrary"), vmem_limit_bytes=VMEM_LIMIT),
        name="norm_in_proj",
    )(x2d, norm_w, mod3, w_main, w_dt, conv_w_cols, conv_b_cols)


def _split3(v):
    hi = v.astype(BF16).astype(F32)
    r1 = v - hi
    mid = r1.astype(BF16).astype(F32)
    return hi, r1, r1 - mid


def _lane_pieces(v, lane):
    hi, r1, r2 = _split3(v)
    return jnp.where(lane < N_HEADS, hi, jnp.where(lane < 2 * N_HEADS, r1, r2)).astype(BF16)


def _ssd_chunk(r0, xs_ref, bc_ref, dt_ref, dtb_ref, alog_ref, dskip_ref, tri_ref, exp_ref,
               o_ref, state, xdtb, xdb, expd):
    q = CHUNK
    rows = slice(r0, r0 + q)

    dt = jax.nn.softplus(dt_ref[rows, :] + dtb_ref[...])
    dta = dt * (-jnp.exp(alog_ref[...]))
    hi, r1, r2 = _split3(dta)
    stacked = jnp.concatenate([hi, r1, r2], axis=0).astype(BF16)
    acs = jnp.dot(tri_ref[...], stacked, preferred_element_type=F32)
    acs_t = acs.T
    acs_last = acs[q - 1:q, :]
    decay_to_end = jnp.exp(acs_last - acs)
    decay_from_start = jnp.exp(acs)

    lane = lax.broadcasted_iota(jnp.int32, (q, LANES), 1)
    pieces = jnp.concatenate([_lane_pieces(dt, lane), _lane_pieces(decay_to_end, lane),
                              _lane_pieces(decay_from_start, lane)], axis=0)
    expd[...] = jnp.dot(pieces, exp_ref[...], preferred_element_type=F32)
    xdt = xs_ref[rows, :].astype(F32) * expd[0:q, :]
    xdtb[...] = xdt.astype(BF16)
    xdb[...] = (xdt * expd[q:2 * q, :]).astype(BF16)

    row = lax.broadcasted_iota(jnp.int32, (q, q), 0)
    col = lax.broadcasted_iota(jnp.int32, (q, q), 1)
    causal = row >= col
    half = lax.broadcasted_iota(jnp.int32, (q, LANES), 1) < HEAD_DIM
    zero_b = jnp.zeros((q, LANES), BF16)

    for g in range(N_GROUPS):
        gs = slice(g * GROUP_CH, (g + 1) * GROUP_CH)
        bg = bc_ref[rows, g * D_STATE:(g + 1) * D_STATE]
        cg = bc_ref[rows, (N_GROUPS + g) * D_STATE:(N_GROUPS + g + 1) * D_STATE]
        scores = lax.dot_general(cg, bg, (((1,), (1,)), ((), ())), preferred_element_type=F32)
        st_old = state[:, gs]
        y_off = jnp.dot(cg, st_old.astype(BF16), preferred_element_type=F32)
        new_t = lax.dot_general(bg, xdb[:, gs], (((0,), (0,)), ((), ())),
                                preferred_element_type=F32)
        state[:, gs] = st_old * expd[3 * q - 1:3 * q, gs] + new_t
        ys = []
        for pr in range(GROUP_CH // LANES):
            h0 = g * (GROUP_CH // HEAD_DIM) + 2 * pr
            ms = []
            for h in (h0, h0 + 1):
                a_col = jnp.broadcast_to(acs[:, h:h + 1], (q, q))
                a_row = jnp.broadcast_to(acs_t[h:h + 1, :], (q, q))
                decay = jnp.exp(jnp.where(causal, a_col - a_row, -jnp.inf))
                ms.append((scores * decay).astype(BF16))
            xp = xdtb[:, g * GROUP_CH + pr * LANES:g * GROUP_CH + (pr + 1) * LANES]
            rhs = jnp.concatenate([jnp.where(half, xp, zero_b), jnp.where(half, zero_b, xp)], axis=0)
            ys.append(jnp.dot(jnp.concatenate(ms, axis=1), rhs, preferred_element_type=F32))
        y = jnp.concatenate(ys, axis=1) + y_off * expd[2 * q:3 * q, gs]
        o_ref[rows, gs] = (y + dskip_ref[:, gs] * xs_ref[rows, gs].astype(F32)).astype(BF16)


def _ssd_kernel(xs_ref, bc_ref, dt_ref, dtb_ref, alog_ref, dskip_ref, tri_ref, exp_ref,
                o_ref, state, xdtb, xdb, expd):
    @pl.when(pl.program_id(1) == 0)
    def _():
        state[...] = jnp.zeros_like(state)

    for ci in range(SSD_CHUNKS_PER_STEP):
        _ssd_chunk(ci * CHUNK, xs_ref, bc_ref, dt_ref, dtb_ref, alog_ref, dskip_ref, tri_ref, exp_ref,
                   o_ref, state, xdtb, xdb, expd)


def _ssd_scan(proj, dt_raw, dt_bias3, a_log3, d_skip_ch, tri3, expand3, batch, seq):
    q = CHUNK
    tm = SSD_CHUNKS_PER_STEP * q
    steps = seq // tm
    t = batch * seq
    rowmap = lambda b, c: b * steps + c
    const = lambda b, c: (0, 0)
    return pl.pallas_call(
        _ssd_kernel,
        grid=(batch, steps),
        in_specs=[pl.BlockSpec((tm, D_INNER), lambda b, c: (rowmap(b, c), COL_XS // D_INNER)),
                  pl.BlockSpec((tm, BC_WIDTH), lambda b, c: (rowmap(b, c), COL_BC // BC_WIDTH)),
                  pl.BlockSpec((tm, LANES), lambda b, c: (rowmap(b, c), 0)),
                  pl.BlockSpec((1, LANES), const),
                  pl.BlockSpec((1, LANES), const),
                  pl.BlockSpec((1, D_INNER), const),
                  pl.BlockSpec((q, DT_REP * q), const),
                  pl.BlockSpec((LANES, D_INNER), const)],
        out_specs=pl.BlockSpec((tm, D_INNER), lambda b, c: (rowmap(b, c), 0)),
        out_shape=jax.ShapeDtypeStruct((t, D_INNER), BF16),
        scratch_shapes=[pltpu.VMEM((D_STATE, D_INNER), F32),
                        pltpu.VMEM((q, D_INNER), BF16),
                        pltpu.VMEM((q, D_INNER), BF16),
                        pltpu.VMEM((DT_REP * q, D_INNER), F32)],
        compiler_params=pltpu.CompilerParams(
            dimension_semantics=("arbitrary", "arbitrary"), vmem_limit_bytes=VMEM_LIMIT),
        name="ssd_scan",
    )(proj, proj, dt_raw, dt_bias3, a_log3, d_skip_ch, tri3, expand3)


def _tail_kernel(x_ref, y_ref, zs_ref, u_ref, g_ref, mod_ref, snw_ref, pw_ref, ps_ref, wbs_ref, wbp_ref,
                 wo_ref, nmlp_ref, wup_ref, wdn_ref, nfin_ref, o_ref, upad, *, tm, tiles_per_batch):
    i = pl.program_id(0)
    halo = POOL_HALO

    @pl.when(i % tiles_per_batch == 0)
    def _():
        upad[0:halo, :] = jnp.zeros((halo, D_MODEL), F32)

    y_gated = []
    for g in range(N_GROUPS):
        gs = slice(g * GROUP_CH, (g + 1) * GROUP_CH)
        yg = y_ref[:, gs].astype(F32) * zs_ref[:, gs].astype(F32)
        y_gated.append((_rms(yg) * snw_ref[:, gs]).astype(BF16))
    y_ssd = jnp.dot(jnp.concatenate(y_gated, axis=1), wbs_ref[...], preferred_element_type=F32)

    upad[halo:halo + tm, :] = u_ref[...].astype(F32)
    pos = (i % tiles_per_batch) * tm + lax.broadcasted_iota(jnp.int32, (tm, POOL_GW), 0)
    pooled = []
    for gi, win in enumerate(POOL_WINDOWS):
        cs = slice(gi * POOL_GW, (gi + 1) * POOL_GW)
        cur = upad[halo:halo + tm, cs]
        tot = cur
        for k in range(1, win):
            tot = tot + upad[halo - k:halo - k + tm, cs]
        count = jnp.minimum(pos + 1, win).astype(F32)
        pooled.append((tot / count - cur).astype(BF16))
    upad[0:halo, :] = upad[tm:tm + halo, :]
    y_pool_in = jnp.concatenate(
        [jnp.dot(pooled[gi], pw_ref[gi], preferred_element_type=F32) for gi in range(len(POOL_WINDOWS))],
        axis=1) * ps_ref[...]
    y_pool = jnp.dot(y_pool_in.astype(BF16), wbp_ref[...], preferred_element_type=F32)

    merged = (g_ref[:, :D_MODEL].astype(F32) * y_ssd + g_ref[:, D_MODEL:].astype(F32) * y_pool)
    mix = jnp.dot(merged.astype(BF16), wo_ref[...], preferred_element_type=F32)

    mod = mod_ref[0]
    gate_m = mod[:, 2 * D_MODEL:3 * D_MODEL]
    shift_f = mod[:, 3 * D_MODEL:4 * D_MODEL]
    scale_f = mod[:, 4 * D_MODEL:5 * D_MODEL]
    gate_f = mod[:, 5 * D_MODEL:6 * D_MODEL]
    x1 = x_ref[...] + gate_m * mix
    h = (_rms(x1) * nmlp_ref[...]) * (1.0 + scale_f) + shift_f
    up = jnp.dot(h.astype(BF16), wup_ref[...], preferred_element_type=F32)
    act = jnp.square(jnp.maximum(up, 0.0)).astype(BF16)
    x2 = x1 + gate_f * jnp.dot(act, wdn_ref[...], preferred_element_type=F32)
    o_ref[...] = _rms(x2) * nfin_ref[...]


def _tail(x2d, y_raw, proj, mod3, ssd_norm_w, pool_w, pool_scale, w_bs, w_bp, w_out, norm_mlp_w, w_up,
          w_dn, norm_final_w, seq):
    t = x2d.shape[0]
    tm = 512
    tiles_per_batch = seq // tm
    const2 = lambda i: (0, 0)
    resident = functools.partial(pl.BlockSpec, pipeline_mode=pl.Buffered(1))
    return pl.pallas_call(
        functools.partial(_tail_kernel, tm=tm, tiles_per_batch=tiles_per_batch),
        grid=(t // tm,),
        in_specs=[pl.BlockSpec((tm, D_MODEL), lambda i: (i, 0)),
                  pl.BlockSpec((tm, D_INNER), lambda i: (i, 0)),
                  pl.BlockSpec((tm, D_INNER), lambda i: (i, COL_Z // D_INNER)),
                  pl.BlockSpec((tm, D_MODEL), lambda i: (i, COL_POOL // D_MODEL)),
                  pl.BlockSpec((tm, 2 * D_MODEL), lambda i: (i, COL_GATE // (2 * D_MODEL))),
                  pl.BlockSpec((1, 1, N_MOD * D_MODEL), lambda i: (i // tiles_per_batch, 0, 0)),
                  resident((1, D_INNER), const2),
                  resident((len(POOL_WINDOWS), POOL_GW, POOL_GW), lambda i: (0, 0, 0)),
                  resident((1, D_MODEL), const2),
                  resident((D_INNER, D_MODEL), const2),
                  resident((D_MODEL, D_MODEL), const2),
                  resident((D_MODEL, D_MODEL), const2),
                  resident((1, D_MODEL), const2),
                  resident((D_MODEL, D_FF), const2),
                  resident((D_FF, D_MODEL), const2),
                  resident((1, D_MODEL), const2)],
        out_specs=pl.BlockSpec((tm, D_MODEL), lambda i: (i, 0)),
        out_shape=jax.ShapeDtypeStruct((t, D_MODEL), F32),
        scratch_shapes=[pltpu.VMEM((tm + POOL_HALO, D_MODEL), F32)],
        compiler_params=pltpu.CompilerParams(
            dimension_semantics=("arbitrary",), vmem_limit_bytes=VMEM_LIMIT),
        name="gate_pool_merge_mlp",
    )(x2d, y_raw, proj, proj, proj, mod3, ssd_norm_w, pool_w, pool_scale, w_bs, w_bp, w_out, norm_mlp_w,
      w_up, w_dn, norm_final_w)


def _cumsum_matrix():
    tri = np.tril(np.ones((CHUNK, CHUNK), np.float32))
    return jnp.asarray(np.concatenate([tri] * DT_REP, axis=1), BF16)


def _head_expand_matrix():
    m = np.zeros((LANES, D_INNER), np.float32)
    for k in range(DT_REP * N_HEADS):
        h = k % N_HEADS
        m[k, h * HEAD_DIM:(h + 1) * HEAD_DIM] = 1.0
    return jnp.asarray(m, BF16)


def _rep_heads(v):
    pad = jnp.zeros((LANES - DT_REP * N_HEADS,), v.dtype)
    return jnp.concatenate([v] * DT_REP + [pad])[None, :]


def _on_proj_cols(v):
    return jnp.pad(v, ((0, 0), (COL_XS, PROJ_COLS - COL_POOL)))


def kernel(x, c, w_ada, b_ada, norm_mix_w, w_in, conv_w, conv_b, dt_bias, a_log, d_skip, ssd_norm_w,
           w_branch_ssd, pool_w, pool_scale, w_branch_pool, w_out, norm_mlp_w, w_up, w_down,
           norm_final_w):
    batch, seq, d = x.shape
    depth = w_ada.shape[0]
    assert depth == 1, "the final norm is fused into the last kernel of a single layer"
    t = batch * seq
    tri3 = _cumsum_matrix()
    expand3 = _head_expand_matrix()
    c_pad = jnp.pad(c, ((0, SUBLANES - batch), (0, 0)))
    dt_lo = 2 * D_INNER + BC_WIDTH
    x2d = x.reshape(t, d)
    i = 0
    mod = _modulation(c_pad, w_ada[i], b_ada[i][None, :])[:batch]
    mod3 = mod[:, None, :]
    wi = w_in[i]
    w_main = jnp.concatenate([wi[:, :dt_lo], wi[:, dt_lo + N_HEADS:]], axis=1).astype(BF16)
    w_dt = wi[:, dt_lo:dt_lo + N_HEADS]
    w_dt3 = jnp.concatenate([w_dt] * DT_REP + [jnp.zeros((d, LANES - DT_REP * N_HEADS), F32)],
                            axis=1).astype(BF16)
    proj, dt_raw = _in_projection(x2d, norm_mix_w[i][None, :], mod3, w_main, w_dt3,
                                  _on_proj_cols(conv_w[i]), _on_proj_cols(conv_b[i][None, :]), seq)
    y_raw = _ssd_scan(proj, dt_raw, _rep_heads(dt_bias[i]), _rep_heads(a_log[i]),
                      jnp.repeat(d_skip[i], HEAD_DIM)[None, :], tri3, expand3, batch, seq)
    out = _tail(x2d, y_raw, proj, mod3, ssd_norm_w[i][None, :], pool_w[i].astype(BF16),
                pool_scale[i][None, :], w_branch_ssd[i].astype(BF16), w_branch_pool[i].astype(BF16),
                w_out[i].astype(BF16), norm_mlp_w[i][None, :], w_up[i].astype(BF16),
                w_down[i].astype(BF16), norm_final_w[None, :], seq)
    return out.reshape(batch, seq, d)
```

```python
import functools

import jax
import jax.numpy as jnp
import numpy as np
from jax import lax
from jax.experimental import pallas as pl
from jax.experimental.pallas import tpu as pltpu

F32 = jnp.float32
BF16 = jnp.bfloat16

D_MODEL = 1024
D_INNER = 2 * D_MODEL
HEAD_DIM = 64
N_HEADS = D_INNER // HEAD_DIM
N_GROUPS = 4
GROUP_CH = D_INNER // N_GROUPS
D_STATE = 128
CONV_K = 4
CHUNK = 128
BC_WIDTH = 2 * N_GROUPS * D_STATE
POOL_WINDOWS = (2, 4, 8, 16)
POOL_GW = D_MODEL // len(POOL_WINDOWS)
D_FF = 4 * D_MODEL
N_MOD = 6
EPS = 1e-5
LANES = 128
SUBLANES = 8
SLABS = CHUNK // SUBLANES
DT_REP = 3

PROJ_TN = 2048
COL_Z = 0
COL_XS = D_INNER
COL_BC = 2 * D_INNER
COL_POOL = 2 * D_INNER + BC_WIDTH
COL_GATE = COL_POOL + D_MODEL
PROJ_COLS = COL_GATE + 2 * D_MODEL
TILE_Z, TILE_XS, TILE_BC_POOL, TILE_GATE = 0, 1, 2, 3
CONV_COL_BLOCK = 512
PROJ_ROW_SPLIT = 8

VMEM_LIMIT = 56 * 1024 * 1024
SSD_CHUNKS_PER_STEP = 2


def _rms(x):
    return x * lax.rsqrt(jnp.mean(x * x, axis=-1, keepdims=True) + EPS)


def _token_of_row(r):
    return (r % SUBLANES) * SLABS + r // SUBLANES


def _permute_chunks(perm, xb):
    n_chunks = xb.shape[0] // CHUNK
    return jnp.concatenate(
        [jnp.dot(perm, xb[c * CHUNK:(c + 1) * CHUNK, :], preferred_element_type=F32).astype(BF16)
         for c in range(n_chunks)], axis=0)


def _slabs_of(x, n_chunks):
    return [[x[(c * SLABS + v) * SUBLANES:(c * SLABS + v + 1) * SUBLANES, :] for v in range(SLABS)]
            for c in range(n_chunks)]


def _wrapped_slabs(slabs, prev_last, max_shift):
    n_chunks = len(slabs)
    width = slabs[0][0].shape[1]
    first_row = lax.broadcasted_iota(jnp.int32, (SUBLANES, width), 0) == 0
    wrapped = {}
    for j in range(1, max_shift + 1):
        stacked = jnp.concatenate([slabs[c][SLABS - j] for c in range(n_chunks)], axis=0)
        rolled = pltpu.roll(stacked, 1, 0)
        head = jnp.where(first_row, pltpu.roll(prev_last[j], 1, 0), rolled[0:SUBLANES, :])
        wrapped[j] = [head] + [rolled[c * SUBLANES:(c + 1) * SUBLANES, :] for c in range(1, n_chunks)]
    return wrapped


def _shifted(slabs, wrapped, c, v, k):
    return slabs[c][v - k] if v >= k else wrapped[k - v][c]


def _mod_kernel(c_ref, w_ref, b_ref, o_ref):
    s = jax.nn.silu(c_ref[...])
    o_ref[...] = jnp.dot(s.astype(BF16), w_ref[...].astype(BF16),
                         preferred_element_type=F32) + b_ref[...]


def _modulation(c_pad, w_ada, b_ada):
    n = w_ada.shape[1]
    tn = 1024
    return pl.pallas_call(
        _mod_kernel,
        grid=(n // tn,),
        in_specs=[pl.BlockSpec((SUBLANES, D_MODEL), lambda j: (0, 0)),
                  pl.BlockSpec((D_MODEL, tn), lambda j: (0, j)),
                  pl.BlockSpec((1, tn), lambda j: (0, j))],
        out_specs=pl.BlockSpec((SUBLANES, tn), lambda j: (0, j)),
        out_shape=jax.ShapeDtypeStruct((SUBLANES, n), F32),
        name="adaln_mod",
    )(c_pad, w_ada, b_ada)


def _conv_silu_store(r, cw_ref, cb_ref, halo_ref, o_ref, width):
    n_chunks = r.shape[0] // CHUNK
    pair = 2 * SUBLANES
    for c0 in range(0, width, CONV_COL_BLOCK):
        cols = slice(c0, c0 + CONV_COL_BLOCK)
        slabs = _slabs_of(r[:, cols], n_chunks)
        prev_last = {j: halo_ref[j - 1, :, cols] for j in range(1, CONV_K)}
        for j in range(1, CONV_K):
            halo_ref[j - 1, :, cols] = slabs[n_chunks - 1][SLABS - j]
        wrapped = _wrapped_slabs(slabs, prev_last, CONV_K - 1)
        taps = [cw_ref[k:k + 1, cols] for k in range(CONV_K)]
        bias = cb_ref[:, cols]
        for c in range(n_chunks):
            for v0 in range(0, SLABS, 2):
                accs = []
                for v in (v0, v0 + 1):
                    acc = slabs[c][v] * taps[CONV_K - 1] + bias
                    for k in range(1, CONV_K):
                        acc = acc + _shifted(slabs, wrapped, c, v, k) * taps[CONV_K - 1 - k]
                    accs.append(acc)
                lo = (c * SLABS + v0) * SUBLANES
                o_ref[lo:lo + pair, cols] = jax.nn.silu(jnp.concatenate(accs, axis=0)).astype(BF16)


def _inproj_kernel(x_ref, nw_ref, mod_ref, w_ref, wdt_ref, cw_ref, cb_ref, perm_ref, o_ref, dt_ref,
                   h_scr, halo_scr, *, tiles_per_batch):
    i = pl.program_id(0)
    j = pl.program_id(1)

    @pl.when(j == 0)
    def _():
        mod = mod_ref[0]
        shift = mod[:, 0:D_MODEL]
        scale = mod[:, D_MODEL:2 * D_MODEL]
        h = (_rms(x_ref[...]) * nw_ref[...]) * (1.0 + scale) + shift
        hb = _permute_chunks(perm_ref[...], h.astype(BF16))
        h_scr[...] = hb
        dt_ref[...] = jnp.dot(hb, wdt_ref[...], preferred_element_type=F32)

    @pl.when(jnp.logical_and(i % tiles_per_batch == 0, j == 0))
    def _():
        halo_scr[...] = jnp.zeros_like(halo_scr)

    def proj():
        part = h_scr.shape[0] // PROJ_ROW_SPLIT
        return jnp.concatenate(
            [jnp.dot(h_scr[m0:m0 + part, :], w_ref[...], preferred_element_type=F32)
             for m0 in range(0, h_scr.shape[0], part)], axis=0)

    @pl.when(j == TILE_Z)
    def _():
        o_ref[...] = jax.nn.silu(proj()).astype(BF16)

    @pl.when(j == TILE_XS)
    def _():
        _conv_silu_store(proj(), cw_ref, cb_ref, halo_scr.at[0], o_ref, PROJ_TN)

    @pl.when(j == TILE_BC_POOL)
    def _():
        r = proj()
        _conv_silu_store(r[:, 0:BC_WIDTH], cw_ref, cb_ref, halo_scr.at[1], o_ref, BC_WIDTH)
        o_ref[:, BC_WIDTH:] = r[:, BC_WIDTH:].astype(BF16)

    @pl.when(j == TILE_GATE)
    def _():
        o_ref[...] = jax.nn.sigmoid(proj()).astype(BF16)


def _in_projection(x2d, norm_w, mod3, w_main, w_dt, conv_w_cols, conv_b_cols, perm, seq):
    t = x2d.shape[0]
    tm, tn = 1024, PROJ_TN
    tiles_per_batch = seq // tm
    return pl.pallas_call(
        functools.partial(_inproj_kernel, tiles_per_batch=tiles_per_batch),
        grid=(t // tm, PROJ_COLS // tn),
        in_specs=[pl.BlockSpec((tm, D_MODEL), lambda i, j: (i, 0)),
                  pl.BlockSpec((1, D_MODEL), lambda i, j: (0, 0)),
                  pl.BlockSpec((1, 1, N_MOD * D_MODEL), lambda i, j: (i // tiles_per_batch, 0, 0)),
                  pl.BlockSpec((D_MODEL, tn), lambda i, j: (0, j)),
                  pl.BlockSpec((D_MODEL, LANES), lambda i, j: (0, 0)),
                  pl.BlockSpec((CONV_K, tn), lambda i, j: (0, j)),
                  pl.BlockSpec((1, tn), lambda i, j: (0, j)),
                  pl.BlockSpec((CHUNK, CHUNK), lambda i, j: (0, 0))],
        out_specs=[pl.BlockSpec((tm, tn), lambda i, j: (i, j)),
                   pl.BlockSpec((tm, LANES), lambda i, j: (i, 0))],
        out_shape=[jax.ShapeDtypeStruct((t, PROJ_COLS), BF16),
                   jax.ShapeDtypeStruct((t, LANES), F32)],
        scratch_shapes=[pltpu.VMEM((tm, D_MODEL), BF16),
                        pltpu.VMEM((2, CONV_K - 1, SUBLANES, tn), F32)],
        compiler_params=pltpu.CompilerParams(
            dimension_semantics=("arbitrary", "arbitrary"), vmem_limit_bytes=VMEM_LIMIT),
        name="norm_in_proj",
    )(x2d, norm_w, mod3, w_main, w_dt, conv_w_cols, conv_b_cols, perm)


def _split3(v):
    hi = v.astype(BF16).astype(F32)
    r1 = v - hi
    mid = r1.astype(BF16).astype(F32)
    return hi, r1, r1 - mid


def _lane_pieces(v, lane):
    hi, r1, r2 = _split3(v)
    return jnp.where(lane < N_HEADS, hi, jnp.where(lane < 2 * N_HEADS, r1, r2)).astype(BF16)


def _ssd_chunk(r0, xs_ref, bc_ref, dt_ref, dtb_ref, alog_ref, dskip_ref, tri_ref, exp_ref,
               o_ref, state, xdtb, xdb, expd):
    q = CHUNK
    rows = slice(r0, r0 + q)

    dt = jax.nn.softplus(dt_ref[rows, :] + dtb_ref[...])
    dta = dt * (-jnp.exp(alog_ref[...]))
    hi, r1, r2 = _split3(dta)
    stacked = jnp.concatenate([hi, r1, r2], axis=0).astype(BF16)
    acs = jnp.dot(tri_ref[...], stacked, preferred_element_type=F32)
    acs_t = acs.T
    acs_last = acs[q - 1:q, :]
    decay_to_end = jnp.exp(acs_last - acs)
    decay_from_start = jnp.exp(acs)

    lane = lax.broadcasted_iota(jnp.int32, (q, LANES), 1)
    pieces = jnp.concatenate([_lane_pieces(dt, lane), _lane_pieces(decay_to_end, lane),
                              _lane_pieces(decay_from_start, lane)], axis=0)
    expd[...] = jnp.dot(pieces, exp_ref[...], preferred_element_type=F32)
    xdt = xs_ref[rows, :].astype(F32) * expd[0:q, :]
    xdtb[...] = xdt.astype(BF16)
    xdb[...] = (xdt * expd[q:2 * q, :]).astype(BF16)

    causal = (_token_of_row(lax.broadcasted_iota(jnp.int32, (q, q), 0))
              >= _token_of_row(lax.broadcasted_iota(jnp.int32, (q, q), 1)))
    half = lax.broadcasted_iota(jnp.int32, (q, LANES), 1) < HEAD_DIM
    zero_b = jnp.zeros((q, LANES), BF16)

    for g in range(N_GROUPS):
        gs = slice(g * GROUP_CH, (g + 1) * GROUP_CH)
        bg = bc_ref[rows, g * D_STATE:(g + 1) * D_STATE]
        cg = bc_ref[rows, (N_GROUPS + g) * D_STATE:(N_GROUPS + g + 1) * D_STATE]
        scores = lax.dot_general(cg, bg, (((1,), (1,)), ((), ())), preferred_element_type=F32)
        st_old = state[:, gs]
        y_off = jnp.dot(cg, st_old.astype(BF16), preferred_element_type=F32)
        new_t = lax.dot_general(bg, xdb[:, gs], (((0,), (0,)), ((), ())),
                                preferred_element_type=F32)
        state[:, gs] = st_old * expd[3 * q - 1:3 * q, gs] + new_t
        ys = []
        for pr in range(GROUP_CH // LANES):
            h0 = g * (GROUP_CH // HEAD_DIM) + 2 * pr
            ms = []
            for h in (h0, h0 + 1):
                a_col = jnp.broadcast_to(acs[:, h:h + 1], (q, q))
                a_row = jnp.broadcast_to(acs_t[h:h + 1, :], (q, q))
                decay = jnp.exp(jnp.where(causal, a_col - a_row, -jnp.inf))
                ms.append((scores * decay).astype(BF16))
            xp = xdtb[:, g * GROUP_CH + pr * LANES:g * GROUP_CH + (pr + 1) * LANES]
            rhs = jnp.concatenate([jnp.where(half, xp, zero_b), jnp.where(half, zero_b, xp)], axis=0)
            ys.append(jnp.dot(jnp.concatenate(ms, axis=1), rhs, preferred_element_type=F32))
        y = jnp.concatenate(ys, axis=1) + y_off * expd[2 * q:3 * q, gs]
        o_ref[rows, gs] = (y + dskip_ref[:, gs] * xs_ref[rows, gs].astype(F32)).astype(BF16)


def _ssd_kernel(xs_ref, bc_ref, dt_ref, dtb_ref, alog_ref, dskip_ref, tri_ref, exp_ref,
                o_ref, state, xdtb, xdb, expd):
    @pl.when(pl.program_id(1) == 0)
    def _():
        state[...] = jnp.zeros_like(state)

    for ci in range(SSD_CHUNKS_PER_STEP):
        _ssd_chunk(ci * CHUNK, xs_ref, bc_ref, dt_ref, dtb_ref, alog_ref, dskip_ref, tri_ref, exp_ref,
                   o_ref, state, xdtb, xdb, expd)


def _ssd_scan(proj, dt_raw, dt_bias3, a_log3, d_skip_ch, tri3, expand3, batch, seq):
    q = CHUNK
    tm = SSD_CHUNKS_PER_STEP * q
    steps = seq // tm
    t = batch * seq
    rowmap = lambda b, c: b * steps + c
    const = lambda b, c: (0, 0)
    return pl.pallas_call(
        _ssd_kernel,
        grid=(batch, steps),
        in_specs=[pl.BlockSpec((tm, D_INNER), lambda b, c: (rowmap(b, c), COL_XS // D_INNER)),
                  pl.BlockSpec((tm, BC_WIDTH), lambda b, c: (rowmap(b, c), COL_BC // BC_WIDTH)),
                  pl.BlockSpec((tm, LANES), lambda b, c: (rowmap(b, c), 0)),
                  pl.BlockSpec((1, LANES), const),
                  pl.BlockSpec((1, LANES), const),
                  pl.BlockSpec((1, D_INNER), const),
                  pl.BlockSpec((q, DT_REP * q), const),
                  pl.BlockSpec((LANES, D_INNER), const)],
        out_specs=pl.BlockSpec((tm, D_INNER), lambda b, c: (rowmap(b, c), 0)),
        out_shape=jax.ShapeDtypeStruct((t, D_INNER), BF16),
        scratch_shapes=[pltpu.VMEM((D_STATE, D_INNER), F32),
                        pltpu.VMEM((q, D_INNER), BF16),
                        pltpu.VMEM((q, D_INNER), BF16),
                        pltpu.VMEM((DT_REP * q, D_INNER), F32)],
        compiler_params=pltpu.CompilerParams(
            dimension_semantics=("arbitrary", "arbitrary"), vmem_limit_bytes=VMEM_LIMIT),
        name="ssd_scan",
    )(proj, proj, dt_raw, dt_bias3, a_log3, d_skip_ch, tri3, expand3)


def _pooled_groups(u, halo_ref, pos0):
    n_chunks = u.shape[0] // CHUNK
    max_shift = max(POOL_WINDOWS) - 1
    slabs = _slabs_of(u, n_chunks)
    prev_last = {j: halo_ref[SLABS - j] for j in range(1, max_shift + 1)}
    for v in range(SLABS):
        halo_ref[v] = slabs[n_chunks - 1][v]
    wrapped = _wrapped_slabs(slabs, prev_last, max_shift)
    sub = lax.broadcasted_iota(jnp.int32, (SUBLANES, POOL_GW), 0)
    pooled = []
    for gi, win in enumerate(POOL_WINDOWS):
        cs = slice(gi * POOL_GW, (gi + 1) * POOL_GW)
        out = []
        for c in range(n_chunks):
            for v in range(SLABS):
                cur = slabs[c][v][:, cs]
                tot = cur
                for k in range(1, win):
                    tot = tot + _shifted(slabs, wrapped, c, v, k)[:, cs]
                pos = pos0 + c * CHUNK + sub * SLABS + v
                count = jnp.minimum(pos + 1, win).astype(F32)
                out.append(tot / count - cur)
        pooled.append(jnp.concatenate(out, axis=0).astype(BF16))
    return pooled


def _tail_kernel(x_ref, y_ref, zs_ref, u_ref, g_ref, mod_ref, snw_ref, pw_ref, ps_ref, wbs_ref, wbp_ref,
                 wo_ref, nmlp_ref, wup_ref, wdn_ref, nfin_ref, unperm_ref, o_ref, uhalo,
                 *, tm, tiles_per_batch):
    i = pl.program_id(0)

    @pl.when(i % tiles_per_batch == 0)
    def _():
        uhalo[...] = jnp.zeros_like(uhalo)

    y_gated = []
    for g in range(N_GROUPS):
        gs = slice(g * GROUP_CH, (g + 1) * GROUP_CH)
        yg = y_ref[:, gs].astype(F32) * zs_ref[:, gs].astype(F32)
        y_gated.append((_rms(yg) * snw_ref[:, gs]).astype(BF16))
    y_ssd = jnp.dot(jnp.concatenate(y_gated, axis=1), wbs_ref[...], preferred_element_type=F32)

    pooled = _pooled_groups(u_ref[...].astype(F32), uhalo, (i % tiles_per_batch) * tm)
    y_pool_in = jnp.concatenate(
        [jnp.dot(pooled[gi], pw_ref[gi], preferred_element_type=F32) for gi in range(len(POOL_WINDOWS))],
        axis=1) * ps_ref[...]
    y_pool = jnp.dot(y_pool_in.astype(BF16), wbp_ref[...], preferred_element_type=F32)

    merged = (g_ref[:, :D_MODEL].astype(F32) * y_ssd + g_ref[:, D_MODEL:].astype(F32) * y_pool)
    merged = _permute_chunks(unperm_ref[...], merged.astype(BF16))
    mix = jnp.dot(merged, wo_ref[...], preferred_element_type=F32)

    mod = mod_ref[0]
    gate_m = mod[:, 2 * D_MODEL:3 * D_MODEL]
    shift_f = mod[:, 3 * D_MODEL:4 * D_MODEL]
    scale_f = mod[:, 4 * D_MODEL:5 * D_MODEL]
    gate_f = mod[:, 5 * D_MODEL:6 * D_MODEL]
    x1 = x_ref[...] + gate_m * mix
    h = (_rms(x1) * nmlp_ref[...]) * (1.0 + scale_f) + shift_f
    up = jnp.dot(h.astype(BF16), wup_ref[...], preferred_element_type=F32)
    act = jnp.square(jnp.maximum(up, 0.0)).astype(BF16)
    x2 = x1 + gate_f * jnp.dot(act, wdn_ref[...], preferred_element_type=F32)
    o_ref[...] = _rms(x2) * nfin_ref[...]


def _tail(x2d, y_raw, proj, mod3, ssd_norm_w, pool_w, pool_scale, w_bs, w_bp, w_out, norm_mlp_w, w_up,
          w_dn, norm_final_w, unperm, seq):
    t = x2d.shape[0]
    tm = 512
    tiles_per_batch = seq // tm
    const2 = lambda i: (0, 0)
    resident = functools.partial(pl.BlockSpec, pipeline_mode=pl.Buffered(1))
    return pl.pallas_call(
        functools.partial(_tail_kernel, tm=tm, tiles_per_batch=tiles_per_batch),
        grid=(t // tm,),
        in_specs=[pl.BlockSpec((tm, D_MODEL), lambda i: (i, 0)),
                  pl.BlockSpec((tm, D_INNER), lambda i: (i, 0)),
                  pl.BlockSpec((tm, D_INNER), lambda i: (i, COL_Z // D_INNER)),
                  pl.BlockSpec((tm, D_MODEL), lambda i: (i, COL_POOL // D_MODEL)),
                  pl.BlockSpec((tm, 2 * D_MODEL), lambda i: (i, COL_GATE // (2 * D_MODEL))),
                  pl.BlockSpec((1, 1, N_MOD * D_MODEL), lambda i: (i // tiles_per_batch, 0, 0)),
                  resident((1, D_INNER), const2),
                  resident((len(POOL_WINDOWS), POOL_GW, POOL_GW), lambda i: (0, 0, 0)),
                  resident((1, D_MODEL), const2),
                  resident((D_INNER, D_MODEL), const2),
                  resident((D_MODEL, D_MODEL), const2),
                  resident((D_MODEL, D_MODEL), const2),
                  resident((1, D_MODEL), const2),
                  resident((D_MODEL, D_FF), const2),
                  resident((D_FF, D_MODEL), const2),
                  resident((1, D_MODEL), const2),
                  resident((CHUNK, CHUNK), const2)],
        out_specs=pl.BlockSpec((tm, D_MODEL), lambda i: (i, 0)),
        out_shape=jax.ShapeDtypeStruct((t, D_MODEL), F32),
        scratch_shapes=[pltpu.VMEM((SLABS, SUBLANES, D_MODEL), F32)],
        compiler_params=pltpu.CompilerParams(
            dimension_semantics=("arbitrary",), vmem_limit_bytes=VMEM_LIMIT),
        name="gate_pool_merge_mlp",
    )(x2d, y_raw, proj, proj, proj, mod3, ssd_norm_w, pool_w, pool_scale, w_bs, w_bp, w_out, norm_mlp_w,
      w_up, w_dn, norm_final_w, unperm)


def _cumsum_matrix():
    tok = _token_of_row(np.arange(CHUNK))
    tri = (tok[None, :] <= tok[:, None]).astype(np.float32)
    return jnp.asarray(np.concatenate([tri] * DT_REP, axis=1), BF16)


def _interleave_matrix():
    p = np.zeros((CHUNK, CHUNK), np.float32)
    p[np.arange(CHUNK), _token_of_row(np.arange(CHUNK))] = 1.0
    return jnp.asarray(p, BF16), jnp.asarray(p.T, BF16)


def _head_expand_matrix():
    m = np.zeros((LANES, D_INNER), np.float32)
    for k in range(DT_REP * N_HEADS):
        h = k % N_HEADS
        m[k, h * HEAD_DIM:(h + 1) * HEAD_DIM] = 1.0
    return jnp.asarray(m, BF16)


def _rep_heads(v):
    pad = jnp.zeros((LANES - DT_REP * N_HEADS,), v.dtype)
    return jnp.concatenate([v] * DT_REP + [pad])[None, :]


def _on_proj_cols(v):
    return jnp.pad(v, ((0, 0), (COL_XS, PROJ_COLS - COL_POOL)))


def kernel(x, c, w_ada, b_ada, norm_mix_w, w_in, conv_w, conv_b, dt_bias, a_log, d_skip, ssd_norm_w,
           w_branch_ssd, pool_w, pool_scale, w_branch_pool, w_out, norm_mlp_w, w_up, w_down,
           norm_final_w):
    batch, seq, d = x.shape
    depth = w_ada.shape[0]
    assert depth == 1, "the final norm is fused into the last kernel of a single layer"
    t = batch * seq
    tri3 = _cumsum_matrix()
    expand3 = _head_expand_matrix()
    perm, unperm = _interleave_matrix()
    c_pad = jnp.pad(c, ((0, SUBLANES - batch), (0, 0)))
    dt_lo = 2 * D_INNER + BC_WIDTH
    x2d = x.reshape(t, d)
    i = 0
    mod = _modulation(c_pad, w_ada[i], b_ada[i][None, :])[:batch]
    mod3 = mod[:, None, :]
    wi = w_in[i]
    w_main = jnp.concatenate([wi[:, :dt_lo], wi[:, dt_lo + N_HEADS:]], axis=1).astype(BF16)
    w_dt = wi[:, dt_lo:dt_lo + N_HEADS]
    w_dt3 = jnp.concatenate([w_dt] * DT_REP + [jnp.zeros((d, LANES - DT_REP * N_HEADS), F32)],
                            axis=1).astype(BF16)
    proj, dt_raw = _in_projection(x2d, norm_mix_w[i][None, :], mod3, w_main, w_dt3,
                                  _on_proj_cols(conv_w[i]), _on_proj_cols(conv_b[i][None, :]), perm, seq)
    y_raw = _ssd_scan(proj, dt_raw, _rep_heads(dt_bias[i]), _rep_heads(a_log[i]),
                      jnp.repeat(d_skip[i], HEAD_DIM)[None, :], tri3, expand3, batch, seq)
    out = _tail(x2d, y_raw, proj, mod3, ssd_norm_w[i][None, :], pool_w[i].astype(BF16),
                pool_scale[i][None, :], w_branch_ssd[i].astype(BF16), w_branch_pool[i].astype(BF16),
                w_out[i].astype(BF16), norm_mlp_w[i][None, :], w_up[i].astype(BF16),
                w_down[i].astype(BF16), norm_final_w[None, :], unperm, seq)
    return out.reshape(batch, seq, d)
```

```python
import functools

import jax
import jax.numpy as jnp
import numpy as np
from jax import lax
from jax.experimental import pallas as pl
from jax.experimental.pallas import tpu as pltpu

F32 = jnp.float32
BF16 = jnp.bfloat16

D_MODEL = 1024
D_INNER = 2 * D_MODEL
HEAD_DIM = 64
N_HEADS = D_INNER // HEAD_DIM
N_GROUPS = 4
GROUP_CH = D_INNER // N_GROUPS
D_STATE = 128
CONV_K = 4
CHUNK = 128
BC_WIDTH = 2 * N_GROUPS * D_STATE
POOL_WINDOWS = (2, 4, 8, 16)
POOL_GW = D_MODEL // len(POOL_WINDOWS)
D_FF = 4 * D_MODEL
N_MOD = 6
EPS = 1e-5
LOG2_E = 1.4426950408889634
LANES = 128
SUBLANES = 8
SLABS = CHUNK // SUBLANES
DT_REP = 3

PROJ_TN = 2048
COL_Z = 0
COL_XS = D_INNER
COL_BC = 2 * D_INNER
COL_POOL = 2 * D_INNER + BC_WIDTH
COL_GATE = COL_POOL + D_MODEL
PROJ_COLS = COL_GATE + 2 * D_MODEL
TILE_Z, TILE_XS, TILE_BC_POOL, TILE_GATE = 0, 1, 2, 3
CONV_COL_BLOCK = 512
PROJ_ROW_SPLIT = 8

VMEM_LIMIT = 56 * 1024 * 1024
SSD_CHUNKS_PER_STEP = 4


def _rms(x):
    return x * lax.rsqrt(jnp.mean(x * x, axis=-1, keepdims=True) + EPS)


def _token_of_row(r):
    return (r % SUBLANES) * SLABS + r // SUBLANES


def _permute_chunks(perm, xb):
    n_chunks = xb.shape[0] // CHUNK
    return jnp.concatenate(
        [jnp.dot(perm, xb[c * CHUNK:(c + 1) * CHUNK, :], preferred_element_type=F32).astype(BF16)
         for c in range(n_chunks)], axis=0)


def _slabs_of(x, n_chunks):
    return [[x[(c * SLABS + v) * SUBLANES:(c * SLABS + v + 1) * SUBLANES, :] for v in range(SLABS)]
            for c in range(n_chunks)]


def _wrapped_slabs(slabs, prev_last, max_shift):
    n_chunks = len(slabs)
    width = slabs[0][0].shape[1]
    first_row = lax.broadcasted_iota(jnp.int32, (SUBLANES, width), 0) == 0
    wrapped = {}
    for j in range(1, max_shift + 1):
        stacked = jnp.concatenate([slabs[c][SLABS - j] for c in range(n_chunks)], axis=0)
        rolled = pltpu.roll(stacked, 1, 0)
        head = jnp.where(first_row, pltpu.roll(prev_last[j], 1, 0), rolled[0:SUBLANES, :])
        wrapped[j] = [head] + [rolled[c * SUBLANES:(c + 1) * SUBLANES, :] for c in range(1, n_chunks)]
    return wrapped


def _shifted(slabs, wrapped, c, v, k):
    return slabs[c][v - k] if v >= k else wrapped[k - v][c]


def _mod_kernel(c_ref, w_ref, b_ref, o_ref):
    s = jax.nn.silu(c_ref[...])
    o_ref[...] = jnp.dot(s.astype(BF16), w_ref[...].astype(BF16),
                         preferred_element_type=F32) + b_ref[...]


def _modulation(c_pad, w_ada, b_ada):
    n = w_ada.shape[1]
    tn = 1024
    return pl.pallas_call(
        _mod_kernel,
        grid=(n // tn,),
        in_specs=[pl.BlockSpec((SUBLANES, D_MODEL), lambda j: (0, 0)),
                  pl.BlockSpec((D_MODEL, tn), lambda j: (0, j)),
                  pl.BlockSpec((1, tn), lambda j: (0, j))],
        out_specs=pl.BlockSpec((SUBLANES, tn), lambda j: (0, j)),
        out_shape=jax.ShapeDtypeStruct((SUBLANES, n), F32),
        name="adaln_mod",
    )(c_pad, w_ada, b_ada)


def _conv_silu_store(r, cw_ref, cb_ref, halo_ref, o_ref, width):
    n_chunks = r.shape[0] // CHUNK
    pair = 2 * SUBLANES
    for c0 in range(0, width, CONV_COL_BLOCK):
        cols = slice(c0, c0 + CONV_COL_BLOCK)
        slabs = _slabs_of(r[:, cols], n_chunks)
        prev_last = {j: halo_ref[j - 1, :, cols] for j in range(1, CONV_K)}
        for j in range(1, CONV_K):
            halo_ref[j - 1, :, cols] = slabs[n_chunks - 1][SLABS - j]
        wrapped = _wrapped_slabs(slabs, prev_last, CONV_K - 1)
        taps = [cw_ref[k:k + 1, cols] for k in range(CONV_K)]
        bias = cb_ref[:, cols]
        for c in range(n_chunks):
            for v0 in range(0, SLABS, 2):
                accs = []
                for v in (v0, v0 + 1):
                    acc = slabs[c][v] * taps[CONV_K - 1] + bias
                    for k in range(1, CONV_K):
                        acc = acc + _shifted(slabs, wrapped, c, v, k) * taps[CONV_K - 1 - k]
                    accs.append(acc)
                lo = (c * SLABS + v0) * SUBLANES
                o_ref[lo:lo + pair, cols] = jax.nn.silu(jnp.concatenate(accs, axis=0)).astype(BF16)


def _inproj_kernel(x_ref, nw_ref, mod_ref, w_ref, wdt_ref, cw_ref, cb_ref, perm_ref, o_ref, dt_ref,
                   h_scr, halo_scr, *, tiles_per_batch):
    i = pl.program_id(0)
    j = pl.program_id(1)

    @pl.when(j == 0)
    def _():
        mod = mod_ref[0]
        shift = mod[:, 0:D_MODEL]
        scale = mod[:, D_MODEL:2 * D_MODEL]
        h = (_rms(x_ref[...]) * nw_ref[...]) * (1.0 + scale) + shift
        hb = _permute_chunks(perm_ref[...], h.astype(BF16))
        h_scr[...] = hb
        dt_ref[...] = jnp.dot(hb, wdt_ref[...], preferred_element_type=F32)

    @pl.when(jnp.logical_and(i % tiles_per_batch == 0, j == 0))
    def _():
        halo_scr[...] = jnp.zeros_like(halo_scr)

    def proj():
        part = h_scr.shape[0] // PROJ_ROW_SPLIT
        return jnp.concatenate(
            [jnp.dot(h_scr[m0:m0 + part, :], w_ref[...], preferred_element_type=F32)
             for m0 in range(0, h_scr.shape[0], part)], axis=0)

    @pl.when(j == TILE_Z)
    def _():
        o_ref[...] = jax.nn.silu(proj()).astype(BF16)

    @pl.when(j == TILE_XS)
    def _():
        _conv_silu_store(proj(), cw_ref, cb_ref, halo_scr.at[0], o_ref, PROJ_TN)

    @pl.when(j == TILE_BC_POOL)
    def _():
        r = proj()
        _conv_silu_store(r[:, 0:BC_WIDTH], cw_ref, cb_ref, halo_scr.at[1], o_ref, BC_WIDTH)
        o_ref[:, BC_WIDTH:] = r[:, BC_WIDTH:].astype(BF16)

    @pl.when(j == TILE_GATE)
    def _():
        o_ref[...] = jax.nn.sigmoid(proj()).astype(BF16)


def _in_projection(x2d, norm_w, mod3, w_main, w_dt, conv_w_cols, conv_b_cols, perm, seq):
    t = x2d.shape[0]
    tm, tn = 1024, PROJ_TN
    tiles_per_batch = seq // tm
    return pl.pallas_call(
        functools.partial(_inproj_kernel, tiles_per_batch=tiles_per_batch),
        grid=(t // tm, PROJ_COLS // tn),
        in_specs=[pl.BlockSpec((tm, D_MODEL), lambda i, j: (i, 0)),
                  pl.BlockSpec((1, D_MODEL), lambda i, j: (0, 0)),
                  pl.BlockSpec((1, 1, N_MOD * D_MODEL), lambda i, j: (i // tiles_per_batch, 0, 0)),
                  pl.BlockSpec((D_MODEL, tn), lambda i, j: (0, j)),
                  pl.BlockSpec((D_MODEL, LANES), lambda i, j: (0, 0)),
                  pl.BlockSpec((CONV_K, tn), lambda i, j: (0, j)),
                  pl.BlockSpec((1, tn), lambda i, j: (0, j)),
                  pl.BlockSpec((CHUNK, CHUNK), lambda i, j: (0, 0))],
        out_specs=[pl.BlockSpec((tm, tn), lambda i, j: (i, j)),
                   pl.BlockSpec((tm, LANES), lambda i, j: (i, 0))],
        out_shape=[jax.ShapeDtypeStruct((t, PROJ_COLS), BF16),
                   jax.ShapeDtypeStruct((t, LANES), F32)],
        scratch_shapes=[pltpu.VMEM((tm, D_MODEL), BF16),
                        pltpu.VMEM((2, CONV_K - 1, SUBLANES, tn), F32)],
        compiler_params=pltpu.CompilerParams(
            dimension_semantics=("arbitrary", "arbitrary"), vmem_limit_bytes=VMEM_LIMIT),
        name="norm_in_proj",
    )(x2d, norm_w, mod3, w_main, w_dt, conv_w_cols, conv_b_cols, perm)


def _split3(v):
    hi = v.astype(BF16).astype(F32)
    r1 = v - hi
    mid = r1.astype(BF16).astype(F32)
    return hi, r1, r1 - mid


def _lane_pieces(v, lane):
    hi, r1, r2 = _split3(v)
    return jnp.where(lane < N_HEADS, hi, jnp.where(lane < 2 * N_HEADS, r1, r2)).astype(BF16)


def _ssd_prologue(xs_ref, dt_ref, dtb_ref, alog_ref, tri_ref, exp_ref, xdb, expd, acs2_scr, src_scr):
    q = CHUNK
    tm = dt_ref.shape[0]
    n_chunks = tm // q
    dt = jax.nn.softplus(dt_ref[...] + dtb_ref[...])
    dta = dt * (-jnp.exp(alog_ref[...]))
    hi, r1, r2 = _split3(dta)
    acs = []
    for c in range(n_chunks):
        rows = slice(c * q, (c + 1) * q)
        stacked = jnp.concatenate([hi[rows], r1[rows], r2[rows]], axis=0).astype(BF16)
        acs.append(jnp.dot(tri_ref[...], stacked, preferred_element_type=F32))
    acs2 = jnp.concatenate(acs, axis=0) * LOG2_E
    acs2_last = jnp.concatenate(
        [jnp.broadcast_to(acs2[(c + 1) * q - 1:(c + 1) * q, :], (q, LANES)) for c in range(n_chunks)], axis=0)
    decay_to_end = jnp.exp2(acs2_last - acs2)
    decay_from_start = jnp.exp2(acs2)
    acs2_scr[...] = acs2
    src = acs2 - jnp.log2(dt)
    for c in range(n_chunks):
        src_scr[c] = src[c * q:(c + 1) * q, :].T

    lane = lax.broadcasted_iota(jnp.int32, (tm, LANES), 1)
    pieces = jnp.concatenate([_lane_pieces(dt * decay_to_end, lane),
                              _lane_pieces(decay_from_start, lane)], axis=0)
    expd[...] = jnp.dot(pieces, exp_ref[...], preferred_element_type=F32)
    xdb[...] = (xs_ref[...].astype(F32) * expd[0:tm, :]).astype(BF16)


def _ssd_chunk(ci, xs_ref, bc_ref, dskip_ref, o_ref, state, xdb, expd, acs2_scr, src_scr):
    q = CHUNK
    tm = xs_ref.shape[0]
    rows = slice(ci * q, (ci + 1) * q)
    off_rows = slice(tm + ci * q, tm + (ci + 1) * q)
    acs2 = acs2_scr[rows, :]
    src_t = src_scr[ci]

    causal = (_token_of_row(lax.broadcasted_iota(jnp.int32, (q, q), 0))
              >= _token_of_row(lax.broadcasted_iota(jnp.int32, (q, q), 1)))
    half = lax.broadcasted_iota(jnp.int32, (q, LANES), 1) < HEAD_DIM
    zero_b = jnp.zeros((q, LANES), BF16)

    for g in range(N_GROUPS):
        gs = slice(g * GROUP_CH, (g + 1) * GROUP_CH)
        bg = bc_ref[rows, g * D_STATE:(g + 1) * D_STATE]
        cg = bc_ref[rows, (N_GROUPS + g) * D_STATE:(N_GROUPS + g + 1) * D_STATE]
        scores = lax.dot_general(cg, bg, (((1,), (1,)), ((), ())), preferred_element_type=F32)
        st_old = state[:, gs]
        y_off = jnp.dot(cg, st_old.astype(BF16), preferred_element_type=F32)
        new_t = lax.dot_general(bg, xdb[rows, gs], (((0,), (0,)), ((), ())),
                                preferred_element_type=F32)
        state[:, gs] = st_old * expd[off_rows.stop - 1:off_rows.stop, gs] + new_t
        ys = []
        for pr in range(GROUP_CH // LANES):
            h0 = g * (GROUP_CH // HEAD_DIM) + 2 * pr
            ms = []
            for h in (h0, h0 + 1):
                a_col = jnp.broadcast_to(acs2[:, h:h + 1], (q, q))
                a_row = jnp.broadcast_to(src_t[h:h + 1, :], (q, q))
                decay_dt = jnp.exp2(jnp.where(causal, a_col - a_row, -jnp.inf))
                ms.append((scores * decay_dt).astype(BF16))
            xp = xs_ref[rows, g * GROUP_CH + pr * LANES:g * GROUP_CH + (pr + 1) * LANES]
            rhs = jnp.concatenate([jnp.where(half, xp, zero_b), jnp.where(half, zero_b, xp)], axis=0)
            ys.append(jnp.dot(jnp.concatenate(ms, axis=1), rhs, preferred_element_type=F32))
        y = jnp.concatenate(ys, axis=1) + y_off * expd[off_rows, gs]
        o_ref[rows, gs] = (y + dskip_ref[:, gs] * xs_ref[rows, gs].astype(F32)).astype(BF16)


def _ssd_kernel(xs_ref, bc_ref, dt_ref, dtb_ref, alog_ref, dskip_ref, tri_ref, exp_ref,
                o_ref, state, xdb, expd, acs2_scr, src_scr):
    @pl.when(pl.program_id(1) == 0)
    def _():
        state[...] = jnp.zeros_like(state)

    _ssd_prologue(xs_ref, dt_ref, dtb_ref, alog_ref, tri_ref, exp_ref, xdb, expd, acs2_scr, src_scr)
    for ci in range(SSD_CHUNKS_PER_STEP):
        _ssd_chunk(ci, xs_ref, bc_ref, dskip_ref, o_ref, state, xdb, expd, acs2_scr, src_scr)


def _ssd_scan(proj, dt_raw, dt_bias3, a_log3, d_skip_ch, tri3, expand3, batch, seq):
    q = CHUNK
    tm = SSD_CHUNKS_PER_STEP * q
    steps = seq // tm
    t = batch * seq
    rowmap = lambda b, c: b * steps + c
    const = lambda b, c: (0, 0)
    return pl.pallas_call(
        _ssd_kernel,
        grid=(batch, steps),
        in_specs=[pl.BlockSpec((tm, D_INNER), lambda b, c: (rowmap(b, c), COL_XS // D_INNER)),
                  pl.BlockSpec((tm, BC_WIDTH), lambda b, c: (rowmap(b, c), COL_BC // BC_WIDTH)),
                  pl.BlockSpec((tm, LANES), lambda b, c: (rowmap(b, c), 0)),
                  pl.BlockSpec((1, LANES), const),
                  pl.BlockSpec((1, LANES), const),
                  pl.BlockSpec((1, D_INNER), const),
                  pl.BlockSpec((q, DT_REP * q), const),
                  pl.BlockSpec((LANES, D_INNER), const)],
        out_specs=pl.BlockSpec((tm, D_INNER), lambda b, c: (rowmap(b, c), 0)),
        out_shape=jax.ShapeDtypeStruct((t, D_INNER), BF16),
        scratch_shapes=[pltpu.VMEM((D_STATE, D_INNER), F32),
                        pltpu.VMEM((tm, D_INNER), BF16),
                        pltpu.VMEM((2 * tm, D_INNER), F32),
                        pltpu.VMEM((tm, LANES), F32),
                        pltpu.VMEM((SSD_CHUNKS_PER_STEP, LANES, q), F32)],
        compiler_params=pltpu.CompilerParams(
            dimension_semantics=("arbitrary", "arbitrary"), vmem_limit_bytes=VMEM_LIMIT),
        name="ssd_scan",
    )(proj, proj, dt_raw, dt_bias3, a_log3, d_skip_ch, tri3, expand3)


def _pooled_groups(u, halo_ref, pos0):
    n_chunks = u.shape[0] // CHUNK
    max_shift = max(POOL_WINDOWS) - 1
    slabs = _slabs_of(u, n_chunks)
    prev_last = {j: halo_ref[SLABS - j] for j in range(1, max_shift + 1)}
    for v in range(SLABS):
        halo_ref[v] = slabs[n_chunks - 1][v]
    wrapped = _wrapped_slabs(slabs, prev_last, max_shift)
    sub = lax.broadcasted_iota(jnp.int32, (SUBLANES, POOL_GW), 0)
    pooled = []
    for gi, win in enumerate(POOL_WINDOWS):
        cs = slice(gi * POOL_GW, (gi + 1) * POOL_GW)
        out = []
        for c in range(n_chunks):
            for v in range(SLABS):
                cur = slabs[c][v][:, cs]
                tot = cur
                for k in range(1, win):
                    tot = tot + _shifted(slabs, wrapped, c, v, k)[:, cs]
                pos = pos0 + c * CHUNK + sub * SLABS + v
                count = jnp.minimum(pos + 1, win).astype(F32)
                out.append(tot / count - cur)
        pooled.append(jnp.concatenate(out, axis=0).astype(BF16))
    return pooled


def _tail_kernel(x_ref, y_ref, zs_ref, u_ref, g_ref, mod_ref, snw_ref, pw_ref, ps_ref, wbs_ref, wbp_ref,
                 wo_ref, nmlp_ref, wup_ref, wdn_ref, nfin_ref, unperm_ref, o_ref, uhalo,
                 *, tm, tiles_per_batch):
    i = pl.program_id(0)

    @pl.when(i % tiles_per_batch == 0)
    def _():
        uhalo[...] = jnp.zeros_like(uhalo)

    y_gated = []
    for g in range(N_GROUPS):
        gs = slice(g * GROUP_CH, (g + 1) * GROUP_CH)
        yg = y_ref[:, gs].astype(F32) * zs_ref[:, gs].astype(F32)
        y_gated.append((_rms(yg) * snw_ref[:, gs]).astype(BF16))
    y_ssd = jnp.dot(jnp.concatenate(y_gated, axis=1), wbs_ref[...], preferred_element_type=F32)

    pooled = _pooled_groups(u_ref[...].astype(F32), uhalo, (i % tiles_per_batch) * tm)
    y_pool_in = jnp.concatenate(
        [jnp.dot(pooled[gi], pw_ref[gi], preferred_element_type=F32) for gi in range(len(POOL_WINDOWS))],
        axis=1) * ps_ref[...]
    y_pool = jnp.dot(y_pool_in.astype(BF16), wbp_ref[...], preferred_element_type=F32)

    merged = (g_ref[:, :D_MODEL].astype(F32) * y_ssd + g_ref[:, D_MODEL:].astype(F32) * y_pool)
    merged = _permute_chunks(unperm_ref[...], merged.astype(BF16))
    mix = jnp.dot(merged, wo_ref[...], preferred_element_type=F32)

    mod = mod_ref[0]
    gate_m = mod[:, 2 * D_MODEL:3 * D_MODEL]
    shift_f = mod[:, 3 * D_MODEL:4 * D_MODEL]
    scale_f = mod[:, 4 * D_MODEL:5 * D_MODEL]
    gate_f = mod[:, 5 * D_MODEL:6 * D_MODEL]
    x1 = x_ref[...] + gate_m * mix
    h = (_rms(x1) * nmlp_ref[...]) * (1.0 + scale_f) + shift_f
    up = jnp.dot(h.astype(BF16), wup_ref[...], preferred_element_type=F32)
    act = jnp.square(jnp.maximum(up, 0.0)).astype(BF16)
    x2 = x1 + gate_f * jnp.dot(act, wdn_ref[...], preferred_element_type=F32)
    o_ref[...] = _rms(x2) * nfin_ref[...]


def _tail(x2d, y_raw, proj, mod3, ssd_norm_w, pool_w, pool_scale, w_bs, w_bp, w_out, norm_mlp_w, w_up,
          w_dn, norm_final_w, unperm, seq):
    t = x2d.shape[0]
    tm = 512
    tiles_per_batch = seq // tm
    const2 = lambda i: (0, 0)
    resident = functools.partial(pl.BlockSpec, pipeline_mode=pl.Buffered(1))
    return pl.pallas_call(
        functools.partial(_tail_kernel, tm=tm, tiles_per_batch=tiles_per_batch),
        grid=(t // tm,),
        in_specs=[pl.BlockSpec((tm, D_MODEL), lambda i: (i, 0)),
                  pl.BlockSpec((tm, D_INNER), lambda i: (i, 0)),
                  pl.BlockSpec((tm, D_INNER), lambda i: (i, COL_Z // D_INNER)),
                  pl.BlockSpec((tm, D_MODEL), lambda i: (i, COL_POOL // D_MODEL)),
                  pl.BlockSpec((tm, 2 * D_MODEL), lambda i: (i, COL_GATE // (2 * D_MODEL))),
                  pl.BlockSpec((1, 1, N_MOD * D_MODEL), lambda i: (i // tiles_per_batch, 0, 0)),
                  resident((1, D_INNER), const2),
                  resident((len(POOL_WINDOWS), POOL_GW, POOL_GW), lambda i: (0, 0, 0)),
                  resident((1, D_MODEL), const2),
                  resident((D_INNER, D_MODEL), const2),
                  resident((D_MODEL, D_MODEL), const2),
                  resident((D_MODEL, D_MODEL), const2),
                  resident((1, D_MODEL), const2),
                  resident((D_MODEL, D_FF), const2),
                  resident((D_FF, D_MODEL), const2),
                  resident((1, D_MODEL), const2),
                  resident((CHUNK, CHUNK), const2)],
        out_specs=pl.BlockSpec((tm, D_MODEL), lambda i: (i, 0)),
        out_shape=jax.ShapeDtypeStruct((t, D_MODEL), F32),
        scratch_shapes=[pltpu.VMEM((SLABS, SUBLANES, D_MODEL), F32)],
        compiler_params=pltpu.CompilerParams(
            dimension_semantics=("arbitrary",), vmem_limit_bytes=VMEM_LIMIT),
        name="gate_pool_merge_mlp",
    )(x2d, y_raw, proj, proj, proj, mod3, ssd_norm_w, pool_w, pool_scale, w_bs, w_bp, w_out, norm_mlp_w,
      w_up, w_dn, norm_final_w, unperm)


def _cumsum_matrix():
    tok = _token_of_row(np.arange(CHUNK))
    tri = (tok[None, :] <= tok[:, None]).astype(np.float32)
    return jnp.asarray(np.concatenate([tri] * DT_REP, axis=1), BF16)


def _interleave_matrix():
    p = np.zeros((CHUNK, CHUNK), np.float32)
    p[np.arange(CHUNK), _token_of_row(np.arange(CHUNK))] = 1.0
    return jnp.asarray(p, BF16), jnp.asarray(p.T, BF16)


def _head_expand_matrix():
    m = np.zeros((LANES, D_INNER), np.float32)
    for k in range(DT_REP * N_HEADS):
        h = k % N_HEADS
        m[k, h * HEAD_DIM:(h + 1) * HEAD_DIM] = 1.0
    return jnp.asarray(m, BF16)


def _rep_heads(v):
    pad = jnp.zeros((LANES - DT_REP * N_HEADS,), v.dtype)
    return jnp.concatenate([v] * DT_REP + [pad])[None, :]


def _on_proj_cols(v):
    return jnp.pad(v, ((0, 0), (COL_XS, PROJ_COLS - COL_POOL)))


def kernel(x, c, w_ada, b_ada, norm_mix_w, w_in, conv_w, conv_b, dt_bias, a_log, d_skip, ssd_norm_w,
           w_branch_ssd, pool_w, pool_scale, w_branch_pool, w_out, norm_mlp_w, w_up, w_down,
           norm_final_w):
    batch, seq, d = x.shape
    depth = w_ada.shape[0]
    assert depth == 1, "the final norm is fused into the last kernel of a single layer"
    t = batch * seq
    tri3 = _cumsum_matrix()
    expand3 = _head_expand_matrix()
    perm, unperm = _interleave_matrix()
    c_pad = jnp.pad(c, ((0, SUBLANES - batch), (0, 0)))
    dt_lo = 2 * D_INNER + BC_WIDTH
    x2d = x.reshape(t, d)
    i = 0
    mod = _modulation(c_pad, w_ada[i], b_ada[i][None, :])[:batch]
    mod3 = mod[:, None, :]
    wi = w_in[i]
    w_main = jnp.concatenate([wi[:, :dt_lo], wi[:, dt_lo + N_HEADS:]], axis=1).astype(BF16)
    w_dt = wi[:, dt_lo:dt_lo + N_HEADS]
    w_dt3 = jnp.concatenate([w_dt] * DT_REP + [jnp.zeros((d, LANES - DT_REP * N_HEADS), F32)],
                            axis=1).astype(BF16)
    proj, dt_raw = _in_projection(x2d, norm_mix_w[i][None, :], mod3, w_main, w_dt3,
                                  _on_proj_cols(conv_w[i]), _on_proj_cols(conv_b[i][None, :]), perm, seq)
    y_raw = _ssd_scan(proj, dt_raw, _rep_heads(dt_bias[i]), _rep_heads(a_log[i]),
                      jnp.repeat(d_skip[i], HEAD_DIM)[None, :], tri3, expand3, batch, seq)
    out = _tail(x2d, y_raw, proj, mod3, ssd_norm_w[i][None, :], pool_w[i].astype(BF16),
                pool_scale[i][None, :], w_branch_ssd[i].astype(BF16), w_branch_pool[i].astype(BF16),
                w_out[i].astype(BF16), norm_mlp_w[i][None, :], w_up[i].astype(BF16),
                w_down[i].astype(BF16), norm_final_w[None, :], unperm, seq)
    return out.reshape(batch, seq, d)
```

```python
import functools

import jax
import jax.numpy as jnp
import numpy as np
from jax import lax
from jax.experimental import pallas as pl
from jax.experimental.pallas import tpu as pltpu

F32 = jnp.float32
BF16 = jnp.bfloat16

D_MODEL = 1024
D_INNER = 2 * D_MODEL
HEAD_DIM = 64
N_HEADS = D_INNER // HEAD_DIM
N_GROUPS = 4
GROUP_CH = D_INNER // N_GROUPS
D_STATE = 128
CONV_K = 4
CHUNK = 128
BC_WIDTH = 2 * N_GROUPS * D_STATE
POOL_WINDOWS = (2, 4, 8, 16)
POOL_GW = D_MODEL // len(POOL_WINDOWS)
D_FF = 4 * D_MODEL
N_MOD = 6
EPS = 1e-5
LOG2_E = 1.4426950408889634
LANES = 128
SUBLANES = 8
SLABS = CHUNK // SUBLANES
DT_REP = 3

PROJ_TN = 2048
COL_Z = 0
COL_XS = D_INNER
COL_BC = 2 * D_INNER
COL_POOL = 2 * D_INNER + BC_WIDTH
COL_GATE = COL_POOL + D_MODEL
PROJ_COLS = COL_GATE + 2 * D_MODEL
TILE_Z, TILE_XS, TILE_BC_POOL, TILE_GATE = 0, 1, 2, 3
CONV_COL_BLOCK = 512
PROJ_ROW_SPLIT = 8

VMEM_LIMIT = 56 * 1024 * 1024
SSD_CHUNKS_PER_STEP = 4


def _rms(x):
    return x * lax.rsqrt(jnp.mean(x * x, axis=-1, keepdims=True) + EPS)


def _token_of_row(r):
    return (r % SUBLANES) * SLABS + r // SUBLANES


def _permute_chunks(perm, xb):
    n_chunks = xb.shape[0] // CHUNK
    return jnp.concatenate(
        [jnp.dot(perm, xb[c * CHUNK:(c + 1) * CHUNK, :], preferred_element_type=F32).astype(BF16)
         for c in range(n_chunks)], axis=0)


def _slabs_of(x, n_chunks):
    return [[x[(c * SLABS + v) * SUBLANES:(c * SLABS + v + 1) * SUBLANES, :] for v in range(SLABS)]
            for c in range(n_chunks)]


def _wrapped_slabs(slabs, prev_last, max_shift):
    n_chunks = len(slabs)
    width = slabs[0][0].shape[1]
    first_row = lax.broadcasted_iota(jnp.int32, (SUBLANES, width), 0) == 0
    wrapped = {}
    for j in range(1, max_shift + 1):
        stacked = jnp.concatenate([slabs[c][SLABS - j] for c in range(n_chunks)], axis=0)
        rolled = pltpu.roll(stacked, 1, 0)
        head = jnp.where(first_row, pltpu.roll(prev_last[j], 1, 0), rolled[0:SUBLANES, :])
        wrapped[j] = [head] + [rolled[c * SUBLANES:(c + 1) * SUBLANES, :] for c in range(1, n_chunks)]
    return wrapped


def _shifted(slabs, wrapped, c, v, k):
    return slabs[c][v - k] if v >= k else wrapped[k - v][c]


def _mod_kernel(c_ref, w_ref, b_ref, o_ref):
    s = jax.nn.silu(c_ref[...])
    o_ref[...] = jnp.dot(s.astype(BF16), w_ref[...].astype(BF16),
                         preferred_element_type=F32) + b_ref[...]


def _modulation(c_pad, w_ada, b_ada):
    n = w_ada.shape[1]
    tn = 1024
    return pl.pallas_call(
        _mod_kernel,
        grid=(n // tn,),
        in_specs=[pl.BlockSpec((SUBLANES, D_MODEL), lambda j: (0, 0)),
                  pl.BlockSpec((D_MODEL, tn), lambda j: (0, j)),
                  pl.BlockSpec((1, tn), lambda j: (0, j))],
        out_specs=pl.BlockSpec((SUBLANES, tn), lambda j: (0, j)),
        out_shape=jax.ShapeDtypeStruct((SUBLANES, n), F32),
        name="adaln_mod",
    )(c_pad, w_ada, b_ada)


def _conv_silu_store(r, cw_ref, cb_ref, halo_ref, o_ref, width):
    n_chunks = r.shape[0] // CHUNK
    pair = 2 * SUBLANES
    for c0 in range(0, width, CONV_COL_BLOCK):
        cols = slice(c0, c0 + CONV_COL_BLOCK)
        slabs = _slabs_of(r[:, cols], n_chunks)
        prev_last = {j: halo_ref[j - 1, :, cols] for j in range(1, CONV_K)}
        for j in range(1, CONV_K):
            halo_ref[j - 1, :, cols] = slabs[n_chunks - 1][SLABS - j]
        wrapped = _wrapped_slabs(slabs, prev_last, CONV_K - 1)
        taps = [cw_ref[k:k + 1, cols] for k in range(CONV_K)]
        bias = cb_ref[:, cols]
        for c in range(n_chunks):
            for v0 in range(0, SLABS, 2):
                accs = []
                for v in (v0, v0 + 1):
                    acc = slabs[c][v] * taps[CONV_K - 1] + bias
                    for k in range(1, CONV_K):
                        acc = acc + _shifted(slabs, wrapped, c, v, k) * taps[CONV_K - 1 - k]
                    accs.append(acc)
                lo = (c * SLABS + v0) * SUBLANES
                o_ref[lo:lo + pair, cols] = jax.nn.silu(jnp.concatenate(accs, axis=0)).astype(BF16)


def _inproj_kernel(x_ref, nw_ref, mod_ref, w_zx_ref, w_bc_ref, w_pool_ref, w_gate_ref, wdt_ref, cw_ref,
                   cb_ref, perm_ref, o_ref, dt_ref, h_scr, halo_scr, *, tiles_per_batch):
    i = pl.program_id(0)
    j = pl.program_id(1)

    @pl.when(j == 0)
    def _():
        mod = mod_ref[0]
        shift = mod[:, 0:D_MODEL]
        scale = mod[:, D_MODEL:2 * D_MODEL]
        h = (_rms(x_ref[...]) * nw_ref[...]) * (1.0 + scale) + shift
        hb = _permute_chunks(perm_ref[...], h.astype(BF16))
        h_scr[...] = hb
        dt_ref[...] = jnp.dot(hb, wdt_ref[...], preferred_element_type=F32)

    @pl.when(jnp.logical_and(i % tiles_per_batch == 0, j == 0))
    def _():
        halo_scr[...] = jnp.zeros_like(halo_scr)

    def proj(w_ref):
        part = h_scr.shape[0] // PROJ_ROW_SPLIT
        return jnp.concatenate(
            [jnp.dot(h_scr[m0:m0 + part, :], w_ref[...], preferred_element_type=F32)
             for m0 in range(0, h_scr.shape[0], part)], axis=0)

    @pl.when(j == TILE_Z)
    def _():
        o_ref[...] = jax.nn.silu(proj(w_zx_ref)).astype(BF16)

    @pl.when(j == TILE_XS)
    def _():
        _conv_silu_store(proj(w_zx_ref), cw_ref, cb_ref, halo_scr.at[0], o_ref, PROJ_TN)

    @pl.when(j == TILE_BC_POOL)
    def _():
        _conv_silu_store(proj(w_bc_ref), cw_ref, cb_ref, halo_scr.at[1], o_ref, BC_WIDTH)
        o_ref[:, BC_WIDTH:] = proj(w_pool_ref).astype(BF16)

    @pl.when(j == TILE_GATE)
    def _():
        o_ref[...] = jax.nn.sigmoid(proj(w_gate_ref)).astype(BF16)


def _in_projection(x2d, norm_w, mod3, w_zxbc, w_pool, w_gate, w_dt, conv_w_cols, conv_b_cols, perm, seq):
    t = x2d.shape[0]
    tm, tn = 1024, PROJ_TN
    tiles_per_batch = seq // tm
    return pl.pallas_call(
        functools.partial(_inproj_kernel, tiles_per_batch=tiles_per_batch),
        grid=(t // tm, PROJ_COLS // tn),
        in_specs=[pl.BlockSpec((tm, D_MODEL), lambda i, j: (i, 0)),
                  pl.BlockSpec((1, D_MODEL), lambda i, j: (0, 0)),
                  pl.BlockSpec((1, 1, N_MOD * D_MODEL), lambda i, j: (i // tiles_per_batch, 0, 0)),
                  pl.BlockSpec((D_MODEL, tn), lambda i, j: (0, jnp.minimum(j, TILE_XS))),
                  pl.BlockSpec((D_MODEL, BC_WIDTH), lambda i, j: (0, COL_BC // BC_WIDTH)),
                  pl.BlockSpec((D_MODEL, D_MODEL), lambda i, j: (0, 0)),
                  pl.BlockSpec((D_MODEL, 2 * D_MODEL), lambda i, j: (0, 0)),
                  pl.BlockSpec((D_MODEL, LANES), lambda i, j: (0, 0)),
                  pl.BlockSpec((CONV_K, tn), lambda i, j: (0, j)),
                  pl.BlockSpec((1, tn), lambda i, j: (0, j)),
                  pl.BlockSpec((CHUNK, CHUNK), lambda i, j: (0, 0))],
        out_specs=[pl.BlockSpec((tm, tn), lambda i, j: (i, j)),
                   pl.BlockSpec((tm, LANES), lambda i, j: (i, 0))],
        out_shape=[jax.ShapeDtypeStruct((t, PROJ_COLS), BF16),
                   jax.ShapeDtypeStruct((t, LANES), F32)],
        scratch_shapes=[pltpu.VMEM((tm, D_MODEL), BF16),
                        pltpu.VMEM((2, CONV_K - 1, SUBLANES, tn), F32)],
        compiler_params=pltpu.CompilerParams(
            dimension_semantics=("arbitrary", "arbitrary"), vmem_limit_bytes=VMEM_LIMIT),
        name="norm_in_proj",
    )(x2d, norm_w, mod3, w_zxbc, w_zxbc, w_pool, w_gate, w_dt, conv_w_cols, conv_b_cols, perm)


def _split3(v):
    hi = v.astype(BF16).astype(F32)
    r1 = v - hi
    mid = r1.astype(BF16).astype(F32)
    return hi, r1, r1 - mid


def _lane_pieces(v, lane):
    hi, r1, r2 = _split3(v)
    return jnp.where(lane < N_HEADS, hi, jnp.where(lane < 2 * N_HEADS, r1, r2)).astype(BF16)


def _ssd_prologue(xs_ref, dt_ref, dtb_ref, alog_ref, tri_ref, exp_ref, xdb, expd, acs2_scr, src_scr):
    q = CHUNK
    tm = dt_ref.shape[0]
    n_chunks = tm // q
    dt = jax.nn.softplus(dt_ref[...] + dtb_ref[...])
    dta = dt * (-jnp.exp(alog_ref[...]))
    hi, r1, r2 = _split3(dta)
    acs = []
    for c in range(n_chunks):
        rows = slice(c * q, (c + 1) * q)
        stacked = jnp.concatenate([hi[rows], r1[rows], r2[rows]], axis=0).astype(BF16)
        acs.append(jnp.dot(tri_ref[...], stacked, preferred_element_type=F32))
    acs2 = jnp.concatenate(acs, axis=0) * LOG2_E
    acs2_last = jnp.concatenate(
        [jnp.broadcast_to(acs2[(c + 1) * q - 1:(c + 1) * q, :], (q, LANES)) for c in range(n_chunks)], axis=0)
    decay_to_end = jnp.exp2(acs2_last - acs2)
    decay_from_start = jnp.exp2(acs2)
    acs2_scr[...] = acs2
    src = acs2 - jnp.log2(dt)
    for c in range(n_chunks):
        src_scr[c] = src[c * q:(c + 1) * q, :].T

    lane = lax.broadcasted_iota(jnp.int32, (tm, LANES), 1)
    pieces = jnp.concatenate([_lane_pieces(dt * decay_to_end, lane),
                              _lane_pieces(decay_from_start, lane)], axis=0)
    expd[...] = jnp.dot(pieces, exp_ref[...], preferred_element_type=F32)
    xdb[...] = (xs_ref[...].astype(F32) * expd[0:tm, :]).astype(BF16)


def _ssd_chunk(ci, xs_ref, bc_ref, dskip_ref, o_ref, state, xdb, expd, acs2_scr, src_scr):
    q = CHUNK
    tm = xs_ref.shape[0]
    rows = slice(ci * q, (ci + 1) * q)
    off_rows = slice(tm + ci * q, tm + (ci + 1) * q)
    acs2 = acs2_scr[rows, :]
    src_t = src_scr[ci]

    causal = (_token_of_row(lax.broadcasted_iota(jnp.int32, (q, q), 0))
              >= _token_of_row(lax.broadcasted_iota(jnp.int32, (q, q), 1)))
    half = lax.broadcasted_iota(jnp.int32, (q, LANES), 1) < HEAD_DIM
    zero_b = jnp.zeros((q, LANES), BF16)

    for g in range(N_GROUPS):
        gs = slice(g * GROUP_CH, (g + 1) * GROUP_CH)
        bg = bc_ref[rows, g * D_STATE:(g + 1) * D_STATE]
        cg = bc_ref[rows, (N_GROUPS + g) * D_STATE:(N_GROUPS + g + 1) * D_STATE]
        scores = lax.dot_general(cg, bg, (((1,), (1,)), ((), ())), preferred_element_type=F32)
        st_old = state[:, gs]
        y_off = jnp.dot(cg, st_old.astype(BF16), preferred_element_type=F32)
        new_t = lax.dot_general(bg, xdb[rows, gs], (((0,), (0,)), ((), ())),
                                preferred_element_type=F32)
        state[:, gs] = st_old * expd[off_rows.stop - 1:off_rows.stop, gs] + new_t
        ys = []
        for pr in range(GROUP_CH // LANES):
            h0 = g * (GROUP_CH // HEAD_DIM) + 2 * pr
            ms = []
            for h in (h0, h0 + 1):
                a_col = jnp.broadcast_to(acs2[:, h:h + 1], (q, q))
                a_row = jnp.broadcast_to(src_t[h:h + 1, :], (q, q))
                decay_dt = jnp.exp2(jnp.where(causal, a_col - a_row, -jnp.inf))
                ms.append((scores * decay_dt).astype(BF16))
            xp = xs_ref[rows, g * GROUP_CH + pr * LANES:g * GROUP_CH + (pr + 1) * LANES]
            rhs = jnp.concatenate([jnp.where(half, xp, zero_b), jnp.where(half, zero_b, xp)], axis=0)
            ys.append(jnp.dot(jnp.concatenate(ms, axis=1), rhs, preferred_element_type=F32))
        y = jnp.concatenate(ys, axis=1) + y_off * expd[off_rows, gs]
        o_ref[rows, gs] = (y + dskip_ref[:, gs] * xs_ref[rows, gs].astype(F32)).astype(BF16)


def _ssd_kernel(xs_ref, bc_ref, dt_ref, dtb_ref, alog_ref, dskip_ref, tri_ref, exp_ref,
                o_ref, state, xdb, expd, acs2_scr, src_scr):
    @pl.when(pl.program_id(1) == 0)
    def _():
        state[...] = jnp.zeros_like(state)

    _ssd_prologue(xs_ref, dt_ref, dtb_ref, alog_ref, tri_ref, exp_ref, xdb, expd, acs2_scr, src_scr)
    for ci in range(SSD_CHUNKS_PER_STEP):
        _ssd_chunk(ci, xs_ref, bc_ref, dskip_ref, o_ref, state, xdb, expd, acs2_scr, src_scr)


def _ssd_scan(proj, dt_raw, dt_bias3, a_log3, d_skip_ch, tri3, expand3, batch, seq):
    q = CHUNK
    tm = SSD_CHUNKS_PER_STEP * q
    steps = seq // tm
    t = batch * seq
    rowmap = lambda b, c: b * steps + c
    const = lambda b, c: (0, 0)
    return pl.pallas_call(
        _ssd_kernel,
        grid=(batch, steps),
        in_specs=[pl.BlockSpec((tm, D_INNER), lambda b, c: (rowmap(b, c), COL_XS // D_INNER)),
                  pl.BlockSpec((tm, BC_WIDTH), lambda b, c: (rowmap(b, c), COL_BC // BC_WIDTH)),
                  pl.BlockSpec((tm, LANES), lambda b, c: (rowmap(b, c), 0)),
                  pl.BlockSpec((1, LANES), const),
                  pl.BlockSpec((1, LANES), const),
                  pl.BlockSpec((1, D_INNER), const),
                  pl.BlockSpec((q, DT_REP * q), const),
                  pl.BlockSpec((LANES, D_INNER), const)],
        out_specs=pl.BlockSpec((tm, D_INNER), lambda b, c: (rowmap(b, c), 0)),
        out_shape=jax.ShapeDtypeStruct((t, D_INNER), BF16),
        scratch_shapes=[pltpu.VMEM((D_STATE, D_INNER), F32),
                        pltpu.VMEM((tm, D_INNER), BF16),
                        pltpu.VMEM((2 * tm, D_INNER), F32),
                        pltpu.VMEM((tm, LANES), F32),
                        pltpu.VMEM((SSD_CHUNKS_PER_STEP, LANES, q), F32)],
        compiler_params=pltpu.CompilerParams(
            dimension_semantics=("arbitrary", "arbitrary"), vmem_limit_bytes=VMEM_LIMIT),
        name="ssd_scan",
    )(proj, proj, dt_raw, dt_bias3, a_log3, d_skip_ch, tri3, expand3)


def _pooled_groups(u, halo_ref, pos0):
    n_chunks = u.shape[0] // CHUNK
    max_shift = max(POOL_WINDOWS) - 1
    slabs = _slabs_of(u, n_chunks)
    prev_last = {j: halo_ref[SLABS - j] for j in range(1, max_shift + 1)}
    for v in range(SLABS):
        halo_ref[v] = slabs[n_chunks - 1][v]
    wrapped = _wrapped_slabs(slabs, prev_last, max_shift)
    sub = lax.broadcasted_iota(jnp.int32, (SUBLANES, POOL_GW), 0)
    pooled = []
    for gi, win in enumerate(POOL_WINDOWS):
        cs = slice(gi * POOL_GW, (gi + 1) * POOL_GW)
        out = []
        for c in range(n_chunks):
            for v in range(SLABS):
                cur = slabs[c][v][:, cs]
                tot = cur
                for k in range(1, win):
                    tot = tot + _shifted(slabs, wrapped, c, v, k)[:, cs]
                pos = pos0 + c * CHUNK + sub * SLABS + v
                count = jnp.minimum(pos + 1, win).astype(F32)
                out.append(tot / count - cur)
        pooled.append(jnp.concatenate(out, axis=0).astype(BF16))
    return pooled


def _tail_kernel(x_ref, y_ref, zs_ref, u_ref, g_ref, mod_ref, snw_ref, pw_ref, ps_ref, wbs_ref, wbp_ref,
                 wo_ref, nmlp_ref, wup_ref, wdn_ref, nfin_ref, unperm_ref, o_ref, uhalo,
                 *, tm, tiles_per_batch):
    i = pl.program_id(0)

    @pl.when(i % tiles_per_batch == 0)
    def _():
        uhalo[...] = jnp.zeros_like(uhalo)

    y_gated = []
    for g in range(N_GROUPS):
        gs = slice(g * GROUP_CH, (g + 1) * GROUP_CH)
        yg = y_ref[:, gs].astype(F32) * zs_ref[:, gs].astype(F32)
        y_gated.append((_rms(yg) * snw_ref[:, gs]).astype(BF16))
    y_ssd = jnp.dot(jnp.concatenate(y_gated, axis=1), wbs_ref[...], preferred_element_type=F32)

    pooled = _pooled_groups(u_ref[...].astype(F32), uhalo, (i % tiles_per_batch) * tm)
    y_pool_in = jnp.concatenate(
        [jnp.dot(pooled[gi], pw_ref[gi], preferred_element_type=F32) for gi in range(len(POOL_WINDOWS))],
        axis=1) * ps_ref[...]
    y_pool = jnp.dot(y_pool_in.astype(BF16), wbp_ref[...], preferred_element_type=F32)

    merged = (g_ref[:, :D_MODEL].astype(F32) * y_ssd + g_ref[:, D_MODEL:].astype(F32) * y_pool)
    merged = _permute_chunks(unperm_ref[...], merged.astype(BF16))
    mix = jnp.dot(merged, wo_ref[...], preferred_element_type=F32)

    mod = mod_ref[0]
    gate_m = mod[:, 2 * D_MODEL:3 * D_MODEL]
    shift_f = mod[:, 3 * D_MODEL:4 * D_MODEL]
    scale_f = mod[:, 4 * D_MODEL:5 * D_MODEL]
    gate_f = mod[:, 5 * D_MODEL:6 * D_MODEL]
    x1 = x_ref[...] + gate_m * mix
    h = (_rms(x1) * nmlp_ref[...]) * (1.0 + scale_f) + shift_f
    up = jnp.dot(h.astype(BF16), wup_ref[...], preferred_element_type=F32)
    act = jnp.square(jnp.maximum(up, 0.0)).astype(BF16)
    x2 = x1 + gate_f * jnp.dot(act, wdn_ref[...], preferred_element_type=F32)
    o_ref[...] = _rms(x2) * nfin_ref[...]


def _tail(x2d, y_raw, proj, mod3, ssd_norm_w, pool_w, pool_scale, w_bs, w_bp, w_out, norm_mlp_w, w_up,
          w_dn, norm_final_w, unperm, seq):
    t = x2d.shape[0]
    tm = 512
    tiles_per_batch = seq // tm
    const2 = lambda i: (0, 0)
    resident = functools.partial(pl.BlockSpec, pipeline_mode=pl.Buffered(1))
    return pl.pallas_call(
        functools.partial(_tail_kernel, tm=tm, tiles_per_batch=tiles_per_batch),
        grid=(t // tm,),
        in_specs=[pl.BlockSpec((tm, D_MODEL), lambda i: (i, 0)),
                  pl.BlockSpec((tm, D_INNER), lambda i: (i, 0)),
                  pl.BlockSpec((tm, D_INNER), lambda i: (i, COL_Z // D_INNER)),
                  pl.BlockSpec((tm, D_MODEL), lambda i: (i, COL_POOL // D_MODEL)),
                  pl.BlockSpec((tm, 2 * D_MODEL), lambda i: (i, COL_GATE // (2 * D_MODEL))),
                  pl.BlockSpec((1, 1, N_MOD * D_MODEL), lambda i: (i // tiles_per_batch, 0, 0)),
                  resident((1, D_INNER), const2),
                  resident((len(POOL_WINDOWS), POOL_GW, POOL_GW), lambda i: (0, 0, 0)),
                  resident((1, D_MODEL), const2),
                  resident((D_INNER, D_MODEL), const2),
                  resident((D_MODEL, D_MODEL), const2),
                  resident((D_MODEL, D_MODEL), const2),
                  resident((1, D_MODEL), const2),
                  resident((D_MODEL, D_FF), const2),
                  resident((D_FF, D_MODEL), const2),
                  resident((1, D_MODEL), const2),
                  resident((CHUNK, CHUNK), const2)],
        out_specs=pl.BlockSpec((tm, D_MODEL), lambda i: (i, 0)),
        out_shape=jax.ShapeDtypeStruct((t, D_MODEL), F32),
        scratch_shapes=[pltpu.VMEM((SLABS, SUBLANES, D_MODEL), F32)],
        compiler_params=pltpu.CompilerParams(
            dimension_semantics=("arbitrary",), vmem_limit_bytes=VMEM_LIMIT),
        name="gate_pool_merge_mlp",
    )(x2d, y_raw, proj, proj, proj, mod3, ssd_norm_w, pool_w, pool_scale, w_bs, w_bp, w_out, norm_mlp_w,
      w_up, w_dn, norm_final_w, unperm)


def _cumsum_matrix():
    tok = _token_of_row(np.arange(CHUNK))
    tri = (tok[None, :] <= tok[:, None]).astype(np.float32)
    return jnp.asarray(np.concatenate([tri] * DT_REP, axis=1), BF16)


def _interleave_matrix():
    p = np.zeros((CHUNK, CHUNK), np.float32)
    p[np.arange(CHUNK), _token_of_row(np.arange(CHUNK))] = 1.0
    return jnp.asarray(p, BF16), jnp.asarray(p.T, BF16)


def _head_expand_matrix():
    m = np.zeros((LANES, D_INNER), np.float32)
    for k in range(DT_REP * N_HEADS):
        h = k % N_HEADS
        m[k, h * HEAD_DIM:(h + 1) * HEAD_DIM] = 1.0
    return jnp.asarray(m, BF16)


def _rep_heads(v):
    pad = jnp.zeros((LANES - DT_REP * N_HEADS,), v.dtype)
    return jnp.concatenate([v] * DT_REP + [pad])[None, :]


def _on_proj_cols(v):
    return jnp.pad(v, ((0, 0), (COL_XS, PROJ_COLS - COL_POOL)))


def kernel(x, c, w_ada, b_ada, norm_mix_w, w_in, conv_w, conv_b, dt_bias, a_log, d_skip, ssd_norm_w,
           w_branch_ssd, pool_w, pool_scale, w_branch_pool, w_out, norm_mlp_w, w_up, w_down,
           norm_final_w):
    batch, seq, d = x.shape
    depth = w_ada.shape[0]
    assert depth == 1, "the final norm is fused into the last kernel of a single layer"
    t = batch * seq
    tri3 = _cumsum_matrix()
    expand3 = _head_expand_matrix()
    perm, unperm = _interleave_matrix()
    c_pad = jnp.pad(c, ((0, SUBLANES - batch), (0, 0)))
    dt_lo = 2 * D_INNER + BC_WIDTH
    x2d = x.reshape(t, d)
    i = 0
    mod = _modulation(c_pad, w_ada[i], b_ada[i][None, :])[:batch]
    mod3 = mod[:, None, :]
    wi = w_in[i]
    pool_lo = dt_lo + N_HEADS
    w_zxbc = wi[:, :dt_lo].astype(BF16)
    w_pool = wi[:, pool_lo:pool_lo + D_MODEL].astype(BF16)
    w_gate = wi[:, pool_lo + D_MODEL:].astype(BF16)
    w_dt = wi[:, dt_lo:pool_lo]
    w_dt3 = jnp.concatenate([w_dt] * DT_REP + [jnp.zeros((d, LANES - DT_REP * N_HEADS), F32)],
                            axis=1).astype(BF16)
    proj, dt_raw = _in_projection(x2d, norm_mix_w[i][None, :], mod3, w_zxbc, w_pool, w_gate, w_dt3,
                                  _on_proj_cols(conv_w[i]), _on_proj_cols(conv_b[i][None, :]), perm, seq)
    y_raw = _ssd_scan(proj, dt_raw, _rep_heads(dt_bias[i]), _rep_heads(a_log[i]),
                      jnp.repeat(d_skip[i], HEAD_DIM)[None, :], tri3, expand3, batch, seq)
    out = _tail(x2d, y_raw, proj, mod3, ssd_norm_w[i][None, :], pool_w[i].astype(BF16),
                pool_scale[i][None, :], w_branch_ssd[i].astype(BF16), w_branch_pool[i].astype(BF16),
                w_out[i].astype(BF16), norm_mlp_w[i][None, :], w_up[i].astype(BF16),
                w_down[i].astype(BF16), norm_final_w[None, :], unperm, seq)
    return out.reshape(batch, seq, d)
```

```python
import functools

import jax
import jax.numpy as jnp
import numpy as np
from jax import lax
from jax.experimental import pallas as pl
from jax.experimental.pallas import tpu as pltpu

F32 = jnp.float32
BF16 = jnp.bfloat16

D_MODEL = 1024
D_INNER = 2 * D_MODEL
HEAD_DIM = 64
N_HEADS = D_INNER // HEAD_DIM
N_GROUPS = 4
GROUP_CH = D_INNER // N_GROUPS
D_STATE = 128
CONV_K = 4
CHUNK = 128
BC_WIDTH = 2 * N_GROUPS * D_STATE
POOL_WINDOWS = (2, 4, 8, 16)
POOL_GW = D_MODEL // len(POOL_WINDOWS)
D_FF = 4 * D_MODEL
N_MOD = 6
EPS = 1e-5
LOG2_E = 1.4426950408889634
LANES = 128
SUBLANES = 8
SLABS = CHUNK // SUBLANES
DT_REP = 3

PROJ_TN = 2048
COL_Z = 0
COL_XS = D_INNER
COL_BC = 2 * D_INNER
COL_POOL = 2 * D_INNER + BC_WIDTH
COL_GATE = COL_POOL + D_MODEL
PROJ_COLS = COL_GATE + 2 * D_MODEL
TILE_Z, TILE_XS, TILE_BC_POOL, TILE_GATE = 0, 1, 2, 3
CONV_COL_BLOCK = 512
PROJ_ROW_SPLIT = 4

VMEM_LIMIT = 56 * 1024 * 1024
SSD_CHUNKS_PER_STEP = 4


def _rms(x):
    return x * lax.rsqrt(jnp.mean(x * x, axis=-1, keepdims=True) + EPS)


def _token_of_row(r):
    return (r % SUBLANES) * SLABS + r // SUBLANES


def _permute_chunks(perm, xb):
    n_chunks = xb.shape[0] // CHUNK
    return jnp.concatenate(
        [jnp.dot(perm, xb[c * CHUNK:(c + 1) * CHUNK, :], preferred_element_type=F32).astype(BF16)
         for c in range(n_chunks)], axis=0)


def _slabs_of(x, n_chunks):
    return [[x[(c * SLABS + v) * SUBLANES:(c * SLABS + v + 1) * SUBLANES, :] for v in range(SLABS)]
            for c in range(n_chunks)]


def _wrapped_slabs(slabs, prev_last, max_shift):
    n_chunks = len(slabs)
    width = slabs[0][0].shape[1]
    first_row = lax.broadcasted_iota(jnp.int32, (SUBLANES, width), 0) == 0
    wrapped = {}
    for j in range(1, max_shift + 1):
        stacked = jnp.concatenate([slabs[c][SLABS - j] for c in range(n_chunks)], axis=0)
        rolled = pltpu.roll(stacked, 1, 0)
        head = jnp.where(first_row, pltpu.roll(prev_last[j], 1, 0), rolled[0:SUBLANES, :])
        wrapped[j] = [head] + [rolled[c * SUBLANES:(c + 1) * SUBLANES, :] for c in range(1, n_chunks)]
    return wrapped


def _shifted(slabs, wrapped, c, v, k):
    return slabs[c][v - k] if v >= k else wrapped[k - v][c]


def _mod_kernel(c_ref, w_ref, b_ref, o_ref):
    s = jax.nn.silu(c_ref[...])
    o_ref[...] = jnp.dot(s.astype(BF16), w_ref[...].astype(BF16),
                         preferred_element_type=F32) + b_ref[...]


def _modulation(c_pad, w_ada, b_ada):
    n = w_ada.shape[1]
    tn = 1024
    return pl.pallas_call(
        _mod_kernel,
        grid=(n // tn,),
        in_specs=[pl.BlockSpec((SUBLANES, D_MODEL), lambda j: (0, 0)),
                  pl.BlockSpec((D_MODEL, tn), lambda j: (0, j)),
                  pl.BlockSpec((1, tn), lambda j: (0, j))],
        out_specs=pl.BlockSpec((SUBLANES, tn), lambda j: (0, j)),
        out_shape=jax.ShapeDtypeStruct((SUBLANES, n), F32),
        name="adaln_mod",
    )(c_pad, w_ada, b_ada)


def _conv_silu_store(r, cw_ref, cb_ref, halo_ref, o_ref, width):
    n_chunks = r.shape[0] // CHUNK
    pair = 2 * SUBLANES
    for c0 in range(0, width, CONV_COL_BLOCK):
        cols = slice(c0, c0 + CONV_COL_BLOCK)
        slabs = _slabs_of(r[:, cols], n_chunks)
        prev_last = {j: halo_ref[j - 1, :, cols] for j in range(1, CONV_K)}
        for j in range(1, CONV_K):
            halo_ref[j - 1, :, cols] = slabs[n_chunks - 1][SLABS - j]
        wrapped = _wrapped_slabs(slabs, prev_last, CONV_K - 1)
        taps = [cw_ref[k:k + 1, cols] for k in range(CONV_K)]
        bias = cb_ref[:, cols]
        for c in range(n_chunks):
            for v0 in range(0, SLABS, 2):
                accs = []
                for v in (v0, v0 + 1):
                    acc = slabs[c][v] * taps[CONV_K - 1] + bias
                    for k in range(1, CONV_K):
                        acc = acc + _shifted(slabs, wrapped, c, v, k) * taps[CONV_K - 1 - k]
                    accs.append(acc)
                lo = (c * SLABS + v0) * SUBLANES
                o_ref[lo:lo + pair, cols] = jax.nn.silu(jnp.concatenate(accs, axis=0)).astype(BF16)


def _inproj_kernel(x_ref, nw_ref, mod_ref, w_ref, wdt_ref, cw_ref, cb_ref, perm_ref, o_ref, dt_ref,
                   h_scr, halo_scr, *, tiles_per_batch):
    i = pl.program_id(0)
    j = pl.program_id(1)

    @pl.when(j == 0)
    def _():
        mod = mod_ref[0]
        shift = mod[:, 0:D_MODEL]
        scale = mod[:, D_MODEL:2 * D_MODEL]
        h = (_rms(x_ref[...]) * nw_ref[...]) * (1.0 + scale) + shift
        hb = _permute_chunks(perm_ref[...], h.astype(BF16))
        h_scr[...] = hb
        dt_ref[...] = jnp.dot(hb, wdt_ref[...], preferred_element_type=F32)

    @pl.when(jnp.logical_and(i % tiles_per_batch == 0, j == 0))
    def _():
        halo_scr[...] = jnp.zeros_like(halo_scr)

    def proj():
        part = h_scr.shape[0] // PROJ_ROW_SPLIT
        return jnp.concatenate(
            [jnp.dot(h_scr[m0:m0 + part, :], w_ref[...], preferred_element_type=F32)
             for m0 in range(0, h_scr.shape[0], part)], axis=0)

    @pl.when(j == TILE_Z)
    def _():
        o_ref[...] = jax.nn.silu(proj()).astype(BF16)

    @pl.when(j == TILE_XS)
    def _():
        _conv_silu_store(proj(), cw_ref, cb_ref, halo_scr.at[0], o_ref, PROJ_TN)

    @pl.when(j == TILE_BC_POOL)
    def _():
        r = proj()
        _conv_silu_store(r[:, 0:BC_WIDTH], cw_ref, cb_ref, halo_scr.at[1], o_ref, BC_WIDTH)
        o_ref[:, BC_WIDTH:] = r[:, BC_WIDTH:].astype(BF16)

    @pl.when(j == TILE_GATE)
    def _():
        o_ref[...] = jax.nn.sigmoid(proj()).astype(BF16)


def _in_projection(x2d, norm_w, mod3, w_main, w_dt, conv_w_cols, conv_b_cols, perm, seq):
    t = x2d.shape[0]
    tm, tn = 1024, PROJ_TN
    tiles_per_batch = seq // tm
    return pl.pallas_call(
        functools.partial(_inproj_kernel, tiles_per_batch=tiles_per_batch),
        grid=(t // tm, PROJ_COLS // tn),
        in_specs=[pl.BlockSpec((tm, D_MODEL), lambda i, j: (i, 0)),
                  pl.BlockSpec((1, D_MODEL), lambda i, j: (0, 0)),
                  pl.BlockSpec((1, 1, N_MOD * D_MODEL), lambda i, j: (i // tiles_per_batch, 0, 0)),
                  pl.BlockSpec((D_MODEL, tn), lambda i, j: (0, j)),
                  pl.BlockSpec((D_MODEL, LANES), lambda i, j: (0, 0)),
                  pl.BlockSpec((CONV_K, tn), lambda i, j: (0, j)),
                  pl.BlockSpec((1, tn), lambda i, j: (0, j)),
                  pl.BlockSpec((CHUNK, CHUNK), lambda i, j: (0, 0))],
        out_specs=[pl.BlockSpec((tm, tn), lambda i, j: (i, j)),
                   pl.BlockSpec((tm, LANES), lambda i, j: (i, 0))],
        out_shape=[jax.ShapeDtypeStruct((t, PROJ_COLS), BF16),
                   jax.ShapeDtypeStruct((t, LANES), F32)],
        scratch_shapes=[pltpu.VMEM((tm, D_MODEL), BF16),
                        pltpu.VMEM((2, CONV_K - 1, SUBLANES, tn), F32)],
        compiler_params=pltpu.CompilerParams(
            dimension_semantics=("arbitrary", "arbitrary"), vmem_limit_bytes=VMEM_LIMIT),
        name="norm_in_proj",
    )(x2d, norm_w, mod3, w_main, w_dt, conv_w_cols, conv_b_cols, perm)


def _split3(v):
    hi = v.astype(BF16).astype(F32)
    r1 = v - hi
    mid = r1.astype(BF16).astype(F32)
    return hi, r1, r1 - mid


def _lane_pieces(v, lane):
    hi, r1, r2 = _split3(v)
    return jnp.where(lane < N_HEADS, hi, jnp.where(lane < 2 * N_HEADS, r1, r2)).astype(BF16)


def _ssd_prologue(xs_ref, dt_ref, dtb_ref, alog_ref, tri_ref, exp_ref, xdb, expd, acs2_scr, src_scr):
    q = CHUNK
    tm = dt_ref.shape[0]
    n_chunks = tm // q
    dt = jax.nn.softplus(dt_ref[...] + dtb_ref[...])
    dta = dt * (-jnp.exp(alog_ref[...]))
    hi, r1, r2 = _split3(dta)
    acs = []
    for c in range(n_chunks):
        rows = slice(c * q, (c + 1) * q)
        stacked = jnp.concatenate([hi[rows], r1[rows], r2[rows]], axis=0).astype(BF16)
        acs.append(jnp.dot(tri_ref[...], stacked, preferred_element_type=F32))
    acs2 = jnp.concatenate(acs, axis=0) * LOG2_E
    acs2_last = jnp.concatenate(
        [jnp.broadcast_to(acs2[(c + 1) * q - 1:(c + 1) * q, :], (q, LANES)) for c in range(n_chunks)], axis=0)
    decay_to_end = jnp.exp2(acs2_last - acs2)
    decay_from_start = jnp.exp2(acs2)
    acs2_scr[...] = acs2
    src = acs2 - jnp.log2(dt)
    for c in range(n_chunks):
        src_scr[c] = src[c * q:(c + 1) * q, :].T

    lane = lax.broadcasted_iota(jnp.int32, (tm, LANES), 1)
    pieces = jnp.concatenate([_lane_pieces(dt * decay_to_end, lane),
                              _lane_pieces(decay_from_start, lane)], axis=0)
    expd[...] = jnp.dot(pieces, exp_ref[...], preferred_element_type=F32)
    xdb[...] = (xs_ref[...].astype(F32) * expd[0:tm, :]).astype(BF16)


def _ssd_chunk(ci, xs_ref, bc_ref, dskip_ref, o_ref, state, xdb, expd, acs2_scr, src_scr):
    q = CHUNK
    tm = xs_ref.shape[0]
    rows = slice(ci * q, (ci + 1) * q)
    off_rows = slice(tm + ci * q, tm + (ci + 1) * q)
    acs2 = acs2_scr[rows, :]
    src_t = src_scr[ci]

    causal = (_token_of_row(lax.broadcasted_iota(jnp.int32, (q, q), 0))
              >= _token_of_row(lax.broadcasted_iota(jnp.int32, (q, q), 1)))
    half = lax.broadcasted_iota(jnp.int32, (q, LANES), 1) < HEAD_DIM
    zero_b = jnp.zeros((q, LANES), BF16)

    for g in range(N_GROUPS):
        gs = slice(g * GROUP_CH, (g + 1) * GROUP_CH)
        bg = bc_ref[rows, g * D_STATE:(g + 1) * D_STATE]
        cg = bc_ref[rows, (N_GROUPS + g) * D_STATE:(N_GROUPS + g + 1) * D_STATE]
        scores = lax.dot_general(cg, bg, (((1,), (1,)), ((), ())), preferred_element_type=F32)
        st_old = state[:, gs]
        y_off = jnp.dot(cg, st_old.astype(BF16), preferred_element_type=F32)
        new_t = lax.dot_general(bg, xdb[rows, gs], (((0,), (0,)), ((), ())),
                                preferred_element_type=F32)
        state[:, gs] = st_old * expd[off_rows.stop - 1:off_rows.stop, gs] + new_t
        ys = []
        for pr in range(GROUP_CH // LANES):
            h0 = g * (GROUP_CH // HEAD_DIM) + 2 * pr
            ms = []
            for h in (h0, h0 + 1):
                a_col = jnp.broadcast_to(acs2[:, h:h + 1], (q, q))
                a_row = jnp.broadcast_to(src_t[h:h + 1, :], (q, q))
                decay_dt = jnp.exp2(jnp.where(causal, a_col - a_row, -jnp.inf))
                ms.append((scores * decay_dt).astype(BF16))
            xp = xs_ref[rows, g * GROUP_CH + pr * LANES:g * GROUP_CH + (pr + 1) * LANES]
            rhs = jnp.concatenate([jnp.where(half, xp, zero_b), jnp.where(half, zero_b, xp)], axis=0)
            ys.append(jnp.dot(jnp.concatenate(ms, axis=1), rhs, preferred_element_type=F32))
        y = jnp.concatenate(ys, axis=1) + y_off * expd[off_rows, gs]
        o_ref[rows, gs] = (y + dskip_ref[:, gs] * xs_ref[rows, gs].astype(F32)).astype(BF16)


def _ssd_kernel(xs_ref, bc_ref, dt_ref, dtb_ref, alog_ref, dskip_ref, tri_ref, exp_ref,
                o_ref, state, xdb, expd, acs2_scr, src_scr):
    @pl.when(pl.program_id(1) == 0)
    def _():
        state[...] = jnp.zeros_like(state)

    _ssd_prologue(xs_ref, dt_ref, dtb_ref, alog_ref, tri_ref, exp_ref, xdb, expd, acs2_scr, src_scr)
    for ci in range(SSD_CHUNKS_PER_STEP):
        _ssd_chunk(ci, xs_ref, bc_ref, dskip_ref, o_ref, state, xdb, expd, acs2_scr, src_scr)


def _ssd_scan(proj, dt_raw, dt_bias3, a_log3, d_skip_ch, tri3, expand3, batch, seq):
    q = CHUNK
    tm = SSD_CHUNKS_PER_STEP * q
    steps = seq // tm
    t = batch * seq
    rowmap = lambda b, c: b * steps + c
    const = lambda b, c: (0, 0)
    return pl.pallas_call(
        _ssd_kernel,
        grid=(batch, steps),
        in_specs=[pl.BlockSpec((tm, D_INNER), lambda b, c: (rowmap(b, c), COL_XS // D_INNER)),
                  pl.BlockSpec((tm, BC_WIDTH), lambda b, c: (rowmap(b, c), COL_BC // BC_WIDTH)),
                  pl.BlockSpec((tm, LANES), lambda b, c: (rowmap(b, c), 0)),
                  pl.BlockSpec((1, LANES), const),
                  pl.BlockSpec((1, LANES), const),
                  pl.BlockSpec((1, D_INNER), const),
                  pl.BlockSpec((q, DT_REP * q), const),
                  pl.BlockSpec((LANES, D_INNER), const)],
        out_specs=pl.BlockSpec((tm, D_INNER), lambda b, c: (rowmap(b, c), 0)),
        out_shape=jax.ShapeDtypeStruct((t, D_INNER), BF16),
        scratch_shapes=[pltpu.VMEM((D_STATE, D_INNER), F32),
                        pltpu.VMEM((tm, D_INNER), BF16),
                        pltpu.VMEM((2 * tm, D_INNER), F32),
                        pltpu.VMEM((tm, LANES), F32),
                        pltpu.VMEM((SSD_CHUNKS_PER_STEP, LANES, q), F32)],
        compiler_params=pltpu.CompilerParams(
            dimension_semantics=("arbitrary", "arbitrary"), vmem_limit_bytes=VMEM_LIMIT),
        name="ssd_scan",
    )(proj, proj, dt_raw, dt_bias3, a_log3, d_skip_ch, tri3, expand3)


def _pooled_groups(u, halo_ref, pos0):
    n_chunks = u.shape[0] // CHUNK
    max_shift = max(POOL_WINDOWS) - 1
    slabs = _slabs_of(u, n_chunks)
    prev_last = {j: halo_ref[SLABS - j] for j in range(1, max_shift + 1)}
    for v in range(SLABS):
        halo_ref[v] = slabs[n_chunks - 1][v]
    wrapped = _wrapped_slabs(slabs, prev_last, max_shift)
    sub = lax.broadcasted_iota(jnp.int32, (SUBLANES, POOL_GW), 0)
    pooled = []
    for gi, win in enumerate(POOL_WINDOWS):
        cs = slice(gi * POOL_GW, (gi + 1) * POOL_GW)
        out = []
        for c in range(n_chunks):
            for v in range(SLABS):
                cur = slabs[c][v][:, cs]
                tot = cur
                for k in range(1, win):
                    tot = tot + _shifted(slabs, wrapped, c, v, k)[:, cs]
                pos = pos0 + c * CHUNK + sub * SLABS + v
                count = jnp.minimum(pos + 1, win).astype(F32)
                out.append(tot / count - cur)
        pooled.append(jnp.concatenate(out, axis=0).astype(BF16))
    return pooled


def _tail_kernel(x_ref, y_ref, zs_ref, u_ref, g_ref, mod_ref, snw_ref, pw_ref, ps_ref, wbs_ref, wbp_ref,
                 wo_ref, nmlp_ref, wup_ref, wdn_ref, nfin_ref, unperm_ref, o_ref, uhalo,
                 *, tm, tiles_per_batch):
    i = pl.program_id(0)

    @pl.when(i % tiles_per_batch == 0)
    def _():
        uhalo[...] = jnp.zeros_like(uhalo)

    y_gated = []
    for g in range(N_GROUPS):
        gs = slice(g * GROUP_CH, (g + 1) * GROUP_CH)
        yg = y_ref[:, gs].astype(F32) * zs_ref[:, gs].astype(F32)
        y_gated.append((_rms(yg) * snw_ref[:, gs]).astype(BF16))
    y_ssd = jnp.dot(jnp.concatenate(y_gated, axis=1), wbs_ref[...], preferred_element_type=F32)

    pooled = _pooled_groups(u_ref[...].astype(F32), uhalo, (i % tiles_per_batch) * tm)
    y_pool_in = jnp.concatenate(
        [jnp.dot(pooled[gi], pw_ref[gi], preferred_element_type=F32) for gi in range(len(POOL_WINDOWS))],
        axis=1) * ps_ref[...]
    y_pool = jnp.dot(y_pool_in.astype(BF16), wbp_ref[...], preferred_element_type=F32)

    merged = (g_ref[:, :D_MODEL].astype(F32) * y_ssd + g_ref[:, D_MODEL:].astype(F32) * y_pool)
    merged = _permute_chunks(unperm_ref[...], merged.astype(BF16))
    mix = jnp.dot(merged, wo_ref[...], preferred_element_type=F32)

    mod = mod_ref[0]
    gate_m = mod[:, 2 * D_MODEL:3 * D_MODEL]
    shift_f = mod[:, 3 * D_MODEL:4 * D_MODEL]
    scale_f = mod[:, 4 * D_MODEL:5 * D_MODEL]
    gate_f = mod[:, 5 * D_MODEL:6 * D_MODEL]
    x1 = x_ref[...] + gate_m * mix
    h = (_rms(x1) * nmlp_ref[...]) * (1.0 + scale_f) + shift_f
    up = jnp.dot(h.astype(BF16), wup_ref[...], preferred_element_type=F32)
    act = jnp.square(jnp.maximum(up, 0.0)).astype(BF16)
    x2 = x1 + gate_f * jnp.dot(act, wdn_ref[...], preferred_element_type=F32)
    o_ref[...] = _rms(x2) * nfin_ref[...]


def _tail(x2d, y_raw, proj, mod3, ssd_norm_w, pool_w, pool_scale, w_bs, w_bp, w_out, norm_mlp_w, w_up,
          w_dn, norm_final_w, unperm, seq):
    t = x2d.shape[0]
    tm = 512
    tiles_per_batch = seq // tm
    const2 = lambda i: (0, 0)
    resident = functools.partial(pl.BlockSpec, pipeline_mode=pl.Buffered(1))
    return pl.pallas_call(
        functools.partial(_tail_kernel, tm=tm, tiles_per_batch=tiles_per_batch),
        grid=(t // tm,),
        in_specs=[pl.BlockSpec((tm, D_MODEL), lambda i: (i, 0)),
                  pl.BlockSpec((tm, D_INNER), lambda i: (i, 0)),
                  pl.BlockSpec((tm, D_INNER), lambda i: (i, COL_Z // D_INNER)),
                  pl.BlockSpec((tm, D_MODEL), lambda i: (i, COL_POOL // D_MODEL)),
                  pl.BlockSpec((tm, 2 * D_MODEL), lambda i: (i, COL_GATE // (2 * D_MODEL))),
                  pl.BlockSpec((1, 1, N_MOD * D_MODEL), lambda i: (i // tiles_per_batch, 0, 0)),
                  resident((1, D_INNER), const2),
                  resident((len(POOL_WINDOWS), POOL_GW, POOL_GW), lambda i: (0, 0, 0)),
                  resident((1, D_MODEL), const2),
                  resident((D_INNER, D_MODEL), const2),
                  resident((D_MODEL, D_MODEL), const2),
                  resident((D_MODEL, D_MODEL), const2),
                  resident((1, D_MODEL), const2),
                  resident((D_MODEL, D_FF), const2),
                  resident((D_FF, D_MODEL), const2),
                  resident((1, D_MODEL), const2),
                  resident((CHUNK, CHUNK), const2)],
        out_specs=pl.BlockSpec((tm, D_MODEL), lambda i: (i, 0)),
        out_shape=jax.ShapeDtypeStruct((t, D_MODEL), F32),
        scratch_shapes=[pltpu.VMEM((SLABS, SUBLANES, D_MODEL), F32)],
        compiler_params=pltpu.CompilerParams(
            dimension_semantics=("arbitrary",), vmem_limit_bytes=VMEM_LIMIT),
        name="gate_pool_merge_mlp",
    )(x2d, y_raw, proj, proj, proj, mod3, ssd_norm_w, pool_w, pool_scale, w_bs, w_bp, w_out, norm_mlp_w,
      w_up, w_dn, norm_final_w, unperm)


def _cumsum_matrix():
    tok = _token_of_row(np.arange(CHUNK))
    tri = (tok[None, :] <= tok[:, None]).astype(np.float32)
    return jnp.asarray(np.concatenate([tri] * DT_REP, axis=1), BF16)


def _interleave_matrix():
    p = np.zeros((CHUNK, CHUNK), np.float32)
    p[np.arange(CHUNK), _token_of_row(np.arange(CHUNK))] = 1.0
    return jnp.asarray(p, BF16), jnp.asarray(p.T, BF16)


def _head_expand_matrix():
    m = np.zeros((LANES, D_INNER), np.float32)
    for k in range(DT_REP * N_HEADS):
        h = k % N_HEADS
        m[k, h * HEAD_DIM:(h + 1) * HEAD_DIM] = 1.0
    return jnp.asarray(m, BF16)


def _rep_heads(v):
    pad = jnp.zeros((LANES - DT_REP * N_HEADS,), v.dtype)
    return jnp.concatenate([v] * DT_REP + [pad])[None, :]


def _on_proj_cols(v):
    return jnp.pad(v, ((0, 0), (COL_XS, PROJ_COLS - COL_POOL)))


def kernel(x, c, w_ada, b_ada, norm_mix_w, w_in, conv_w, conv_b, dt_bias, a_log, d_skip, ssd_norm_w,
           w_branch_ssd, pool_w, pool_scale, w_branch_pool, w_out, norm_mlp_w, w_up, w_down,
           norm_final_w):
    batch, seq, d = x.shape
    depth = w_ada.shape[0]
    assert depth == 1, "the final norm is fused into the last kernel of a single layer"
    t = batch * seq
    tri3 = _cumsum_matrix()
    expand3 = _head_expand_matrix()
    perm, unperm = _interleave_matrix()
    c_pad = jnp.pad(c, ((0, SUBLANES - batch), (0, 0)))
    dt_lo = 2 * D_INNER + BC_WIDTH
    x2d = x.reshape(t, d)
    i = 0
    mod = _modulation(c_pad, w_ada[i], b_ada[i][None, :])[:batch]
    mod3 = mod[:, None, :]
    wi = w_in[i]
    w_main = jnp.concatenate([wi[:, :dt_lo], wi[:, dt_lo + N_HEADS:]], axis=1).astype(BF16)
    w_dt = wi[:, dt_lo:dt_lo + N_HEADS]
    w_dt3 = jnp.concatenate([w_dt] * DT_REP + [jnp.zeros((d, LANES - DT_REP * N_HEADS), F32)],
                            axis=1).astype(BF16)
    proj, dt_raw = _in_projection(x2d, norm_mix_w[i][None, :], mod3, w_main, w_dt3,
                                  _on_proj_cols(conv_w[i]), _on_proj_cols(conv_b[i][None, :]), perm, seq)
    y_raw = _ssd_scan(proj, dt_raw, _rep_heads(dt_bias[i]), _rep_heads(a_log[i]),
                      jnp.repeat(d_skip[i], HEAD_DIM)[None, :], tri3, expand3, batch, seq)
    out = _tail(x2d, y_raw, proj, mod3, ssd_norm_w[i][None, :], pool_w[i].astype(BF16),
                pool_scale[i][None, :], w_branch_ssd[i].astype(BF16), w_branch_pool[i].astype(BF16),
                w_out[i].astype(BF16), norm_mlp_w[i][None, :], w_up[i].astype(BF16),
                w_down[i].astype(BF16), norm_final_w[None, :], unperm, seq)
    return out.reshape(batch, seq, d)
```

```python
import functools

import jax
import jax.numpy as jnp
import numpy as np
from jax import lax
from jax.experimental import pallas as pl
from jax.experimental.pallas import tpu as pltpu

F32 = jnp.float32
BF16 = jnp.bfloat16

D_MODEL = 1024
D_INNER = 2 * D_MODEL
HEAD_DIM = 64
N_HEADS = D_INNER // HEAD_DIM
N_GROUPS = 4
GROUP_CH = D_INNER // N_GROUPS
D_STATE = 128
CONV_K = 4
CHUNK = 128
BC_WIDTH = 2 * N_GROUPS * D_STATE
POOL_WINDOWS = (2, 4, 8, 16)
POOL_GW = D_MODEL // len(POOL_WINDOWS)
D_FF = 4 * D_MODEL
N_MOD = 6
EPS = 1e-5
LOG2_E = 1.4426950408889634
LANES = 128
SUBLANES = 8
SLABS = CHUNK // SUBLANES
DT_REP = 3

PROJ_TN = 2048
COL_Z = 0
COL_XS = D_INNER
COL_BC = 2 * D_INNER
COL_POOL = 2 * D_INNER + BC_WIDTH
COL_GATE = COL_POOL + D_MODEL
PROJ_COLS = COL_GATE + 2 * D_MODEL
TILE_Z, TILE_XS, TILE_BC_POOL, TILE_GATE = 0, 1, 2, 3
CONV_COL_BLOCK = 512
PROJ_ROW_SPLIT = 4

VMEM_LIMIT = 56 * 1024 * 1024
SSD_CHUNKS_PER_STEP = 4


def _rms(x):
    return x * lax.rsqrt(jnp.mean(x * x, axis=-1, keepdims=True) + EPS)


def _token_of_row(r):
    return (r % SUBLANES) * SLABS + r // SUBLANES


def _permute_chunks(perm, xb):
    n_chunks = xb.shape[0] // CHUNK
    return jnp.concatenate(
        [jnp.dot(perm, xb[c * CHUNK:(c + 1) * CHUNK, :], preferred_element_type=F32).astype(BF16)
         for c in range(n_chunks)], axis=0)


def _slabs_of(x, n_chunks):
    return [[x[(c * SLABS + v) * SUBLANES:(c * SLABS + v + 1) * SUBLANES, :] for v in range(SLABS)]
            for c in range(n_chunks)]


def _wrapped_slabs(slabs, prev_last, max_shift):
    n_chunks = len(slabs)
    width = slabs[0][0].shape[1]
    first_row = lax.broadcasted_iota(jnp.int32, (SUBLANES, width), 0) == 0
    wrapped = {}
    for j in range(1, max_shift + 1):
        stacked = jnp.concatenate([slabs[c][SLABS - j] for c in range(n_chunks)], axis=0)
        rolled = pltpu.roll(stacked, 1, 0)
        head = jnp.where(first_row, pltpu.roll(prev_last[j], 1, 0), rolled[0:SUBLANES, :])
        wrapped[j] = [head] + [rolled[c * SUBLANES:(c + 1) * SUBLANES, :] for c in range(1, n_chunks)]
    return wrapped


def _shifted(slabs, wrapped, c, v, k):
    return slabs[c][v - k] if v >= k else wrapped[k - v][c]


def _mod_kernel(c_ref, w_ref, b_ref, o_ref):
    s = jax.nn.silu(c_ref[...])
    o_ref[...] = jnp.dot(s.astype(BF16), w_ref[...].astype(BF16),
                         preferred_element_type=F32) + b_ref[...]


def _modulation(c_pad, w_ada, b_ada):
    n = w_ada.shape[1]
    tn = 1024
    return pl.pallas_call(
        _mod_kernel,
        grid=(n // tn,),
        in_specs=[pl.BlockSpec((SUBLANES, D_MODEL), lambda j: (0, 0)),
                  pl.BlockSpec((D_MODEL, tn), lambda j: (0, j)),
                  pl.BlockSpec((1, tn), lambda j: (0, j))],
        out_specs=pl.BlockSpec((SUBLANES, tn), lambda j: (0, j)),
        out_shape=jax.ShapeDtypeStruct((SUBLANES, n), F32),
        name="adaln_mod",
    )(c_pad, w_ada, b_ada)


def _w_in_prep_kernel(w_ref, o_ref, odt_ref):
    dt_lo = COL_POOL
    w = w_ref[...]
    o_ref[:, 0:dt_lo] = w[:, 0:dt_lo].astype(BF16)
    o_ref[:, dt_lo:] = w[:, dt_lo + N_HEADS:].astype(BF16)
    w_dt = w[:, dt_lo:dt_lo + N_HEADS]
    pad = jnp.zeros((w.shape[0], LANES - DT_REP * N_HEADS), F32)
    odt_ref[...] = jnp.concatenate([w_dt] * DT_REP + [pad], axis=1).astype(BF16)


def _w_in_prep(w_in):
    rows = 128
    k, n = w_in.shape
    return pl.pallas_call(
        _w_in_prep_kernel,
        grid=(k // rows,),
        in_specs=[pl.BlockSpec((rows, n), lambda r: (r, 0))],
        out_specs=[pl.BlockSpec((rows, PROJ_COLS), lambda r: (r, 0)),
                   pl.BlockSpec((rows, LANES), lambda r: (r, 0))],
        out_shape=[jax.ShapeDtypeStruct((k, PROJ_COLS), BF16),
                   jax.ShapeDtypeStruct((k, LANES), BF16)],
        name="w_in_prep",
    )(w_in)


def _conv_silu_store(r, cw_ref, cb_ref, halo_ref, o_ref, width):
    n_chunks = r.shape[0] // CHUNK
    pair = 2 * SUBLANES
    for c0 in range(0, width, CONV_COL_BLOCK):
        cols = slice(c0, c0 + CONV_COL_BLOCK)
        slabs = _slabs_of(r[:, cols], n_chunks)
        prev_last = {j: halo_ref[j - 1, :, cols] for j in range(1, CONV_K)}
        for j in range(1, CONV_K):
            halo_ref[j - 1, :, cols] = slabs[n_chunks - 1][SLABS - j]
        wrapped = _wrapped_slabs(slabs, prev_last, CONV_K - 1)
        taps = [cw_ref[k:k + 1, cols] for k in range(CONV_K)]
        bias = cb_ref[:, cols]
        for c in range(n_chunks):
            for v0 in range(0, SLABS, 2):
                accs = []
                for v in (v0, v0 + 1):
                    acc = slabs[c][v] * taps[CONV_K - 1] + bias
                    for k in range(1, CONV_K):
                        acc = acc + _shifted(slabs, wrapped, c, v, k) * taps[CONV_K - 1 - k]
                    accs.append(acc)
                lo = (c * SLABS + v0) * SUBLANES
                o_ref[lo:lo + pair, cols] = jax.nn.silu(jnp.concatenate(accs, axis=0)).astype(BF16)


def _inproj_kernel(x_ref, nw_ref, mod_ref, w_ref, wdt_ref, cw_ref, cb_ref, perm_ref, o_ref, dt_ref,
                   h_scr, halo_scr, *, tiles_per_batch):
    i = pl.program_id(0)
    j = pl.program_id(1)

    @pl.when(j == 0)
    def _():
        mod = mod_ref[0]
        shift = mod[:, 0:D_MODEL]
        scale = mod[:, D_MODEL:2 * D_MODEL]
        h = (_rms(x_ref[...]) * nw_ref[...]) * (1.0 + scale) + shift
        hb = _permute_chunks(perm_ref[...], h.astype(BF16))
        h_scr[...] = hb
        dt_ref[...] = jnp.dot(hb, wdt_ref[...], preferred_element_type=F32)

    @pl.when(jnp.logical_and(i % tiles_per_batch == 0, j == 0))
    def _():
        halo_scr[...] = jnp.zeros_like(halo_scr)

    def proj():
        part = h_scr.shape[0] // PROJ_ROW_SPLIT
        return jnp.concatenate(
            [jnp.dot(h_scr[m0:m0 + part, :], w_ref[...], preferred_element_type=F32)
             for m0 in range(0, h_scr.shape[0], part)], axis=0)

    @pl.when(j == TILE_Z)
    def _():
        o_ref[...] = jax.nn.silu(proj()).astype(BF16)

    @pl.when(j == TILE_XS)
    def _():
        _conv_silu_store(proj(), cw_ref, cb_ref, halo_scr.at[0], o_ref, PROJ_TN)

    @pl.when(j == TILE_BC_POOL)
    def _():
        r = proj()
        _conv_silu_store(r[:, 0:BC_WIDTH], cw_ref, cb_ref, halo_scr.at[1], o_ref, BC_WIDTH)
        o_ref[:, BC_WIDTH:] = r[:, BC_WIDTH:].astype(BF16)

    @pl.when(j == TILE_GATE)
    def _():
        o_ref[...] = jax.nn.sigmoid(proj()).astype(BF16)


def _in_projection(x2d, norm_w, mod3, w_main, w_dt, conv_w_cols, conv_b_cols, perm, seq):
    t = x2d.shape[0]
    tm, tn = 1024, PROJ_TN
    tiles_per_batch = seq // tm
    return pl.pallas_call(
        functools.partial(_inproj_kernel, tiles_per_batch=tiles_per_batch),
        grid=(t // tm, PROJ_COLS // tn),
        in_specs=[pl.BlockSpec((tm, D_MODEL), lambda i, j: (i, 0)),
                  pl.BlockSpec((1, D_MODEL), lambda i, j: (0, 0)),
                  pl.BlockSpec((1, 1, N_MOD * D_MODEL), lambda i, j: (i // tiles_per_batch, 0, 0)),
                  pl.BlockSpec((D_MODEL, tn), lambda i, j: (0, j)),
                  pl.BlockSpec((D_MODEL, LANES), lambda i, j: (0, 0)),
                  pl.BlockSpec((CONV_K, tn), lambda i, j: (0, j)),
                  pl.BlockSpec((1, tn), lambda i, j: (0, j)),
                  pl.BlockSpec((CHUNK, CHUNK), lambda i, j: (0, 0))],
        out_specs=[pl.BlockSpec((tm, tn), lambda i, j: (i, j)),
                   pl.BlockSpec((tm, LANES), lambda i, j: (i, 0))],
        out_shape=[jax.ShapeDtypeStruct((t, PROJ_COLS), BF16),
                   jax.ShapeDtypeStruct((t, LANES), F32)],
        scratch_shapes=[pltpu.VMEM((tm, D_MODEL), BF16),
                        pltpu.VMEM((2, CONV_K - 1, SUBLANES, tn), F32)],
        compiler_params=pltpu.CompilerParams(
            dimension_semantics=("arbitrary", "arbitrary"), vmem_limit_bytes=VMEM_LIMIT),
        name="norm_in_proj",
    )(x2d, norm_w, mod3, w_main, w_dt, conv_w_cols, conv_b_cols, perm)


def _split3(v):
    hi = v.astype(BF16).astype(F32)
    r1 = v - hi
    mid = r1.astype(BF16).astype(F32)
    return hi, r1, r1 - mid


def _lane_pieces(v, lane):
    hi, r1, r2 = _split3(v)
    return jnp.where(lane < N_HEADS, hi, jnp.where(lane < 2 * N_HEADS, r1, r2)).astype(BF16)


def _ssd_prologue(xs_ref, dt_ref, dtb_ref, alog_ref, tri_ref, exp_ref, xdb, expd, acs2_scr, src_scr):
    q = CHUNK
    tm = dt_ref.shape[0]
    n_chunks = tm // q
    dt = jax.nn.softplus(dt_ref[...] + dtb_ref[...])
    dta = dt * (-jnp.exp(alog_ref[...]))
    hi, r1, r2 = _split3(dta)
    acs = []
    for c in range(n_chunks):
        rows = slice(c * q, (c + 1) * q)
        stacked = jnp.concatenate([hi[rows], r1[rows], r2[rows]], axis=0).astype(BF16)
        acs.append(jnp.dot(tri_ref[...], stacked, preferred_element_type=F32))
    acs2 = jnp.concatenate(acs, axis=0) * LOG2_E
    acs2_last = jnp.concatenate(
        [jnp.broadcast_to(acs2[(c + 1) * q - 1:(c + 1) * q, :], (q, LANES)) for c in range(n_chunks)], axis=0)
    decay_to_end = jnp.exp2(acs2_last - acs2)
    decay_from_start = jnp.exp2(acs2)
    acs2_scr[...] = acs2
    src = acs2 - jnp.log2(dt)
    for c in range(n_chunks):
        src_scr[c] = src[c * q:(c + 1) * q, :].T

    lane = lax.broadcasted_iota(jnp.int32, (tm, LANES), 1)
    pieces = jnp.concatenate([_lane_pieces(dt * decay_to_end, lane),
                              _lane_pieces(decay_from_start, lane)], axis=0)
    expd[...] = jnp.dot(pieces, exp_ref[...], preferred_element_type=F32)
    xdb[...] = (xs_ref[...].astype(F32) * expd[0:tm, :]).astype(BF16)


def _ssd_chunk(ci, xs_ref, bc_ref, dskip_ref, o_ref, state, xdb, expd, acs2_scr, src_scr):
    q = CHUNK
    tm = xs_ref.shape[0]
    rows = slice(ci * q, (ci + 1) * q)
    off_rows = slice(tm + ci * q, tm + (ci + 1) * q)
    acs2 = acs2_scr[rows, :]
    src_t = src_scr[ci]

    causal = (_token_of_row(lax.broadcasted_iota(jnp.int32, (q, q), 0))
              >= _token_of_row(lax.broadcasted_iota(jnp.int32, (q, q), 1)))
    half = lax.broadcasted_iota(jnp.int32, (q, LANES), 1) < HEAD_DIM
    zero_b = jnp.zeros((q, LANES), BF16)

    for g in range(N_GROUPS):
        gs = slice(g * GROUP_CH, (g + 1) * GROUP_CH)
        bg = bc_ref[rows, g * D_STATE:(g + 1) * D_STATE]
        cg = bc_ref[rows, (N_GROUPS + g) * D_STATE:(N_GROUPS + g + 1) * D_STATE]
        scores = lax.dot_general(cg, bg, (((1,), (1,)), ((), ())), preferred_element_type=F32)
        st_old = state[:, gs]
        y_off = jnp.dot(cg, st_old.astype(BF16), preferred_element_type=F32)
        new_t = lax.dot_general(bg, xdb[rows, gs], (((0,), (0,)), ((), ())),
                                preferred_element_type=F32)
        state[:, gs] = st_old * expd[off_rows.stop - 1:off_rows.stop, gs] + new_t
        ys = []
        for pr in range(GROUP_CH // LANES):
            h0 = g * (GROUP_CH // HEAD_DIM) + 2 * pr
            ms = []
            for h in (h0, h0 + 1):
                a_col = jnp.broadcast_to(acs2[:, h:h + 1], (q, q))
                a_row = jnp.broadcast_to(src_t[h:h + 1, :], (q, q))
                decay_dt = jnp.exp2(jnp.where(causal, a_col - a_row, -jnp.inf))
                ms.append((scores * decay_dt).astype(BF16))
            xp = xs_ref[rows, g * GROUP_CH + pr * LANES:g * GROUP_CH + (pr + 1) * LANES]
            rhs = jnp.concatenate([jnp.where(half, xp, zero_b), jnp.where(half, zero_b, xp)], axis=0)
            ys.append(jnp.dot(jnp.concatenate(ms, axis=1), rhs, preferred_element_type=F32))
        y = jnp.concatenate(ys, axis=1) + y_off * expd[off_rows, gs]
        o_ref[rows, gs] = (y + dskip_ref[:, gs] * xs_ref[rows, gs].astype(F32)).astype(BF16)


def _ssd_kernel(xs_ref, bc_ref, dt_ref, dtb_ref, alog_ref, dskip_ref, tri_ref, exp_ref, *rest, n_cast):
    cast_in = rest[:n_cast]
    o_ref = rest[n_cast]
    cast_out = rest[n_cast + 1:2 * n_cast + 1]
    state, xdb, expd, acs2_scr, src_scr = rest[2 * n_cast + 1:]

    @pl.when(pl.program_id(1) == 0)
    def _():
        state[...] = jnp.zeros_like(state)

    for w_ref, wb_ref in zip(cast_in, cast_out):
        wb_ref[...] = w_ref[...].astype(BF16)

    _ssd_prologue(xs_ref, dt_ref, dtb_ref, alog_ref, tri_ref, exp_ref, xdb, expd, acs2_scr, src_scr)
    for ci in range(SSD_CHUNKS_PER_STEP):
        _ssd_chunk(ci, xs_ref, bc_ref, dskip_ref, o_ref, state, xdb, expd, acs2_scr, src_scr)


def _ssd_scan(proj, dt_raw, dt_bias3, a_log3, d_skip_ch, tri3, expand3, cast_weights, batch, seq):
    q = CHUNK
    tm = SSD_CHUNKS_PER_STEP * q
    steps = seq // tm
    t = batch * seq
    n_steps = batch * steps
    rowmap = lambda b, c: b * steps + c
    const = lambda b, c: (0, 0)
    slab_specs = [pl.BlockSpec((w.shape[0] // n_steps, w.shape[1]), lambda b, c: (rowmap(b, c), 0))
                  for w in cast_weights]
    outs = pl.pallas_call(
        functools.partial(_ssd_kernel, n_cast=len(cast_weights)),
        grid=(batch, steps),
        in_specs=[pl.BlockSpec((tm, D_INNER), lambda b, c: (rowmap(b, c), COL_XS // D_INNER)),
                  pl.BlockSpec((tm, BC_WIDTH), lambda b, c: (rowmap(b, c), COL_BC // BC_WIDTH)),
                  pl.BlockSpec((tm, LANES), lambda b, c: (rowmap(b, c), 0)),
                  pl.BlockSpec((1, LANES), const),
                  pl.BlockSpec((1, LANES), const),
                  pl.BlockSpec((1, D_INNER), const),
                  pl.BlockSpec((q, DT_REP * q), const),
                  pl.BlockSpec((LANES, D_INNER), const)] + slab_specs,
        out_specs=[pl.BlockSpec((tm, D_INNER), lambda b, c: (rowmap(b, c), 0))] + slab_specs,
        out_shape=[jax.ShapeDtypeStruct((t, D_INNER), BF16)]
        + [jax.ShapeDtypeStruct(w.shape, BF16) for w in cast_weights],
        scratch_shapes=[pltpu.VMEM((D_STATE, D_INNER), F32),
                        pltpu.VMEM((tm, D_INNER), BF16),
                        pltpu.VMEM((2 * tm, D_INNER), F32),
                        pltpu.VMEM((tm, LANES), F32),
                        pltpu.VMEM((SSD_CHUNKS_PER_STEP, LANES, q), F32)],
        compiler_params=pltpu.CompilerParams(
            dimension_semantics=("arbitrary", "arbitrary"), vmem_limit_bytes=VMEM_LIMIT),
        name="ssd_scan",
    )(proj, proj, dt_raw, dt_bias3, a_log3, d_skip_ch, tri3, expand3, *cast_weights)
    return outs[0], outs[1:]


def _pooled_groups(u, halo_ref, pos0):
    n_chunks = u.shape[0] // CHUNK
    max_shift = max(POOL_WINDOWS) - 1
    slabs = _slabs_of(u, n_chunks)
    prev_last = {j: halo_ref[SLABS - j] for j in range(1, max_shift + 1)}
    for v in range(SLABS):
        halo_ref[v] = slabs[n_chunks - 1][v]
    wrapped = _wrapped_slabs(slabs, prev_last, max_shift)
    sub = lax.broadcasted_iota(jnp.int32, (SUBLANES, POOL_GW), 0)
    pooled = []
    for gi, win in enumerate(POOL_WINDOWS):
        cs = slice(gi * POOL_GW, (gi + 1) * POOL_GW)
        out = []
        for c in range(n_chunks):
            for v in range(SLABS):
                cur = slabs[c][v][:, cs]
                tot = cur
                for k in range(1, win):
                    tot = tot + _shifted(slabs, wrapped, c, v, k)[:, cs]
                pos = pos0 + c * CHUNK + sub * SLABS + v
                count = jnp.minimum(pos + 1, win).astype(F32)
                out.append(tot / count - cur)
        pooled.append(jnp.concatenate(out, axis=0).astype(BF16))
    return pooled


def _tail_kernel(x_ref, y_ref, zs_ref, u_ref, g_ref, mod_ref, snw_ref, pw_ref, ps_ref, wbs_ref, wbp_ref,
                 wo_ref, nmlp_ref, wup_ref, wdn_ref, nfin_ref, unperm_ref, o_ref, uhalo,
                 *, tm, tiles_per_batch):
    i = pl.program_id(0)

    @pl.when(i % tiles_per_batch == 0)
    def _():
        uhalo[...] = jnp.zeros_like(uhalo)

    y_gated = []
    for g in range(N_GROUPS):
        gs = slice(g * GROUP_CH, (g + 1) * GROUP_CH)
        yg = y_ref[:, gs].astype(F32) * zs_ref[:, gs].astype(F32)
        y_gated.append((_rms(yg) * snw_ref[:, gs]).astype(BF16))
    y_ssd = jnp.dot(jnp.concatenate(y_gated, axis=1), wbs_ref[...], preferred_element_type=F32)

    pooled = _pooled_groups(u_ref[...].astype(F32), uhalo, (i % tiles_per_batch) * tm)
    y_pool_in = jnp.concatenate(
        [jnp.dot(pooled[gi], pw_ref[gi], preferred_element_type=F32) for gi in range(len(POOL_WINDOWS))],
        axis=1) * ps_ref[...]
    y_pool = jnp.dot(y_pool_in.astype(BF16), wbp_ref[...], preferred_element_type=F32)

    merged = (g_ref[:, :D_MODEL].astype(F32) * y_ssd + g_ref[:, D_MODEL:].astype(F32) * y_pool)
    merged = _permute_chunks(unperm_ref[...], merged.astype(BF16))
    mix = jnp.dot(merged, wo_ref[...], preferred_element_type=F32)

    mod = mod_ref[0]
    gate_m = mod[:, 2 * D_MODEL:3 * D_MODEL]
    shift_f = mod[:, 3 * D_MODEL:4 * D_MODEL]
    scale_f = mod[:, 4 * D_MODEL:5 * D_MODEL]
    gate_f = mod[:, 5 * D_MODEL:6 * D_MODEL]
    x1 = x_ref[...] + gate_m * mix
    h = (_rms(x1) * nmlp_ref[...]) * (1.0 + scale_f) + shift_f
    up = jnp.dot(h.astype(BF16), wup_ref[...], preferred_element_type=F32)
    act = jnp.square(jnp.maximum(up, 0.0)).astype(BF16)
    x2 = x1 + gate_f * jnp.dot(act, wdn_ref[...], preferred_element_type=F32)
    o_ref[...] = _rms(x2) * nfin_ref[...]


def _tail(x2d, y_raw, proj, mod3, ssd_norm_w, pool_w, pool_scale, w_bs, w_bp, w_out, norm_mlp_w, w_up,
          w_dn, norm_final_w, unperm, seq):
    t = x2d.shape[0]
    tm = 512
    tiles_per_batch = seq // tm
    const2 = lambda i: (0, 0)
    resident = functools.partial(pl.BlockSpec, pipeline_mode=pl.Buffered(1))
    return pl.pallas_call(
        functools.partial(_tail_kernel, tm=tm, tiles_per_batch=tiles_per_batch),
        grid=(t // tm,),
        in_specs=[pl.BlockSpec((tm, D_MODEL), lambda i: (i, 0)),
                  pl.BlockSpec((tm, D_INNER), lambda i: (i, 0)),
                  pl.BlockSpec((tm, D_INNER), lambda i: (i, COL_Z // D_INNER)),
                  pl.BlockSpec((tm, D_MODEL), lambda i: (i, COL_POOL // D_MODEL)),
                  pl.BlockSpec((tm, 2 * D_MODEL), lambda i: (i, COL_GATE // (2 * D_MODEL))),
                  pl.BlockSpec((1, 1, N_MOD * D_MODEL), lambda i: (i // tiles_per_batch, 0, 0)),
                  resident((1, D_INNER), const2),
                  resident((len(POOL_WINDOWS), POOL_GW, POOL_GW), lambda i: (0, 0, 0)),
                  resident((1, D_MODEL), const2),
                  resident((D_INNER, D_MODEL), const2),
                  resident((D_MODEL, D_MODEL), const2),
                  resident((D_MODEL, D_MODEL), const2),
                  resident((1, D_MODEL), const2),
                  resident((D_MODEL, D_FF), const2),
                  resident((D_FF, D_MODEL), const2),
                  resident((1, D_MODEL), const2),
                  resident((CHUNK, CHUNK), const2)],
        out_specs=pl.BlockSpec((tm, D_MODEL), lambda i: (i, 0)),
        out_shape=jax.ShapeDtypeStruct((t, D_MODEL), F32),
        scratch_shapes=[pltpu.VMEM((SLABS, SUBLANES, D_MODEL), F32)],
        compiler_params=pltpu.CompilerParams(
            dimension_semantics=("arbitrary",), vmem_limit_bytes=VMEM_LIMIT),
        name="gate_pool_merge_mlp",
    )(x2d, y_raw, proj, proj, proj, mod3, ssd_norm_w, pool_w, pool_scale, w_bs, w_bp, w_out, norm_mlp_w,
      w_up, w_dn, norm_final_w, unperm)


def _cumsum_matrix():
    tok = _token_of_row(np.arange(CHUNK))
    tri = (tok[None, :] <= tok[:, None]).astype(np.float32)
    return jnp.asarray(np.concatenate([tri] * DT_REP, axis=1), BF16)


def _interleave_matrix():
    p = np.zeros((CHUNK, CHUNK), np.float32)
    p[np.arange(CHUNK), _token_of_row(np.arange(CHUNK))] = 1.0
    return jnp.asarray(p, BF16), jnp.asarray(p.T, BF16)


def _head_expand_matrix():
    m = np.zeros((LANES, D_INNER), np.float32)
    for k in range(DT_REP * N_HEADS):
        h = k % N_HEADS
        m[k, h * HEAD_DIM:(h + 1) * HEAD_DIM] = 1.0
    return jnp.asarray(m, BF16)


def _rep_heads(v):
    pad = jnp.zeros((LANES - DT_REP * N_HEADS,), v.dtype)
    return jnp.concatenate([v] * DT_REP + [pad])[None, :]


def _on_proj_cols(v):
    return jnp.pad(v, ((0, 0), (COL_XS, PROJ_COLS - COL_POOL)))


def kernel(x, c, w_ada, b_ada, norm_mix_w, w_in, conv_w, conv_b, dt_bias, a_log, d_skip, ssd_norm_w,
           w_branch_ssd, pool_w, pool_scale, w_branch_pool, w_out, norm_mlp_w, w_up, w_down,
           norm_final_w):
    batch, seq, d = x.shape
    depth = w_ada.shape[0]
    assert depth == 1, "the final norm is fused into the last kernel of a single layer"
    t = batch * seq
    tri3 = _cumsum_matrix()
    expand3 = _head_expand_matrix()
    perm, unperm = _interleave_matrix()
    c_pad = jnp.pad(c, ((0, SUBLANES - batch), (0, 0)))
    dt_lo = 2 * D_INNER + BC_WIDTH
    x2d = x.reshape(t, d)
    i = 0
    mod = _modulation(c_pad, w_ada[i], b_ada[i][None, :])[:batch]
    mod3 = mod[:, None, :]
    w_main, w_dt3 = _w_in_prep(w_in[i])
    proj, dt_raw = _in_projection(x2d, norm_mix_w[i][None, :], mod3, w_main, w_dt3,
                                  _on_proj_cols(conv_w[i]), _on_proj_cols(conv_b[i][None, :]), perm, seq)
    n_pool = len(POOL_WINDOWS)
    tail_weights = [pool_w[i].reshape(n_pool * POOL_GW, POOL_GW), w_branch_ssd[i], w_branch_pool[i],
                    w_out[i], w_up[i], w_down[i]]
    y_raw, (pool_wb, w_bs, w_bp, w_ob, w_upb, w_dnb) = _ssd_scan(
        proj, dt_raw, _rep_heads(dt_bias[i]), _rep_heads(a_log[i]),
        jnp.repeat(d_skip[i], HEAD_DIM)[None, :], tri3, expand3, tail_weights, batch, seq)
    out = _tail(x2d, y_raw, proj, mod3, ssd_norm_w[i][None, :], pool_wb.reshape(n_pool, POOL_GW, POOL_GW),
                pool_scale[i][None, :], w_bs, w_bp, w_ob, norm_mlp_w[i][None, :], w_upb, w_dnb,
                norm_final_w[None, :], unperm, seq)
    return out.reshape(batch, seq, d)
```

```python
import functools

import jax
import jax.numpy as jnp
import numpy as np
from jax import lax
from jax.experimental import pallas as pl
from jax.experimental.pallas import tpu as pltpu

F32 = jnp.float32
BF16 = jnp.bfloat16

D_MODEL = 1024
D_INNER = 2 * D_MODEL
HEAD_DIM = 64
N_HEADS = D_INNER // HEAD_DIM
N_GROUPS = 4
GROUP_CH = D_INNER // N_GROUPS
D_STATE = 128
CONV_K = 4
CHUNK = 128
BC_WIDTH = 2 * N_GROUPS * D_STATE
POOL_WINDOWS = (2, 4, 8, 16)
POOL_GW = D_MODEL // len(POOL_WINDOWS)
D_FF = 4 * D_MODEL
N_MOD = 6
EPS = 1e-5
LOG2_E = 1.4426950408889634
LANES = 128
SUBLANES = 8
SLABS = CHUNK // SUBLANES
DT_REP = 3

PROJ_TN = 2048
COL_Z = 0
COL_XS = D_INNER
COL_BC = 2 * D_INNER
COL_POOL = 2 * D_INNER + BC_WIDTH
COL_GATE = COL_POOL + D_MODEL
PROJ_COLS = COL_GATE + 2 * D_MODEL
TILE_Z, TILE_XS, TILE_BC_POOL, TILE_GATE = 0, 1, 2, 3
CONV_COL_BLOCK = 512
PROJ_ROW_SPLIT = 4

VMEM_LIMIT = 56 * 1024 * 1024
SSD_CHUNKS_PER_STEP = 4


def _rms(x):
    return x * lax.rsqrt(jnp.mean(x * x, axis=-1, keepdims=True) + EPS)


def _token_of_row(r):
    return (r % SUBLANES) * SLABS + r // SUBLANES


def _permute_chunks(perm, xb):
    n_chunks = xb.shape[0] // CHUNK
    return jnp.concatenate(
        [jnp.dot(perm, xb[c * CHUNK:(c + 1) * CHUNK, :], preferred_element_type=F32).astype(BF16)
         for c in range(n_chunks)], axis=0)


def _slabs_of(x, n_chunks):
    return [[x[(c * SLABS + v) * SUBLANES:(c * SLABS + v + 1) * SUBLANES, :] for v in range(SLABS)]
            for c in range(n_chunks)]


def _wrapped_slabs(slabs, prev_last, max_shift):
    n_chunks = len(slabs)
    width = slabs[0][0].shape[1]
    first_row = lax.broadcasted_iota(jnp.int32, (SUBLANES, width), 0) == 0
    wrapped = {}
    for j in range(1, max_shift + 1):
        stacked = jnp.concatenate([slabs[c][SLABS - j] for c in range(n_chunks)], axis=0)
        rolled = pltpu.roll(stacked, 1, 0)
        head = jnp.where(first_row, pltpu.roll(prev_last[j], 1, 0), rolled[0:SUBLANES, :])
        wrapped[j] = [head] + [rolled[c * SUBLANES:(c + 1) * SUBLANES, :] for c in range(1, n_chunks)]
    return wrapped


def _shifted(slabs, wrapped, c, v, k):
    return slabs[c][v - k] if v >= k else wrapped[k - v][c]


def _mod_kernel(c_ref, w_ref, b_ref, o_ref):
    s = jax.nn.silu(c_ref[...])
    o_ref[...] = jnp.dot(s.astype(BF16), w_ref[...].astype(BF16),
                         preferred_element_type=F32) + b_ref[...]


def _modulation(c_pad, w_ada, b_ada):
    n = w_ada.shape[1]
    tn = 1024
    return pl.pallas_call(
        _mod_kernel,
        grid=(n // tn,),
        in_specs=[pl.BlockSpec((SUBLANES, D_MODEL), lambda j: (0, 0)),
                  pl.BlockSpec((D_MODEL, tn), lambda j: (0, j)),
                  pl.BlockSpec((1, tn), lambda j: (0, j))],
        out_specs=pl.BlockSpec((SUBLANES, tn), lambda j: (0, j)),
        out_shape=jax.ShapeDtypeStruct((SUBLANES, n), F32),
        name="adaln_mod",
    )(c_pad, w_ada, b_ada)


def _w_in_prep_kernel(wt_ref, wdt_ref, o_ref, odt_ref):
    o_ref[...] = wt_ref[...].T.astype(BF16)

    @pl.when(pl.program_id(0) == 0)
    def _():
        w_dt = wdt_ref[...].T[:, 0:N_HEADS]
        pad = jnp.zeros((w_dt.shape[0], LANES - DT_REP * N_HEADS), F32)
        odt_ref[...] = jnp.concatenate([w_dt] * DT_REP + [pad], axis=1)


def _w_in_prep(w_in_t):
    cols = 256
    k = w_in_t.shape[1]
    first_after_dt = COL_POOL // cols

    def src_row(r):
        return (r * (cols // N_HEADS) + jnp.where(r >= first_after_dt, 1, 0)) * N_HEADS

    return pl.pallas_call(
        _w_in_prep_kernel,
        grid=(PROJ_COLS // cols,),
        in_specs=[pl.BlockSpec((pl.Element(cols), pl.Element(k)), lambda r: (src_row(r), 0)),
                  pl.BlockSpec((LANES, k), lambda r: (COL_POOL // LANES, 0))],
        out_specs=[pl.BlockSpec((k, cols), lambda r: (0, r)),
                   pl.BlockSpec((k, LANES), lambda r: (0, 0))],
        out_shape=[jax.ShapeDtypeStruct((k, PROJ_COLS), BF16),
                   jax.ShapeDtypeStruct((k, LANES), F32)],
        name="w_in_prep",
    )(w_in_t, w_in_t)


def _conv_silu_store(r, cw_ref, cb_ref, halo_ref, o_ref, width):
    n_chunks = r.shape[0] // CHUNK
    pair = 2 * SUBLANES
    for c0 in range(0, width, CONV_COL_BLOCK):
        cols = slice(c0, c0 + CONV_COL_BLOCK)
        slabs = _slabs_of(r[:, cols], n_chunks)
        prev_last = {j: halo_ref[j - 1, :, cols] for j in range(1, CONV_K)}
        for j in range(1, CONV_K):
            halo_ref[j - 1, :, cols] = slabs[n_chunks - 1][SLABS - j]
        wrapped = _wrapped_slabs(slabs, prev_last, CONV_K - 1)
        taps = [cw_ref[k:k + 1, cols] for k in range(CONV_K)]
        bias = cb_ref[:, cols]
        for c in range(n_chunks):
            for v0 in range(0, SLABS, 2):
                accs = []
                for v in (v0, v0 + 1):
                    acc = slabs[c][v] * taps[CONV_K - 1] + bias
                    for k in range(1, CONV_K):
                        acc = acc + _shifted(slabs, wrapped, c, v, k) * taps[CONV_K - 1 - k]
                    accs.append(acc)
                lo = (c * SLABS + v0) * SUBLANES
                o_ref[lo:lo + pair, cols] = jax.nn.silu(jnp.concatenate(accs, axis=0)).astype(BF16)


def _inproj_kernel(x_ref, nw_ref, mod_ref, w_ref, wdt_ref, cw_ref, cb_ref, perm_ref, o_ref, dt_ref,
                   h_scr, halo_scr, *, tiles_per_batch):
    i = pl.program_id(0)
    j = pl.program_id(1)

    @pl.when(j == 0)
    def _():
        mod = mod_ref[0]
        shift = mod[:, 0:D_MODEL]
        scale = mod[:, D_MODEL:2 * D_MODEL]
        h = (_rms(x_ref[...]) * nw_ref[...]) * (1.0 + scale) + shift
        hb = _permute_chunks(perm_ref[...], h.astype(BF16))
        h_scr[...] = hb
        dt_ref[...] = jnp.dot(hb, wdt_ref[...].astype(BF16), preferred_element_type=F32)

    @pl.when(jnp.logical_and(i % tiles_per_batch == 0, j == 0))
    def _():
        halo_scr[...] = jnp.zeros_like(halo_scr)

    def proj():
        part = h_scr.shape[0] // PROJ_ROW_SPLIT
        return jnp.concatenate(
            [jnp.dot(h_scr[m0:m0 + part, :], w_ref[...], preferred_element_type=F32)
             for m0 in range(0, h_scr.shape[0], part)], axis=0)

    @pl.when(j == TILE_Z)
    def _():
        o_ref[...] = jax.nn.silu(proj()).astype(BF16)

    @pl.when(j == TILE_XS)
    def _():
        _conv_silu_store(proj(), cw_ref, cb_ref, halo_scr.at[0], o_ref, PROJ_TN)

    @pl.when(j == TILE_BC_POOL)
    def _():
        r = proj()
        _conv_silu_store(r[:, 0:BC_WIDTH], cw_ref, cb_ref, halo_scr.at[1], o_ref, BC_WIDTH)
        o_ref[:, BC_WIDTH:] = r[:, BC_WIDTH:].astype(BF16)

    @pl.when(j == TILE_GATE)
    def _():
        o_ref[...] = jax.nn.sigmoid(proj()).astype(BF16)


def _in_projection(x2d, norm_w, mod3, w_main, w_dt, conv_w_cols, conv_b_cols, perm, seq):
    t = x2d.shape[0]
    tm, tn = 1024, PROJ_TN
    tiles_per_batch = seq // tm
    return pl.pallas_call(
        functools.partial(_inproj_kernel, tiles_per_batch=tiles_per_batch),
        grid=(t // tm, PROJ_COLS // tn),
        in_specs=[pl.BlockSpec((tm, D_MODEL), lambda i, j: (i, 0)),
                  pl.BlockSpec((1, D_MODEL), lambda i, j: (0, 0)),
                  pl.BlockSpec((1, 1, N_MOD * D_MODEL), lambda i, j: (i // tiles_per_batch, 0, 0)),
                  pl.BlockSpec((D_MODEL, tn), lambda i, j: (0, j)),
                  pl.BlockSpec((D_MODEL, LANES), lambda i, j: (0, 0)),
                  pl.BlockSpec((CONV_K, tn), lambda i, j: (0, j)),
                  pl.BlockSpec((1, tn), lambda i, j: (0, j)),
                  pl.BlockSpec((CHUNK, CHUNK), lambda i, j: (0, 0))],
        out_specs=[pl.BlockSpec((tm, tn), lambda i, j: (i, j)),
                   pl.BlockSpec((tm, LANES), lambda i, j: (i, 0))],
        out_shape=[jax.ShapeDtypeStruct((t, PROJ_COLS), BF16),
                   jax.ShapeDtypeStruct((t, LANES), F32)],
        scratch_shapes=[pltpu.VMEM((tm, D_MODEL), BF16),
                        pltpu.VMEM((2, CONV_K - 1, SUBLANES, tn), F32)],
        compiler_params=pltpu.CompilerParams(
            dimension_semantics=("arbitrary", "arbitrary"), vmem_limit_bytes=VMEM_LIMIT),
        name="norm_in_proj",
    )(x2d, norm_w, mod3, w_main, w_dt, conv_w_cols, conv_b_cols, perm)


def _split3(v):
    hi = v.astype(BF16).astype(F32)
    r1 = v - hi
    mid = r1.astype(BF16).astype(F32)
    return hi, r1, r1 - mid


def _lane_pieces(v, lane):
    hi, r1, r2 = _split3(v)
    return jnp.where(lane < N_HEADS, hi, jnp.where(lane < 2 * N_HEADS, r1, r2)).astype(BF16)


def _ssd_prologue(xs_ref, dt_ref, dtb_ref, alog_ref, tri_ref, exp_ref, xdb, expd, acs2_scr, src_scr):
    q = CHUNK
    tm = dt_ref.shape[0]
    n_chunks = tm // q
    dt = jax.nn.softplus(dt_ref[...] + dtb_ref[...])
    dta = dt * (-jnp.exp(alog_ref[...]))
    hi, r1, r2 = _split3(dta)
    acs = []
    for c in range(n_chunks):
        rows = slice(c * q, (c + 1) * q)
        stacked = jnp.concatenate([hi[rows], r1[rows], r2[rows]], axis=0).astype(BF16)
        acs.append(jnp.dot(tri_ref[...], stacked, preferred_element_type=F32))
    acs2 = jnp.concatenate(acs, axis=0) * LOG2_E
    acs2_last = jnp.concatenate(
        [jnp.broadcast_to(acs2[(c + 1) * q - 1:(c + 1) * q, :], (q, LANES)) for c in range(n_chunks)], axis=0)
    decay_to_end = jnp.exp2(acs2_last - acs2)
    decay_from_start = jnp.exp2(acs2)
    acs2_scr[...] = acs2
    src = acs2 - jnp.log2(dt)
    for c in range(n_chunks):
        src_scr[c] = src[c * q:(c + 1) * q, :].T

    lane = lax.broadcasted_iota(jnp.int32, (tm, LANES), 1)
    pieces = jnp.concatenate([_lane_pieces(dt * decay_to_end, lane),
                              _lane_pieces(decay_from_start, lane)], axis=0)
    expd[...] = jnp.dot(pieces, exp_ref[...], preferred_element_type=F32)
    xdb[...] = (xs_ref[...].astype(F32) * expd[0:tm, :]).astype(BF16)


def _ssd_chunk(ci, xs_ref, bc_ref, dskip_ref, o_ref, state, xdb, expd, acs2_scr, src_scr):
    q = CHUNK
    tm = xs_ref.shape[0]
    rows = slice(ci * q, (ci + 1) * q)
    off_rows = slice(tm + ci * q, tm + (ci + 1) * q)
    acs2 = acs2_scr[rows, :]
    src_t = src_scr[ci]

    causal = (_token_of_row(lax.broadcasted_iota(jnp.int32, (q, q), 0))
              >= _token_of_row(lax.broadcasted_iota(jnp.int32, (q, q), 1)))
    half = lax.broadcasted_iota(jnp.int32, (q, LANES), 1) < HEAD_DIM
    zero_b = jnp.zeros((q, LANES), BF16)

    for g in range(N_GROUPS):
        gs = slice(g * GROUP_CH, (g + 1) * GROUP_CH)
        bg = bc_ref[rows, g * D_STATE:(g + 1) * D_STATE]
        cg = bc_ref[rows, (N_GROUPS + g) * D_STATE:(N_GROUPS + g + 1) * D_STATE]
        scores = lax.dot_general(cg, bg, (((1,), (1,)), ((), ())), preferred_element_type=F32)
        st_old = state[:, gs]
        y_off = jnp.dot(cg, st_old.astype(BF16), preferred_element_type=F32)
        new_t = lax.dot_general(bg, xdb[rows, gs], (((0,), (0,)), ((), ())),
                                preferred_element_type=F32)
        state[:, gs] = st_old * expd[off_rows.stop - 1:off_rows.stop, gs] + new_t
        ys = []
        for pr in range(GROUP_CH // LANES):
            h0 = g * (GROUP_CH // HEAD_DIM) + 2 * pr
            ms = []
            for h in (h0, h0 + 1):
                a_col = jnp.broadcast_to(acs2[:, h:h + 1], (q, q))
                a_row = jnp.broadcast_to(src_t[h:h + 1, :], (q, q))
                decay_dt = jnp.exp2(jnp.where(causal, a_col - a_row, -jnp.inf))
                ms.append((scores * decay_dt).astype(BF16))
            xp = xs_ref[rows, g * GROUP_CH + pr * LANES:g * GROUP_CH + (pr + 1) * LANES]
            rhs = jnp.concatenate([jnp.where(half, xp, zero_b), jnp.where(half, zero_b, xp)], axis=0)
            ys.append(jnp.dot(jnp.concatenate(ms, axis=1), rhs, preferred_element_type=F32))
        y = jnp.concatenate(ys, axis=1) + y_off * expd[off_rows, gs]
        o_ref[rows, gs] = (y + dskip_ref[:, gs] * xs_ref[rows, gs].astype(F32)).astype(BF16)


def _ssd_kernel(xs_ref, bc_ref, dt_ref, dtb_ref, alog_ref, dskip_ref, tri_ref, exp_ref, *rest, n_cast):
    cast_in = rest[:n_cast]
    o_ref = rest[n_cast]
    cast_out = rest[n_cast + 1:2 * n_cast + 1]
    state, xdb, expd, acs2_scr, src_scr = rest[2 * n_cast + 1:]

    @pl.when(pl.program_id(1) == 0)
    def _():
        state[...] = jnp.zeros_like(state)

    for w_ref, wb_ref in zip(cast_in, cast_out):
        wb_ref[...] = w_ref[...].astype(BF16)

    _ssd_prologue(xs_ref, dt_ref, dtb_ref, alog_ref, tri_ref, exp_ref, xdb, expd, acs2_scr, src_scr)
    for ci in range(SSD_CHUNKS_PER_STEP):
        _ssd_chunk(ci, xs_ref, bc_ref, dskip_ref, o_ref, state, xdb, expd, acs2_scr, src_scr)


def _ssd_scan(proj, dt_raw, dt_bias3, a_log3, d_skip_ch, tri3, expand3, cast_weights, batch, seq):
    q = CHUNK
    tm = SSD_CHUNKS_PER_STEP * q
    steps = seq // tm
    t = batch * seq
    n_steps = batch * steps
    rowmap = lambda b, c: b * steps + c
    const = lambda b, c: (0, 0)
    slab_specs = [pl.BlockSpec((w.shape[0] // n_steps, w.shape[1]), lambda b, c: (rowmap(b, c), 0))
                  for w in cast_weights]
    outs = pl.pallas_call(
        functools.partial(_ssd_kernel, n_cast=len(cast_weights)),
        grid=(batch, steps),
        in_specs=[pl.BlockSpec((tm, D_INNER), lambda b, c: (rowmap(b, c), COL_XS // D_INNER)),
                  pl.BlockSpec((tm, BC_WIDTH), lambda b, c: (rowmap(b, c), COL_BC // BC_WIDTH)),
                  pl.BlockSpec((tm, LANES), lambda b, c: (rowmap(b, c), 0)),
                  pl.BlockSpec((1, LANES), const),
                  pl.BlockSpec((1, LANES), const),
                  pl.BlockSpec((1, D_INNER), const),
                  pl.BlockSpec((q, DT_REP * q), const),
                  pl.BlockSpec((LANES, D_INNER), const)] + slab_specs,
        out_specs=[pl.BlockSpec((tm, D_INNER), lambda b, c: (rowmap(b, c), 0))] + slab_specs,
        out_shape=[jax.ShapeDtypeStruct((t, D_INNER), BF16)]
        + [jax.ShapeDtypeStruct(w.shape, BF16) for w in cast_weights],
        scratch_shapes=[pltpu.VMEM((D_STATE, D_INNER), F32),
                        pltpu.VMEM((tm, D_INNER), BF16),
                        pltpu.VMEM((2 * tm, D_INNER), F32),
                        pltpu.VMEM((tm, LANES), F32),
                        pltpu.VMEM((SSD_CHUNKS_PER_STEP, LANES, q), F32)],
        compiler_params=pltpu.CompilerParams(
            dimension_semantics=("arbitrary", "arbitrary"), vmem_limit_bytes=VMEM_LIMIT),
        name="ssd_scan",
    )(proj, proj, dt_raw, dt_bias3, a_log3, d_skip_ch, tri3, expand3, *cast_weights)
    return outs[0], outs[1:]


def _pooled_groups(u, halo_ref, pos0):
    n_chunks = u.shape[0] // CHUNK
    max_shift = max(POOL_WINDOWS) - 1
    slabs = _slabs_of(u, n_chunks)
    prev_last = {j: halo_ref[SLABS - j] for j in range(1, max_shift + 1)}
    for v in range(SLABS):
        halo_ref[v] = slabs[n_chunks - 1][v]
    wrapped = _wrapped_slabs(slabs, prev_last, max_shift)
    sub = lax.broadcasted_iota(jnp.int32, (SUBLANES, POOL_GW), 0)
    pooled = []
    for gi, win in enumerate(POOL_WINDOWS):
        cs = slice(gi * POOL_GW, (gi + 1) * POOL_GW)
        out = []
        for c in range(n_chunks):
            for v in range(SLABS):
                cur = slabs[c][v][:, cs]
                tot = cur
                for k in range(1, win):
                    tot = tot + _shifted(slabs, wrapped, c, v, k)[:, cs]
                pos = pos0 + c * CHUNK + sub * SLABS + v
                count = jnp.minimum(pos + 1, win).astype(F32)
                out.append(tot / count - cur)
        pooled.append(jnp.concatenate(out, axis=0).astype(BF16))
    return pooled


def _tail_kernel(x_ref, y_ref, zs_ref, u_ref, g_ref, mod_ref, snw_ref, pw_ref, ps_ref, wbs_ref, wbp_ref,
                 wo_ref, nmlp_ref, wup_ref, wdn_ref, nfin_ref, unperm_ref, o_ref, uhalo,
                 *, tm, tiles_per_batch):
    i = pl.program_id(0)

    @pl.when(i % tiles_per_batch == 0)
    def _():
        uhalo[...] = jnp.zeros_like(uhalo)

    y_gated = []
    for g in range(N_GROUPS):
        gs = slice(g * GROUP_CH, (g + 1) * GROUP_CH)
        yg = y_ref[:, gs].astype(F32) * zs_ref[:, gs].astype(F32)
        y_gated.append((_rms(yg) * snw_ref[:, gs]).astype(BF16))
    y_ssd = jnp.dot(jnp.concatenate(y_gated, axis=1), wbs_ref[...], preferred_element_type=F32)

    pooled = _pooled_groups(u_ref[...].astype(F32), uhalo, (i % tiles_per_batch) * tm)
    y_pool_in = jnp.concatenate(
        [jnp.dot(pooled[gi], pw_ref[gi], preferred_element_type=F32) for gi in range(len(POOL_WINDOWS))],
        axis=1) * ps_ref[...]
    y_pool = jnp.dot(y_pool_in.astype(BF16), wbp_ref[...], preferred_element_type=F32)

    merged = (g_ref[:, :D_MODEL].astype(F32) * y_ssd + g_ref[:, D_MODEL:].astype(F32) * y_pool)
    merged = _permute_chunks(unperm_ref[...], merged.astype(BF16))
    mix = jnp.dot(merged, wo_ref[...], preferred_element_type=F32)

    mod = mod_ref[0]
    gate_m = mod[:, 2 * D_MODEL:3 * D_MODEL]
    shift_f = mod[:, 3 * D_MODEL:4 * D_MODEL]
    scale_f = mod[:, 4 * D_MODEL:5 * D_MODEL]
    gate_f = mod[:, 5 * D_MODEL:6 * D_MODEL]
    x1 = x_ref[...] + gate_m * mix
    h = (_rms(x1) * nmlp_ref[...]) * (1.0 + scale_f) + shift_f
    up = jnp.dot(h.astype(BF16), wup_ref[...], preferred_element_type=F32)
    act = jnp.square(jnp.maximum(up, 0.0)).astype(BF16)
    x2 = x1 + gate_f * jnp.dot(act, wdn_ref[...], preferred_element_type=F32)
    o_ref[...] = _rms(x2) * nfin_ref[...]


def _tail(x2d, y_raw, proj, mod3, ssd_norm_w, pool_w, pool_scale, w_bs, w_bp, w_out, norm_mlp_w, w_up,
          w_dn, norm_final_w, unperm, seq):
    t = x2d.shape[0]
    tm = 512
    tiles_per_batch = seq // tm
    const2 = lambda i: (0, 0)
    resident = functools.partial(pl.BlockSpec, pipeline_mode=pl.Buffered(1))
    return pl.pallas_call(
        functools.partial(_tail_kernel, tm=tm, tiles_per_batch=tiles_per_batch),
        grid=(t // tm,),
        in_specs=[pl.BlockSpec((tm, D_MODEL), lambda i: (i, 0)),
                  pl.BlockSpec((tm, D_INNER), lambda i: (i, 0)),
                  pl.BlockSpec((tm, D_INNER), lambda i: (i, COL_Z // D_INNER)),
                  pl.BlockSpec((tm, D_MODEL), lambda i: (i, COL_POOL // D_MODEL)),
                  pl.BlockSpec((tm, 2 * D_MODEL), lambda i: (i, COL_GATE // (2 * D_MODEL))),
                  pl.BlockSpec((1, 1, N_MOD * D_MODEL), lambda i: (i // tiles_per_batch, 0, 0)),
                  resident((1, D_INNER), const2),
                  resident((len(POOL_WINDOWS), POOL_GW, POOL_GW), lambda i: (0, 0, 0)),
                  resident((1, D_MODEL), const2),
                  resident((D_INNER, D_MODEL), const2),
                  resident((D_MODEL, D_MODEL), const2),
                  resident((D_MODEL, D_MODEL), const2),
                  resident((1, D_MODEL), const2),
                  resident((D_MODEL, D_FF), const2),
                  resident((D_FF, D_MODEL), const2),
                  resident((1, D_MODEL), const2),
                  resident((CHUNK, CHUNK), const2)],
        out_specs=pl.BlockSpec((tm, D_MODEL), lambda i: (i, 0)),
        out_shape=jax.ShapeDtypeStruct((t, D_MODEL), F32),
        scratch_shapes=[pltpu.VMEM((SLABS, SUBLANES, D_MODEL), F32)],
        compiler_params=pltpu.CompilerParams(
            dimension_semantics=("arbitrary",), vmem_limit_bytes=VMEM_LIMIT),
        name="gate_pool_merge_mlp",
    )(x2d, y_raw, proj, proj, proj, mod3, ssd_norm_w, pool_w, pool_scale, w_bs, w_bp, w_out, norm_mlp_w,
      w_up, w_dn, norm_final_w, unperm)


def _cumsum_matrix():
    tok = _token_of_row(np.arange(CHUNK))
    tri = (tok[None, :] <= tok[:, None]).astype(np.float32)
    return jnp.asarray(np.concatenate([tri] * DT_REP, axis=1), BF16)


def _interleave_matrix():
    p = np.zeros((CHUNK, CHUNK), np.float32)
    p[np.arange(CHUNK), _token_of_row(np.arange(CHUNK))] = 1.0
    return jnp.asarray(p, BF16), jnp.asarray(p.T, BF16)


def _head_expand_matrix():
    m = np.zeros((LANES, D_INNER), np.float32)
    for k in range(DT_REP * N_HEADS):
        h = k % N_HEADS
        m[k, h * HEAD_DIM:(h + 1) * HEAD_DIM] = 1.0
    return jnp.asarray(m, BF16)


def _rep_heads(v):
    pad = jnp.zeros((LANES - DT_REP * N_HEADS,), v.dtype)
    return jnp.concatenate([v] * DT_REP + [pad])[None, :]


def _on_proj_cols(v):
    return jnp.pad(v, ((0, 0), (COL_XS, PROJ_COLS - COL_POOL)))


def kernel(x, c, w_ada, b_ada, norm_mix_w, w_in, conv_w, conv_b, dt_bias, a_log, d_skip, ssd_norm_w,
           w_branch_ssd, pool_w, pool_scale, w_branch_pool, w_out, norm_mlp_w, w_up, w_down,
           norm_final_w):
    batch, seq, d = x.shape
    depth = w_ada.shape[0]
    assert depth == 1, "the final norm is fused into the last kernel of a single layer"
    t = batch * seq
    tri3 = _cumsum_matrix()
    expand3 = _head_expand_matrix()
    perm, unperm = _interleave_matrix()
    c_pad = jnp.pad(c, ((0, SUBLANES - batch), (0, 0)))
    x2d = x.reshape(t, d)
    i = 0
    mod = _modulation(c_pad, w_ada[i], b_ada[i][None, :])[:batch]
    mod3 = mod[:, None, :]
    w_in_t = jnp.swapaxes(w_in[i], 0, 1)
    w_main, w_dt3 = _w_in_prep(w_in_t)
    proj, dt_raw = _in_projection(x2d, norm_mix_w[i][None, :], mod3, w_main, w_dt3,
                                  _on_proj_cols(conv_w[i]), _on_proj_cols(conv_b[i][None, :]), perm, seq)
    n_pool = len(POOL_WINDOWS)
    tail_weights = [pool_w[i].reshape(n_pool * POOL_GW, POOL_GW), w_branch_ssd[i], w_branch_pool[i],
                    w_out[i], w_up[i], w_down[i]]
    y_raw, (pool_wb, w_bs, w_bp, w_ob, w_upb, w_dnb) = _ssd_scan(
        proj, dt_raw, _rep_heads(dt_bias[i]), _rep_heads(a_log[i]),
        jnp.repeat(d_skip[i], HEAD_DIM)[None, :], tri3, expand3, tail_weights, batch, seq)
    out = _tail(x2d, y_raw, proj, mod3, ssd_norm_w[i][None, :], pool_wb.reshape(n_pool, POOL_GW, POOL_GW),
                pool_scale[i][None, :], w_bs, w_bp, w_ob, norm_mlp_w[i][None, :], w_upb, w_dnb,
                norm_final_w[None, :], unperm, seq)
    return out.reshape(batch, seq, d)
```

```python
import functools

import jax
import jax.numpy as jnp
import numpy as np
from jax import lax
from jax.experimental import pallas as pl
from jax.experimental.pallas import tpu as pltpu

F32 = jnp.float32
BF16 = jnp.bfloat16

D_MODEL = 1024
D_INNER = 2 * D_MODEL
HEAD_DIM = 64
N_HEADS = D_INNER // HEAD_DIM
N_GROUPS = 4
GROUP_CH = D_INNER // N_GROUPS
D_STATE = 128
CONV_K = 4
CHUNK = 128
BC_WIDTH = 2 * N_GROUPS * D_STATE
POOL_WINDOWS = (2, 4, 8, 16)
POOL_GW = D_MODEL // len(POOL_WINDOWS)
D_FF = 4 * D_MODEL
N_MOD = 6
EPS = 1e-5
LOG2_E = 1.4426950408889634
LANES = 128
SUBLANES = 8
SLABS = CHUNK // SUBLANES
DT_REP = 3

PROJ_TN = 2048
COL_Z = 0
COL_XS = D_INNER
COL_BC = 2 * D_INNER
COL_POOL = 2 * D_INNER + BC_WIDTH
COL_GATE = COL_POOL + D_MODEL
PROJ_COLS = COL_GATE + 2 * D_MODEL
TILE_Z, TILE_XS, TILE_BC_POOL, TILE_GATE = 0, 1, 2, 3
CONV_COL_BLOCK = 512
PROJ_ROW_SPLIT = 4
TAIL_ROW_SPLIT = 1

VMEM_LIMIT = 56 * 1024 * 1024
SSD_CHUNKS_PER_STEP = 4


def _rms(x):
    return x * lax.rsqrt(jnp.mean(x * x, axis=-1, keepdims=True) + EPS)


def _dot_row_parts(x, w, parts):
    rows = x.shape[0] // parts
    return jnp.concatenate(
        [jnp.dot(x[m0:m0 + rows, :], w, preferred_element_type=F32) for m0 in range(0, x.shape[0], rows)],
        axis=0)


def _token_of_row(r):
    return (r % SUBLANES) * SLABS + r // SUBLANES


def _permute_chunks(perm, xb):
    n_chunks = xb.shape[0] // CHUNK
    return jnp.concatenate(
        [jnp.dot(perm, xb[c * CHUNK:(c + 1) * CHUNK, :], preferred_element_type=F32).astype(BF16)
         for c in range(n_chunks)], axis=0)


def _slabs_of(x, n_chunks):
    return [[x[(c * SLABS + v) * SUBLANES:(c * SLABS + v + 1) * SUBLANES, :] for v in range(SLABS)]
            for c in range(n_chunks)]


def _wrapped_slabs(slabs, prev_last, max_shift):
    n_chunks = len(slabs)
    width = slabs[0][0].shape[1]
    first_row = lax.broadcasted_iota(jnp.int32, (SUBLANES, width), 0) == 0
    wrapped = {}
    for j in range(1, max_shift + 1):
        stacked = jnp.concatenate([slabs[c][SLABS - j] for c in range(n_chunks)], axis=0)
        rolled = pltpu.roll(stacked, 1, 0)
        head = jnp.where(first_row, pltpu.roll(prev_last[j], 1, 0), rolled[0:SUBLANES, :])
        wrapped[j] = [head] + [rolled[c * SUBLANES:(c + 1) * SUBLANES, :] for c in range(1, n_chunks)]
    return wrapped


def _shifted(slabs, wrapped, c, v, k):
    return slabs[c][v - k] if v >= k else wrapped[k - v][c]


def _mod_kernel(c_ref, w_ref, b_ref, o_ref):
    s = jax.nn.silu(c_ref[...])
    o_ref[...] = jnp.dot(s.astype(BF16), w_ref[...].astype(BF16),
                         preferred_element_type=F32) + b_ref[...]


def _modulation(c_pad, w_ada, b_ada):
    n = w_ada.shape[1]
    tn = 1024
    return pl.pallas_call(
        _mod_kernel,
        grid=(n // tn,),
        in_specs=[pl.BlockSpec((SUBLANES, D_MODEL), lambda j: (0, 0)),
                  pl.BlockSpec((D_MODEL, tn), lambda j: (0, j)),
                  pl.BlockSpec((1, tn), lambda j: (0, j))],
        out_specs=pl.BlockSpec((SUBLANES, tn), lambda j: (0, j)),
        out_shape=jax.ShapeDtypeStruct((SUBLANES, n), F32),
        name="adaln_mod",
    )(c_pad, w_ada, b_ada)


def _w_in_prep_kernel(wt_ref, wdt_ref, o_ref, odt_ref):
    o_ref[...] = wt_ref[...].T.astype(BF16)

    @pl.when(pl.program_id(0) == 0)
    def _():
        w_dt = wdt_ref[...].T[:, 0:N_HEADS]
        pad = jnp.zeros((w_dt.shape[0], LANES - DT_REP * N_HEADS), F32)
        odt_ref[...] = jnp.concatenate([w_dt] * DT_REP + [pad], axis=1)


def _w_in_prep(w_in_t):
    cols = 1024
    k = w_in_t.shape[1]
    first_after_dt = COL_POOL // cols

    def src_row(r):
        return (r * (cols // N_HEADS) + jnp.where(r >= first_after_dt, 1, 0)) * N_HEADS

    return pl.pallas_call(
        _w_in_prep_kernel,
        grid=(PROJ_COLS // cols,),
        in_specs=[pl.BlockSpec((pl.Element(cols), pl.Element(k)), lambda r: (src_row(r), 0)),
                  pl.BlockSpec((LANES, k), lambda r: (COL_POOL // LANES, 0))],
        out_specs=[pl.BlockSpec((None, k, cols), lambda r: (r // (PROJ_TN // cols), 0, r % (PROJ_TN // cols))),
                   pl.BlockSpec((k, LANES), lambda r: (0, 0))],
        out_shape=[jax.ShapeDtypeStruct((PROJ_COLS // PROJ_TN, k, PROJ_TN), BF16),
                   jax.ShapeDtypeStruct((k, LANES), F32)],
        name="w_in_prep",
    )(w_in_t, w_in_t)


def _conv_silu_store(r, cw_ref, cb_ref, halo_ref, o_ref, width):
    n_chunks = r.shape[0] // CHUNK
    pair = 2 * SUBLANES
    for c0 in range(0, width, CONV_COL_BLOCK):
        cols = slice(c0, c0 + CONV_COL_BLOCK)
        slabs = _slabs_of(r[:, cols], n_chunks)
        prev_last = {j: halo_ref[j - 1, :, cols] for j in range(1, CONV_K)}
        for j in range(1, CONV_K):
            halo_ref[j - 1, :, cols] = slabs[n_chunks - 1][SLABS - j]
        wrapped = _wrapped_slabs(slabs, prev_last, CONV_K - 1)
        taps = [cw_ref[k:k + 1, cols] for k in range(CONV_K)]
        bias = cb_ref[:, cols]
        for c in range(n_chunks):
            for v0 in range(0, SLABS, 2):
                accs = []
                for v in (v0, v0 + 1):
                    acc = slabs[c][v] * taps[CONV_K - 1] + bias
                    for k in range(1, CONV_K):
                        acc = acc + _shifted(slabs, wrapped, c, v, k) * taps[CONV_K - 1 - k]
                    accs.append(acc)
                lo = (c * SLABS + v0) * SUBLANES
                o_ref[lo:lo + pair, cols] = jax.nn.silu(jnp.concatenate(accs, axis=0)).astype(BF16)


def _inproj_kernel(x_ref, nw_ref, mod_ref, w_ref, wdt_ref, cw_ref, cb_ref, perm_ref, o_ref, dt_ref,
                   h_scr, halo_scr, *, tiles_per_batch):
    i = pl.program_id(0)
    j = pl.program_id(1)

    @pl.when(j == 0)
    def _():
        mod = mod_ref[0]
        shift = mod[:, 0:D_MODEL]
        scale = mod[:, D_MODEL:2 * D_MODEL]
        h = (_rms(x_ref[...]) * nw_ref[...]) * (1.0 + scale) + shift
        hb = _permute_chunks(perm_ref[...], h.astype(BF16))
        h_scr[...] = hb
        dt_ref[...] = jnp.dot(hb, wdt_ref[...].astype(BF16), preferred_element_type=F32)

    @pl.when(jnp.logical_and(i % tiles_per_batch == 0, j == 0))
    def _():
        halo_scr[...] = jnp.zeros_like(halo_scr)

    def proj():
        return _dot_row_parts(h_scr[...], w_ref[...], PROJ_ROW_SPLIT)

    @pl.when(j == TILE_Z)
    def _():
        o_ref[...] = jax.nn.silu(proj()).astype(BF16)

    @pl.when(j == TILE_XS)
    def _():
        _conv_silu_store(proj(), cw_ref, cb_ref, halo_scr.at[0], o_ref, PROJ_TN)

    @pl.when(j == TILE_BC_POOL)
    def _():
        r = proj()
        _conv_silu_store(r[:, 0:BC_WIDTH], cw_ref, cb_ref, halo_scr.at[1], o_ref, BC_WIDTH)
        o_ref[:, BC_WIDTH:] = r[:, BC_WIDTH:].astype(BF16)

    @pl.when(j == TILE_GATE)
    def _():
        o_ref[...] = jax.nn.sigmoid(proj()).astype(BF16)


def _in_projection(x2d, norm_w, mod3, w_main, w_dt, conv_w_cols, conv_b_cols, perm, seq):
    t = x2d.shape[0]
    tm, tn = 1024, PROJ_TN
    tiles_per_batch = seq // tm
    return pl.pallas_call(
        functools.partial(_inproj_kernel, tiles_per_batch=tiles_per_batch),
        grid=(t // tm, PROJ_COLS // tn),
        in_specs=[pl.BlockSpec((tm, D_MODEL), lambda i, j: (i, 0)),
                  pl.BlockSpec((1, D_MODEL), lambda i, j: (0, 0)),
                  pl.BlockSpec((1, 1, N_MOD * D_MODEL), lambda i, j: (i // tiles_per_batch, 0, 0)),
                  pl.BlockSpec((None, D_MODEL, tn), lambda i, j: (j, 0, 0)),
                  pl.BlockSpec((D_MODEL, LANES), lambda i, j: (0, 0)),
                  pl.BlockSpec((CONV_K, tn), lambda i, j: (0, j)),
                  pl.BlockSpec((1, tn), lambda i, j: (0, j)),
                  pl.BlockSpec((CHUNK, CHUNK), lambda i, j: (0, 0))],
        out_specs=[pl.BlockSpec((None, tm, tn), lambda i, j: (j, i, 0)),
                   pl.BlockSpec((tm, LANES), lambda i, j: (i, 0))],
        out_shape=[jax.ShapeDtypeStruct((PROJ_COLS // tn, t, tn), BF16),
                   jax.ShapeDtypeStruct((t, LANES), F32)],
        scratch_shapes=[pltpu.VMEM((tm, D_MODEL), BF16),
                        pltpu.VMEM((2, CONV_K - 1, SUBLANES, tn), F32)],
        compiler_params=pltpu.CompilerParams(
            dimension_semantics=("arbitrary", "arbitrary"), vmem_limit_bytes=VMEM_LIMIT),
        name="norm_in_proj",
    )(x2d, norm_w, mod3, w_main, w_dt, conv_w_cols, conv_b_cols, perm)


def _split3(v):
    hi = v.astype(BF16).astype(F32)
    r1 = v - hi
    mid = r1.astype(BF16).astype(F32)
    return hi, r1, r1 - mid


def _lane_pieces(v, lane):
    hi, r1, r2 = _split3(v)
    return jnp.where(lane < N_HEADS, hi, jnp.where(lane < 2 * N_HEADS, r1, r2)).astype(BF16)


def _ssd_prologue(xs_ref, dt_ref, dtb_ref, alog_ref, tri_ref, exp_ref, xdb, expd, acs2_scr, src_scr):
    q = CHUNK
    tm = dt_ref.shape[0]
    n_chunks = tm // q
    dt = jax.nn.softplus(dt_ref[...] + dtb_ref[...])
    dta = dt * (-jnp.exp(alog_ref[...]))
    hi, r1, r2 = _split3(dta)
    acs = []
    for c in range(n_chunks):
        rows = slice(c * q, (c + 1) * q)
        stacked = jnp.concatenate([hi[rows], r1[rows], r2[rows]], axis=0).astype(BF16)
        acs.append(jnp.dot(tri_ref[...], stacked, preferred_element_type=F32))
    acs2 = jnp.concatenate(acs, axis=0) * LOG2_E
    acs2_last = jnp.concatenate(
        [jnp.broadcast_to(acs2[(c + 1) * q - 1:(c + 1) * q, :], (q, LANES)) for c in range(n_chunks)], axis=0)
    decay_to_end = jnp.exp2(acs2_last - acs2)
    decay_from_start = jnp.exp2(acs2)
    acs2_scr[...] = acs2
    src = acs2 - jnp.log2(dt)
    for c in range(n_chunks):
        src_scr[c] = src[c * q:(c + 1) * q, :].T

    lane = lax.broadcasted_iota(jnp.int32, (tm, LANES), 1)
    pieces = jnp.concatenate([_lane_pieces(dt * decay_to_end, lane),
                              _lane_pieces(decay_from_start, lane)], axis=0)
    expd[...] = jnp.dot(pieces, exp_ref[...], preferred_element_type=F32)
    xdb[...] = (xs_ref[...].astype(F32) * expd[0:tm, :]).astype(BF16)


def _ssd_chunk(ci, xs_ref, bc_ref, dskip_ref, o_ref, state, xdb, expd, acs2_scr, src_scr):
    q = CHUNK
    tm = xs_ref.shape[0]
    rows = slice(ci * q, (ci + 1) * q)
    off_rows = slice(tm + ci * q, tm + (ci + 1) * q)
    acs2 = acs2_scr[rows, :]
    src_t = src_scr[ci]

    causal = (_token_of_row(lax.broadcasted_iota(jnp.int32, (q, q), 0))
              >= _token_of_row(lax.broadcasted_iota(jnp.int32, (q, q), 1)))
    half = lax.broadcasted_iota(jnp.int32, (q, LANES), 1) < HEAD_DIM
    zero_b = jnp.zeros((q, LANES), BF16)

    for g in range(N_GROUPS):
        gs = slice(g * GROUP_CH, (g + 1) * GROUP_CH)
        bg = bc_ref[rows, g * D_STATE:(g + 1) * D_STATE]
        cg = bc_ref[rows, (N_GROUPS + g) * D_STATE:(N_GROUPS + g + 1) * D_STATE]
        scores = lax.dot_general(cg, bg, (((1,), (1,)), ((), ())), preferred_element_type=F32)
        st_old = state[:, gs]
        y_off = jnp.dot(cg, st_old.astype(BF16), preferred_element_type=F32)
        new_t = lax.dot_general(bg, xdb[rows, gs], (((0,), (0,)), ((), ())),
                                preferred_element_type=F32)
        state[:, gs] = st_old * expd[off_rows.stop - 1:off_rows.stop, gs] + new_t
        ys = []
        for pr in range(GROUP_CH // LANES):
            h0 = g * (GROUP_CH // HEAD_DIM) + 2 * pr
            ms = []
            for h in (h0, h0 + 1):
                a_col = jnp.broadcast_to(acs2[:, h:h + 1], (q, q))
                a_row = jnp.broadcast_to(src_t[h:h + 1, :], (q, q))
                decay_dt = jnp.exp2(jnp.where(causal, a_col - a_row, -jnp.inf))
                ms.append((scores * decay_dt).astype(BF16))
            xp = xs_ref[rows, g * GROUP_CH + pr * LANES:g * GROUP_CH + (pr + 1) * LANES]
            rhs = jnp.concatenate([jnp.where(half, xp, zero_b), jnp.where(half, zero_b, xp)], axis=0)
            ys.append(jnp.dot(jnp.concatenate(ms, axis=1), rhs, preferred_element_type=F32))
        y = jnp.concatenate(ys, axis=1) + y_off * expd[off_rows, gs]
        o_ref[rows, gs] = (y + dskip_ref[:, gs] * xs_ref[rows, gs].astype(F32)).astype(BF16)


def _ssd_kernel(xs_ref, bc_ref, dt_ref, dtb_ref, alog_ref, dskip_ref, tri_ref, exp_ref, *rest, n_cast):
    cast_in = rest[:n_cast]
    o_ref = rest[n_cast]
    cast_out = rest[n_cast + 1:2 * n_cast + 1]
    state, xdb, expd, acs2_scr, src_scr = rest[2 * n_cast + 1:]

    @pl.when(pl.program_id(1) == 0)
    def _():
        state[...] = jnp.zeros_like(state)

    for w_ref, wb_ref in zip(cast_in, cast_out):
        wb_ref[...] = w_ref[...].astype(BF16)

    _ssd_prologue(xs_ref, dt_ref, dtb_ref, alog_ref, tri_ref, exp_ref, xdb, expd, acs2_scr, src_scr)
    for ci in range(SSD_CHUNKS_PER_STEP):
        _ssd_chunk(ci, xs_ref, bc_ref, dskip_ref, o_ref, state, xdb, expd, acs2_scr, src_scr)


def _ssd_scan(proj, dt_raw, dt_bias3, a_log3, d_skip_ch, tri3, expand3, cast_weights, batch, seq):
    q = CHUNK
    tm = SSD_CHUNKS_PER_STEP * q
    steps = seq // tm
    t = batch * seq
    n_steps = batch * steps
    rowmap = lambda b, c: b * steps + c
    const = lambda b, c: (0, 0)
    slab_specs = [pl.BlockSpec((w.shape[0] // n_steps, w.shape[1]), lambda b, c: (rowmap(b, c), 0))
                  for w in cast_weights]
    outs = pl.pallas_call(
        functools.partial(_ssd_kernel, n_cast=len(cast_weights)),
        grid=(batch, steps),
        in_specs=[pl.BlockSpec((None, tm, D_INNER), lambda b, c: (TILE_XS, rowmap(b, c), 0)),
                  pl.BlockSpec((None, tm, BC_WIDTH), lambda b, c: (TILE_BC_POOL, rowmap(b, c), 0)),
                  pl.BlockSpec((tm, LANES), lambda b, c: (rowmap(b, c), 0)),
                  pl.BlockSpec((1, LANES), const),
                  pl.BlockSpec((1, LANES), const),
                  pl.BlockSpec((1, D_INNER), const),
                  pl.BlockSpec((q, DT_REP * q), const),
                  pl.BlockSpec((LANES, D_INNER), const)] + slab_specs,
        out_specs=[pl.BlockSpec((tm, D_INNER), lambda b, c: (rowmap(b, c), 0))] + slab_specs,
        out_shape=[jax.ShapeDtypeStruct((t, D_INNER), BF16)]
        + [jax.ShapeDtypeStruct(w.shape, BF16) for w in cast_weights],
        scratch_shapes=[pltpu.VMEM((D_STATE, D_INNER), F32),
                        pltpu.VMEM((tm, D_INNER), BF16),
                        pltpu.VMEM((2 * tm, D_INNER), F32),
                        pltpu.VMEM((tm, LANES), F32),
                        pltpu.VMEM((SSD_CHUNKS_PER_STEP, LANES, q), F32)],
        compiler_params=pltpu.CompilerParams(
            dimension_semantics=("arbitrary", "arbitrary"), vmem_limit_bytes=VMEM_LIMIT),
        name="ssd_scan",
    )(proj, proj, dt_raw, dt_bias3, a_log3, d_skip_ch, tri3, expand3, *cast_weights)
    return outs[0], outs[1:]


def _pooled_groups(u, halo_ref, pos0):
    n_chunks = u.shape[0] // CHUNK
    max_shift = max(POOL_WINDOWS) - 1
    slabs = _slabs_of(u, n_chunks)
    prev_last = {j: halo_ref[SLABS - j] for j in range(1, max_shift + 1)}
    for v in range(SLABS):
        halo_ref[v] = slabs[n_chunks - 1][v]
    wrapped = _wrapped_slabs(slabs, prev_last, max_shift)
    sub = lax.broadcasted_iota(jnp.int32, (SUBLANES, POOL_GW), 0)
    pooled = []
    for gi, win in enumerate(POOL_WINDOWS):
        cs = slice(gi * POOL_GW, (gi + 1) * POOL_GW)
        out = []
        for c in range(n_chunks):
            for v in range(SLABS):
                cur = slabs[c][v][:, cs]
                tot = cur
                for k in range(1, win):
                    tot = tot + _shifted(slabs, wrapped, c, v, k)[:, cs]
                pos = pos0 + c * CHUNK + sub * SLABS + v
                count = jnp.minimum(pos + 1, win).astype(F32)
                out.append(tot / count - cur)
        pooled.append(jnp.concatenate(out, axis=0).astype(BF16))
    return pooled


def _tail_kernel(x_ref, y_ref, zs_ref, u_ref, g_ref, mod_ref, snw_ref, pw_ref, ps_ref, wbs_ref, wbp_ref,
                 wo_ref, nmlp_ref, wup_ref, wdn_ref, nfin_ref, unperm_ref, o_ref, uhalo,
                 *, tm, tiles_per_batch):
    i = pl.program_id(0)

    @pl.when(i % tiles_per_batch == 0)
    def _():
        uhalo[...] = jnp.zeros_like(uhalo)

    y_gated = []
    for g in range(N_GROUPS):
        gs = slice(g * GROUP_CH, (g + 1) * GROUP_CH)
        yg = y_ref[:, gs].astype(F32) * zs_ref[:, gs].astype(F32)
        y_gated.append((_rms(yg) * snw_ref[:, gs]).astype(BF16))
    y_ssd = _dot_row_parts(jnp.concatenate(y_gated, axis=1), wbs_ref[...], TAIL_ROW_SPLIT)

    pooled = _pooled_groups(u_ref[...].astype(F32), uhalo, (i % tiles_per_batch) * tm)
    y_pool_in = jnp.concatenate(
        [jnp.dot(pooled[gi], pw_ref[gi], preferred_element_type=F32) for gi in range(len(POOL_WINDOWS))],
        axis=1) * ps_ref[...]
    y_pool = _dot_row_parts(y_pool_in.astype(BF16), wbp_ref[...], TAIL_ROW_SPLIT)

    merged = (g_ref[:, :D_MODEL].astype(F32) * y_ssd + g_ref[:, D_MODEL:].astype(F32) * y_pool)
    merged = _permute_chunks(unperm_ref[...], merged.astype(BF16))
    mix = _dot_row_parts(merged, wo_ref[...], TAIL_ROW_SPLIT)

    mod = mod_ref[0]
    gate_m = mod[:, 2 * D_MODEL:3 * D_MODEL]
    shift_f = mod[:, 3 * D_MODEL:4 * D_MODEL]
    scale_f = mod[:, 4 * D_MODEL:5 * D_MODEL]
    gate_f = mod[:, 5 * D_MODEL:6 * D_MODEL]
    x1 = x_ref[...] + gate_m * mix
    h = (_rms(x1) * nmlp_ref[...]) * (1.0 + scale_f) + shift_f
    up = _dot_row_parts(h.astype(BF16), wup_ref[...], TAIL_ROW_SPLIT)
    act = jnp.square(jnp.maximum(up, 0.0)).astype(BF16)
    x2 = x1 + gate_f * _dot_row_parts(act, wdn_ref[...], TAIL_ROW_SPLIT)
    o_ref[...] = _rms(x2) * nfin_ref[...]


def _tail(x2d, y_raw, proj, mod3, ssd_norm_w, pool_w, pool_scale, w_bs, w_bp, w_out, norm_mlp_w, w_up,
          w_dn, norm_final_w, unperm, seq):
    t = x2d.shape[0]
    tm = 512
    tiles_per_batch = seq // tm
    const2 = lambda i: (0, 0)
    resident = functools.partial(pl.BlockSpec, pipeline_mode=pl.Buffered(1))
    return pl.pallas_call(
        functools.partial(_tail_kernel, tm=tm, tiles_per_batch=tiles_per_batch),
        grid=(t // tm,),
        in_specs=[pl.BlockSpec((tm, D_MODEL), lambda i: (i, 0)),
                  pl.BlockSpec((tm, D_INNER), lambda i: (i, 0)),
                  pl.BlockSpec((None, tm, D_INNER), lambda i: (TILE_Z, i, 0)),
                  pl.BlockSpec((None, tm, D_MODEL), lambda i: (TILE_BC_POOL, i, BC_WIDTH // D_MODEL)),
                  pl.BlockSpec((None, tm, 2 * D_MODEL), lambda i: (TILE_GATE, i, 0)),
                  pl.BlockSpec((1, 1, N_MOD * D_MODEL), lambda i: (i // tiles_per_batch, 0, 0)),
                  resident((1, D_INNER), const2),
                  resident((len(POOL_WINDOWS), POOL_GW, POOL_GW), lambda i: (0, 0, 0)),
                  resident((1, D_MODEL), const2),
                  resident((D_INNER, D_MODEL), const2),
                  resident((D_MODEL, D_MODEL), const2),
                  resident((D_MODEL, D_MODEL), const2),
                  resident((1, D_MODEL), const2),
                  resident((D_MODEL, D_FF), const2),
                  resident((D_FF, D_MODEL), const2),
                  resident((1, D_MODEL), const2),
                  resident((CHUNK, CHUNK), const2)],
        out_specs=pl.BlockSpec((tm, D_MODEL), lambda i: (i, 0)),
        out_shape=jax.ShapeDtypeStruct((t, D_MODEL), F32),
        scratch_shapes=[pltpu.VMEM((SLABS, SUBLANES, D_MODEL), F32)],
        compiler_params=pltpu.CompilerParams(
            dimension_semantics=("arbitrary",), vmem_limit_bytes=VMEM_LIMIT),
        name="gate_pool_merge_mlp",
    )(x2d, y_raw, proj, proj, proj, mod3, ssd_norm_w, pool_w, pool_scale, w_bs, w_bp, w_out, norm_mlp_w,
      w_up, w_dn, norm_final_w, unperm)


def _cumsum_matrix():
    tok = _token_of_row(np.arange(CHUNK))
    tri = (tok[None, :] <= tok[:, None]).astype(np.float32)
    return jnp.asarray(np.concatenate([tri] * DT_REP, axis=1), BF16)


def _interleave_matrix():
    p = np.zeros((CHUNK, CHUNK), np.float32)
    p[np.arange(CHUNK), _token_of_row(np.arange(CHUNK))] = 1.0
    return jnp.asarray(p, BF16), jnp.asarray(p.T, BF16)


def _head_expand_matrix():
    m = np.zeros((LANES, D_INNER), np.float32)
    for k in range(DT_REP * N_HEADS):
        h = k % N_HEADS
        m[k, h * HEAD_DIM:(h + 1) * HEAD_DIM] = 1.0
    return jnp.asarray(m, BF16)


def _rep_heads(v):
    pad = jnp.zeros((LANES - DT_REP * N_HEADS,), v.dtype)
    return jnp.concatenate([v] * DT_REP + [pad])[None, :]


def _on_proj_cols(v):
    return jnp.pad(v, ((0, 0), (COL_XS, PROJ_COLS - COL_POOL)))


def kernel(x, c, w_ada, b_ada, norm_mix_w, w_in, conv_w, conv_b, dt_bias, a_log, d_skip, ssd_norm_w,
           w_branch_ssd, pool_w, pool_scale, w_branch_pool, w_out, norm_mlp_w, w_up, w_down,
           norm_final_w):
    batch, seq, d = x.shape
    depth = w_ada.shape[0]
    assert depth == 1, "the final norm is fused into the last kernel of a single layer"
    t = batch * seq
    tri3 = _cumsum_matrix()
    expand3 = _head_expand_matrix()
    perm, unperm = _interleave_matrix()
    c_pad = jnp.pad(c, ((0, SUBLANES - batch), (0, 0)))
    x2d = x.reshape(t, d)
    i = 0
    mod = _modulation(c_pad, w_ada[i], b_ada[i][None, :])[:batch]
    mod3 = mod[:, None, :]
    w_in_t = jnp.swapaxes(w_in[i], 0, 1)
    w_main, w_dt3 = _w_in_prep(w_in_t)
    proj, dt_raw = _in_projection(x2d, norm_mix_w[i][None, :], mod3, w_main, w_dt3,
                                  _on_proj_cols(conv_w[i]), _on_proj_cols(conv_b[i][None, :]), perm, seq)
    n_pool = len(POOL_WINDOWS)
    tail_weights = [pool_w[i].reshape(n_pool * POOL_GW, POOL_GW), w_branch_ssd[i], w_branch_pool[i],
                    w_out[i], w_up[i], w_down[i]]
    y_raw, (pool_wb, w_bs, w_bp, w_ob, w_upb, w_dnb) = _ssd_scan(
        proj, dt_raw, _rep_heads(dt_bias[i]), _rep_heads(a_log[i]),
        jnp.repeat(d_skip[i], HEAD_DIM)[None, :], tri3, expand3, tail_weights, batch, seq)
    out = _tail(x2d, y_raw, proj, mod3, ssd_norm_w[i][None, :], pool_wb.reshape(n_pool, POOL_GW, POOL_GW),
                pool_scale[i][None, :], w_bs, w_bp, w_ob, norm_mlp_w[i][None, :], w_upb, w_dnb,
                norm_final_w[None, :], unperm, seq)
    return out.reshape(batch, seq, d)
```

```python
import functools

import jax
import jax.numpy as jnp
import numpy as np
from jax import lax
from jax.experimental import pallas as pl
from jax.experimental.pallas import tpu as pltpu

F32 = jnp.float32
BF16 = jnp.bfloat16

D_MODEL = 1024
D_INNER = 2 * D_MODEL
HEAD_DIM = 64
N_HEADS = D_INNER // HEAD_DIM
N_GROUPS = 4
GROUP_CH = D_INNER // N_GROUPS
D_STATE = 128
CONV_K = 4
CHUNK = 128
BC_WIDTH = 2 * N_GROUPS * D_STATE
POOL_WINDOWS = (2, 4, 8, 16)
POOL_GW = D_MODEL // len(POOL_WINDOWS)
D_FF = 4 * D_MODEL
N_MOD = 6
EPS = 1e-5
LOG2_E = 1.4426950408889634
LANES = 128
SUBLANES = 8
SLABS = CHUNK // SUBLANES
DT_REP = 3

PROJ_TN = 2048
COL_XS = D_INNER
COL_POOL = 2 * D_INNER + BC_WIDTH
PROJ_COLS = COL_POOL + 3 * D_MODEL
TILE_Z, TILE_XS, TILE_BC_POOL, TILE_GATE = 0, 1, 2, 3
CONV_COL_BLOCK = 512
PROJ_ROW_SPLIT = 4

VMEM_LIMIT = 56 * 1024 * 1024
SSD_CHUNKS_PER_STEP = 4


def _rms(x):
    return x * lax.rsqrt(jnp.mean(x * x, axis=-1, keepdims=True) + EPS)


def _dot_row_parts(x, w, parts):
    rows = x.shape[0] // parts
    return jnp.concatenate(
        [jnp.dot(x[m0:m0 + rows, :], w, preferred_element_type=F32) for m0 in range(0, x.shape[0], rows)],
        axis=0)


def _token_of_row(r):
    return (r % SUBLANES) * SLABS + r // SUBLANES


def _permute_chunks(perm, xb):
    n_chunks = xb.shape[0] // CHUNK
    return jnp.concatenate(
        [jnp.dot(perm, xb[c * CHUNK:(c + 1) * CHUNK, :], preferred_element_type=F32).astype(BF16)
         for c in range(n_chunks)], axis=0)


def _slabs_of(x, n_chunks):
    return [[x[(c * SLABS + v) * SUBLANES:(c * SLABS + v + 1) * SUBLANES, :] for v in range(SLABS)]
            for c in range(n_chunks)]


def _wrapped_slabs(slabs, prev_last, max_shift):
    n_chunks = len(slabs)
    width = slabs[0][0].shape[1]
    first_row = lax.broadcasted_iota(jnp.int32, (SUBLANES, width), 0) == 0
    wrapped = {}
    for j in range(1, max_shift + 1):
        stacked = jnp.concatenate([slabs[c][SLABS - j] for c in range(n_chunks)], axis=0)
        rolled = pltpu.roll(stacked, 1, 0)
        head = jnp.where(first_row, pltpu.roll(prev_last[j], 1, 0), rolled[0:SUBLANES, :])
        wrapped[j] = [head] + [rolled[c * SUBLANES:(c + 1) * SUBLANES, :] for c in range(1, n_chunks)]
    return wrapped


def _shifted(slabs, wrapped, c, v, k):
    return slabs[c][v - k] if v >= k else wrapped[k - v][c]


def _mod_kernel(c_ref, w_ref, b_ref, o_ref):
    s = jax.nn.silu(c_ref[...])
    o_ref[...] = jnp.dot(s.astype(BF16), w_ref[...].astype(BF16),
                         preferred_element_type=F32) + b_ref[...]


def _modulation(c_pad, w_ada, b_ada):
    n = w_ada.shape[1]
    tn = 1024
    return pl.pallas_call(
        _mod_kernel,
        grid=(n // tn,),
        in_specs=[pl.BlockSpec((SUBLANES, D_MODEL), lambda j: (0, 0)),
                  pl.BlockSpec((D_MODEL, tn), lambda j: (0, j)),
                  pl.BlockSpec((1, tn), lambda j: (0, j))],
        out_specs=pl.BlockSpec((SUBLANES, tn), lambda j: (0, j)),
        out_shape=jax.ShapeDtypeStruct((SUBLANES, n), F32),
        name="adaln_mod",
    )(c_pad, w_ada, b_ada)


def _w_in_prep_kernel(wt_ref, wdt_ref, o_ref, odt_ref):
    o_ref[...] = wt_ref[...].T.astype(BF16)

    @pl.when(pl.program_id(0) == 0)
    def _():
        w_dt = wdt_ref[...].T[:, 0:N_HEADS]
        pad = jnp.zeros((w_dt.shape[0], LANES - DT_REP * N_HEADS), F32)
        odt_ref[...] = jnp.concatenate([w_dt] * DT_REP + [pad], axis=1)


def _w_in_prep(w_in_t):
    cols = 1024
    k = w_in_t.shape[1]
    first_after_dt = COL_POOL // cols

    def src_row(r):
        return (r * (cols // N_HEADS) + jnp.where(r >= first_after_dt, 1, 0)) * N_HEADS

    return pl.pallas_call(
        _w_in_prep_kernel,
        grid=(PROJ_COLS // cols,),
        in_specs=[pl.BlockSpec((pl.Element(cols), pl.Element(k)), lambda r: (src_row(r), 0)),
                  pl.BlockSpec((LANES, k), lambda r: (COL_POOL // LANES, 0))],
        out_specs=[pl.BlockSpec((None, k, cols), lambda r: (r // (PROJ_TN // cols), 0, r % (PROJ_TN // cols))),
                   pl.BlockSpec((k, LANES), lambda r: (0, 0))],
        out_shape=[jax.ShapeDtypeStruct((PROJ_COLS // PROJ_TN, k, PROJ_TN), BF16),
                   jax.ShapeDtypeStruct((k, LANES), F32)],
        name="w_in_prep",
    )(w_in_t, w_in_t)


def _conv_silu_store(r, cw_ref, cb_ref, halo_ref, o_ref, width):
    n_chunks = r.shape[0] // CHUNK
    pair = 2 * SUBLANES
    for c0 in range(0, width, CONV_COL_BLOCK):
        cols = slice(c0, c0 + CONV_COL_BLOCK)
        slabs = _slabs_of(r[:, cols], n_chunks)
        prev_last = {j: halo_ref[j - 1, :, cols] for j in range(1, CONV_K)}
        for j in range(1, CONV_K):
            halo_ref[j - 1, :, cols] = slabs[n_chunks - 1][SLABS - j]
        wrapped = _wrapped_slabs(slabs, prev_last, CONV_K - 1)
        taps = [cw_ref[k:k + 1, cols] for k in range(CONV_K)]
        bias = cb_ref[:, cols]
        for c in range(n_chunks):
            for v0 in range(0, SLABS, 2):
                accs = []
                for v in (v0, v0 + 1):
                    acc = slabs[c][v] * taps[CONV_K - 1] + bias
                    for k in range(1, CONV_K):
                        acc = acc + _shifted(slabs, wrapped, c, v, k) * taps[CONV_K - 1 - k]
                    accs.append(acc)
                lo = (c * SLABS + v0) * SUBLANES
                o_ref[lo:lo + pair, cols] = jax.nn.silu(jnp.concatenate(accs, axis=0)).astype(BF16)


def _inproj_kernel(x_ref, nw_ref, mod_ref, w_ref, wdt_ref, cw_ref, cb_ref, perm_ref, o_ref, dt_ref,
                   h_scr, halo_scr, *, tiles_per_batch):
    i = pl.program_id(0)
    j = pl.program_id(1)

    @pl.when(j == 0)
    def _():
        mod = mod_ref[0]
        shift = mod[:, 0:D_MODEL]
        scale = mod[:, D_MODEL:2 * D_MODEL]
        h = (_rms(x_ref[...]) * nw_ref[...]) * (1.0 + scale) + shift
        hb = _permute_chunks(perm_ref[...], h.astype(BF16))
        h_scr[...] = hb
        dt_ref[...] = jnp.dot(hb, wdt_ref[...].astype(BF16), preferred_element_type=F32)

    @pl.when(jnp.logical_and(i % tiles_per_batch == 0, j == 0))
    def _():
        halo_scr[...] = jnp.zeros_like(halo_scr)

    def proj():
        return _dot_row_parts(h_scr[...], w_ref[...], PROJ_ROW_SPLIT)

    @pl.when(j == TILE_Z)
    def _():
        o_ref[...] = jax.nn.silu(proj()).astype(BF16)

    @pl.when(j == TILE_XS)
    def _():
        _conv_silu_store(proj(), cw_ref, cb_ref, halo_scr.at[0], o_ref, PROJ_TN)

    @pl.when(j == TILE_BC_POOL)
    def _():
        r = proj()
        _conv_silu_store(r[:, 0:BC_WIDTH], cw_ref, cb_ref, halo_scr.at[1], o_ref, BC_WIDTH)
        o_ref[:, BC_WIDTH:] = r[:, BC_WIDTH:].astype(BF16)

    @pl.when(j == TILE_GATE)
    def _():
        o_ref[...] = jax.nn.sigmoid(proj()).astype(BF16)


def _in_projection(x2d, norm_w, mod3, w_main, w_dt, conv_w_cols, conv_b_cols, perm, seq):
    t = x2d.shape[0]
    tm, tn = 1024, PROJ_TN
    tiles_per_batch = seq // tm
    return pl.pallas_call(
        functools.partial(_inproj_kernel, tiles_per_batch=tiles_per_batch),
        grid=(t // tm, PROJ_COLS // tn),
        in_specs=[pl.BlockSpec((tm, D_MODEL), lambda i, j: (i, 0)),
                  pl.BlockSpec((1, D_MODEL), lambda i, j: (0, 0)),
                  pl.BlockSpec((1, 1, N_MOD * D_MODEL), lambda i, j: (i // tiles_per_batch, 0, 0)),
                  pl.BlockSpec((None, D_MODEL, tn), lambda i, j: (j, 0, 0)),
                  pl.BlockSpec((D_MODEL, LANES), lambda i, j: (0, 0)),
                  pl.BlockSpec((CONV_K, tn), lambda i, j: (0, j)),
                  pl.BlockSpec((1, tn), lambda i, j: (0, j)),
                  pl.BlockSpec((CHUNK, CHUNK), lambda i, j: (0, 0))],
        out_specs=[pl.BlockSpec((None, tm, tn), lambda i, j: (j, i, 0)),
                   pl.BlockSpec((tm, LANES), lambda i, j: (i, 0))],
        out_shape=[jax.ShapeDtypeStruct((PROJ_COLS // tn, t, tn), BF16),
                   jax.ShapeDtypeStruct((t, LANES), F32)],
        scratch_shapes=[pltpu.VMEM((tm, D_MODEL), BF16),
                        pltpu.VMEM((2, CONV_K - 1, SUBLANES, tn), F32)],
        compiler_params=pltpu.CompilerParams(
            dimension_semantics=("arbitrary", "arbitrary"), vmem_limit_bytes=VMEM_LIMIT),
        name="norm_in_proj",
    )(x2d, norm_w, mod3, w_main, w_dt, conv_w_cols, conv_b_cols, perm)


def _split3(v):
    hi = v.astype(BF16).astype(F32)
    r1 = v - hi
    mid = r1.astype(BF16).astype(F32)
    return hi, r1, r1 - mid


def _lane_pieces(v, lane):
    hi, r1, r2 = _split3(v)
    return jnp.where(lane < N_HEADS, hi, jnp.where(lane < 2 * N_HEADS, r1, r2)).astype(BF16)


def _ssd_prologue(xs_ref, dt_ref, dtb_ref, alog_ref, tri_ref, exp_ref, xdb, expd, acs2_scr, src_scr):
    q = CHUNK
    tm = dt_ref.shape[0]
    n_chunks = tm // q
    dt = jax.nn.softplus(dt_ref[...] + dtb_ref[...])
    dta = dt * (-jnp.exp(alog_ref[...]))
    hi, r1, r2 = _split3(dta)
    acs = []
    for c in range(n_chunks):
        rows = slice(c * q, (c + 1) * q)
        stacked = jnp.concatenate([hi[rows], r1[rows], r2[rows]], axis=0).astype(BF16)
        acs.append(jnp.dot(tri_ref[...], stacked, preferred_element_type=F32))
    acs2 = jnp.concatenate(acs, axis=0) * LOG2_E
    acs2_last = jnp.concatenate(
        [jnp.broadcast_to(acs2[(c + 1) * q - 1:(c + 1) * q, :], (q, LANES)) for c in range(n_chunks)], axis=0)
    decay_to_end = jnp.exp2(acs2_last - acs2)
    decay_from_start = jnp.exp2(acs2)
    acs2_scr[...] = acs2
    src = acs2 - jnp.log2(dt)
    for c in range(n_chunks):
        src_scr[c] = src[c * q:(c + 1) * q, :].T

    lane = lax.broadcasted_iota(jnp.int32, (tm, LANES), 1)
    pieces = jnp.concatenate([_lane_pieces(dt * decay_to_end, lane),
                              _lane_pieces(decay_from_start, lane)], axis=0)
    expd[...] = jnp.dot(pieces, exp_ref[...], preferred_element_type=F32)
    xdb[...] = (xs_ref[...].astype(F32) * expd[0:tm, :]).astype(BF16)


def _ssd_chunk(ci, xs_ref, bc_ref, dskip_ref, o_ref, state, xdb, expd, acs2_scr, src_scr):
    q = CHUNK
    tm = xs_ref.shape[0]
    rows = slice(ci * q, (ci + 1) * q)
    off_rows = slice(tm + ci * q, tm + (ci + 1) * q)
    acs2 = acs2_scr[rows, :]
    src_t = src_scr[ci]

    causal = (_token_of_row(lax.broadcasted_iota(jnp.int32, (q, q), 0))
              >= _token_of_row(lax.broadcasted_iota(jnp.int32, (q, q), 1)))
    half = lax.broadcasted_iota(jnp.int32, (q, LANES), 1) < HEAD_DIM
    zero_b = jnp.zeros((q, LANES), BF16)

    for g in range(N_GROUPS):
        gs = slice(g * GROUP_CH, (g + 1) * GROUP_CH)
        bg = bc_ref[rows, g * D_STATE:(g + 1) * D_STATE]
        cg = bc_ref[rows, (N_GROUPS + g) * D_STATE:(N_GROUPS + g + 1) * D_STATE]
        scores = lax.dot_general(cg, bg, (((1,), (1,)), ((), ())), preferred_element_type=F32)
        st_old = state[:, gs]
        y_off = jnp.dot(cg, st_old.astype(BF16), preferred_element_type=F32)
        new_t = lax.dot_general(bg, xdb[rows, gs], (((0,), (0,)), ((), ())),
                                preferred_element_type=F32)
        state[:, gs] = st_old * expd[off_rows.stop - 1:off_rows.stop, gs] + new_t
        ys = []
        for pr in range(GROUP_CH // LANES):
            h0 = g * (GROUP_CH // HEAD_DIM) + 2 * pr
            ms = []
            for h in (h0, h0 + 1):
                a_col = jnp.broadcast_to(acs2[:, h:h + 1], (q, q))
                a_row = jnp.broadcast_to(src_t[h:h + 1, :], (q, q))
                decay_dt = jnp.exp2(jnp.where(causal, a_col - a_row, -jnp.inf))
                ms.append((scores * decay_dt).astype(BF16))
            xp = xs_ref[rows, g * GROUP_CH + pr * LANES:g * GROUP_CH + (pr + 1) * LANES]
            rhs = jnp.concatenate([jnp.where(half, xp, zero_b), jnp.where(half, zero_b, xp)], axis=0)
            ys.append(jnp.dot(jnp.concatenate(ms, axis=1), rhs, preferred_element_type=F32))
        y = jnp.concatenate(ys, axis=1) + y_off * expd[off_rows, gs]
        o_ref[rows, gs] = (y + dskip_ref[:, gs] * xs_ref[rows, gs].astype(F32)).astype(BF16)


def _ssd_kernel(xs_ref, bc_ref, dt_ref, dtb_ref, alog_ref, dskip_ref, tri_ref, exp_ref, *rest, n_cast):
    cast_in = rest[:n_cast]
    o_ref = rest[n_cast]
    cast_out = rest[n_cast + 1:2 * n_cast + 1]
    state, xdb, expd, acs2_scr, src_scr = rest[2 * n_cast + 1:]

    @pl.when(pl.program_id(1) == 0)
    def _():
        state[...] = jnp.zeros_like(state)

    for w_ref, wb_ref in zip(cast_in, cast_out):
        wb_ref[...] = w_ref[...].astype(BF16)

    _ssd_prologue(xs_ref, dt_ref, dtb_ref, alog_ref, tri_ref, exp_ref, xdb, expd, acs2_scr, src_scr)
    for ci in range(SSD_CHUNKS_PER_STEP):
        _ssd_chunk(ci, xs_ref, bc_ref, dskip_ref, o_ref, state, xdb, expd, acs2_scr, src_scr)


def _ssd_scan(proj, dt_raw, dt_bias3, a_log3, d_skip_ch, tri3, expand3, cast_weights, batch, seq):
    q = CHUNK
    tm = SSD_CHUNKS_PER_STEP * q
    steps = seq // tm
    t = batch * seq
    n_steps = batch * steps
    rowmap = lambda b, c: b * steps + c
    const = lambda b, c: (0, 0)
    slab_specs = [pl.BlockSpec((w.shape[0] // n_steps, w.shape[1]), lambda b, c: (rowmap(b, c), 0))
                  for w in cast_weights]
    outs = pl.pallas_call(
        functools.partial(_ssd_kernel, n_cast=len(cast_weights)),
        grid=(batch, steps),
        in_specs=[pl.BlockSpec((None, tm, D_INNER), lambda b, c: (TILE_XS, rowmap(b, c), 0)),
                  pl.BlockSpec((None, tm, BC_WIDTH), lambda b, c: (TILE_BC_POOL, rowmap(b, c), 0)),
                  pl.BlockSpec((tm, LANES), lambda b, c: (rowmap(b, c), 0)),
                  pl.BlockSpec((1, LANES), const),
                  pl.BlockSpec((1, LANES), const),
                  pl.BlockSpec((1, D_INNER), const),
                  pl.BlockSpec((q, DT_REP * q), const),
                  pl.BlockSpec((LANES, D_INNER), const)] + slab_specs,
        out_specs=[pl.BlockSpec((tm, D_INNER), lambda b, c: (rowmap(b, c), 0))] + slab_specs,
        out_shape=[jax.ShapeDtypeStruct((t, D_INNER), BF16)]
        + [jax.ShapeDtypeStruct(w.shape, BF16) for w in cast_weights],
        scratch_shapes=[pltpu.VMEM((D_STATE, D_INNER), F32),
                        pltpu.VMEM((tm, D_INNER), BF16),
                        pltpu.VMEM((2 * tm, D_INNER), F32),
                        pltpu.VMEM((tm, LANES), F32),
                        pltpu.VMEM((SSD_CHUNKS_PER_STEP, LANES, q), F32)],
        compiler_params=pltpu.CompilerParams(
            dimension_semantics=("arbitrary", "arbitrary"), vmem_limit_bytes=VMEM_LIMIT),
        name="ssd_scan",
    )(proj, proj, dt_raw, dt_bias3, a_log3, d_skip_ch, tri3, expand3, *cast_weights)
    return outs[0], outs[1:]


def _pooled_groups(u, halo_ref, pos0):
    n_chunks = u.shape[0] // CHUNK
    max_shift = max(POOL_WINDOWS) - 1
    slabs = _slabs_of(u, n_chunks)
    prev_last = {j: halo_ref[SLABS - j] for j in range(1, max_shift + 1)}
    for v in range(SLABS):
        halo_ref[v] = slabs[n_chunks - 1][v]
    wrapped = _wrapped_slabs(slabs, prev_last, max_shift)
    sub = lax.broadcasted_iota(jnp.int32, (SUBLANES, POOL_GW), 0)
    pooled = []
    for gi, win in enumerate(POOL_WINDOWS):
        cs = slice(gi * POOL_GW, (gi + 1) * POOL_GW)
        out = []
        for c in range(n_chunks):
            for v in range(SLABS):
                cur = slabs[c][v][:, cs]
                tot = cur
                for k in range(1, win):
                    tot = tot + _shifted(slabs, wrapped, c, v, k)[:, cs]
                pos = pos0 + c * CHUNK + sub * SLABS + v
                count = jnp.minimum(pos + 1, win).astype(F32)
                out.append(tot / count - cur)
        pooled.append(jnp.concatenate(out, axis=0).astype(BF16))
    return pooled


def _tail_kernel(x_ref, y_ref, zs_ref, u_ref, g_ref, mod_ref, snw_ref, pw_ref, ps_ref, wbs_ref, wbp_ref,
                 wo_ref, nmlp_ref, wup_ref, wdn_ref, nfin_ref, unperm_ref, o_ref, uhalo,
                 *, tm, tiles_per_batch):
    i = pl.program_id(0)

    @pl.when(i % tiles_per_batch == 0)
    def _():
        uhalo[...] = jnp.zeros_like(uhalo)

    y_gated = []
    for g in range(N_GROUPS):
        gs = slice(g * GROUP_CH, (g + 1) * GROUP_CH)
        yg = y_ref[:, gs].astype(F32) * zs_ref[:, gs].astype(F32)
        y_gated.append((_rms(yg) * snw_ref[:, gs]).astype(BF16))
    y_ssd = jnp.dot(jnp.concatenate(y_gated, axis=1), wbs_ref[...], preferred_element_type=F32)

    pooled = _pooled_groups(u_ref[...].astype(F32), uhalo, (i % tiles_per_batch) * tm)
    y_pool_in = jnp.concatenate(
        [jnp.dot(pooled[gi], pw_ref[gi], preferred_element_type=F32) for gi in range(len(POOL_WINDOWS))],
        axis=1) * ps_ref[...]
    y_pool = jnp.dot(y_pool_in.astype(BF16), wbp_ref[...], preferred_element_type=F32)

    merged = (g_ref[:, :D_MODEL].astype(F32) * y_ssd + g_ref[:, D_MODEL:].astype(F32) * y_pool)
    merged = _permute_chunks(unperm_ref[...], merged.astype(BF16))
    mix = jnp.dot(merged, wo_ref[...], preferred_element_type=F32)

    mod = mod_ref[0]
    gate_m = mod[:, 2 * D_MODEL:3 * D_MODEL]
    shift_f = mod[:, 3 * D_MODEL:4 * D_MODEL]
    scale_f = mod[:, 4 * D_MODEL:5 * D_MODEL]
    gate_f = mod[:, 5 * D_MODEL:6 * D_MODEL]
    x1 = x_ref[...] + gate_m * mix
    h = (_rms(x1) * nmlp_ref[...]) * (1.0 + scale_f) + shift_f
    up = jnp.dot(h.astype(BF16), wup_ref[...], preferred_element_type=F32)
    act = jnp.square(jnp.maximum(up, 0.0)).astype(BF16)
    x2 = x1 + gate_f * jnp.dot(act, wdn_ref[...], preferred_element_type=F32)
    o_ref[...] = _rms(x2) * nfin_ref[...]


def _tail(x2d, y_raw, proj, mod3, ssd_norm_w, pool_w, pool_scale, w_bs, w_bp, w_out, norm_mlp_w, w_up,
          w_dn, norm_final_w, unperm, seq):
    t = x2d.shape[0]
    tm = 512
    tiles_per_batch = seq // tm
    const2 = lambda i: (0, 0)
    resident = functools.partial(pl.BlockSpec, pipeline_mode=pl.Buffered(1))
    return pl.pallas_call(
        functools.partial(_tail_kernel, tm=tm, tiles_per_batch=tiles_per_batch),
        grid=(t // tm,),
        in_specs=[pl.BlockSpec((tm, D_MODEL), lambda i: (i, 0)),
                  pl.BlockSpec((tm, D_INNER), lambda i: (i, 0)),
                  pl.BlockSpec((None, tm, D_INNER), lambda i: (TILE_Z, i, 0)),
                  pl.BlockSpec((None, tm, D_MODEL), lambda i: (TILE_BC_POOL, i, BC_WIDTH // D_MODEL)),
                  pl.BlockSpec((None, tm, 2 * D_MODEL), lambda i: (TILE_GATE, i, 0)),
                  pl.BlockSpec((1, 1, N_MOD * D_MODEL), lambda i: (i // tiles_per_batch, 0, 0)),
                  resident((1, D_INNER), const2),
                  resident((len(POOL_WINDOWS), POOL_GW, POOL_GW), lambda i: (0, 0, 0)),
                  resident((1, D_MODEL), const2),
                  resident((D_INNER, D_MODEL), const2),
                  resident((D_MODEL, D_MODEL), const2),
                  resident((D_MODEL, D_MODEL), const2),
                  resident((1, D_MODEL), const2),
                  resident((D_MODEL, D_FF), const2),
                  resident((D_FF, D_MODEL), const2),
                  resident((1, D_MODEL), const2),
                  resident((CHUNK, CHUNK), const2)],
        out_specs=pl.BlockSpec((tm, D_MODEL), lambda i: (i, 0)),
        out_shape=jax.ShapeDtypeStruct((t, D_MODEL), F32),
        scratch_shapes=[pltpu.VMEM((SLABS, SUBLANES, D_MODEL), F32)],
        compiler_params=pltpu.CompilerParams(
            dimension_semantics=("arbitrary",), vmem_limit_bytes=VMEM_LIMIT),
        name="gate_pool_merge_mlp",
    )(x2d, y_raw, proj, proj, proj, mod3, ssd_norm_w, pool_w, pool_scale, w_bs, w_bp, w_out, norm_mlp_w,
      w_up, w_dn, norm_final_w, unperm)


def _cumsum_matrix():
    tok = _token_of_row(np.arange(CHUNK))
    tri = (tok[None, :] <= tok[:, None]).astype(np.float32)
    return jnp.asarray(np.concatenate([tri] * DT_REP, axis=1), BF16)


def _interleave_matrix():
    p = np.zeros((CHUNK, CHUNK), np.float32)
    p[np.arange(CHUNK), _token_of_row(np.arange(CHUNK))] = 1.0
    return jnp.asarray(p, BF16), jnp.asarray(p.T, BF16)


def _head_expand_matrix():
    m = np.zeros((LANES, D_INNER), np.float32)
    for k in range(DT_REP * N_HEADS):
        h = k % N_HEADS
        m[k, h * HEAD_DIM:(h + 1) * HEAD_DIM] = 1.0
    return jnp.asarray(m, BF16)


def _rep_heads(v):
    pad = jnp.zeros((LANES - DT_REP * N_HEADS,), v.dtype)
    return jnp.concatenate([v] * DT_REP + [pad])[None, :]


def _on_proj_cols(v):
    return jnp.pad(v, ((0, 0), (COL_XS, PROJ_COLS - COL_POOL)))


def kernel(x, c, w_ada, b_ada, norm_mix_w, w_in, conv_w, conv_b, dt_bias, a_log, d_skip, ssd_norm_w,
           w_branch_ssd, pool_w, pool_scale, w_branch_pool, w_out, norm_mlp_w, w_up, w_down,
           norm_final_w):
    batch, seq, d = x.shape
    depth = w_ada.shape[0]
    assert depth == 1, "the final norm is fused into the last kernel of a single layer"
    t = batch * seq
    tri3 = _cumsum_matrix()
    expand3 = _head_expand_matrix()
    perm, unperm = _interleave_matrix()
    c_pad = jnp.pad(c, ((0, SUBLANES - batch), (0, 0)))
    x2d = x.reshape(t, d)
    i = 0
    mod = _modulation(c_pad, w_ada[i], b_ada[i][None, :])[:batch]
    mod3 = mod[:, None, :]
    w_in_t = jnp.swapaxes(w_in[i], 0, 1)
    w_main, w_dt3 = _w_in_prep(w_in_t)
    proj, dt_raw = _in_projection(x2d, norm_mix_w[i][None, :], mod3, w_main, w_dt3,
                                  _on_proj_cols(conv_w[i]), _on_proj_cols(conv_b[i][None, :]), perm, seq)
    n_pool = len(POOL_WINDOWS)
    tail_weights = [pool_w[i].reshape(n_pool * POOL_GW, POOL_GW), w_branch_ssd[i], w_branch_pool[i],
                    w_out[i], w_up[i], w_down[i]]
    y_raw, (pool_wb, w_bs, w_bp, w_ob, w_upb, w_dnb) = _ssd_scan(
        proj, dt_raw, _rep_heads(dt_bias[i]), _rep_heads(a_log[i]),
        jnp.repeat(d_skip[i], HEAD_DIM)[None, :], tri3, expand3, tail_weights, batch, seq)
    out = _tail(x2d, y_raw, proj, mod3, ssd_norm_w[i][None, :], pool_wb.reshape(n_pool, POOL_GW, POOL_GW),
                pool_scale[i][None, :], w_bs, w_bp, w_ob, norm_mlp_w[i][None, :], w_upb, w_dnb,
                norm_final_w[None, :], unperm, seq)
    return out.reshape(batch, seq, d)
```

```python
import functools

import jax
import jax.numpy as jnp
import numpy as np
from jax import lax
from jax.experimental import pallas as pl
from jax.experimental.pallas import tpu as pltpu

F32 = jnp.float32
BF16 = jnp.bfloat16

D_MODEL = 1024
D_INNER = 2 * D_MODEL
HEAD_DIM = 64
N_HEADS = D_INNER // HEAD_DIM
N_GROUPS = 4
GROUP_CH = D_INNER // N_GROUPS
D_STATE = 128
CONV_K = 4
CHUNK = 128
BC_WIDTH = 2 * N_GROUPS * D_STATE
POOL_WINDOWS = (2, 4, 8, 16)
POOL_GW = D_MODEL // len(POOL_WINDOWS)
D_FF = 4 * D_MODEL
N_MOD = 6
EPS = 1e-5
LOG2_E = 1.4426950408889634
LANES = 128
SUBLANES = 8
SLABS = CHUNK // SUBLANES
DT_REP = 3

PROJ_TN = 2048
COL_XS = D_INNER
COL_POOL = 2 * D_INNER + BC_WIDTH
PROJ_COLS = COL_POOL + 3 * D_MODEL
TILE_Z, TILE_XS, TILE_BC_POOL, TILE_GATE = 0, 1, 2, 3
CONV_COL_BLOCK = 512
PROJ_ROW_SPLIT = 4
PROJ_ROW_SPLIT_CONV = 8

VMEM_LIMIT = 56 * 1024 * 1024
SSD_CHUNKS_PER_STEP = 4


def _rms(x):
    return x * lax.rsqrt(jnp.mean(x * x, axis=-1, keepdims=True) + EPS)


def _dot_row_parts(x, w, parts):
    rows = x.shape[0] // parts
    return jnp.concatenate(
        [jnp.dot(x[m0:m0 + rows, :], w, preferred_element_type=F32) for m0 in range(0, x.shape[0], rows)],
        axis=0)


def _token_of_row(r):
    return (r % SUBLANES) * SLABS + r // SUBLANES


def _permute_chunks(perm, xb):
    n_chunks = xb.shape[0] // CHUNK
    return jnp.concatenate(
        [jnp.dot(perm, xb[c * CHUNK:(c + 1) * CHUNK, :], preferred_element_type=F32).astype(BF16)
         for c in range(n_chunks)], axis=0)


def _slabs_of(x, n_chunks):
    return [[x[(c * SLABS + v) * SUBLANES:(c * SLABS + v + 1) * SUBLANES, :] for v in range(SLABS)]
            for c in range(n_chunks)]


def _wrapped_slabs(slabs, prev_last, max_shift):
    n_chunks = len(slabs)
    width = slabs[0][0].shape[1]
    first_row = lax.broadcasted_iota(jnp.int32, (SUBLANES, width), 0) == 0
    wrapped = {}
    for j in range(1, max_shift + 1):
        stacked = jnp.concatenate([slabs[c][SLABS - j] for c in range(n_chunks)], axis=0)
        rolled = pltpu.roll(stacked, 1, 0)
        head = jnp.where(first_row, pltpu.roll(prev_last[j], 1, 0), rolled[0:SUBLANES, :])
        wrapped[j] = [head] + [rolled[c * SUBLANES:(c + 1) * SUBLANES, :] for c in range(1, n_chunks)]
    return wrapped


def _shifted(slabs, wrapped, c, v, k):
    return slabs[c][v - k] if v >= k else wrapped[k - v][c]


def _mod_kernel(c_ref, w_ref, b_ref, o_ref):
    s = jax.nn.silu(c_ref[...])
    o_ref[...] = jnp.dot(s.astype(BF16), w_ref[...].astype(BF16),
                         preferred_element_type=F32) + b_ref[...]


def _modulation(c_pad, w_ada, b_ada):
    n = w_ada.shape[1]
    tn = 1024
    return pl.pallas_call(
        _mod_kernel,
        grid=(n // tn,),
        in_specs=[pl.BlockSpec((SUBLANES, D_MODEL), lambda j: (0, 0)),
                  pl.BlockSpec((D_MODEL, tn), lambda j: (0, j)),
                  pl.BlockSpec((1, tn), lambda j: (0, j))],
        out_specs=pl.BlockSpec((SUBLANES, tn), lambda j: (0, j)),
        out_shape=jax.ShapeDtypeStruct((SUBLANES, n), F32),
        name="adaln_mod",
    )(c_pad, w_ada, b_ada)


def _w_in_prep_kernel(wt_ref, wdt_ref, o_ref, odt_ref):
    o_ref[...] = wt_ref[...].T.astype(BF16)

    @pl.when(pl.program_id(0) == 0)
    def _():
        w_dt = wdt_ref[...].T[:, 0:N_HEADS]
        pad = jnp.zeros((w_dt.shape[0], LANES - DT_REP * N_HEADS), F32)
        odt_ref[...] = jnp.concatenate([w_dt] * DT_REP + [pad], axis=1)


def _w_in_prep(w_in_t):
    cols = 1024
    k = w_in_t.shape[1]
    first_after_dt = COL_POOL // cols

    def src_row(r):
        return (r * (cols // N_HEADS) + jnp.where(r >= first_after_dt, 1, 0)) * N_HEADS

    return pl.pallas_call(
        _w_in_prep_kernel,
        grid=(PROJ_COLS // cols,),
        in_specs=[pl.BlockSpec((pl.Element(cols), pl.Element(k)), lambda r: (src_row(r), 0)),
                  pl.BlockSpec((LANES, k), lambda r: (COL_POOL // LANES, 0))],
        out_specs=[pl.BlockSpec((None, k, cols), lambda r: (r // (PROJ_TN // cols), 0, r % (PROJ_TN // cols))),
                   pl.BlockSpec((k, LANES), lambda r: (0, 0))],
        out_shape=[jax.ShapeDtypeStruct((PROJ_COLS // PROJ_TN, k, PROJ_TN), BF16),
                   jax.ShapeDtypeStruct((k, LANES), F32)],
        name="w_in_prep",
    )(w_in_t, w_in_t)


def _conv_silu_store(r, cw_ref, cb_ref, halo_ref, o_ref, width):
    n_chunks = r.shape[0] // CHUNK
    pair = 2 * SUBLANES
    for c0 in range(0, width, CONV_COL_BLOCK):
        cols = slice(c0, c0 + CONV_COL_BLOCK)
        slabs = _slabs_of(r[:, cols], n_chunks)
        prev_last = {j: halo_ref[j - 1, :, cols] for j in range(1, CONV_K)}
        for j in range(1, CONV_K):
            halo_ref[j - 1, :, cols] = slabs[n_chunks - 1][SLABS - j]
        wrapped = _wrapped_slabs(slabs, prev_last, CONV_K - 1)
        taps = [cw_ref[k:k + 1, cols] for k in range(CONV_K)]
        bias = cb_ref[:, cols]
        for c in range(n_chunks):
            for v0 in range(0, SLABS, 2):
                accs = []
                for v in (v0, v0 + 1):
                    acc = slabs[c][v] * taps[CONV_K - 1] + bias
                    for k in range(1, CONV_K):
                        acc = acc + _shifted(slabs, wrapped, c, v, k) * taps[CONV_K - 1 - k]
                    accs.append(acc)
                lo = (c * SLABS + v0) * SUBLANES
                o_ref[lo:lo + pair, cols] = jax.nn.silu(jnp.concatenate(accs, axis=0)).astype(BF16)


def _inproj_kernel(x_ref, nw_ref, mod_ref, w_ref, wdt_ref, cw_ref, cb_ref, perm_ref, o_ref, dt_ref,
                   h_scr, halo_scr, *, tiles_per_batch):
    i = pl.program_id(0)
    j = pl.program_id(1)

    @pl.when(j == 0)
    def _():
        mod = mod_ref[0]
        shift = mod[:, 0:D_MODEL]
        scale = mod[:, D_MODEL:2 * D_MODEL]
        h = (_rms(x_ref[...]) * nw_ref[...]) * (1.0 + scale) + shift
        hb = _permute_chunks(perm_ref[...], h.astype(BF16))
        h_scr[...] = hb
        dt_ref[...] = jnp.dot(hb, wdt_ref[...].astype(BF16), preferred_element_type=F32)

    @pl.when(jnp.logical_and(i % tiles_per_batch == 0, j == 0))
    def _():
        halo_scr[...] = jnp.zeros_like(halo_scr)

    def proj(parts=PROJ_ROW_SPLIT):
        return _dot_row_parts(h_scr[...], w_ref[...], parts)

    @pl.when(j == TILE_Z)
    def _():
        o_ref[...] = jax.nn.silu(proj()).astype(BF16)

    @pl.when(j == TILE_XS)
    def _():
        _conv_silu_store(proj(PROJ_ROW_SPLIT_CONV), cw_ref, cb_ref, halo_scr.at[0], o_ref, PROJ_TN)

    @pl.when(j == TILE_BC_POOL)
    def _():
        r = proj(PROJ_ROW_SPLIT_CONV)
        _conv_silu_store(r[:, 0:BC_WIDTH], cw_ref, cb_ref, halo_scr.at[1], o_ref, BC_WIDTH)
        o_ref[:, BC_WIDTH:] = r[:, BC_WIDTH:].astype(BF16)

    @pl.when(j == TILE_GATE)
    def _():
        o_ref[...] = jax.nn.sigmoid(proj()).astype(BF16)


def _in_projection(x2d, norm_w, mod3, w_main, w_dt, conv_w_cols, conv_b_cols, perm, seq):
    t = x2d.shape[0]
    tm, tn = 1024, PROJ_TN
    tiles_per_batch = seq // tm
    return pl.pallas_call(
        functools.partial(_inproj_kernel, tiles_per_batch=tiles_per_batch),
        grid=(t // tm, PROJ_COLS // tn),
        in_specs=[pl.BlockSpec((tm, D_MODEL), lambda i, j: (i, 0)),
                  pl.BlockSpec((1, D_MODEL), lambda i, j: (0, 0)),
                  pl.BlockSpec((1, 1, N_MOD * D_MODEL), lambda i, j: (i // tiles_per_batch, 0, 0)),
                  pl.BlockSpec((None, D_MODEL, tn), lambda i, j: (j, 0, 0)),
                  pl.BlockSpec((D_MODEL, LANES), lambda i, j: (0, 0)),
                  pl.BlockSpec((CONV_K, tn), lambda i, j: (0, j)),
                  pl.BlockSpec((1, tn), lambda i, j: (0, j)),
                  pl.BlockSpec((CHUNK, CHUNK), lambda i, j: (0, 0))],
        out_specs=[pl.BlockSpec((None, tm, tn), lambda i, j: (j, i, 0)),
                   pl.BlockSpec((tm, LANES), lambda i, j: (i, 0))],
        out_shape=[jax.ShapeDtypeStruct((PROJ_COLS // tn, t, tn), BF16),
                   jax.ShapeDtypeStruct((t, LANES), F32)],
        scratch_shapes=[pltpu.VMEM((tm, D_MODEL), BF16),
                        pltpu.VMEM((2, CONV_K - 1, SUBLANES, tn), F32)],
        compiler_params=pltpu.CompilerParams(
            dimension_semantics=("arbitrary", "arbitrary"), vmem_limit_bytes=VMEM_LIMIT),
        name="norm_in_proj",
    )(x2d, norm_w, mod3, w_main, w_dt, conv_w_cols, conv_b_cols, perm)


def _split3(v):
    hi = v.astype(BF16).astype(F32)
    r1 = v - hi
    mid = r1.astype(BF16).astype(F32)
    return hi, r1, r1 - mid


def _lane_pieces(v, lane):
    hi, r1, r2 = _split3(v)
    return jnp.where(lane < N_HEADS, hi, jnp.where(lane < 2 * N_HEADS, r1, r2)).astype(BF16)


def _ssd_prologue(xs_ref, dt_ref, dtb_ref, alog_ref, tri_ref, exp_ref, xdb, expd, acs2_scr, src_scr):
    q = CHUNK
    tm = dt_ref.shape[0]
    n_chunks = tm // q
    dt = jax.nn.softplus(dt_ref[...] + dtb_ref[...])
    dta = dt * (-jnp.exp(alog_ref[...]))
    hi, r1, r2 = _split3(dta)
    acs = []
    for c in range(n_chunks):
        rows = slice(c * q, (c + 1) * q)
        stacked = jnp.concatenate([hi[rows], r1[rows], r2[rows]], axis=0).astype(BF16)
        acs.append(jnp.dot(tri_ref[...], stacked, preferred_element_type=F32))
    acs2 = jnp.concatenate(acs, axis=0) * LOG2_E
    acs2_last = jnp.concatenate(
        [jnp.broadcast_to(acs2[(c + 1) * q - 1:(c + 1) * q, :], (q, LANES)) for c in range(n_chunks)], axis=0)
    decay_to_end = jnp.exp2(acs2_last - acs2)
    decay_from_start = jnp.exp2(acs2)
    acs2_scr[...] = acs2
    src = acs2 - jnp.log2(dt)
    for c in range(n_chunks):
        src_scr[c] = src[c * q:(c + 1) * q, :].T

    lane = lax.broadcasted_iota(jnp.int32, (tm, LANES), 1)
    pieces = jnp.concatenate([_lane_pieces(dt * decay_to_end, lane),
                              _lane_pieces(decay_from_start, lane)], axis=0)
    expd[...] = jnp.dot(pieces, exp_ref[...], preferred_element_type=F32)
    xdb[...] = (xs_ref[...].astype(F32) * expd[0:tm, :]).astype(BF16)


def _ssd_chunk(ci, xs_ref, bc_ref, dskip_ref, o_ref, state, xdb, expd, acs2_scr, src_scr):
    q = CHUNK
    tm = xs_ref.shape[0]
    rows = slice(ci * q, (ci + 1) * q)
    off_rows = slice(tm + ci * q, tm + (ci + 1) * q)
    acs2 = acs2_scr[rows, :]
    src_t = src_scr[ci]

    causal = (_token_of_row(lax.broadcasted_iota(jnp.int32, (q, q), 0))
              >= _token_of_row(lax.broadcasted_iota(jnp.int32, (q, q), 1)))
    half = lax.broadcasted_iota(jnp.int32, (q, LANES), 1) < HEAD_DIM
    zero_b = jnp.zeros((q, LANES), BF16)

    for g in range(N_GROUPS):
        gs = slice(g * GROUP_CH, (g + 1) * GROUP_CH)
        bg = bc_ref[rows, g * D_STATE:(g + 1) * D_STATE]
        cg = bc_ref[rows, (N_GROUPS + g) * D_STATE:(N_GROUPS + g + 1) * D_STATE]
        scores = lax.dot_general(cg, bg, (((1,), (1,)), ((), ())), preferred_element_type=F32)
        st_old = state[:, gs]
        y_off = jnp.dot(cg, st_old.astype(BF16), preferred_element_type=F32)
        new_t = lax.dot_general(bg, xdb[rows, gs], (((0,), (0,)), ((), ())),
                                preferred_element_type=F32)
        state[:, gs] = st_old * expd[off_rows.stop - 1:off_rows.stop, gs] + new_t
        ys = []
        for pr in range(GROUP_CH // LANES):
            h0 = g * (GROUP_CH // HEAD_DIM) + 2 * pr
            ms = []
            for h in (h0, h0 + 1):
                a_col = jnp.broadcast_to(acs2[:, h:h + 1], (q, q))
                a_row = jnp.broadcast_to(src_t[h:h + 1, :], (q, q))
                decay_dt = jnp.exp2(jnp.where(causal, a_col - a_row, -jnp.inf))
                ms.append((scores * decay_dt).astype(BF16))
            xp = xs_ref[rows, g * GROUP_CH + pr * LANES:g * GROUP_CH + (pr + 1) * LANES]
            rhs = jnp.concatenate([jnp.where(half, xp, zero_b), jnp.where(half, zero_b, xp)], axis=0)
            ys.append(jnp.dot(jnp.concatenate(ms, axis=1), rhs, preferred_element_type=F32))
        y = jnp.concatenate(ys, axis=1) + y_off * expd[off_rows, gs]
        o_ref[rows, gs] = (y + dskip_ref[:, gs] * xs_ref[rows, gs].astype(F32)).astype(BF16)


def _ssd_kernel(xs_ref, bc_ref, dt_ref, dtb_ref, alog_ref, dskip_ref, tri_ref, exp_ref, *rest, n_cast):
    cast_in = rest[:n_cast]
    o_ref = rest[n_cast]
    cast_out = rest[n_cast + 1:2 * n_cast + 1]
    state, xdb, expd, acs2_scr, src_scr = rest[2 * n_cast + 1:]

    @pl.when(pl.program_id(1) == 0)
    def _():
        state[...] = jnp.zeros_like(state)

    for w_ref, wb_ref in zip(cast_in, cast_out):
        wb_ref[...] = w_ref[...].astype(BF16)

    _ssd_prologue(xs_ref, dt_ref, dtb_ref, alog_ref, tri_ref, exp_ref, xdb, expd, acs2_scr, src_scr)
    for ci in range(SSD_CHUNKS_PER_STEP):
        _ssd_chunk(ci, xs_ref, bc_ref, dskip_ref, o_ref, state, xdb, expd, acs2_scr, src_scr)


def _ssd_scan(proj, dt_raw, dt_bias3, a_log3, d_skip_ch, tri3, expand3, cast_weights, batch, seq):
    q = CHUNK
    tm = SSD_CHUNKS_PER_STEP * q
    steps = seq // tm
    t = batch * seq
    n_steps = batch * steps
    rowmap = lambda b, c: b * steps + c
    const = lambda b, c: (0, 0)
    slab_specs = [pl.BlockSpec((w.shape[0] // n_steps, w.shape[1]), lambda b, c: (rowmap(b, c), 0))
                  for w in cast_weights]
    outs = pl.pallas_call(
        functools.partial(_ssd_kernel, n_cast=len(cast_weights)),
        grid=(batch, steps),
        in_specs=[pl.BlockSpec((None, tm, D_INNER), lambda b, c: (TILE_XS, rowmap(b, c), 0)),
                  pl.BlockSpec((None, tm, BC_WIDTH), lambda b, c: (TILE_BC_POOL, rowmap(b, c), 0)),
                  pl.BlockSpec((tm, LANES), lambda b, c: (rowmap(b, c), 0)),
                  pl.BlockSpec((1, LANES), const),
                  pl.BlockSpec((1, LANES), const),
                  pl.BlockSpec((1, D_INNER), const),
                  pl.BlockSpec((q, DT_REP * q), const),
                  pl.BlockSpec((LANES, D_INNER), const)] + slab_specs,
        out_specs=[pl.BlockSpec((tm, D_INNER), lambda b, c: (rowmap(b, c), 0))] + slab_specs,
        out_shape=[jax.ShapeDtypeStruct((t, D_INNER), BF16)]
        + [jax.ShapeDtypeStruct(w.shape, BF16) for w in cast_weights],
        scratch_shapes=[pltpu.VMEM((D_STATE, D_INNER), F32),
                        pltpu.VMEM((tm, D_INNER), BF16),
                        pltpu.VMEM((2 * tm, D_INNER), F32),
                        pltpu.VMEM((tm, LANES), F32),
                        pltpu.VMEM((SSD_CHUNKS_PER_STEP, LANES, q), F32)],
        compiler_params=pltpu.CompilerParams(
            dimension_semantics=("arbitrary", "arbitrary"), vmem_limit_bytes=VMEM_LIMIT),
        name="ssd_scan",
    )(proj, proj, dt_raw, dt_bias3, a_log3, d_skip_ch, tri3, expand3, *cast_weights)
    return outs[0], outs[1:]


def _pooled_groups(u, halo_ref, pos0):
    n_chunks = u.shape[0] // CHUNK
    max_shift = max(POOL_WINDOWS) - 1
    slabs = _slabs_of(u, n_chunks)
    prev_last = {j: halo_ref[SLABS - j] for j in range(1, max_shift + 1)}
    for v in range(SLABS):
        halo_ref[v] = slabs[n_chunks - 1][v]
    wrapped = _wrapped_slabs(slabs, prev_last, max_shift)
    sub = lax.broadcasted_iota(jnp.int32, (SUBLANES, POOL_GW), 0)
    pooled = []
    for gi, win in enumerate(POOL_WINDOWS):
        cs = slice(gi * POOL_GW, (gi + 1) * POOL_GW)
        out = []
        for c in range(n_chunks):
            for v in range(SLABS):
                cur = slabs[c][v][:, cs]
                tot = cur
                for k in range(1, win):
                    tot = tot + _shifted(slabs, wrapped, c, v, k)[:, cs]
                pos = pos0 + c * CHUNK + sub * SLABS + v
                count = jnp.minimum(pos + 1, win).astype(F32)
                out.append(tot / count - cur)
        pooled.append(jnp.concatenate(out, axis=0).astype(BF16))
    return pooled


def _pool_fold_kernel(pw_ref, ps_ref, wbp_ref, o_ref):
    scaled = ps_ref[...] * wbp_ref[...]
    o_ref[...] = jnp.dot(pw_ref[...], scaled, precision=lax.Precision.HIGHEST,
                         preferred_element_type=F32).astype(BF16)


def _pool_fold(pool_w, pool_scale_col, w_bp):
    n = w_bp.shape[1]
    return pl.pallas_call(
        _pool_fold_kernel,
        grid=(len(POOL_WINDOWS),),
        in_specs=[pl.BlockSpec((None, POOL_GW, POOL_GW), lambda g: (g, 0, 0)),
                  pl.BlockSpec((POOL_GW, 1), lambda g: (g, 0)),
                  pl.BlockSpec((POOL_GW, n), lambda g: (g, 0))],
        out_specs=pl.BlockSpec((POOL_GW, n), lambda g: (g, 0)),
        out_shape=jax.ShapeDtypeStruct((D_MODEL, n), BF16),
        name="pool_weight_fold",
    )(pool_w, pool_scale_col, w_bp)


def _tail_kernel(x_ref, y_ref, zs_ref, u_ref, g_ref, mod_ref, snw_ref, wbs_ref, wpool_ref,
                 wo_ref, nmlp_ref, wup_ref, wdn_ref, nfin_ref, unperm_ref, o_ref, uhalo,
                 *, tm, tiles_per_batch):
    i = pl.program_id(0)

    @pl.when(i % tiles_per_batch == 0)
    def _():
        uhalo[...] = jnp.zeros_like(uhalo)

    y_gated = []
    for g in range(N_GROUPS):
        gs = slice(g * GROUP_CH, (g + 1) * GROUP_CH)
        yg = y_ref[:, gs].astype(F32) * zs_ref[:, gs].astype(F32)
        y_gated.append((_rms(yg) * snw_ref[:, gs]).astype(BF16))
    y_ssd = jnp.dot(jnp.concatenate(y_gated, axis=1), wbs_ref[...], preferred_element_type=F32)

    pooled = _pooled_groups(u_ref[...].astype(F32), uhalo, (i % tiles_per_batch) * tm)
    y_pool = jnp.dot(jnp.concatenate(pooled, axis=1), wpool_ref[...], preferred_element_type=F32)

    merged = (g_ref[:, :D_MODEL].astype(F32) * y_ssd + g_ref[:, D_MODEL:].astype(F32) * y_pool)
    merged = _permute_chunks(unperm_ref[...], merged.astype(BF16))
    mix = jnp.dot(merged, wo_ref[...], preferred_element_type=F32)

    mod = mod_ref[0]
    gate_m = mod[:, 2 * D_MODEL:3 * D_MODEL]
    shift_f = mod[:, 3 * D_MODEL:4 * D_MODEL]
    scale_f = mod[:, 4 * D_MODEL:5 * D_MODEL]
    gate_f = mod[:, 5 * D_MODEL:6 * D_MODEL]
    x1 = x_ref[...] + gate_m * mix
    h = (_rms(x1) * nmlp_ref[...]) * (1.0 + scale_f) + shift_f
    up = jnp.dot(h.astype(BF16), wup_ref[...], preferred_element_type=F32)
    act = jnp.square(jnp.maximum(up, 0.0)).astype(BF16)
    x2 = x1 + gate_f * jnp.dot(act, wdn_ref[...], preferred_element_type=F32)
    o_ref[...] = _rms(x2) * nfin_ref[...]


def _tail(x2d, y_raw, proj, mod3, ssd_norm_w, w_bs, w_pool, w_out, norm_mlp_w, w_up, w_dn, norm_final_w,
          unperm, seq):
    t = x2d.shape[0]
    tm = 512
    tiles_per_batch = seq // tm
    const2 = lambda i: (0, 0)
    resident = functools.partial(pl.BlockSpec, pipeline_mode=pl.Buffered(1))
    return pl.pallas_call(
        functools.partial(_tail_kernel, tm=tm, tiles_per_batch=tiles_per_batch),
        grid=(t // tm,),
        in_specs=[pl.BlockSpec((tm, D_MODEL), lambda i: (i, 0)),
                  pl.BlockSpec((tm, D_INNER), lambda i: (i, 0)),
                  pl.BlockSpec((None, tm, D_INNER), lambda i: (TILE_Z, i, 0)),
                  pl.BlockSpec((None, tm, D_MODEL), lambda i: (TILE_BC_POOL, i, BC_WIDTH // D_MODEL)),
                  pl.BlockSpec((None, tm, 2 * D_MODEL), lambda i: (TILE_GATE, i, 0)),
                  pl.BlockSpec((1, 1, N_MOD * D_MODEL), lambda i: (i // tiles_per_batch, 0, 0)),
                  resident((1, D_INNER), const2),
                  resident((D_INNER, D_MODEL), const2),
                  resident((D_MODEL, D_MODEL), const2),
                  resident((D_MODEL, D_MODEL), const2),
                  resident((1, D_MODEL), const2),
                  resident((D_MODEL, D_FF), const2),
                  resident((D_FF, D_MODEL), const2),
                  resident((1, D_MODEL), const2),
                  resident((CHUNK, CHUNK), const2)],
        out_specs=pl.BlockSpec((tm, D_MODEL), lambda i: (i, 0)),
        out_shape=jax.ShapeDtypeStruct((t, D_MODEL), F32),
        scratch_shapes=[pltpu.VMEM((SLABS, SUBLANES, D_MODEL), F32)],
        compiler_params=pltpu.CompilerParams(
            dimension_semantics=("arbitrary",), vmem_limit_bytes=VMEM_LIMIT),
        name="gate_pool_merge_mlp",
    )(x2d, y_raw, proj, proj, proj, mod3, ssd_norm_w, w_bs, w_pool, w_out, norm_mlp_w, w_up, w_dn,
      norm_final_w, unperm)


def _cumsum_matrix():
    tok = _token_of_row(np.arange(CHUNK))
    tri = (tok[None, :] <= tok[:, None]).astype(np.float32)
    return jnp.asarray(np.concatenate([tri] * DT_REP, axis=1), BF16)


def _interleave_matrix():
    p = np.zeros((CHUNK, CHUNK), np.float32)
    p[np.arange(CHUNK), _token_of_row(np.arange(CHUNK))] = 1.0
    return jnp.asarray(p, BF16), jnp.asarray(p.T, BF16)


def _head_expand_matrix():
    m = np.zeros((LANES, D_INNER), np.float32)
    for k in range(DT_REP * N_HEADS):
        h = k % N_HEADS
        m[k, h * HEAD_DIM:(h + 1) * HEAD_DIM] = 1.0
    return jnp.asarray(m, BF16)


def _rep_heads(v):
    pad = jnp.zeros((LANES - DT_REP * N_HEADS,), v.dtype)
    return jnp.concatenate([v] * DT_REP + [pad])[None, :]


def _on_proj_cols(v):
    return jnp.pad(v, ((0, 0), (COL_XS, PROJ_COLS - COL_POOL)))


def kernel(x, c, w_ada, b_ada, norm_mix_w, w_in, conv_w, conv_b, dt_bias, a_log, d_skip, ssd_norm_w,
           w_branch_ssd, pool_w, pool_scale, w_branch_pool, w_out, norm_mlp_w, w_up, w_down,
           norm_final_w):
    batch, seq, d = x.shape
    depth = w_ada.shape[0]
    assert depth == 1, "the final norm is fused into the last kernel of a single layer"
    t = batch * seq
    tri3 = _cumsum_matrix()
    expand3 = _head_expand_matrix()
    perm, unperm = _interleave_matrix()
    c_pad = jnp.pad(c, ((0, SUBLANES - batch), (0, 0)))
    x2d = x.reshape(t, d)
    i = 0
    mod = _modulation(c_pad, w_ada[i], b_ada[i][None, :])[:batch]
    mod3 = mod[:, None, :]
    w_in_t = jnp.swapaxes(w_in[i], 0, 1)
    w_main, w_dt3 = _w_in_prep(w_in_t)
    proj, dt_raw = _in_projection(x2d, norm_mix_w[i][None, :], mod3, w_main, w_dt3,
                                  _on_proj_cols(conv_w[i]), _on_proj_cols(conv_b[i][None, :]), perm, seq)
    w_pool = _pool_fold(pool_w[i], pool_scale[i][:, None], w_branch_pool[i])
    tail_weights = [w_branch_ssd[i], w_out[i], w_up[i], w_down[i]]
    y_raw, (w_bs, w_ob, w_upb, w_dnb) = _ssd_scan(
        proj, dt_raw, _rep_heads(dt_bias[i]), _rep_heads(a_log[i]),
        jnp.repeat(d_skip[i], HEAD_DIM)[None, :], tri3, expand3, tail_weights, batch, seq)
    out = _tail(x2d, y_raw, proj, mod3, ssd_norm_w[i][None, :], w_bs, w_pool, w_ob, norm_mlp_w[i][None, :],
                w_upb, w_dnb, norm_final_w[None, :], unperm, seq)
    return out.reshape(batch, seq, d)
```

```python
import functools

import jax
import jax.numpy as jnp
import numpy as np
from jax import lax
from jax.experimental import pallas as pl
from jax.experimental.pallas import tpu as pltpu

F32 = jnp.float32
BF16 = jnp.bfloat16

D_MODEL = 1024
D_INNER = 2 * D_MODEL
HEAD_DIM = 64
N_HEADS = D_INNER // HEAD_DIM
N_GROUPS = 4
GROUP_CH = D_INNER // N_GROUPS
D_STATE = 128
CONV_K = 4
CHUNK = 128
BC_WIDTH = 2 * N_GROUPS * D_STATE
POOL_WINDOWS = (2, 4, 8, 16)
POOL_GW = D_MODEL // len(POOL_WINDOWS)
D_FF = 4 * D_MODEL
N_MOD = 6
EPS = 1e-5
LOG2_E = 1.4426950408889634
LANES = 128
SUBLANES = 8
SLABS = CHUNK // SUBLANES
DT_REP = 3

PROJ_TN = 2048
COL_XS = D_INNER
COL_POOL = 2 * D_INNER + BC_WIDTH
PROJ_COLS = COL_POOL + 3 * D_MODEL
TILE_Z, TILE_XS, TILE_BC_POOL, TILE_GATE = 0, 1, 2, 3
CONV_COL_BLOCK = 512
PROJ_ROW_SPLIT = 8
PROJ_ROW_SPLIT_CONV = 4

VMEM_LIMIT = 56 * 1024 * 1024
SSD_CHUNKS_PER_STEP = 4


def _rms(x):
    return x * lax.rsqrt(jnp.mean(x * x, axis=-1, keepdims=True) + EPS)


def _dot_row_parts(x, w, parts):
    rows = x.shape[0] // parts
    return jnp.concatenate(
        [jnp.dot(x[m0:m0 + rows, :], w, preferred_element_type=F32) for m0 in range(0, x.shape[0], rows)],
        axis=0)


def _token_of_row(r):
    return (r % SUBLANES) * SLABS + r // SUBLANES


def _permute_chunks(perm, xb):
    n_chunks = xb.shape[0] // CHUNK
    return jnp.concatenate(
        [jnp.dot(perm, xb[c * CHUNK:(c + 1) * CHUNK, :], preferred_element_type=F32).astype(BF16)
         for c in range(n_chunks)], axis=0)


def _slabs_of(x, n_chunks):
    return [[x[(c * SLABS + v) * SUBLANES:(c * SLABS + v + 1) * SUBLANES, :] for v in range(SLABS)]
            for c in range(n_chunks)]


def _wrapped_slabs(slabs, prev_last, max_shift):
    n_chunks = len(slabs)
    width = slabs[0][0].shape[1]
    first_row = lax.broadcasted_iota(jnp.int32, (SUBLANES, width), 0) == 0
    wrapped = {}
    for j in range(1, max_shift + 1):
        stacked = jnp.concatenate([slabs[c][SLABS - j] for c in range(n_chunks)], axis=0)
        rolled = pltpu.roll(stacked, 1, 0)
        head = jnp.where(first_row, pltpu.roll(prev_last[j], 1, 0), rolled[0:SUBLANES, :])
        wrapped[j] = [head] + [rolled[c * SUBLANES:(c + 1) * SUBLANES, :] for c in range(1, n_chunks)]
    return wrapped


def _shifted(slabs, wrapped, c, v, k):
    return slabs[c][v - k] if v >= k else wrapped[k - v][c]


def _mod_kernel(c_ref, w_ref, b_ref, o_ref):
    s = jax.nn.silu(c_ref[...])
    o_ref[...] = jnp.dot(s.astype(BF16), w_ref[...].astype(BF16),
                         preferred_element_type=F32) + b_ref[...]


def _modulation(c_pad, w_ada, b_ada):
    n = w_ada.shape[1]
    tn = 1024
    return pl.pallas_call(
        _mod_kernel,
        grid=(n // tn,),
        in_specs=[pl.BlockSpec((SUBLANES, D_MODEL), lambda j: (0, 0)),
                  pl.BlockSpec((D_MODEL, tn), lambda j: (0, j)),
                  pl.BlockSpec((1, tn), lambda j: (0, j))],
        out_specs=pl.BlockSpec((SUBLANES, tn), lambda j: (0, j)),
        out_shape=jax.ShapeDtypeStruct((SUBLANES, n), F32),
        name="adaln_mod",
    )(c_pad, w_ada, b_ada)


def _weight_prep_kernel(wt_ref, wdt_ref, pw_ref, ps_ref, wbp_ref, o_ref, odt_ref, opool_ref):
    r = pl.program_id(0)
    o_ref[...] = wt_ref[...].T.astype(BF16)

    @pl.when(r == 0)
    def _():
        w_dt = wdt_ref[...].T[:, 0:N_HEADS]
        pad = jnp.zeros((w_dt.shape[0], LANES - DT_REP * N_HEADS), F32)
        odt_ref[...] = jnp.concatenate([w_dt] * DT_REP + [pad], axis=1)

    @pl.when(r < len(POOL_WINDOWS))
    def _():
        scaled = ps_ref[...] * wbp_ref[...]
        opool_ref[...] = jnp.dot(pw_ref[...], scaled, precision=lax.Precision.HIGHEST,
                                 preferred_element_type=F32).astype(BF16)


def _weight_prep(w_in_t, pool_w, pool_scale_col, w_bp):
    cols = 1024
    k = w_in_t.shape[1]
    n = w_bp.shape[1]
    first_after_dt = COL_POOL // cols
    last_group = len(POOL_WINDOWS) - 1
    group = lambda r: jnp.minimum(r, last_group)

    def src_row(r):
        return (r * (cols // N_HEADS) + jnp.where(r >= first_after_dt, 1, 0)) * N_HEADS

    return pl.pallas_call(
        _weight_prep_kernel,
        grid=(PROJ_COLS // cols,),
        in_specs=[pl.BlockSpec((pl.Element(cols), pl.Element(k)), lambda r: (src_row(r), 0)),
                  pl.BlockSpec((LANES, k), lambda r: (COL_POOL // LANES, 0)),
                  pl.BlockSpec((None, POOL_GW, POOL_GW), lambda r: (group(r), 0, 0)),
                  pl.BlockSpec((POOL_GW, 1), lambda r: (group(r), 0)),
                  pl.BlockSpec((POOL_GW, n), lambda r: (group(r), 0))],
        out_specs=[pl.BlockSpec((None, k, cols), lambda r: (r // (PROJ_TN // cols), 0, r % (PROJ_TN // cols))),
                   pl.BlockSpec((k, LANES), lambda r: (0, 0)),
                   pl.BlockSpec((POOL_GW, n), lambda r: (group(r), 0))],
        out_shape=[jax.ShapeDtypeStruct((PROJ_COLS // PROJ_TN, k, PROJ_TN), BF16),
                   jax.ShapeDtypeStruct((k, LANES), F32),
                   jax.ShapeDtypeStruct((D_MODEL, n), BF16)],
        name="weight_prep",
    )(w_in_t, w_in_t, pool_w, pool_scale_col, w_bp)


def _conv_silu_store(r, cw_ref, cb_ref, halo_ref, o_ref, width):
    n_chunks = r.shape[0] // CHUNK
    pair = 2 * SUBLANES
    for c0 in range(0, width, CONV_COL_BLOCK):
        cols = slice(c0, c0 + CONV_COL_BLOCK)
        slabs = _slabs_of(r[:, cols], n_chunks)
        prev_last = {j: halo_ref[j - 1, :, cols] for j in range(1, CONV_K)}
        for j in range(1, CONV_K):
            halo_ref[j - 1, :, cols] = slabs[n_chunks - 1][SLABS - j]
        wrapped = _wrapped_slabs(slabs, prev_last, CONV_K - 1)
        taps = [cw_ref[k:k + 1, cols] for k in range(CONV_K)]
        bias = cb_ref[:, cols]
        for c in range(n_chunks):
            for v0 in range(0, SLABS, 2):
                accs = []
                for v in (v0, v0 + 1):
                    acc = slabs[c][v] * taps[CONV_K - 1] + bias
                    for k in range(1, CONV_K):
                        acc = acc + _shifted(slabs, wrapped, c, v, k) * taps[CONV_K - 1 - k]
                    accs.append(acc)
                lo = (c * SLABS + v0) * SUBLANES
                o_ref[lo:lo + pair, cols] = jax.nn.silu(jnp.concatenate(accs, axis=0)).astype(BF16)


def _inproj_kernel(x_ref, nw_ref, mod_ref, w_ref, wdt_ref, cw_ref, cb_ref, perm_ref, o_ref, dt_ref,
                   h_scr, halo_scr, *, tiles_per_batch):
    i = pl.program_id(0)
    j = pl.program_id(1)

    @pl.when(j == 0)
    def _():
        mod = mod_ref[0]
        shift = mod[:, 0:D_MODEL]
        scale = mod[:, D_MODEL:2 * D_MODEL]
        h = (_rms(x_ref[...]) * nw_ref[...]) * (1.0 + scale) + shift
        hb = _permute_chunks(perm_ref[...], h.astype(BF16))
        h_scr[...] = hb
        dt_ref[...] = jnp.dot(hb, wdt_ref[...].astype(BF16), preferred_element_type=F32)

    @pl.when(jnp.logical_and(i % tiles_per_batch == 0, j == 0))
    def _():
        halo_scr[...] = jnp.zeros_like(halo_scr)

    def proj(parts=PROJ_ROW_SPLIT):
        return _dot_row_parts(h_scr[...], w_ref[...], parts)

    @pl.when(j == TILE_Z)
    def _():
        o_ref[...] = jax.nn.silu(proj()).astype(BF16)

    @pl.when(j == TILE_XS)
    def _():
        _conv_silu_store(proj(PROJ_ROW_SPLIT_CONV), cw_ref, cb_ref, halo_scr.at[0], o_ref, PROJ_TN)

    @pl.when(j == TILE_BC_POOL)
    def _():
        r = proj(PROJ_ROW_SPLIT_CONV)
        _conv_silu_store(r[:, 0:BC_WIDTH], cw_ref, cb_ref, halo_scr.at[1], o_ref, BC_WIDTH)
        o_ref[:, BC_WIDTH:] = r[:, BC_WIDTH:].astype(BF16)

    @pl.when(j == TILE_GATE)
    def _():
        o_ref[...] = jax.nn.sigmoid(proj()).astype(BF16)


def _in_projection(x2d, norm_w, mod3, w_main, w_dt, conv_w_cols, conv_b_cols, perm, seq):
    t = x2d.shape[0]
    tm, tn = 1024, PROJ_TN
    tiles_per_batch = seq // tm
    return pl.pallas_call(
        functools.partial(_inproj_kernel, tiles_per_batch=tiles_per_batch),
        grid=(t // tm, PROJ_COLS // tn),
        in_specs=[pl.BlockSpec((tm, D_MODEL), lambda i, j: (i, 0)),
                  pl.BlockSpec((1, D_MODEL), lambda i, j: (0, 0)),
                  pl.BlockSpec((1, 1, N_MOD * D_MODEL), lambda i, j: (i // tiles_per_batch, 0, 0)),
                  pl.BlockSpec((None, D_MODEL, tn), lambda i, j: (j, 0, 0)),
                  pl.BlockSpec((D_MODEL, LANES), lambda i, j: (0, 0)),
                  pl.BlockSpec((CONV_K, tn), lambda i, j: (0, j)),
                  pl.BlockSpec((1, tn), lambda i, j: (0, j)),
                  pl.BlockSpec((CHUNK, CHUNK), lambda i, j: (0, 0))],
        out_specs=[pl.BlockSpec((None, tm, tn), lambda i, j: (j, i, 0)),
                   pl.BlockSpec((tm, LANES), lambda i, j: (i, 0))],
        out_shape=[jax.ShapeDtypeStruct((PROJ_COLS // tn, t, tn), BF16),
                   jax.ShapeDtypeStruct((t, LANES), F32)],
        scratch_shapes=[pltpu.VMEM((tm, D_MODEL), BF16),
                        pltpu.VMEM((2, CONV_K - 1, SUBLANES, tn), F32)],
        compiler_params=pltpu.CompilerParams(
            dimension_semantics=("arbitrary", "arbitrary"), vmem_limit_bytes=VMEM_LIMIT),
        name="norm_in_proj",
    )(x2d, norm_w, mod3, w_main, w_dt, conv_w_cols, conv_b_cols, perm)


def _split3(v):
    hi = v.astype(BF16).astype(F32)
    r1 = v - hi
    mid = r1.astype(BF16).astype(F32)
    return hi, r1, r1 - mid


def _lane_pieces(v, lane):
    hi, r1, r2 = _split3(v)
    return jnp.where(lane < N_HEADS, hi, jnp.where(lane < 2 * N_HEADS, r1, r2)).astype(BF16)


def _ssd_prologue(xs_ref, dt_ref, dtb_ref, alog_ref, tri_ref, exp_ref, xdb, expd, acs2_scr, src_scr):
    q = CHUNK
    tm = dt_ref.shape[0]
    n_chunks = tm // q
    dt = jax.nn.softplus(dt_ref[...] + dtb_ref[...])
    dta = dt * (-jnp.exp(alog_ref[...]))
    hi, r1, r2 = _split3(dta)
    acs = []
    for c in range(n_chunks):
        rows = slice(c * q, (c + 1) * q)
        stacked = jnp.concatenate([hi[rows], r1[rows], r2[rows]], axis=0).astype(BF16)
        acs.append(jnp.dot(tri_ref[...], stacked, preferred_element_type=F32))
    acs2 = jnp.concatenate(acs, axis=0) * LOG2_E
    acs2_last = jnp.concatenate(
        [jnp.broadcast_to(acs2[(c + 1) * q - 1:(c + 1) * q, :], (q, LANES)) for c in range(n_chunks)], axis=0)
    decay_to_end = jnp.exp2(acs2_last - acs2)
    decay_from_start = jnp.exp2(acs2)
    acs2_scr[...] = acs2
    src = acs2 - jnp.log2(dt)
    for c in range(n_chunks):
        src_scr[c] = src[c * q:(c + 1) * q, :].T

    lane = lax.broadcasted_iota(jnp.int32, (tm, LANES), 1)
    pieces = jnp.concatenate([_lane_pieces(dt * decay_to_end, lane),
                              _lane_pieces(decay_from_start, lane)], axis=0)
    expd[...] = jnp.dot(pieces, exp_ref[...], preferred_element_type=F32)
    xdb[...] = (xs_ref[...].astype(F32) * expd[0:tm, :]).astype(BF16)


def _ssd_chunk(ci, xs_ref, bc_ref, dskip_ref, o_ref, state, xdb, expd, acs2_scr, src_scr):
    q = CHUNK
    tm = xs_ref.shape[0]
    rows = slice(ci * q, (ci + 1) * q)
    off_rows = slice(tm + ci * q, tm + (ci + 1) * q)
    acs2 = acs2_scr[rows, :]
    src_t = src_scr[ci]

    causal = (_token_of_row(lax.broadcasted_iota(jnp.int32, (q, q), 0))
              >= _token_of_row(lax.broadcasted_iota(jnp.int32, (q, q), 1)))
    half = lax.broadcasted_iota(jnp.int32, (q, LANES), 1) < HEAD_DIM
    zero_b = jnp.zeros((q, LANES), BF16)

    for g in range(N_GROUPS):
        gs = slice(g * GROUP_CH, (g + 1) * GROUP_CH)
        bg = bc_ref[rows, g * D_STATE:(g + 1) * D_STATE]
        cg = bc_ref[rows, (N_GROUPS + g) * D_STATE:(N_GROUPS + g + 1) * D_STATE]
        scores = lax.dot_general(cg, bg, (((1,), (1,)), ((), ())), preferred_element_type=F32)
        st_old = state[:, gs]
        y_off = jnp.dot(cg, st_old.astype(BF16), preferred_element_type=F32)
        new_t = lax.dot_general(bg, xdb[rows, gs], (((0,), (0,)), ((), ())),
                                preferred_element_type=F32)
        state[:, gs] = st_old * expd[off_rows.stop - 1:off_rows.stop, gs] + new_t
        ys = []
        for pr in range(GROUP_CH // LANES):
            h0 = g * (GROUP_CH // HEAD_DIM) + 2 * pr
            ms = []
            for h in (h0, h0 + 1):
                a_col = jnp.broadcast_to(acs2[:, h:h + 1], (q, q))
                a_row = jnp.broadcast_to(src_t[h:h + 1, :], (q, q))
                decay_dt = jnp.exp2(jnp.where(causal, a_col - a_row, -jnp.inf))
                ms.append((scores * decay_dt).astype(BF16))
            xp = xs_ref[rows, g * GROUP_CH + pr * LANES:g * GROUP_CH + (pr + 1) * LANES]
            rhs = jnp.concatenate([jnp.where(half, xp, zero_b), jnp.where(half, zero_b, xp)], axis=0)
            ys.append(jnp.dot(jnp.concatenate(ms, axis=1), rhs, preferred_element_type=F32))
        y = jnp.concatenate(ys, axis=1) + y_off * expd[off_rows, gs]
        o_ref[rows, gs] = (y + dskip_ref[:, gs] * xs_ref[rows, gs].astype(F32)).astype(BF16)


def _ssd_kernel(xs_ref, bc_ref, dt_ref, dtb_ref, alog_ref, dskip_ref, tri_ref, exp_ref, *rest, n_cast):
    cast_in = rest[:n_cast]
    o_ref = rest[n_cast]
    cast_out = rest[n_cast + 1:2 * n_cast + 1]
    state, xdb, expd, acs2_scr, src_scr = rest[2 * n_cast + 1:]

    @pl.when(pl.program_id(1) == 0)
    def _():
        state[...] = jnp.zeros_like(state)

    for w_ref, wb_ref in zip(cast_in, cast_out):
        wb_ref[...] = w_ref[...].astype(BF16)

    _ssd_prologue(xs_ref, dt_ref, dtb_ref, alog_ref, tri_ref, exp_ref, xdb, expd, acs2_scr, src_scr)
    for ci in range(SSD_CHUNKS_PER_STEP):
        _ssd_chunk(ci, xs_ref, bc_ref, dskip_ref, o_ref, state, xdb, expd, acs2_scr, src_scr)


def _ssd_scan(proj, dt_raw, dt_bias3, a_log3, d_skip_ch, tri3, expand3, cast_weights, batch, seq):
    q = CHUNK
    tm = SSD_CHUNKS_PER_STEP * q
    steps = seq // tm
    t = batch * seq
    n_steps = batch * steps
    rowmap = lambda b, c: b * steps + c
    const = lambda b, c: (0, 0)
    slab_specs = [pl.BlockSpec((w.shape[0] // n_steps, w.shape[1]), lambda b, c: (rowmap(b, c), 0))
                  for w in cast_weights]
    outs = pl.pallas_call(
        functools.partial(_ssd_kernel, n_cast=len(cast_weights)),
        grid=(batch, steps),
        in_specs=[pl.BlockSpec((None, tm, D_INNER), lambda b, c: (TILE_XS, rowmap(b, c), 0)),
                  pl.BlockSpec((None, tm, BC_WIDTH), lambda b, c: (TILE_BC_POOL, rowmap(b, c), 0)),
                  pl.BlockSpec((tm, LANES), lambda b, c: (rowmap(b, c), 0)),
                  pl.BlockSpec((1, LANES), const),
                  pl.BlockSpec((1, LANES), const),
                  pl.BlockSpec((1, D_INNER), const),
                  pl.BlockSpec((q, DT_REP * q), const),
                  pl.BlockSpec((LANES, D_INNER), const)] + slab_specs,
        out_specs=[pl.BlockSpec((tm, D_INNER), lambda b, c: (rowmap(b, c), 0))] + slab_specs,
        out_shape=[jax.ShapeDtypeStruct((t, D_INNER), BF16)]
        + [jax.ShapeDtypeStruct(w.shape, BF16) for w in cast_weights],
        scratch_shapes=[pltpu.VMEM((D_STATE, D_INNER), F32),
                        pltpu.VMEM((tm, D_INNER), BF16),
                        pltpu.VMEM((2 * tm, D_INNER), F32),
                        pltpu.VMEM((tm, LANES), F32),
                        pltpu.VMEM((SSD_CHUNKS_PER_STEP, LANES, q), F32)],
        compiler_params=pltpu.CompilerParams(
            dimension_semantics=("arbitrary", "arbitrary"), vmem_limit_bytes=VMEM_LIMIT),
        name="ssd_scan",
    )(proj, proj, dt_raw, dt_bias3, a_log3, d_skip_ch, tri3, expand3, *cast_weights)
    return outs[0], outs[1:]


def _pooled_groups(u, halo_ref, pos0):
    n_chunks = u.shape[0] // CHUNK
    max_shift = max(POOL_WINDOWS) - 1
    slabs = _slabs_of(u, n_chunks)
    prev_last = {j: halo_ref[SLABS - j] for j in range(1, max_shift + 1)}
    for v in range(SLABS):
        halo_ref[v] = slabs[n_chunks - 1][v]
    wrapped = _wrapped_slabs(slabs, prev_last, max_shift)
    sub = lax.broadcasted_iota(jnp.int32, (SUBLANES, POOL_GW), 0)
    pooled = []
    for gi, win in enumerate(POOL_WINDOWS):
        cs = slice(gi * POOL_GW, (gi + 1) * POOL_GW)
        out = []
        for c in range(n_chunks):
            for v in range(SLABS):
                cur = slabs[c][v][:, cs]
                tot = cur
                for k in range(1, win):
                    tot = tot + _shifted(slabs, wrapped, c, v, k)[:, cs]
                pos = pos0 + c * CHUNK + sub * SLABS + v
                count = jnp.minimum(pos + 1, win).astype(F32)
                out.append(tot / count - cur)
        pooled.append(jnp.concatenate(out, axis=0).astype(BF16))
    return pooled


def _tail_kernel(x_ref, y_ref, zs_ref, u_ref, g_ref, mod_ref, snw_ref, wbs_ref, wpool_ref,
                 wo_ref, nmlp_ref, wup_ref, wdn_ref, nfin_ref, unperm_ref, o_ref, uhalo,
                 *, tm, tiles_per_batch):
    i = pl.program_id(0)

    @pl.when(i % tiles_per_batch == 0)
    def _():
        uhalo[...] = jnp.zeros_like(uhalo)

    y_gated = []
    for g in range(N_GROUPS):
        gs = slice(g * GROUP_CH, (g + 1) * GROUP_CH)
        yg = y_ref[:, gs].astype(F32) * zs_ref[:, gs].astype(F32)
        y_gated.append((_rms(yg) * snw_ref[:, gs]).astype(BF16))
    y_ssd = jnp.dot(jnp.concatenate(y_gated, axis=1), wbs_ref[...], preferred_element_type=F32)

    pooled = _pooled_groups(u_ref[...].astype(F32), uhalo, (i % tiles_per_batch) * tm)
    y_pool = jnp.dot(jnp.concatenate(pooled, axis=1), wpool_ref[...], preferred_element_type=F32)

    merged = (g_ref[:, :D_MODEL].astype(F32) * y_ssd + g_ref[:, D_MODEL:].astype(F32) * y_pool)
    merged = _permute_chunks(unperm_ref[...], merged.astype(BF16))
    mix = jnp.dot(merged, wo_ref[...], preferred_element_type=F32)

    mod = mod_ref[0]
    gate_m = mod[:, 2 * D_MODEL:3 * D_MODEL]
    shift_f = mod[:, 3 * D_MODEL:4 * D_MODEL]
    scale_f = mod[:, 4 * D_MODEL:5 * D_MODEL]
    gate_f = mod[:, 5 * D_MODEL:6 * D_MODEL]
    x1 = x_ref[...] + gate_m * mix
    h = (_rms(x1) * nmlp_ref[...]) * (1.0 + scale_f) + shift_f
    up = jnp.dot(h.astype(BF16), wup_ref[...], preferred_element_type=F32)
    act = jnp.square(jnp.maximum(up, 0.0)).astype(BF16)
    x2 = x1 + gate_f * jnp.dot(act, wdn_ref[...], preferred_element_type=F32)
    o_ref[...] = _rms(x2) * nfin_ref[...]


def _tail(x2d, y_raw, proj, mod3, ssd_norm_w, w_bs, w_pool, w_out, norm_mlp_w, w_up, w_dn, norm_final_w,
          unperm, seq):
    t = x2d.shape[0]
    tm = 512
    tiles_per_batch = seq // tm
    const2 = lambda i: (0, 0)
    resident = functools.partial(pl.BlockSpec, pipeline_mode=pl.Buffered(1))
    return pl.pallas_call(
        functools.partial(_tail_kernel, tm=tm, tiles_per_batch=tiles_per_batch),
        grid=(t // tm,),
        in_specs=[pl.BlockSpec((tm, D_MODEL), lambda i: (i, 0)),
                  pl.BlockSpec((tm, D_INNER), lambda i: (i, 0)),
                  pl.BlockSpec((None, tm, D_INNER), lambda i: (TILE_Z, i, 0)),
                  pl.BlockSpec((None, tm, D_MODEL), lambda i: (TILE_BC_POOL, i, BC_WIDTH // D_MODEL)),
                  pl.BlockSpec((None, tm, 2 * D_MODEL), lambda i: (TILE_GATE, i, 0)),
                  pl.BlockSpec((1, 1, N_MOD * D_MODEL), lambda i: (i // tiles_per_batch, 0, 0)),
                  resident((1, D_INNER), const2),
                  resident((D_INNER, D_MODEL), const2),
                  resident((D_MODEL, D_MODEL), const2),
                  resident((D_MODEL, D_MODEL), const2),
                  resident((1, D_MODEL), const2),
                  resident((D_MODEL, D_FF), const2),
                  resident((D_FF, D_MODEL), const2),
                  resident((1, D_MODEL), const2),
                  resident((CHUNK, CHUNK), const2)],
        out_specs=pl.BlockSpec((tm, D_MODEL), lambda i: (i, 0)),
        out_shape=jax.ShapeDtypeStruct((t, D_MODEL), F32),
        scratch_shapes=[pltpu.VMEM((SLABS, SUBLANES, D_MODEL), F32)],
        compiler_params=pltpu.CompilerParams(
            dimension_semantics=("arbitrary",), vmem_limit_bytes=VMEM_LIMIT),
        name="gate_pool_merge_mlp",
    )(x2d, y_raw, proj, proj, proj, mod3, ssd_norm_w, w_bs, w_pool, w_out, norm_mlp_w, w_up, w_dn,
      norm_final_w, unperm)


def _cumsum_matrix():
    tok = _token_of_row(np.arange(CHUNK))
    tri = (tok[None, :] <= tok[:, None]).astype(np.float32)
    return jnp.asarray(np.concatenate([tri] * DT_REP, axis=1), BF16)


def _interleave_matrix():
    p = np.zeros((CHUNK, CHUNK), np.float32)
    p[np.arange(CHUNK), _token_of_row(np.arange(CHUNK))] = 1.0
    return jnp.asarray(p, BF16), jnp.asarray(p.T, BF16)


def _head_expand_matrix():
    m = np.zeros((LANES, D_INNER), np.float32)
    for k in range(DT_REP * N_HEADS):
        h = k % N_HEADS
        m[k, h * HEAD_DIM:(h + 1) * HEAD_DIM] = 1.0
    return jnp.asarray(m, BF16)


def _rep_heads(v):
    pad = jnp.zeros((LANES - DT_REP * N_HEADS,), v.dtype)
    return jnp.concatenate([v] * DT_REP + [pad])[None, :]


def _on_proj_cols(v):
    return jnp.pad(v, ((0, 0), (COL_XS, PROJ_COLS - COL_POOL)))


def kernel(x, c, w_ada, b_ada, norm_mix_w, w_in, conv_w, conv_b, dt_bias, a_log, d_skip, ssd_norm_w,
           w_branch_ssd, pool_w, pool_scale, w_branch_pool, w_out, norm_mlp_w, w_up, w_down,
           norm_final_w):
    batch, seq, d = x.shape
    depth = w_ada.shape[0]
    assert depth == 1, "the final norm is fused into the last kernel of a single layer"
    t = batch * seq
    tri3 = _cumsum_matrix()
    expand3 = _head_expand_matrix()
    perm, unperm = _interleave_matrix()
    c_pad = jnp.pad(c, ((0, SUBLANES - batch), (0, 0)))
    x2d = x.reshape(t, d)
    i = 0
    mod = _modulation(c_pad, w_ada[i], b_ada[i][None, :])[:batch]
    mod3 = mod[:, None, :]
    w_in_t = jnp.swapaxes(w_in[i], 0, 1)
    w_main, w_dt3, w_pool = _weight_prep(w_in_t, pool_w[i], pool_scale[i][:, None], w_branch_pool[i])
    proj, dt_raw = _in_projection(x2d, norm_mix_w[i][None, :], mod3, w_main, w_dt3,
                                  _on_proj_cols(conv_w[i]), _on_proj_cols(conv_b[i][None, :]), perm, seq)
    tail_weights = [w_branch_ssd[i], w_out[i], w_up[i], w_down[i]]
    y_raw, (w_bs, w_ob, w_upb, w_dnb) = _ssd_scan(
        proj, dt_raw, _rep_heads(dt_bias[i]), _rep_heads(a_log[i]),
        jnp.repeat(d_skip[i], HEAD_DIM)[None, :], tri3, expand3, tail_weights, batch, seq)
    out = _tail(x2d, y_raw, proj, mod3, ssd_norm_w[i][None, :], w_bs, w_pool, w_ob, norm_mlp_w[i][None, :],
                w_upb, w_dnb, norm_final_w[None, :], unperm, seq)
    return out.reshape(batch, seq, d)
```

```python
import functools

import jax
import jax.numpy as jnp
import numpy as np
from jax import lax
from jax.experimental import pallas as pl
from jax.experimental.pallas import tpu as pltpu

F32 = jnp.float32
BF16 = jnp.bfloat16

D_MODEL = 1024
D_INNER = 2 * D_MODEL
HEAD_DIM = 64
N_HEADS = D_INNER // HEAD_DIM
N_GROUPS = 4
GROUP_CH = D_INNER // N_GROUPS
D_STATE = 128
CONV_K = 4
CHUNK = 128
BC_WIDTH = 2 * N_GROUPS * D_STATE
POOL_WINDOWS = (2, 4, 8, 16)
POOL_GW = D_MODEL // len(POOL_WINDOWS)
D_FF = 4 * D_MODEL
N_MOD = 6
EPS = 1e-5
LOG2_E = 1.4426950408889634
LANES = 128
SUBLANES = 8
SLABS = CHUNK // SUBLANES
DT_REP = 3

PROJ_TN = 2048
COL_XS = D_INNER
COL_POOL = 2 * D_INNER + BC_WIDTH
PROJ_COLS = COL_POOL + 3 * D_MODEL
TILE_Z, TILE_XS, TILE_BC_POOL, TILE_GATE = 0, 1, 2, 3
CONV_COL_BLOCK = 512
PROJ_ROW_SPLIT = 4

VMEM_LIMIT = 56 * 1024 * 1024
SSD_CHUNKS_PER_STEP = 4


def _rms(x):
    return x * lax.rsqrt(jnp.mean(x * x, axis=-1, keepdims=True) + EPS)


def _dot_row_parts(x, w, parts):
    rows = x.shape[0] // parts
    return jnp.concatenate(
        [jnp.dot(x[m0:m0 + rows, :], w, preferred_element_type=F32) for m0 in range(0, x.shape[0], rows)],
        axis=0)


def _token_of_row(r):
    return (r % SUBLANES) * SLABS + r // SUBLANES


def _permute_chunks(perm, xb):
    n_chunks = xb.shape[0] // CHUNK
    return jnp.concatenate(
        [jnp.dot(perm, xb[c * CHUNK:(c + 1) * CHUNK, :], preferred_element_type=F32).astype(BF16)
         for c in range(n_chunks)], axis=0)


def _slabs_of(x, n_chunks):
    return [[x[(c * SLABS + v) * SUBLANES:(c * SLABS + v + 1) * SUBLANES, :] for v in range(SLABS)]
            for c in range(n_chunks)]


def _wrapped_slabs(slabs, prev_last, max_shift):
    n_chunks = len(slabs)
    width = slabs[0][0].shape[1]
    first_row = lax.broadcasted_iota(jnp.int32, (SUBLANES, width), 0) == 0
    wrapped = {}
    for j in range(1, max_shift + 1):
        stacked = jnp.concatenate([slabs[c][SLABS - j] for c in range(n_chunks)], axis=0)
        rolled = pltpu.roll(stacked, 1, 0)
        head = jnp.where(first_row, pltpu.roll(prev_last[j], 1, 0), rolled[0:SUBLANES, :])
        wrapped[j] = [head] + [rolled[c * SUBLANES:(c + 1) * SUBLANES, :] for c in range(1, n_chunks)]
    return wrapped


def _shifted(slabs, wrapped, c, v, k):
    return slabs[c][v - k] if v >= k else wrapped[k - v][c]


def _mod_kernel(c_ref, w_ref, b_ref, o_ref):
    batch = c_ref.shape[0]
    s = jax.nn.silu(c_ref[...])
    s = jnp.concatenate([s, jnp.zeros((SUBLANES - batch, s.shape[1]), F32)], axis=0)
    mod = jnp.dot(s.astype(BF16), w_ref[...].astype(BF16), preferred_element_type=F32) + b_ref[...]
    o_ref[...] = mod[0:batch, :]


def _modulation(c, w_ada, b_ada):
    batch = c.shape[0]
    n = w_ada.shape[1]
    tn = 1024
    return pl.pallas_call(
        _mod_kernel,
        grid=(n // tn,),
        in_specs=[pl.BlockSpec((batch, D_MODEL), lambda j: (0, 0)),
                  pl.BlockSpec((D_MODEL, tn), lambda j: (0, j)),
                  pl.BlockSpec((1, tn), lambda j: (0, j))],
        out_specs=pl.BlockSpec((batch, tn), lambda j: (0, j)),
        out_shape=jax.ShapeDtypeStruct((batch, n), F32),
        name="adaln_mod",
    )(c, w_ada, b_ada)


def _rep_heads(v):
    pad = jnp.zeros((v.shape[0], LANES - DT_REP * N_HEADS), v.dtype)
    return jnp.concatenate([v] * DT_REP + [pad], axis=1)


def _weight_prep_kernel(wt_ref, wdt_ref, pw_ref, ps_ref, wbp_ref, cw_ref, cb_ref, dtb_ref, alog_ref,
                        o_ref, odt_ref, opool_ref, ocw_ref, ocb_ref, odtb_ref, oalog_ref, *, conv_steps):
    r = pl.program_id(0)
    o_ref[...] = wt_ref[...].T.astype(BF16)

    on_conv = jnp.logical_and(r >= conv_steps[0], r < conv_steps[1])
    ocw_ref[...] = jnp.where(on_conv, cw_ref[...], 0.0)
    ocb_ref[...] = jnp.where(on_conv, cb_ref[...], 0.0)

    @pl.when(r == 0)
    def _():
        odt_ref[...] = _rep_heads(wdt_ref[...].T[:, 0:N_HEADS])
        odtb_ref[...] = _rep_heads(dtb_ref[...])
        oalog_ref[...] = _rep_heads(alog_ref[...])

    @pl.when(r < len(POOL_WINDOWS))
    def _():
        scaled = ps_ref[...] * wbp_ref[...]
        opool_ref[...] = jnp.dot(pw_ref[...], scaled, precision=lax.Precision.HIGHEST,
                                 preferred_element_type=F32).astype(BF16)


def _weight_prep(w_in_t, pool_w, pool_scale_col, w_bp, conv_w, conv_b, dt_bias, a_log):
    cols = 1024
    k = w_in_t.shape[1]
    n = w_bp.shape[1]
    first_after_dt = COL_POOL // cols
    last_group = len(POOL_WINDOWS) - 1
    group = lambda r: jnp.minimum(r, last_group)
    conv_steps = (COL_XS // cols, COL_POOL // cols)
    conv_blk = lambda r: (0, jnp.clip(r - conv_steps[0], 0, conv_steps[1] - conv_steps[0] - 1))
    whole = lambda r: (0, 0)

    def src_row(r):
        return (r * (cols // N_HEADS) + jnp.where(r >= first_after_dt, 1, 0)) * N_HEADS

    return pl.pallas_call(
        functools.partial(_weight_prep_kernel, conv_steps=conv_steps),
        grid=(PROJ_COLS // cols,),
        in_specs=[pl.BlockSpec((pl.Element(cols), pl.Element(k)), lambda r: (src_row(r), 0)),
                  pl.BlockSpec((LANES, k), lambda r: (COL_POOL // LANES, 0)),
                  pl.BlockSpec((None, POOL_GW, POOL_GW), lambda r: (group(r), 0, 0)),
                  pl.BlockSpec((POOL_GW, 1), lambda r: (group(r), 0)),
                  pl.BlockSpec((POOL_GW, n), lambda r: (group(r), 0)),
                  pl.BlockSpec((CONV_K, cols), conv_blk),
                  pl.BlockSpec((1, cols), conv_blk),
                  pl.BlockSpec((1, N_HEADS), whole),
                  pl.BlockSpec((1, N_HEADS), whole)],
        out_specs=[pl.BlockSpec((None, k, cols), lambda r: (r // (PROJ_TN // cols), 0, r % (PROJ_TN // cols))),
                   pl.BlockSpec((k, LANES), whole),
                   pl.BlockSpec((POOL_GW, n), lambda r: (group(r), 0)),
                   pl.BlockSpec((CONV_K, cols), lambda r: (0, r)),
                   pl.BlockSpec((1, cols), lambda r: (0, r)),
                   pl.BlockSpec((1, LANES), whole),
                   pl.BlockSpec((1, LANES), whole)],
        out_shape=[jax.ShapeDtypeStruct((PROJ_COLS // PROJ_TN, k, PROJ_TN), BF16),
                   jax.ShapeDtypeStruct((k, LANES), F32),
                   jax.ShapeDtypeStruct((D_MODEL, n), BF16),
                   jax.ShapeDtypeStruct((CONV_K, PROJ_COLS), F32),
                   jax.ShapeDtypeStruct((1, PROJ_COLS), F32),
                   jax.ShapeDtypeStruct((1, LANES), F32),
                   jax.ShapeDtypeStruct((1, LANES), F32)],
        name="weight_prep",
    )(w_in_t, w_in_t, pool_w, pool_scale_col, w_bp, conv_w, conv_b, dt_bias, a_log)


def _conv_silu_store(r, cw_ref, cb_ref, halo_ref, o_ref, width):
    n_chunks = r.shape[0] // CHUNK
    pair = 2 * SUBLANES
    for c0 in range(0, width, CONV_COL_BLOCK):
        cols = slice(c0, c0 + CONV_COL_BLOCK)
        slabs = _slabs_of(r[:, cols], n_chunks)
        prev_last = {j: halo_ref[j - 1, :, cols] for j in range(1, CONV_K)}
        for j in range(1, CONV_K):
            halo_ref[j - 1, :, cols] = slabs[n_chunks - 1][SLABS - j]
        wrapped = _wrapped_slabs(slabs, prev_last, CONV_K - 1)
        taps = [cw_ref[k:k + 1, cols] for k in range(CONV_K)]
        bias = cb_ref[:, cols]
        for c in range(n_chunks):
            for v0 in range(0, SLABS, 2):
                accs = []
                for v in (v0, v0 + 1):
                    acc = slabs[c][v] * taps[CONV_K - 1] + bias
                    for k in range(1, CONV_K):
                        acc = acc + _shifted(slabs, wrapped, c, v, k) * taps[CONV_K - 1 - k]
                    accs.append(acc)
                lo = (c * SLABS + v0) * SUBLANES
                o_ref[lo:lo + pair, cols] = jax.nn.silu(jnp.concatenate(accs, axis=0)).astype(BF16)


def _inproj_kernel(x_ref, nw_ref, mod_ref, w_ref, wdt_ref, cw_ref, cb_ref, perm_ref, o_ref, dt_ref,
                   h_scr, halo_scr, *, tiles_per_batch):
    i = pl.program_id(0)
    j = pl.program_id(1)

    @pl.when(j == 0)
    def _():
        mod = mod_ref[0]
        shift = mod[:, 0:D_MODEL]
        scale = mod[:, D_MODEL:2 * D_MODEL]
        h = (_rms(x_ref[...]) * nw_ref[...]) * (1.0 + scale) + shift
        hb = _permute_chunks(perm_ref[...], h.astype(BF16))
        h_scr[...] = hb
        dt_ref[...] = jnp.dot(hb, wdt_ref[...].astype(BF16), preferred_element_type=F32)

    @pl.when(jnp.logical_and(i % tiles_per_batch == 0, j == 0))
    def _():
        halo_scr[...] = jnp.zeros_like(halo_scr)

    def proj():
        return _dot_row_parts(h_scr[...], w_ref[...], PROJ_ROW_SPLIT)

    @pl.when(j == TILE_Z)
    def _():
        o_ref[...] = jax.nn.silu(proj()).astype(BF16)

    @pl.when(j == TILE_XS)
    def _():
        _conv_silu_store(proj(), cw_ref, cb_ref, halo_scr.at[0], o_ref, PROJ_TN)

    @pl.when(j == TILE_BC_POOL)
    def _():
        r = proj()
        _conv_silu_store(r[:, 0:BC_WIDTH], cw_ref, cb_ref, halo_scr.at[1], o_ref, BC_WIDTH)
        o_ref[:, BC_WIDTH:] = r[:, BC_WIDTH:].astype(BF16)

    @pl.when(j == TILE_GATE)
    def _():
        o_ref[...] = jax.nn.sigmoid(proj()).astype(BF16)


def _in_projection(x2d, norm_w, mod3, w_main, w_dt, conv_w_cols, conv_b_cols, perm, seq):
    t = x2d.shape[0]
    tm, tn = 1024, PROJ_TN
    tiles_per_batch = seq // tm
    return pl.pallas_call(
        functools.partial(_inproj_kernel, tiles_per_batch=tiles_per_batch),
        grid=(t // tm, PROJ_COLS // tn),
        in_specs=[pl.BlockSpec((tm, D_MODEL), lambda i, j: (i, 0)),
                  pl.BlockSpec((1, D_MODEL), lambda i, j: (0, 0)),
                  pl.BlockSpec((1, 1, N_MOD * D_MODEL), lambda i, j: (i // tiles_per_batch, 0, 0)),
                  pl.BlockSpec((None, D_MODEL, tn), lambda i, j: (j, 0, 0)),
                  pl.BlockSpec((D_MODEL, LANES), lambda i, j: (0, 0)),
                  pl.BlockSpec((CONV_K, tn), lambda i, j: (0, j)),
                  pl.BlockSpec((1, tn), lambda i, j: (0, j)),
                  pl.BlockSpec((CHUNK, CHUNK), lambda i, j: (0, 0))],
        out_specs=[pl.BlockSpec((None, tm, tn), lambda i, j: (j, i, 0)),
                   pl.BlockSpec((tm, LANES), lambda i, j: (i, 0))],
        out_shape=[jax.ShapeDtypeStruct((PROJ_COLS // tn, t, tn), BF16),
                   jax.ShapeDtypeStruct((t, LANES), F32)],
        scratch_shapes=[pltpu.VMEM((tm, D_MODEL), BF16),
                        pltpu.VMEM((2, CONV_K - 1, SUBLANES, tn), F32)],
        compiler_params=pltpu.CompilerParams(
            dimension_semantics=("arbitrary", "arbitrary"), vmem_limit_bytes=VMEM_LIMIT),
        name="norm_in_proj",
    )(x2d, norm_w, mod3, w_main, w_dt, conv_w_cols, conv_b_cols, perm)


def _split3(v):
    hi = v.astype(BF16).astype(F32)
    r1 = v - hi
    mid = r1.astype(BF16).astype(F32)
    return hi, r1, r1 - mid


def _lane_pieces(v, lane):
    hi, r1, r2 = _split3(v)
    return jnp.where(lane < N_HEADS, hi, jnp.where(lane < 2 * N_HEADS, r1, r2)).astype(BF16)


def _ssd_prologue(xs_ref, dt_ref, dtb_ref, alog_ref, tri_ref, exp_ref, xdb, expd, acs2_scr, src_scr):
    q = CHUNK
    tm = dt_ref.shape[0]
    n_chunks = tm // q
    dt = jax.nn.softplus(dt_ref[...] + dtb_ref[...])
    dta = dt * (-jnp.exp(alog_ref[...]))
    hi, r1, r2 = _split3(dta)
    acs = []
    for c in range(n_chunks):
        rows = slice(c * q, (c + 1) * q)
        stacked = jnp.concatenate([hi[rows], r1[rows], r2[rows]], axis=0).astype(BF16)
        acs.append(jnp.dot(tri_ref[...], stacked, preferred_element_type=F32))
    acs2 = jnp.concatenate(acs, axis=0) * LOG2_E
    acs2_last = jnp.concatenate(
        [jnp.broadcast_to(acs2[(c + 1) * q - 1:(c + 1) * q, :], (q, LANES)) for c in range(n_chunks)], axis=0)
    decay_to_end = jnp.exp2(acs2_last - acs2)
    decay_from_start = jnp.exp2(acs2)
    acs2_scr[...] = acs2
    src = acs2 - jnp.log2(dt)
    for c in range(n_chunks):
        src_scr[c] = src[c * q:(c + 1) * q, :].T

    lane = lax.broadcasted_iota(jnp.int32, (tm, LANES), 1)
    pieces = jnp.concatenate([_lane_pieces(dt * decay_to_end, lane),
                              _lane_pieces(decay_from_start, lane)], axis=0)
    expd[...] = jnp.dot(pieces, exp_ref[...], preferred_element_type=F32)
    xdb[...] = (xs_ref[...].astype(F32) * expd[0:tm, :]).astype(BF16)


def _ssd_chunk(ci, xs_ref, bc_ref, dskip_ref, o_ref, state, xdb, expd, acs2_scr, src_scr):
    q = CHUNK
    tm = xs_ref.shape[0]
    rows = slice(ci * q, (ci + 1) * q)
    off_rows = slice(tm + ci * q, tm + (ci + 1) * q)
    acs2 = acs2_scr[rows, :]
    src_t = src_scr[ci]

    causal = (_token_of_row(lax.broadcasted_iota(jnp.int32, (q, q), 0))
              >= _token_of_row(lax.broadcasted_iota(jnp.int32, (q, q), 1)))
    half = lax.broadcasted_iota(jnp.int32, (q, LANES), 1) < HEAD_DIM
    zero_b = jnp.zeros((q, LANES), BF16)

    for g in range(N_GROUPS):
        gs = slice(g * GROUP_CH, (g + 1) * GROUP_CH)
        bg = bc_ref[rows, g * D_STATE:(g + 1) * D_STATE]
        cg = bc_ref[rows, (N_GROUPS + g) * D_STATE:(N_GROUPS + g + 1) * D_STATE]
        scores = lax.dot_general(cg, bg, (((1,), (1,)), ((), ())), preferred_element_type=F32)
        st_old = state[:, gs]
        y_off = jnp.dot(cg, st_old.astype(BF16), preferred_element_type=F32)
        new_t = lax.dot_general(bg, xdb[rows, gs], (((0,), (0,)), ((), ())),
                                preferred_element_type=F32)
        state[:, gs] = st_old * expd[off_rows.stop - 1:off_rows.stop, gs] + new_t
        ys = []
        for pr in range(GROUP_CH // LANES):
            h0 = g * (GROUP_CH // HEAD_DIM) + 2 * pr
            ms = []
            for h in (h0, h0 + 1):
                a_col = jnp.broadcast_to(acs2[:, h:h + 1], (q, q))
                a_row = jnp.broadcast_to(src_t[h:h + 1, :], (q, q))
                decay_dt = jnp.exp2(jnp.where(causal, a_col - a_row, -jnp.inf))
                ms.append((scores * decay_dt).astype(BF16))
            xp = xs_ref[rows, g * GROUP_CH + pr * LANES:g * GROUP_CH + (pr + 1) * LANES]
            rhs = jnp.concatenate([jnp.where(half, xp, zero_b), jnp.where(half, zero_b, xp)], axis=0)
            ys.append(jnp.dot(jnp.concatenate(ms, axis=1), rhs, preferred_element_type=F32))
        y = jnp.concatenate(ys, axis=1) + y_off * expd[off_rows, gs]
        o_ref[rows, gs] = (y + dskip_ref[:, gs] * xs_ref[rows, gs].astype(F32)).astype(BF16)


def _ssd_kernel(xs_ref, bc_ref, dt_ref, dtb_ref, alog_ref, dskip_ref, tri_ref, exp_ref, *rest, n_cast):
    cast_in = rest[:n_cast]
    o_ref = rest[n_cast]
    cast_out = rest[n_cast + 1:2 * n_cast + 1]
    state, xdb, expd, acs2_scr, src_scr = rest[2 * n_cast + 1:]

    @pl.when(pl.program_id(1) == 0)
    def _():
        state[...] = jnp.zeros_like(state)

    for w_ref, wb_ref in zip(cast_in, cast_out):
        wb_ref[...] = w_ref[...].astype(BF16)

    _ssd_prologue(xs_ref, dt_ref, dtb_ref, alog_ref, tri_ref, exp_ref, xdb, expd, acs2_scr, src_scr)
    for ci in range(SSD_CHUNKS_PER_STEP):
        _ssd_chunk(ci, xs_ref, bc_ref, dskip_ref, o_ref, state, xdb, expd, acs2_scr, src_scr)


def _ssd_scan(proj, dt_raw, dt_bias3, a_log3, d_skip_ch, tri3, expand3, cast_weights, batch, seq):
    q = CHUNK
    tm = SSD_CHUNKS_PER_STEP * q
    steps = seq // tm
    t = batch * seq
    n_steps = batch * steps
    rowmap = lambda b, c: b * steps + c
    const = lambda b, c: (0, 0)
    slab_specs = [pl.BlockSpec((w.shape[0] // n_steps, w.shape[1]), lambda b, c: (rowmap(b, c), 0))
                  for w in cast_weights]
    outs = pl.pallas_call(
        functools.partial(_ssd_kernel, n_cast=len(cast_weights)),
        grid=(batch, steps),
        in_specs=[pl.BlockSpec((None, tm, D_INNER), lambda b, c: (TILE_XS, rowmap(b, c), 0)),
                  pl.BlockSpec((None, tm, BC_WIDTH), lambda b, c: (TILE_BC_POOL, rowmap(b, c), 0)),
                  pl.BlockSpec((tm, LANES), lambda b, c: (rowmap(b, c), 0)),
                  pl.BlockSpec((1, LANES), const),
                  pl.BlockSpec((1, LANES), const),
                  pl.BlockSpec((1, D_INNER), const),
                  pl.BlockSpec((q, DT_REP * q), const),
                  pl.BlockSpec((LANES, D_INNER), const)] + slab_specs,
        out_specs=[pl.BlockSpec((tm, D_INNER), lambda b, c: (rowmap(b, c), 0))] + slab_specs,
        out_shape=[jax.ShapeDtypeStruct((t, D_INNER), BF16)]
        + [jax.ShapeDtypeStruct(w.shape, BF16) for w in cast_weights],
        scratch_shapes=[pltpu.VMEM((D_STATE, D_INNER), F32),
                        pltpu.VMEM((tm, D_INNER), BF16),
                        pltpu.VMEM((2 * tm, D_INNER), F32),
                        pltpu.VMEM((tm, LANES), F32),
                        pltpu.VMEM((SSD_CHUNKS_PER_STEP, LANES, q), F32)],
        compiler_params=pltpu.CompilerParams(
            dimension_semantics=("arbitrary", "arbitrary"), vmem_limit_bytes=VMEM_LIMIT),
        name="ssd_scan",
    )(proj, proj, dt_raw, dt_bias3, a_log3, d_skip_ch, tri3, expand3, *cast_weights)
    return outs[0], outs[1:]


def _pooled_groups(u, halo_ref, pos0):
    n_chunks = u.shape[0] // CHUNK
    max_shift = max(POOL_WINDOWS) - 1
    slabs = _slabs_of(u, n_chunks)
    prev_last = {j: halo_ref[SLABS - j] for j in range(1, max_shift + 1)}
    for v in range(SLABS):
        halo_ref[v] = slabs[n_chunks - 1][v]
    wrapped = _wrapped_slabs(slabs, prev_last, max_shift)
    sub = lax.broadcasted_iota(jnp.int32, (SUBLANES, POOL_GW), 0)
    pooled = []
    for gi, win in enumerate(POOL_WINDOWS):
        cs = slice(gi * POOL_GW, (gi + 1) * POOL_GW)
        out = []
        for c in range(n_chunks):
            for v in range(SLABS):
                cur = slabs[c][v][:, cs]
                tot = cur
                for k in range(1, win):
                    tot = tot + _shifted(slabs, wrapped, c, v, k)[:, cs]
                pos = pos0 + c * CHUNK + sub * SLABS + v
                count = jnp.minimum(pos + 1, win).astype(F32)
                out.append(tot / count - cur)
        pooled.append(jnp.concatenate(out, axis=0).astype(BF16))
    return pooled


def _tail_kernel(x_ref, y_ref, zs_ref, u_ref, g_ref, mod_ref, snw_ref, wbs_ref, wpool_ref,
                 wo_ref, nmlp_ref, wup_ref, wdn_ref, nfin_ref, unperm_ref, o_ref, uhalo,
                 *, tm, tiles_per_batch):
    i = pl.program_id(0)

    @pl.when(i % tiles_per_batch == 0)
    def _():
        uhalo[...] = jnp.zeros_like(uhalo)

    y_gated = []
    for g in range(N_GROUPS):
        gs = slice(g * GROUP_CH, (g + 1) * GROUP_CH)
        yg = y_ref[:, gs].astype(F32) * zs_ref[:, gs].astype(F32)
        y_gated.append((_rms(yg) * snw_ref[:, gs]).astype(BF16))
    y_ssd = jnp.dot(jnp.concatenate(y_gated, axis=1), wbs_ref[...], preferred_element_type=F32)

    pooled = _pooled_groups(u_ref[...].astype(F32), uhalo, (i % tiles_per_batch) * tm)
    y_pool = jnp.dot(jnp.concatenate(pooled, axis=1), wpool_ref[...], preferred_element_type=F32)

    merged = (g_ref[:, :D_MODEL].astype(F32) * y_ssd + g_ref[:, D_MODEL:].astype(F32) * y_pool)
    merged = _permute_chunks(unperm_ref[...], merged.astype(BF16))
    mix = jnp.dot(merged, wo_ref[...], preferred_element_type=F32)

    mod = mod_ref[0]
    gate_m = mod[:, 2 * D_MODEL:3 * D_MODEL]
    shift_f = mod[:, 3 * D_MODEL:4 * D_MODEL]
    scale_f = mod[:, 4 * D_MODEL:5 * D_MODEL]
    gate_f = mod[:, 5 * D_MODEL:6 * D_MODEL]
    x1 = x_ref[...] + gate_m * mix
    h = (_rms(x1) * nmlp_ref[...]) * (1.0 + scale_f) + shift_f
    up = jnp.dot(h.astype(BF16), wup_ref[...], preferred_element_type=F32)
    act = jnp.square(jnp.maximum(up, 0.0)).astype(BF16)
    x2 = x1 + gate_f * jnp.dot(act, wdn_ref[...], preferred_element_type=F32)
    o_ref[...] = _rms(x2) * nfin_ref[...]


def _tail(x2d, y_raw, proj, mod3, ssd_norm_w, w_bs, w_pool, w_out, norm_mlp_w, w_up, w_dn, norm_final_w,
          unperm, seq):
    t = x2d.shape[0]
    tm = 512
    tiles_per_batch = seq // tm
    const2 = lambda i: (0, 0)
    resident = functools.partial(pl.BlockSpec, pipeline_mode=pl.Buffered(1))
    return pl.pallas_call(
        functools.partial(_tail_kernel, tm=tm, tiles_per_batch=tiles_per_batch),
        grid=(t // tm,),
        in_specs=[pl.BlockSpec((tm, D_MODEL), lambda i: (i, 0)),
                  pl.BlockSpec((tm, D_INNER), lambda i: (i, 0)),
                  pl.BlockSpec((None, tm, D_INNER), lambda i: (TILE_Z, i, 0)),
                  pl.BlockSpec((None, tm, D_MODEL), lambda i: (TILE_BC_POOL, i, BC_WIDTH // D_MODEL)),
                  pl.BlockSpec((None, tm, 2 * D_MODEL), lambda i: (TILE_GATE, i, 0)),
                  pl.BlockSpec((1, 1, N_MOD * D_MODEL), lambda i: (i // tiles_per_batch, 0, 0)),
                  resident((1, D_INNER), const2),
                  resident((D_INNER, D_MODEL), const2),
                  resident((D_MODEL, D_MODEL), const2),
                  resident((D_MODEL, D_MODEL), const2),
                  resident((1, D_MODEL), const2),
                  resident((D_MODEL, D_FF), const2),
                  resident((D_FF, D_MODEL), const2),
                  resident((1, D_MODEL), const2),
                  resident((CHUNK, CHUNK), const2)],
        out_specs=pl.BlockSpec((tm, D_MODEL), lambda i: (i, 0)),
        out_shape=jax.ShapeDtypeStruct((t, D_MODEL), F32),
        scratch_shapes=[pltpu.VMEM((SLABS, SUBLANES, D_MODEL), F32)],
        compiler_params=pltpu.CompilerParams(
            dimension_semantics=("arbitrary",), vmem_limit_bytes=VMEM_LIMIT),
        name="gate_pool_merge_mlp",
    )(x2d, y_raw, proj, proj, proj, mod3, ssd_norm_w, w_bs, w_pool, w_out, norm_mlp_w, w_up, w_dn,
      norm_final_w, unperm)


def _cumsum_matrix():
    tok = _token_of_row(np.arange(CHUNK))
    tri = (tok[None, :] <= tok[:, None]).astype(np.float32)
    return jnp.asarray(np.concatenate([tri] * DT_REP, axis=1), BF16)


def _interleave_matrix():
    p = np.zeros((CHUNK, CHUNK), np.float32)
    p[np.arange(CHUNK), _token_of_row(np.arange(CHUNK))] = 1.0
    return jnp.asarray(p, BF16), jnp.asarray(p.T, BF16)


def _head_expand_matrix():
    m = np.zeros((LANES, D_INNER), np.float32)
    for k in range(DT_REP * N_HEADS):
        h = k % N_HEADS
        m[k, h * HEAD_DIM:(h + 1) * HEAD_DIM] = 1.0
    return jnp.asarray(m, BF16)


def kernel(x, c, w_ada, b_ada, norm_mix_w, w_in, conv_w, conv_b, dt_bias, a_log, d_skip, ssd_norm_w,
           w_branch_ssd, pool_w, pool_scale, w_branch_pool, w_out, norm_mlp_w, w_up, w_down,
           norm_final_w):
    batch, seq, d = x.shape
    depth = w_ada.shape[0]
    assert depth == 1, "the final norm is fused into the last kernel of a single layer"
    t = batch * seq
    tri3 = _cumsum_matrix()
    expand3 = _head_expand_matrix()
    perm, unperm = _interleave_matrix()
    x2d = x.reshape(t, d)
    i = 0
    mod3 = _modulation(c, w_ada[i], b_ada[i][None, :])[:, None, :]
    w_in_t = jnp.swapaxes(w_in[i], 0, 1)
    w_main, w_dt3, w_pool, conv_w_cols, conv_b_cols, dt_bias3, a_log3 = _weight_prep(
        w_in_t, pool_w[i], pool_scale[i][:, None], w_branch_pool[i], conv_w[i], conv_b[i][None, :],
        dt_bias[i][None, :], a_log[i][None, :])
    proj, dt_raw = _in_projection(x2d, norm_mix_w[i][None, :], mod3, w_main, w_dt3, conv_w_cols,
                                  conv_b_cols, perm, seq)
    tail_weights = [w_branch_ssd[i], w_out[i], w_up[i], w_down[i]]
    y_raw, (w_bs, w_ob, w_upb, w_dnb) = _ssd_scan(
        proj, dt_raw, dt_bias3, a_log3, jnp.repeat(d_skip[i], HEAD_DIM)[None, :], tri3, expand3,
        tail_weights, batch, seq)
    out = _tail(x2d, y_raw, proj, mod3, ssd_norm_w[i][None, :], w_bs, w_pool, w_ob, norm_mlp_w[i][None, :],
                w_upb, w_dnb, norm_final_w[None, :], unperm, seq)
    return out.reshape(batch, seq, d)
```

```python
import functools

import jax
import jax.numpy as jnp
import numpy as np
from jax import lax
from jax.experimental import pallas as pl
from jax.experimental.pallas import tpu as pltpu

F32 = jnp.float32
BF16 = jnp.bfloat16

D_MODEL = 1024
D_INNER = 2 * D_MODEL
HEAD_DIM = 64
N_HEADS = D_INNER // HEAD_DIM
N_GROUPS = 4
GROUP_CH = D_INNER // N_GROUPS
D_STATE = 128
CONV_K = 4
CHUNK = 128
BC_WIDTH = 2 * N_GROUPS * D_STATE
POOL_WINDOWS = (2, 4, 8, 16)
POOL_GW = D_MODEL // len(POOL_WINDOWS)
D_FF = 4 * D_MODEL
N_MOD = 6
EPS = 1e-5
LOG2_E = 1.4426950408889634
LANES = 128
SUBLANES = 8
SLABS = CHUNK // SUBLANES
DT_REP = 3

PROJ_TN = 2048
COL_XS = D_INNER
COL_POOL = 2 * D_INNER + BC_WIDTH
PROJ_COLS = COL_POOL + 3 * D_MODEL
TILE_Z, TILE_XS, TILE_BC_POOL, TILE_GATE = 0, 1, 2, 3
CONV_COL_BLOCK = 512
PROJ_ROW_SPLIT = 4

VMEM_LIMIT = 56 * 1024 * 1024
SSD_CHUNKS_PER_STEP = 4


def _rms(x):
    return x * lax.rsqrt(jnp.mean(x * x, axis=-1, keepdims=True) + EPS)


def _dot_row_parts(x, w, parts):
    rows = x.shape[0] // parts
    return jnp.concatenate(
        [jnp.dot(x[m0:m0 + rows, :], w, preferred_element_type=F32) for m0 in range(0, x.shape[0], rows)],
        axis=0)


def _token_of_row(r):
    return (r % SUBLANES) * SLABS + r // SUBLANES


def _permute_chunks(perm, xb):
    n_chunks = xb.shape[0] // CHUNK
    return jnp.concatenate(
        [jnp.dot(perm, xb[c * CHUNK:(c + 1) * CHUNK, :], preferred_element_type=F32).astype(BF16)
         for c in range(n_chunks)], axis=0)


def _slabs_of(x, n_chunks):
    return [[x[(c * SLABS + v) * SUBLANES:(c * SLABS + v + 1) * SUBLANES, :] for v in range(SLABS)]
            for c in range(n_chunks)]


def _wrapped_slabs(slabs, prev_last, max_shift):
    n_chunks = len(slabs)
    width = slabs[0][0].shape[1]
    first_row = lax.broadcasted_iota(jnp.int32, (SUBLANES, width), 0) == 0
    wrapped = {}
    for j in range(1, max_shift + 1):
        stacked = jnp.concatenate([slabs[c][SLABS - j] for c in range(n_chunks)], axis=0)
        rolled = pltpu.roll(stacked, 1, 0)
        head = jnp.where(first_row, pltpu.roll(prev_last[j], 1, 0), rolled[0:SUBLANES, :])
        wrapped[j] = [head] + [rolled[c * SUBLANES:(c + 1) * SUBLANES, :] for c in range(1, n_chunks)]
    return wrapped


def _shifted(slabs, wrapped, c, v, k):
    return slabs[c][v - k] if v >= k else wrapped[k - v][c]


def _mod_kernel(c_ref, w_ref, b_ref, o_ref):
    batch = c_ref.shape[0]
    s = jax.nn.silu(c_ref[...])
    s = jnp.concatenate([s, jnp.zeros((SUBLANES - batch, s.shape[1]), F32)], axis=0)
    mod = jnp.dot(s.astype(BF16), w_ref[...].astype(BF16), preferred_element_type=F32) + b_ref[...]
    o_ref[...] = mod[0:batch, :]


def _modulation(c, w_ada, b_ada):
    batch = c.shape[0]
    n = w_ada.shape[1]
    tn = 1024
    return pl.pallas_call(
        _mod_kernel,
        grid=(n // tn,),
        in_specs=[pl.BlockSpec((batch, D_MODEL), lambda j: (0, 0)),
                  pl.BlockSpec((D_MODEL, tn), lambda j: (0, j)),
                  pl.BlockSpec((1, tn), lambda j: (0, j))],
        out_specs=pl.BlockSpec((batch, tn), lambda j: (0, j)),
        out_shape=jax.ShapeDtypeStruct((batch, n), F32),
        name="adaln_mod",
    )(c, w_ada, b_ada)


def _rep_heads(v):
    pad = jnp.zeros((v.shape[0], LANES - DT_REP * N_HEADS), v.dtype)
    return jnp.concatenate([v] * DT_REP + [pad], axis=1)


def _weight_prep_kernel(wt_ref, wdt_ref, pw_ref, ps_ref, wbp_ref, cw_ref, cb_ref, dtb_ref, alog_ref,
                        o_ref, odt_ref, opool_ref, ocw_ref, ocb_ref, odtb_ref, oalog_ref, *, conv_steps):
    r = pl.program_id(0)
    o_ref[...] = wt_ref[...].T.astype(BF16)

    on_conv = jnp.logical_and(r >= conv_steps[0], r < conv_steps[1])
    ocw_ref[...] = jnp.where(on_conv, cw_ref[...], 0.0)
    ocb_ref[...] = jnp.where(on_conv, cb_ref[...], 0.0)

    @pl.when(r == 0)
    def _():
        odt_ref[...] = _rep_heads(wdt_ref[...].T[:, 0:N_HEADS])
        odtb_ref[...] = _rep_heads(dtb_ref[...])
        oalog_ref[...] = _rep_heads(alog_ref[...])

    @pl.when(r < len(POOL_WINDOWS))
    def _():
        scaled = ps_ref[...] * wbp_ref[...]
        opool_ref[...] = jnp.dot(pw_ref[...], scaled, precision=lax.Precision.HIGHEST,
                                 preferred_element_type=F32).astype(BF16)


def _weight_prep(w_in_t, pool_w, pool_scale_col, w_bp, conv_w, conv_b, dt_bias, a_log):
    cols = 1024
    k = w_in_t.shape[1]
    n = w_bp.shape[1]
    first_after_dt = COL_POOL // cols
    last_group = len(POOL_WINDOWS) - 1
    group = lambda r: jnp.minimum(r, last_group)
    conv_steps = (COL_XS // cols, COL_POOL // cols)
    conv_blk = lambda r: (0, jnp.clip(r - conv_steps[0], 0, conv_steps[1] - conv_steps[0] - 1))
    whole = lambda r: (0, 0)

    def src_row(r):
        return (r * (cols // N_HEADS) + jnp.where(r >= first_after_dt, 1, 0)) * N_HEADS

    return pl.pallas_call(
        functools.partial(_weight_prep_kernel, conv_steps=conv_steps),
        grid=(PROJ_COLS // cols,),
        in_specs=[pl.BlockSpec((pl.Element(cols), pl.Element(k)), lambda r: (src_row(r), 0)),
                  pl.BlockSpec((LANES, k), lambda r: (COL_POOL // LANES, 0)),
                  pl.BlockSpec((None, POOL_GW, POOL_GW), lambda r: (group(r), 0, 0)),
                  pl.BlockSpec((POOL_GW, 1), lambda r: (group(r), 0)),
                  pl.BlockSpec((POOL_GW, n), lambda r: (group(r), 0)),
                  pl.BlockSpec((CONV_K, cols), conv_blk),
                  pl.BlockSpec((1, cols), conv_blk),
                  pl.BlockSpec((1, N_HEADS), whole),
                  pl.BlockSpec((1, N_HEADS), whole)],
        out_specs=[pl.BlockSpec((None, k, cols), lambda r: (r // (PROJ_TN // cols), 0, r % (PROJ_TN // cols))),
                   pl.BlockSpec((k, LANES), whole),
                   pl.BlockSpec((POOL_GW, n), lambda r: (group(r), 0)),
                   pl.BlockSpec((CONV_K, cols), lambda r: (0, r)),
                   pl.BlockSpec((1, cols), lambda r: (0, r)),
                   pl.BlockSpec((1, LANES), whole),
                   pl.BlockSpec((1, LANES), whole)],
        out_shape=[jax.ShapeDtypeStruct((PROJ_COLS // PROJ_TN, k, PROJ_TN), BF16),
                   jax.ShapeDtypeStruct((k, LANES), F32),
                   jax.ShapeDtypeStruct((D_MODEL, n), BF16),
                   jax.ShapeDtypeStruct((CONV_K, PROJ_COLS), F32),
                   jax.ShapeDtypeStruct((1, PROJ_COLS), F32),
                   jax.ShapeDtypeStruct((1, LANES), F32),
                   jax.ShapeDtypeStruct((1, LANES), F32)],
        name="weight_prep",
    )(w_in_t, w_in_t, pool_w, pool_scale_col, w_bp, conv_w, conv_b, dt_bias, a_log)


def _conv_silu_store(r, cw_ref, cb_ref, halo_ref, o_ref, width):
    n_chunks = r.shape[0] // CHUNK
    pair = 2 * SUBLANES
    for c0 in range(0, width, CONV_COL_BLOCK):
        cols = slice(c0, c0 + CONV_COL_BLOCK)
        slabs = _slabs_of(r[:, cols], n_chunks)
        prev_last = {j: halo_ref[j - 1, :, cols] for j in range(1, CONV_K)}
        for j in range(1, CONV_K):
            halo_ref[j - 1, :, cols] = slabs[n_chunks - 1][SLABS - j]
        wrapped = _wrapped_slabs(slabs, prev_last, CONV_K - 1)
        taps = [cw_ref[k:k + 1, cols] for k in range(CONV_K)]
        bias = cb_ref[:, cols]
        for c in range(n_chunks):
            for v0 in range(0, SLABS, 2):
                accs = []
                for v in (v0, v0 + 1):
                    acc = slabs[c][v] * taps[CONV_K - 1] + bias
                    for k in range(1, CONV_K):
                        acc = acc + _shifted(slabs, wrapped, c, v, k) * taps[CONV_K - 1 - k]
                    accs.append(acc)
                lo = (c * SLABS + v0) * SUBLANES
                o_ref[lo:lo + pair, cols] = jax.nn.silu(jnp.concatenate(accs, axis=0)).astype(BF16)


def _inproj_kernel(x_ref, nw_ref, mod_ref, w_ref, wdt_ref, cw_ref, cb_ref, perm_ref, o_ref, dt_ref,
                   h_scr, halo_scr, *, tiles_per_batch):
    i = pl.program_id(0)
    j = pl.program_id(1)

    @pl.when(j == 0)
    def _():
        mod = mod_ref[0]
        shift = mod[:, 0:D_MODEL]
        scale = mod[:, D_MODEL:2 * D_MODEL]
        h = _rms(x_ref[...]) * (nw_ref[...] * (1.0 + scale)) + shift
        hb = _permute_chunks(perm_ref[...], h.astype(BF16))
        h_scr[...] = hb
        dt_ref[...] = jnp.dot(hb, wdt_ref[...].astype(BF16), preferred_element_type=F32)

    @pl.when(jnp.logical_and(i % tiles_per_batch == 0, j == 0))
    def _():
        halo_scr[...] = jnp.zeros_like(halo_scr)

    def proj():
        return _dot_row_parts(h_scr[...], w_ref[...], PROJ_ROW_SPLIT)

    @pl.when(j == TILE_Z)
    def _():
        o_ref[...] = jax.nn.silu(proj()).astype(BF16)

    @pl.when(j == TILE_XS)
    def _():
        _conv_silu_store(proj(), cw_ref, cb_ref, halo_scr.at[0], o_ref, PROJ_TN)

    @pl.when(j == TILE_BC_POOL)
    def _():
        r = proj()
        _conv_silu_store(r[:, 0:BC_WIDTH], cw_ref, cb_ref, halo_scr.at[1], o_ref, BC_WIDTH)
        o_ref[:, BC_WIDTH:] = r[:, BC_WIDTH:].astype(BF16)

    @pl.when(j == TILE_GATE)
    def _():
        o_ref[...] = jax.nn.sigmoid(proj()).astype(BF16)


def _in_projection(x2d, norm_w, mod3, w_main, w_dt, conv_w_cols, conv_b_cols, perm, seq):
    t = x2d.shape[0]
    tm, tn = 1024, PROJ_TN
    tiles_per_batch = seq // tm
    return pl.pallas_call(
        functools.partial(_inproj_kernel, tiles_per_batch=tiles_per_batch),
        grid=(t // tm, PROJ_COLS // tn),
        in_specs=[pl.BlockSpec((tm, D_MODEL), lambda i, j: (i, 0)),
                  pl.BlockSpec((1, D_MODEL), lambda i, j: (0, 0)),
                  pl.BlockSpec((1, 1, N_MOD * D_MODEL), lambda i, j: (i // tiles_per_batch, 0, 0)),
                  pl.BlockSpec((None, D_MODEL, tn), lambda i, j: (j, 0, 0)),
                  pl.BlockSpec((D_MODEL, LANES), lambda i, j: (0, 0)),
                  pl.BlockSpec((CONV_K, tn), lambda i, j: (0, j)),
                  pl.BlockSpec((1, tn), lambda i, j: (0, j)),
                  pl.BlockSpec((CHUNK, CHUNK), lambda i, j: (0, 0))],
        out_specs=[pl.BlockSpec((None, tm, tn), lambda i, j: (j, i, 0)),
                   pl.BlockSpec((tm, LANES), lambda i, j: (i, 0))],
        out_shape=[jax.ShapeDtypeStruct((PROJ_COLS // tn, t, tn), BF16),
                   jax.ShapeDtypeStruct((t, LANES), F32)],
        scratch_shapes=[pltpu.VMEM((tm, D_MODEL), BF16),
                        pltpu.VMEM((2, CONV_K - 1, SUBLANES, tn), F32)],
        compiler_params=pltpu.CompilerParams(
            dimension_semantics=("arbitrary", "arbitrary"), vmem_limit_bytes=VMEM_LIMIT),
        name="norm_in_proj",
    )(x2d, norm_w, mod3, w_main, w_dt, conv_w_cols, conv_b_cols, perm)


def _split3(v):
    hi = v.astype(BF16).astype(F32)
    r1 = v - hi
    mid = r1.astype(BF16).astype(F32)
    return hi, r1, r1 - mid


def _lane_pieces(v, lane):
    hi, r1, r2 = _split3(v)
    return jnp.where(lane < N_HEADS, hi, jnp.where(lane < 2 * N_HEADS, r1, r2)).astype(BF16)


def _ssd_prologue(xs_ref, dt_ref, dtb_ref, alog_ref, tri_ref, exp_ref, xdb, expd, acs2_scr, src_scr):
    q = CHUNK
    tm = dt_ref.shape[0]
    n_chunks = tm // q
    dt = jax.nn.softplus(dt_ref[...] + dtb_ref[...])
    dta = dt * (-jnp.exp(alog_ref[...]))
    hi, r1, r2 = _split3(dta)
    acs = []
    for c in range(n_chunks):
        rows = slice(c * q, (c + 1) * q)
        stacked = jnp.concatenate([hi[rows], r1[rows], r2[rows]], axis=0).astype(BF16)
        acs.append(jnp.dot(tri_ref[...], stacked, preferred_element_type=F32))
    acs2 = jnp.concatenate(acs, axis=0) * LOG2_E
    acs2_last = jnp.concatenate(
        [jnp.broadcast_to(acs2[(c + 1) * q - 1:(c + 1) * q, :], (q, LANES)) for c in range(n_chunks)], axis=0)
    decay_to_end = jnp.exp2(acs2_last - acs2)
    decay_from_start = jnp.exp2(acs2)
    acs2_scr[...] = acs2
    src = acs2 - jnp.log2(dt)
    for c in range(n_chunks):
        src_scr[c] = src[c * q:(c + 1) * q, :].T

    lane = lax.broadcasted_iota(jnp.int32, (tm, LANES), 1)
    pieces = jnp.concatenate([_lane_pieces(dt * decay_to_end, lane),
                              _lane_pieces(decay_from_start, lane)], axis=0)
    expd[...] = jnp.dot(pieces, exp_ref[...], preferred_element_type=F32)
    xdb[...] = (xs_ref[...].astype(F32) * expd[0:tm, :]).astype(BF16)


def _ssd_chunk(ci, xs_ref, bc_ref, dskip_ref, o_ref, state, xdb, expd, acs2_scr, src_scr):
    q = CHUNK
    tm = xs_ref.shape[0]
    rows = slice(ci * q, (ci + 1) * q)
    off_rows = slice(tm + ci * q, tm + (ci + 1) * q)
    acs2 = acs2_scr[rows, :]
    src_t = src_scr[ci]

    causal = (_token_of_row(lax.broadcasted_iota(jnp.int32, (q, q), 0))
              >= _token_of_row(lax.broadcasted_iota(jnp.int32, (q, q), 1)))
    half = lax.broadcasted_iota(jnp.int32, (q, LANES), 1) < HEAD_DIM
    zero_b = jnp.zeros((q, LANES), BF16)

    for g in range(N_GROUPS):
        gs = slice(g * GROUP_CH, (g + 1) * GROUP_CH)
        bg = bc_ref[rows, g * D_STATE:(g + 1) * D_STATE]
        cg = bc_ref[rows, (N_GROUPS + g) * D_STATE:(N_GROUPS + g + 1) * D_STATE]
        scores = lax.dot_general(cg, bg, (((1,), (1,)), ((), ())), preferred_element_type=F32)
        st_old = state[:, gs]
        y_off = jnp.dot(cg, st_old.astype(BF16), preferred_element_type=F32)
        new_t = lax.dot_general(bg, xdb[rows, gs], (((0,), (0,)), ((), ())),
                                preferred_element_type=F32)
        state[:, gs] = st_old * expd[off_rows.stop - 1:off_rows.stop, gs] + new_t
        ys = []
        for pr in range(GROUP_CH // LANES):
            h0 = g * (GROUP_CH // HEAD_DIM) + 2 * pr
            ms = []
            for h in (h0, h0 + 1):
                a_col = jnp.broadcast_to(acs2[:, h:h + 1], (q, q))
                a_row = jnp.broadcast_to(src_t[h:h + 1, :], (q, q))
                decay_dt = jnp.exp2(jnp.where(causal, a_col - a_row, -jnp.inf))
                ms.append((scores * decay_dt).astype(BF16))
            xp = xs_ref[rows, g * GROUP_CH + pr * LANES:g * GROUP_CH + (pr + 1) * LANES]
            rhs = jnp.concatenate([jnp.where(half, xp, zero_b), jnp.where(half, zero_b, xp)], axis=0)
            ys.append(jnp.dot(jnp.concatenate(ms, axis=1), rhs, preferred_element_type=F32))
        y = jnp.concatenate(ys, axis=1) + y_off * expd[off_rows, gs]
        o_ref[rows, gs] = (y + dskip_ref[:, gs] * xs_ref[rows, gs].astype(F32)).astype(BF16)


def _ssd_kernel(xs_ref, bc_ref, dt_ref, dtb_ref, alog_ref, dskip_ref, tri_ref, exp_ref, scale0_ref, *rest,
                n_cast):
    cast_in = rest[:n_cast]
    o_ref = rest[n_cast]
    cast_out = rest[n_cast + 1:2 * n_cast + 1]
    state, xdb, expd, acs2_scr, src_scr = rest[2 * n_cast + 1:]

    @pl.when(pl.program_id(1) == 0)
    def _():
        state[...] = jnp.zeros_like(state)

    cast_out[0][...] = (cast_in[0][...] * scale0_ref[...]).astype(BF16)
    for w_ref, wb_ref in zip(cast_in[1:], cast_out[1:]):
        wb_ref[...] = w_ref[...].astype(BF16)

    _ssd_prologue(xs_ref, dt_ref, dtb_ref, alog_ref, tri_ref, exp_ref, xdb, expd, acs2_scr, src_scr)
    for ci in range(SSD_CHUNKS_PER_STEP):
        _ssd_chunk(ci, xs_ref, bc_ref, dskip_ref, o_ref, state, xdb, expd, acs2_scr, src_scr)


def _ssd_scan(proj, dt_raw, dt_bias3, a_log3, d_skip_ch, tri3, expand3, row_scale0, cast_weights, batch, seq):
    q = CHUNK
    tm = SSD_CHUNKS_PER_STEP * q
    steps = seq // tm
    t = batch * seq
    n_steps = batch * steps
    rowmap = lambda b, c: b * steps + c
    const = lambda b, c: (0, 0)
    slab_specs = [pl.BlockSpec((w.shape[0] // n_steps, w.shape[1]), lambda b, c: (rowmap(b, c), 0))
                  for w in cast_weights]
    outs = pl.pallas_call(
        functools.partial(_ssd_kernel, n_cast=len(cast_weights)),
        grid=(batch, steps),
        in_specs=[pl.BlockSpec((None, tm, D_INNER), lambda b, c: (TILE_XS, rowmap(b, c), 0)),
                  pl.BlockSpec((None, tm, BC_WIDTH), lambda b, c: (TILE_BC_POOL, rowmap(b, c), 0)),
                  pl.BlockSpec((tm, LANES), lambda b, c: (rowmap(b, c), 0)),
                  pl.BlockSpec((1, LANES), const),
                  pl.BlockSpec((1, LANES), const),
                  pl.BlockSpec((1, D_INNER), const),
                  pl.BlockSpec((q, DT_REP * q), const),
                  pl.BlockSpec((LANES, D_INNER), const),
                  pl.BlockSpec((cast_weights[0].shape[0] // n_steps, 1), lambda b, c: (rowmap(b, c), 0))]
        + slab_specs,
        out_specs=[pl.BlockSpec((tm, D_INNER), lambda b, c: (rowmap(b, c), 0))] + slab_specs,
        out_shape=[jax.ShapeDtypeStruct((t, D_INNER), BF16)]
        + [jax.ShapeDtypeStruct(w.shape, BF16) for w in cast_weights],
        scratch_shapes=[pltpu.VMEM((D_STATE, D_INNER), F32),
                        pltpu.VMEM((tm, D_INNER), BF16),
                        pltpu.VMEM((2 * tm, D_INNER), F32),
                        pltpu.VMEM((tm, LANES), F32),
                        pltpu.VMEM((SSD_CHUNKS_PER_STEP, LANES, q), F32)],
        compiler_params=pltpu.CompilerParams(
            dimension_semantics=("arbitrary", "arbitrary"), vmem_limit_bytes=VMEM_LIMIT),
        name="ssd_scan",
    )(proj, proj, dt_raw, dt_bias3, a_log3, d_skip_ch, tri3, expand3, row_scale0, *cast_weights)
    return outs[0], outs[1:]


def _pooled_groups(u, halo_ref, pos0):
    n_chunks = u.shape[0] // CHUNK
    max_shift = max(POOL_WINDOWS) - 1
    slabs = _slabs_of(u, n_chunks)
    prev_last = {j: halo_ref[SLABS - j] for j in range(1, max_shift + 1)}
    for v in range(SLABS):
        halo_ref[v] = slabs[n_chunks - 1][v]
    wrapped = _wrapped_slabs(slabs, prev_last, max_shift)
    sub = lax.broadcasted_iota(jnp.int32, (SUBLANES, POOL_GW), 0)
    pooled = []
    for gi, win in enumerate(POOL_WINDOWS):
        cs = slice(gi * POOL_GW, (gi + 1) * POOL_GW)
        out = []
        for c in range(n_chunks):
            for v in range(SLABS):
                cur = slabs[c][v][:, cs]
                tot = cur
                for k in range(1, win):
                    tot = tot + _shifted(slabs, wrapped, c, v, k)[:, cs]
                pos = pos0 + c * CHUNK + sub * SLABS + v
                count = jnp.minimum(pos + 1, win).astype(F32)
                out.append(tot / count - cur)
        pooled.append(jnp.concatenate(out, axis=0).astype(BF16))
    return pooled


def _tail_kernel(x_ref, y_ref, zs_ref, u_ref, g_ref, mod_ref, wbs_ref, wpool_ref,
                 wo_ref, nmlp_ref, wup_ref, wdn_ref, nfin_ref, unperm_ref, o_ref, uhalo,
                 *, tm, tiles_per_batch):
    i = pl.program_id(0)

    @pl.when(i % tiles_per_batch == 0)
    def _():
        uhalo[...] = jnp.zeros_like(uhalo)

    y_gated = []
    for g in range(N_GROUPS):
        gs = slice(g * GROUP_CH, (g + 1) * GROUP_CH)
        yg = y_ref[:, gs].astype(F32) * zs_ref[:, gs].astype(F32)
        y_gated.append(_rms(yg).astype(BF16))
    y_ssd = jnp.dot(jnp.concatenate(y_gated, axis=1), wbs_ref[...], preferred_element_type=F32)

    pooled = _pooled_groups(u_ref[...].astype(F32), uhalo, (i % tiles_per_batch) * tm)
    y_pool = jnp.dot(jnp.concatenate(pooled, axis=1), wpool_ref[...], preferred_element_type=F32)

    merged = (g_ref[:, :D_MODEL].astype(F32) * y_ssd + g_ref[:, D_MODEL:].astype(F32) * y_pool)
    merged = _permute_chunks(unperm_ref[...], merged.astype(BF16))
    mix = jnp.dot(merged, wo_ref[...], preferred_element_type=F32)

    mod = mod_ref[0]
    gate_m = mod[:, 2 * D_MODEL:3 * D_MODEL]
    shift_f = mod[:, 3 * D_MODEL:4 * D_MODEL]
    scale_f = mod[:, 4 * D_MODEL:5 * D_MODEL]
    gate_f = mod[:, 5 * D_MODEL:6 * D_MODEL]
    x1 = x_ref[...] + gate_m * mix
    h = _rms(x1) * (nmlp_ref[...] * (1.0 + scale_f)) + shift_f
    up = jnp.dot(h.astype(BF16), wup_ref[...], preferred_element_type=F32)
    act = jnp.square(jnp.maximum(up, 0.0)).astype(BF16)
    x2 = x1 + gate_f * jnp.dot(act, wdn_ref[...], preferred_element_type=F32)
    o_ref[...] = _rms(x2) * nfin_ref[...]


def _tail(x2d, y_raw, proj, mod3, w_bs, w_pool, w_out, norm_mlp_w, w_up, w_dn, norm_final_w,
          unperm, seq):
    t = x2d.shape[0]
    tm = 512
    tiles_per_batch = seq // tm
    const2 = lambda i: (0, 0)
    resident = functools.partial(pl.BlockSpec, pipeline_mode=pl.Buffered(1))
    return pl.pallas_call(
        functools.partial(_tail_kernel, tm=tm, tiles_per_batch=tiles_per_batch),
        grid=(t // tm,),
        in_specs=[pl.BlockSpec((tm, D_MODEL), lambda i: (i, 0)),
                  pl.BlockSpec((tm, D_INNER), lambda i: (i, 0)),
                  pl.BlockSpec((None, tm, D_INNER), lambda i: (TILE_Z, i, 0)),
                  pl.BlockSpec((None, tm, D_MODEL), lambda i: (TILE_BC_POOL, i, BC_WIDTH // D_MODEL)),
                  pl.BlockSpec((None, tm, 2 * D_MODEL), lambda i: (TILE_GATE, i, 0)),
                  pl.BlockSpec((1, 1, N_MOD * D_MODEL), lambda i: (i // tiles_per_batch, 0, 0)),
                  resident((D_INNER, D_MODEL), const2),
                  resident((D_MODEL, D_MODEL), const2),
                  resident((D_MODEL, D_MODEL), const2),
                  resident((1, D_MODEL), const2),
                  resident((D_MODEL, D_FF), const2),
                  resident((D_FF, D_MODEL), const2),
                  resident((1, D_MODEL), const2),
                  resident((CHUNK, CHUNK), const2)],
        out_specs=pl.BlockSpec((tm, D_MODEL), lambda i: (i, 0)),
        out_shape=jax.ShapeDtypeStruct((t, D_MODEL), F32),
        scratch_shapes=[pltpu.VMEM((SLABS, SUBLANES, D_MODEL), F32)],
        compiler_params=pltpu.CompilerParams(
            dimension_semantics=("arbitrary",), vmem_limit_bytes=VMEM_LIMIT),
        name="gate_pool_merge_mlp",
    )(x2d, y_raw, proj, proj, proj, mod3, w_bs, w_pool, w_out, norm_mlp_w, w_up, w_dn,
      norm_final_w, unperm)


def _cumsum_matrix():
    tok = _token_of_row(np.arange(CHUNK))
    tri = (tok[None, :] <= tok[:, None]).astype(np.float32)
    return jnp.asarray(np.concatenate([tri] * DT_REP, axis=1), BF16)


def _interleave_matrix():
    p = np.zeros((CHUNK, CHUNK), np.float32)
    p[np.arange(CHUNK), _token_of_row(np.arange(CHUNK))] = 1.0
    return jnp.asarray(p, BF16), jnp.asarray(p.T, BF16)


def _head_expand_matrix():
    m = np.zeros((LANES, D_INNER), np.float32)
    for k in range(DT_REP * N_HEADS):
        h = k % N_HEADS
        m[k, h * HEAD_DIM:(h + 1) * HEAD_DIM] = 1.0
    return jnp.asarray(m, BF16)


def kernel(x, c, w_ada, b_ada, norm_mix_w, w_in, conv_w, conv_b, dt_bias, a_log, d_skip, ssd_norm_w,
           w_branch_ssd, pool_w, pool_scale, w_branch_pool, w_out, norm_mlp_w, w_up, w_down,
           norm_final_w):
    batch, seq, d = x.shape
    depth = w_ada.shape[0]
    assert depth == 1, "the final norm is fused into the last kernel of a single layer"
    t = batch * seq
    tri3 = _cumsum_matrix()
    expand3 = _head_expand_matrix()
    perm, unperm = _interleave_matrix()
    x2d = x.reshape(t, d)
    i = 0
    mod3 = _modulation(c, w_ada[i], b_ada[i][None, :])[:, None, :]
    w_in_t = jnp.swapaxes(w_in[i], 0, 1)
    w_main, w_dt3, w_pool, conv_w_cols, conv_b_cols, dt_bias3, a_log3 = _weight_prep(
        w_in_t, pool_w[i], pool_scale[i][:, None], w_branch_pool[i], conv_w[i], conv_b[i][None, :],
        dt_bias[i][None, :], a_log[i][None, :])
    proj, dt_raw = _in_projection(x2d, norm_mix_w[i][None, :], mod3, w_main, w_dt3, conv_w_cols,
                                  conv_b_cols, perm, seq)
    tail_weights = [w_branch_ssd[i], w_out[i], w_up[i], w_down[i]]
    y_raw, (w_bs, w_ob, w_upb, w_dnb) = _ssd_scan(
        proj, dt_raw, dt_bias3, a_log3, jnp.repeat(d_skip[i], HEAD_DIM)[None, :], tri3, expand3,
        ssd_norm_w[i][:, None], tail_weights, batch, seq)
    out = _tail(x2d, y_raw, proj, mod3, w_bs, w_pool, w_ob, norm_mlp_w[i][None, :], w_upb, w_dnb,
                norm_final_w[None, :], unperm, seq)
    return out.reshape(batch, seq, d)
```

```python
import functools

import jax
import jax.numpy as jnp
import numpy as np
from jax import lax
from jax.experimental import pallas as pl
from jax.experimental.pallas import tpu as pltpu

F32 = jnp.float32
BF16 = jnp.bfloat16

D_MODEL = 1024
D_INNER = 2 * D_MODEL
HEAD_DIM = 64
N_HEADS = D_INNER // HEAD_DIM
N_GROUPS = 4
GROUP_CH = D_INNER // N_GROUPS
D_STATE = 128
CONV_K = 4
CHUNK = 128
BC_WIDTH = 2 * N_GROUPS * D_STATE
POOL_WINDOWS = (2, 4, 8, 16)
POOL_GW = D_MODEL // len(POOL_WINDOWS)
D_FF = 4 * D_MODEL
N_MOD = 6
EPS = 1e-5
LOG2_E = 1.4426950408889634
LANES = 128
SUBLANES = 8
SLABS = CHUNK // SUBLANES
DT_REP = 3

PROJ_TN = 2048
COL_XS = D_INNER
COL_POOL = 2 * D_INNER + BC_WIDTH
PROJ_COLS = COL_POOL + 3 * D_MODEL
TILE_Z, TILE_XS, TILE_BC_POOL, TILE_GATE = 0, 1, 2, 3
CONV_COL_BLOCK = 512
PROJ_ROW_SPLIT = 4

VMEM_LIMIT = 56 * 1024 * 1024
SSD_CHUNKS_PER_STEP = 4


def _rms(x):
    return x * lax.rsqrt(jnp.mean(x * x, axis=-1, keepdims=True) + EPS)


def _dot_row_parts(x, w, parts):
    rows = x.shape[0] // parts
    return jnp.concatenate(
        [jnp.dot(x[m0:m0 + rows, :], w, preferred_element_type=F32) for m0 in range(0, x.shape[0], rows)],
        axis=0)


def _token_of_row(r):
    return (r % SUBLANES) * SLABS + r // SUBLANES


def _permute_chunks(perm, xb):
    n_chunks = xb.shape[0] // CHUNK
    return jnp.concatenate(
        [jnp.dot(perm, xb[c * CHUNK:(c + 1) * CHUNK, :], preferred_element_type=F32).astype(BF16)
         for c in range(n_chunks)], axis=0)


def _slabs_of(x, n_chunks):
    return [[x[(c * SLABS + v) * SUBLANES:(c * SLABS + v + 1) * SUBLANES, :] for v in range(SLABS)]
            for c in range(n_chunks)]


def _wrapped_slabs(slabs, prev_last, max_shift):
    n_chunks = len(slabs)
    width = slabs[0][0].shape[1]
    first_row = lax.broadcasted_iota(jnp.int32, (SUBLANES, width), 0) == 0
    wrapped = {}
    for j in range(1, max_shift + 1):
        stacked = jnp.concatenate([slabs[c][SLABS - j] for c in range(n_chunks)], axis=0)
        rolled = pltpu.roll(stacked, 1, 0)
        head = jnp.where(first_row, pltpu.roll(prev_last[j], 1, 0), rolled[0:SUBLANES, :])
        wrapped[j] = [head] + [rolled[c * SUBLANES:(c + 1) * SUBLANES, :] for c in range(1, n_chunks)]
    return wrapped


def _shifted(slabs, wrapped, c, v, k):
    return slabs[c][v - k] if v >= k else wrapped[k - v][c]


def _mod_kernel(c_ref, w_ref, b_ref, o_ref):
    batch = c_ref.shape[0]
    s = jax.nn.silu(c_ref[...])
    s = jnp.concatenate([s, jnp.zeros((SUBLANES - batch, s.shape[1]), F32)], axis=0)
    mod = jnp.dot(s.astype(BF16), w_ref[...].astype(BF16), preferred_element_type=F32) + b_ref[...]
    o_ref[...] = mod[0:batch, :]


def _modulation(c, w_ada, b_ada):
    batch = c.shape[0]
    n = w_ada.shape[1]
    tn = 2048
    return pl.pallas_call(
        _mod_kernel,
        grid=(n // tn,),
        in_specs=[pl.BlockSpec((batch, D_MODEL), lambda j: (0, 0)),
                  pl.BlockSpec((D_MODEL, tn), lambda j: (0, j)),
                  pl.BlockSpec((1, tn), lambda j: (0, j))],
        out_specs=pl.BlockSpec((batch, tn), lambda j: (0, j)),
        out_shape=jax.ShapeDtypeStruct((batch, n), F32),
        compiler_params=pltpu.CompilerParams(vmem_limit_bytes=VMEM_LIMIT),
        name="adaln_mod",
    )(c, w_ada, b_ada)


def _rep_heads(v):
    pad = jnp.zeros((v.shape[0], LANES - DT_REP * N_HEADS), v.dtype)
    return jnp.concatenate([v] * DT_REP + [pad], axis=1)


def _weight_prep_kernel(wt_ref, wdt_ref, pw_ref, ps_ref, wbp_ref, cw_ref, cb_ref, dtb_ref, alog_ref,
                        dskip_ref, exp_ref,
                        o_ref, odt_ref, opool_ref, ocw_ref, ocb_ref, odtb_ref, oalog_ref, odskip_ref,
                        *, conv_steps):
    r = pl.program_id(0)
    o_ref[...] = wt_ref[...].T.astype(BF16)

    on_conv = jnp.logical_and(r >= conv_steps[0], r < conv_steps[1])
    ocw_ref[...] = jnp.where(on_conv, cw_ref[...], 0.0)
    ocb_ref[...] = jnp.where(on_conv, cb_ref[...], 0.0)

    @pl.when(r == 0)
    def _():
        odt_ref[...] = _rep_heads(wdt_ref[...].T[:, 0:N_HEADS])
        odtb_ref[...] = _rep_heads(dtb_ref[...])
        oalog_ref[...] = _rep_heads(alog_ref[...])
        d_row = jnp.concatenate([dskip_ref[...], jnp.zeros((1, LANES - N_HEADS), F32)], axis=1)
        d8 = jnp.concatenate([d_row, jnp.zeros((SUBLANES - 1, LANES), F32)], axis=0)
        odskip_ref[...] = jnp.dot(d8, exp_ref[...].astype(F32), precision=lax.Precision.HIGHEST,
                                  preferred_element_type=F32)[0:1, :]

    @pl.when(r < len(POOL_WINDOWS))
    def _():
        opool_ref[...] = jnp.dot(pw_ref[...] * ps_ref[...], wbp_ref[...], precision=lax.Precision.HIGHEST,
                                 preferred_element_type=F32).astype(BF16)


def _weight_prep(w_in_t, pool_w, pool_scale, w_bp, conv_w, conv_b, dt_bias, a_log, d_skip, expand):
    cols = 1024
    k = w_in_t.shape[1]
    n = w_bp.shape[1]
    first_after_dt = COL_POOL // cols
    last_group = len(POOL_WINDOWS) - 1
    group = lambda r: jnp.minimum(r, last_group)
    conv_steps = (COL_XS // cols, COL_POOL // cols)
    conv_blk = lambda r: (0, jnp.clip(r - conv_steps[0], 0, conv_steps[1] - conv_steps[0] - 1))
    whole = lambda r: (0, 0)

    def src_row(r):
        return (r * (cols // N_HEADS) + jnp.where(r >= first_after_dt, 1, 0)) * N_HEADS

    return pl.pallas_call(
        functools.partial(_weight_prep_kernel, conv_steps=conv_steps),
        grid=(PROJ_COLS // cols,),
        in_specs=[pl.BlockSpec((pl.Element(cols), pl.Element(k)), lambda r: (src_row(r), 0)),
                  pl.BlockSpec((LANES, k), lambda r: (COL_POOL // LANES, 0)),
                  pl.BlockSpec((None, POOL_GW, POOL_GW), lambda r: (group(r), 0, 0)),
                  pl.BlockSpec((1, POOL_GW), lambda r: (0, group(r))),
                  pl.BlockSpec((POOL_GW, n), lambda r: (group(r), 0)),
                  pl.BlockSpec((CONV_K, cols), conv_blk),
                  pl.BlockSpec((1, cols), conv_blk),
                  pl.BlockSpec((1, N_HEADS), whole),
                  pl.BlockSpec((1, N_HEADS), whole),
                  pl.BlockSpec((1, N_HEADS), whole),
                  pl.BlockSpec((LANES, D_INNER), whole)],
        out_specs=[pl.BlockSpec((None, k, cols), lambda r: (r // (PROJ_TN // cols), 0, r % (PROJ_TN // cols))),
                   pl.BlockSpec((k, LANES), whole),
                   pl.BlockSpec((POOL_GW, n), lambda r: (group(r), 0)),
                   pl.BlockSpec((CONV_K, cols), lambda r: (0, r)),
                   pl.BlockSpec((1, cols), lambda r: (0, r)),
                   pl.BlockSpec((1, LANES), whole),
                   pl.BlockSpec((1, LANES), whole),
                   pl.BlockSpec((1, D_INNER), whole)],
        out_shape=[jax.ShapeDtypeStruct((PROJ_COLS // PROJ_TN, k, PROJ_TN), BF16),
                   jax.ShapeDtypeStruct((k, LANES), F32),
                   jax.ShapeDtypeStruct((D_MODEL, n), BF16),
                   jax.ShapeDtypeStruct((CONV_K, PROJ_COLS), F32),
                   jax.ShapeDtypeStruct((1, PROJ_COLS), F32),
                   jax.ShapeDtypeStruct((1, LANES), F32),
                   jax.ShapeDtypeStruct((1, LANES), F32),
                   jax.ShapeDtypeStruct((1, D_INNER), F32)],
        name="weight_prep",
    )(w_in_t, w_in_t, pool_w, pool_scale, w_bp, conv_w, conv_b, dt_bias, a_log, d_skip, expand)


def _conv_silu_store(r, cw_ref, cb_ref, col0, halo_ref, o_ref, width):
    n_chunks = r.shape[0] // CHUNK
    pair = 2 * SUBLANES
    for c0 in range(0, width, CONV_COL_BLOCK):
        cols = slice(c0, c0 + CONV_COL_BLOCK)
        slabs = _slabs_of(r[:, cols], n_chunks)
        prev_last = {j: halo_ref[j - 1, :, cols] for j in range(1, CONV_K)}
        for j in range(1, CONV_K):
            halo_ref[j - 1, :, cols] = slabs[n_chunks - 1][SLABS - j]
        wrapped = _wrapped_slabs(slabs, prev_last, CONV_K - 1)
        pcols = slice(col0 + c0, col0 + c0 + CONV_COL_BLOCK)
        taps = [cw_ref[k:k + 1, pcols] for k in range(CONV_K)]
        bias = cb_ref[:, pcols]
        for c in range(n_chunks):
            for v0 in range(0, SLABS, 2):
                accs = []
                for v in (v0, v0 + 1):
                    acc = slabs[c][v] * taps[CONV_K - 1] + bias
                    for k in range(1, CONV_K):
                        acc = acc + _shifted(slabs, wrapped, c, v, k) * taps[CONV_K - 1 - k]
                    accs.append(acc)
                lo = (c * SLABS + v0) * SUBLANES
                o_ref[lo:lo + pair, cols] = jax.nn.silu(jnp.concatenate(accs, axis=0)).astype(BF16)


def _inproj_kernel(x_ref, nw_ref, mod_ref, w_ref, wdt_ref, cw_ref, cb_ref, perm_ref, o_ref, dt_ref,
                   h_scr, halo_scr, *, tiles_per_batch):
    i = pl.program_id(0)
    j = pl.program_id(1)

    @pl.when(j == 0)
    def _():
        mod = mod_ref[0]
        shift = mod[:, 0:D_MODEL]
        scale = mod[:, D_MODEL:2 * D_MODEL]
        h = _rms(x_ref[...]) * (nw_ref[...] * (1.0 + scale)) + shift
        hb = _permute_chunks(perm_ref[...], h.astype(BF16))
        h_scr[...] = hb
        dt_ref[...] = jnp.dot(hb, wdt_ref[...].astype(BF16), preferred_element_type=F32)

    @pl.when(jnp.logical_and(i % tiles_per_batch == 0, j == 0))
    def _():
        halo_scr[...] = jnp.zeros_like(halo_scr)

    def proj():
        return _dot_row_parts(h_scr[...], w_ref[...], PROJ_ROW_SPLIT)

    @pl.when(j == TILE_Z)
    def _():
        o_ref[...] = jax.nn.silu(proj()).astype(BF16)

    @pl.when(j == TILE_XS)
    def _():
        _conv_silu_store(proj(), cw_ref, cb_ref, TILE_XS * PROJ_TN, halo_scr.at[0], o_ref, PROJ_TN)

    @pl.when(j == TILE_BC_POOL)
    def _():
        r = proj()
        _conv_silu_store(r[:, 0:BC_WIDTH], cw_ref, cb_ref, TILE_BC_POOL * PROJ_TN, halo_scr.at[1], o_ref,
                         BC_WIDTH)
        o_ref[:, BC_WIDTH:] = r[:, BC_WIDTH:].astype(BF16)

    @pl.when(j == TILE_GATE)
    def _():
        o_ref[...] = jax.nn.sigmoid(proj()).astype(BF16)


def _in_projection(x2d, norm_w, mod3, w_main, w_dt, conv_w_cols, conv_b_cols, perm, seq):
    t = x2d.shape[0]
    tm, tn = 1024, PROJ_TN
    tiles_per_batch = seq // tm
    return pl.pallas_call(
        functools.partial(_inproj_kernel, tiles_per_batch=tiles_per_batch),
        grid=(t // tm, PROJ_COLS // tn),
        in_specs=[pl.BlockSpec((tm, D_MODEL), lambda i, j: (i, 0)),
                  pl.BlockSpec((1, D_MODEL), lambda i, j: (0, 0)),
                  pl.BlockSpec((1, 1, N_MOD * D_MODEL), lambda i, j: (i // tiles_per_batch, 0, 0)),
                  pl.BlockSpec((None, D_MODEL, tn), lambda i, j: (j, 0, 0)),
                  pl.BlockSpec((D_MODEL, LANES), lambda i, j: (0, 0)),
                  pl.BlockSpec((CONV_K, PROJ_COLS), lambda i, j: (0, 0)),
                  pl.BlockSpec((1, PROJ_COLS), lambda i, j: (0, 0)),
                  pl.BlockSpec((CHUNK, CHUNK), lambda i, j: (0, 0))],
        out_specs=[pl.BlockSpec((None, tm, tn), lambda i, j: (j, i, 0)),
                   pl.BlockSpec((tm, LANES), lambda i, j: (i, 0))],
        out_shape=[jax.ShapeDtypeStruct((PROJ_COLS // tn, t, tn), BF16),
                   jax.ShapeDtypeStruct((t, LANES), F32)],
        scratch_shapes=[pltpu.VMEM((tm, D_MODEL), BF16),
                        pltpu.VMEM((2, CONV_K - 1, SUBLANES, tn), F32)],
        compiler_params=pltpu.CompilerParams(
            dimension_semantics=("arbitrary", "arbitrary"), vmem_limit_bytes=VMEM_LIMIT),
        name="norm_in_proj",
    )(x2d, norm_w, mod3, w_main, w_dt, conv_w_cols, conv_b_cols, perm)


def _split3(v):
    hi = v.astype(BF16).astype(F32)
    r1 = v - hi
    mid = r1.astype(BF16).astype(F32)
    return hi, r1, r1 - mid


def _lane_pieces(v, lane):
    hi, r1, r2 = _split3(v)
    return jnp.where(lane < N_HEADS, hi, jnp.where(lane < 2 * N_HEADS, r1, r2)).astype(BF16)


def _ssd_prologue(xs_ref, dt_ref, dtb_ref, alog_ref, tri_ref, exp_ref, xdb, expd, acs2_scr, src_scr):
    q = CHUNK
    tm = dt_ref.shape[0]
    n_chunks = tm // q
    dt = jax.nn.softplus(dt_ref[...] + dtb_ref[...])
    dta = dt * (-jnp.exp(alog_ref[...]))
    hi, r1, r2 = _split3(dta)
    acs = []
    for c in range(n_chunks):
        rows = slice(c * q, (c + 1) * q)
        stacked = jnp.concatenate([hi[rows], r1[rows], r2[rows]], axis=0).astype(BF16)
        acs.append(jnp.dot(tri_ref[...], stacked, preferred_element_type=F32))
    acs2 = jnp.concatenate(acs, axis=0) * LOG2_E
    acs2_last = jnp.concatenate(
        [jnp.broadcast_to(acs2[(c + 1) * q - 1:(c + 1) * q, :], (q, LANES)) for c in range(n_chunks)], axis=0)
    decay_to_end = jnp.exp2(acs2_last - acs2)
    decay_from_start = jnp.exp2(acs2)
    acs2_scr[...] = acs2
    src = acs2 - jnp.log2(dt)
    for c in range(n_chunks):
        src_scr[c] = src[c * q:(c + 1) * q, :].T

    lane = lax.broadcasted_iota(jnp.int32, (tm, LANES), 1)
    pieces = jnp.concatenate([_lane_pieces(dt * decay_to_end, lane),
                              _lane_pieces(decay_from_start, lane)], axis=0)
    expd[...] = jnp.dot(pieces, exp_ref[...], preferred_element_type=F32)
    xdb[...] = (xs_ref[...].astype(F32) * expd[0:tm, :]).astype(BF16)


def _ssd_chunk(ci, xs_ref, bc_ref, dskip_ref, o_ref, state, xdb, expd, acs2_scr, src_scr):
    q = CHUNK
    tm = xs_ref.shape[0]
    rows = slice(ci * q, (ci + 1) * q)
    off_rows = slice(tm + ci * q, tm + (ci + 1) * q)
    acs2 = acs2_scr[rows, :]
    src_t = src_scr[ci]

    causal = (_token_of_row(lax.broadcasted_iota(jnp.int32, (q, q), 0))
              >= _token_of_row(lax.broadcasted_iota(jnp.int32, (q, q), 1)))
    half = lax.broadcasted_iota(jnp.int32, (q, LANES), 1) < HEAD_DIM
    zero_b = jnp.zeros((q, LANES), BF16)

    for g in range(N_GROUPS):
        gs = slice(g * GROUP_CH, (g + 1) * GROUP_CH)
        bg = bc_ref[rows, g * D_STATE:(g + 1) * D_STATE]
        cg = bc_ref[rows, (N_GROUPS + g) * D_STATE:(N_GROUPS + g + 1) * D_STATE]
        scores = lax.dot_general(cg, bg, (((1,), (1,)), ((), ())), preferred_element_type=F32)
        st_old = state[:, gs]
        y_off = jnp.dot(cg, st_old.astype(BF16), preferred_element_type=F32)
        new_t = lax.dot_general(bg, xdb[rows, gs], (((0,), (0,)), ((), ())),
                                preferred_element_type=F32)
        state[:, gs] = st_old * expd[off_rows.stop - 1:off_rows.stop, gs] + new_t
        ys = []
        for pr in range(GROUP_CH // LANES):
            h0 = g * (GROUP_CH // HEAD_DIM) + 2 * pr
            ms = []
            for h in (h0, h0 + 1):
                a_col = jnp.broadcast_to(acs2[:, h:h + 1], (q, q))
                a_row = jnp.broadcast_to(src_t[h:h + 1, :], (q, q))
                decay_dt = jnp.exp2(jnp.where(causal, a_col - a_row, -jnp.inf))
                ms.append((scores * decay_dt).astype(BF16))
            xp = xs_ref[rows, g * GROUP_CH + pr * LANES:g * GROUP_CH + (pr + 1) * LANES]
            rhs = jnp.concatenate([jnp.where(half, xp, zero_b), jnp.where(half, zero_b, xp)], axis=0)
            ys.append(jnp.dot(jnp.concatenate(ms, axis=1), rhs, preferred_element_type=F32))
        y = jnp.concatenate(ys, axis=1) + y_off * expd[off_rows, gs]
        o_ref[rows, gs] = (y + dskip_ref[:, gs] * xs_ref[rows, gs].astype(F32)).astype(BF16)


def _ssd_kernel(xs_ref, bc_ref, dt_ref, dtb_ref, alog_ref, dskip_ref, tri_ref, exp_ref, scale0_ref, *rest,
                n_cast):
    cast_in = rest[:n_cast]
    o_ref = rest[n_cast]
    cast_out = rest[n_cast + 1:2 * n_cast + 1]
    state, xdb, expd, acs2_scr, src_scr = rest[2 * n_cast + 1:]

    @pl.when(pl.program_id(1) == 0)
    def _():
        state[...] = jnp.zeros_like(state)

    cast_out[0][...] = (cast_in[0][...] * scale0_ref[...]).astype(BF16)
    for w_ref, wb_ref in zip(cast_in[1:], cast_out[1:]):
        wb_ref[...] = w_ref[...].astype(BF16)

    _ssd_prologue(xs_ref, dt_ref, dtb_ref, alog_ref, tri_ref, exp_ref, xdb, expd, acs2_scr, src_scr)
    for ci in range(SSD_CHUNKS_PER_STEP):
        _ssd_chunk(ci, xs_ref, bc_ref, dskip_ref, o_ref, state, xdb, expd, acs2_scr, src_scr)


def _ssd_scan(proj, dt_raw, dt_bias3, a_log3, d_skip_ch, tri3, expand3, row_scale0, cast_weights, batch, seq):
    q = CHUNK
    tm = SSD_CHUNKS_PER_STEP * q
    steps = seq // tm
    t = batch * seq
    n_steps = batch * steps
    rowmap = lambda b, c: b * steps + c
    const = lambda b, c: (0, 0)
    slab_specs = [pl.BlockSpec((w.shape[0] // n_steps, w.shape[1]), lambda b, c: (rowmap(b, c), 0))
                  for w in cast_weights]
    outs = pl.pallas_call(
        functools.partial(_ssd_kernel, n_cast=len(cast_weights)),
        grid=(batch, steps),
        in_specs=[pl.BlockSpec((None, tm, D_INNER), lambda b, c: (TILE_XS, rowmap(b, c), 0)),
                  pl.BlockSpec((None, tm, BC_WIDTH), lambda b, c: (TILE_BC_POOL, rowmap(b, c), 0)),
                  pl.BlockSpec((tm, LANES), lambda b, c: (rowmap(b, c), 0)),
                  pl.BlockSpec((1, LANES), const),
                  pl.BlockSpec((1, LANES), const),
                  pl.BlockSpec((1, D_INNER), const),
                  pl.BlockSpec((q, DT_REP * q), const),
                  pl.BlockSpec((LANES, D_INNER), const),
                  pl.BlockSpec((cast_weights[0].shape[0] // n_steps, 1), lambda b, c: (rowmap(b, c), 0))]
        + slab_specs,
        out_specs=[pl.BlockSpec((tm, D_INNER), lambda b, c: (rowmap(b, c), 0))] + slab_specs,
        out_shape=[jax.ShapeDtypeStruct((t, D_INNER), BF16)]
        + [jax.ShapeDtypeStruct(w.shape, BF16) for w in cast_weights],
        scratch_shapes=[pltpu.VMEM((D_STATE, D_INNER), F32),
                        pltpu.VMEM((tm, D_INNER), BF16),
                        pltpu.VMEM((2 * tm, D_INNER), F32),
                        pltpu.VMEM((tm, LANES), F32),
                        pltpu.VMEM((SSD_CHUNKS_PER_STEP, LANES, q), F32)],
        compiler_params=pltpu.CompilerParams(
            dimension_semantics=("arbitrary", "arbitrary"), vmem_limit_bytes=VMEM_LIMIT),
        name="ssd_scan",
    )(proj, proj, dt_raw, dt_bias3, a_log3, d_skip_ch, tri3, expand3, row_scale0, *cast_weights)
    return outs[0], outs[1:]


def _pooled_groups(u, halo_ref, pos0):
    n_chunks = u.shape[0] // CHUNK
    max_shift = max(POOL_WINDOWS) - 1
    slabs = _slabs_of(u, n_chunks)
    prev_last = {j: halo_ref[SLABS - j] for j in range(1, max_shift + 1)}
    for v in range(SLABS):
        halo_ref[v] = slabs[n_chunks - 1][v]
    wrapped = _wrapped_slabs(slabs, prev_last, max_shift)
    sub = lax.broadcasted_iota(jnp.int32, (SUBLANES, POOL_GW), 0)
    pooled = []
    for gi, win in enumerate(POOL_WINDOWS):
        cs = slice(gi * POOL_GW, (gi + 1) * POOL_GW)
        out = []
        for c in range(n_chunks):
            for v in range(SLABS):
                cur = slabs[c][v][:, cs]
                tot = cur
                for k in range(1, win):
                    tot = tot + _shifted(slabs, wrapped, c, v, k)[:, cs]
                pos = pos0 + c * CHUNK + sub * SLABS + v
                count = jnp.minimum(pos + 1, win).astype(F32)
                out.append(tot / count - cur)
        pooled.append(jnp.concatenate(out, axis=0).astype(BF16))
    return pooled


def _tail_kernel(x_ref, y_ref, zs_ref, u_ref, g_ref, mod_ref, wbs_ref, wpool_ref,
                 wo_ref, nmlp_ref, wup_ref, wdn_ref, nfin_ref, unperm_ref, o_ref, uhalo,
                 *, tm, tiles_per_batch):
    i = pl.program_id(0)

    @pl.when(i % tiles_per_batch == 0)
    def _():
        uhalo[...] = jnp.zeros_like(uhalo)

    y_gated = []
    for g in range(N_GROUPS):
        gs = slice(g * GROUP_CH, (g + 1) * GROUP_CH)
        yg = y_ref[:, gs].astype(F32) * zs_ref[:, gs].astype(F32)
        y_gated.append(_rms(yg).astype(BF16))
    y_ssd = jnp.dot(jnp.concatenate(y_gated, axis=1), wbs_ref[...], preferred_element_type=F32)

    pooled = _pooled_groups(u_ref[...].astype(F32), uhalo, (i % tiles_per_batch) * tm)
    y_pool = jnp.dot(jnp.concatenate(pooled, axis=1), wpool_ref[...], preferred_element_type=F32)

    merged = (g_ref[:, :D_MODEL].astype(F32) * y_ssd + g_ref[:, D_MODEL:].astype(F32) * y_pool)
    merged = _permute_chunks(unperm_ref[...], merged.astype(BF16))
    mix = jnp.dot(merged, wo_ref[...], preferred_element_type=F32)

    mod = mod_ref[0]
    gate_m = mod[:, 2 * D_MODEL:3 * D_MODEL]
    shift_f = mod[:, 3 * D_MODEL:4 * D_MODEL]
    scale_f = mod[:, 4 * D_MODEL:5 * D_MODEL]
    gate_f = mod[:, 5 * D_MODEL:6 * D_MODEL]
    x1 = x_ref[...] + gate_m * mix
    h = _rms(x1) * (nmlp_ref[...] * (1.0 + scale_f)) + shift_f
    up = jnp.dot(h.astype(BF16), wup_ref[...], preferred_element_type=F32)
    act = jnp.square(jnp.maximum(up, 0.0)).astype(BF16)
    x2 = x1 + gate_f * jnp.dot(act, wdn_ref[...], preferred_element_type=F32)
    o_ref[...] = _rms(x2) * nfin_ref[...]


def _tail(x2d, y_raw, proj, mod3, w_bs, w_pool, w_out, norm_mlp_w, w_up, w_dn, norm_final_w,
          unperm, seq):
    t = x2d.shape[0]
    tm = 512
    tiles_per_batch = seq // tm
    const2 = lambda i: (0, 0)
    resident = functools.partial(pl.BlockSpec, pipeline_mode=pl.Buffered(1))
    return pl.pallas_call(
        functools.partial(_tail_kernel, tm=tm, tiles_per_batch=tiles_per_batch),
        grid=(t // tm,),
        in_specs=[pl.BlockSpec((tm, D_MODEL), lambda i: (i, 0)),
                  pl.BlockSpec((tm, D_INNER), lambda i: (i, 0)),
                  pl.BlockSpec((None, tm, D_INNER), lambda i: (TILE_Z, i, 0)),
                  pl.BlockSpec((None, tm, D_MODEL), lambda i: (TILE_BC_POOL, i, BC_WIDTH // D_MODEL)),
                  pl.BlockSpec((None, tm, 2 * D_MODEL), lambda i: (TILE_GATE, i, 0)),
                  pl.BlockSpec((1, 1, N_MOD * D_MODEL), lambda i: (i // tiles_per_batch, 0, 0)),
                  resident((D_INNER, D_MODEL), const2),
                  resident((D_MODEL, D_MODEL), const2),
                  resident((D_MODEL, D_MODEL), const2),
                  resident((1, D_MODEL), const2),
                  resident((D_MODEL, D_FF), const2),
                  resident((D_FF, D_MODEL), const2),
                  resident((1, D_MODEL), const2),
                  resident((CHUNK, CHUNK), const2)],
        out_specs=pl.BlockSpec((tm, D_MODEL), lambda i: (i, 0)),
        out_shape=jax.ShapeDtypeStruct((t, D_MODEL), F32),
        scratch_shapes=[pltpu.VMEM((SLABS, SUBLANES, D_MODEL), F32)],
        compiler_params=pltpu.CompilerParams(
            dimension_semantics=("arbitrary",), vmem_limit_bytes=VMEM_LIMIT),
        name="gate_pool_merge_mlp",
    )(x2d, y_raw, proj, proj, proj, mod3, w_bs, w_pool, w_out, norm_mlp_w, w_up, w_dn,
      norm_final_w, unperm)


def _cumsum_matrix():
    tok = _token_of_row(np.arange(CHUNK))
    tri = (tok[None, :] <= tok[:, None]).astype(np.float32)
    return jnp.asarray(np.concatenate([tri] * DT_REP, axis=1), BF16)


def _interleave_matrix():
    p = np.zeros((CHUNK, CHUNK), np.float32)
    p[np.arange(CHUNK), _token_of_row(np.arange(CHUNK))] = 1.0
    return jnp.asarray(p, BF16), jnp.asarray(p.T, BF16)


def _head_expand_matrix():
    m = np.zeros((LANES, D_INNER), np.float32)
    for k in range(DT_REP * N_HEADS):
        h = k % N_HEADS
        m[k, h * HEAD_DIM:(h + 1) * HEAD_DIM] = 1.0
    return jnp.asarray(m, BF16)


def kernel(x, c, w_ada, b_ada, norm_mix_w, w_in, conv_w, conv_b, dt_bias, a_log, d_skip, ssd_norm_w,
           w_branch_ssd, pool_w, pool_scale, w_branch_pool, w_out, norm_mlp_w, w_up, w_down,
           norm_final_w):
    batch, seq, d = x.shape
    depth = w_ada.shape[0]
    assert depth == 1, "the final norm is fused into the last kernel of a single layer"
    t = batch * seq
    tri3 = _cumsum_matrix()
    expand3 = _head_expand_matrix()
    perm, unperm = _interleave_matrix()
    x2d = x.reshape(t, d)
    i = 0
    mod3 = _modulation(c, w_ada[i], b_ada[i][None, :])[:, None, :]
    w_in_t = jnp.swapaxes(w_in[i], 0, 1)
    w_main, w_dt3, w_pool, conv_w_cols, conv_b_cols, dt_bias3, a_log3, d_skip_ch = _weight_prep(
        w_in_t, pool_w[i], pool_scale[i][None, :], w_branch_pool[i], conv_w[i], conv_b[i][None, :],
        dt_bias[i][None, :], a_log[i][None, :], d_skip[i][None, :], expand3)
    proj, dt_raw = _in_projection(x2d, norm_mix_w[i][None, :], mod3, w_main, w_dt3, conv_w_cols,
                                  conv_b_cols, perm, seq)
    tail_weights = [w_branch_ssd[i], w_out[i], w_up[i], w_down[i]]
    y_raw, (w_bs, w_ob, w_upb, w_dnb) = _ssd_scan(
        proj, dt_raw, dt_bias3, a_log3, d_skip_ch, tri3, expand3,
        ssd_norm_w[i][:, None], tail_weights, batch, seq)
    out = _tail(x2d, y_raw, proj, mod3, w_bs, w_pool, w_ob, norm_mlp_w[i][None, :], w_upb, w_dnb,
                norm_final_w[None, :], unperm, seq)
    return out.reshape(batch, seq, d)
```

```python
import functools

import jax
import jax.numpy as jnp
import numpy as np
from jax import lax
from jax.experimental import pallas as pl
from jax.experimental.pallas import tpu as pltpu

F32 = jnp.float32
BF16 = jnp.bfloat16

D_MODEL = 1024
D_INNER = 2 * D_MODEL
HEAD_DIM = 64
N_HEADS = D_INNER // HEAD_DIM
N_GROUPS = 4
GROUP_CH = D_INNER // N_GROUPS
D_STATE = 128
CONV_K = 4
CHUNK = 128
BC_WIDTH = 2 * N_GROUPS * D_STATE
POOL_WINDOWS = (2, 4, 8, 16)
POOL_GW = D_MODEL // len(POOL_WINDOWS)
D_FF = 4 * D_MODEL
N_MOD = 6
EPS = 1e-5
LOG2_E = 1.4426950408889634
LANES = 128
SUBLANES = 8
SLABS = CHUNK // SUBLANES
DT_REP = 3

PROJ_TN = 2048
COL_XS = D_INNER
COL_POOL = 2 * D_INNER + BC_WIDTH
PROJ_COLS = COL_POOL + 3 * D_MODEL
TILE_Z, TILE_XS, TILE_BC_POOL, TILE_GATE = 0, 1, 2, 3
CONV_COL_BLOCK = 512
PROJ_ROW_SPLIT = 4

VMEM_LIMIT = 56 * 1024 * 1024
SSD_CHUNKS_PER_STEP = 4


def _rms(x):
    return x * lax.rsqrt(jnp.mean(x * x, axis=-1, keepdims=True) + EPS)


def _dot_row_parts(x, w, parts):
    rows = x.shape[0] // parts
    return jnp.concatenate(
        [jnp.dot(x[m0:m0 + rows, :], w, preferred_element_type=F32) for m0 in range(0, x.shape[0], rows)],
        axis=0)


def _token_of_row(r):
    return (r % SUBLANES) * SLABS + r // SUBLANES


def _permute_chunks(perm, xb):
    n_chunks = xb.shape[0] // CHUNK
    return jnp.concatenate(
        [jnp.dot(perm, xb[c * CHUNK:(c + 1) * CHUNK, :], preferred_element_type=F32).astype(BF16)
         for c in range(n_chunks)], axis=0)


def _slabs_of(x, n_chunks):
    return [[x[(c * SLABS + v) * SUBLANES:(c * SLABS + v + 1) * SUBLANES, :] for v in range(SLABS)]
            for c in range(n_chunks)]


def _wrapped_slabs(slabs, prev_last, max_shift):
    n_chunks = len(slabs)
    width = slabs[0][0].shape[1]
    first_row = lax.broadcasted_iota(jnp.int32, (SUBLANES, width), 0) == 0
    wrapped = {}
    for j in range(1, max_shift + 1):
        stacked = jnp.concatenate([slabs[c][SLABS - j] for c in range(n_chunks)], axis=0)
        rolled = pltpu.roll(stacked, 1, 0)
        head = jnp.where(first_row, pltpu.roll(prev_last[j], 1, 0), rolled[0:SUBLANES, :])
        wrapped[j] = [head] + [rolled[c * SUBLANES:(c + 1) * SUBLANES, :] for c in range(1, n_chunks)]
    return wrapped


def _shifted(slabs, wrapped, c, v, k):
    return slabs[c][v - k] if v >= k else wrapped[k - v][c]


def _mod_kernel(c_ref, w_ref, b_ref, o_ref):
    batch = c_ref.shape[0]
    s = jax.nn.silu(c_ref[...])
    s = jnp.concatenate([s, jnp.zeros((SUBLANES - batch, s.shape[1]), F32)], axis=0)
    mod = jnp.dot(s.astype(BF16), w_ref[...].astype(BF16), preferred_element_type=F32) + b_ref[...]
    o_ref[...] = mod[0:batch, :]


def _modulation(c, w_ada, b_ada):
    batch = c.shape[0]
    n = w_ada.shape[1]
    tn = 2048
    return pl.pallas_call(
        _mod_kernel,
        grid=(n // tn,),
        in_specs=[pl.BlockSpec((batch, D_MODEL), lambda j: (0, 0)),
                  pl.BlockSpec((D_MODEL, tn), lambda j: (0, j)),
                  pl.BlockSpec((1, tn), lambda j: (0, j))],
        out_specs=pl.BlockSpec((batch, tn), lambda j: (0, j)),
        out_shape=jax.ShapeDtypeStruct((batch, n), F32),
        compiler_params=pltpu.CompilerParams(vmem_limit_bytes=VMEM_LIMIT),
        name="adaln_mod",
    )(c, w_ada, b_ada)


def _rep_heads(v):
    pad = jnp.zeros((v.shape[0], LANES - DT_REP * N_HEADS), v.dtype)
    return jnp.concatenate([v] * DT_REP + [pad], axis=1)


def _weight_prep_kernel(wt_ref, wdt_ref, pw_ref, ps_ref, wbp_ref, cw_ref, cb_ref, dtb_ref, alog_ref,
                        dskip_ref, exp_ref,
                        o_ref, odt_ref, opool_ref, ocw_ref, ocb_ref, odtb_ref, oalog_ref, odskip_ref,
                        *, conv_steps):
    r = pl.program_id(0)
    o_ref[...] = wt_ref[...].T.astype(BF16)

    on_conv = jnp.logical_and(r >= conv_steps[0], r < conv_steps[1])
    ocw_ref[...] = jnp.where(on_conv, cw_ref[...], 0.0)
    ocb_ref[...] = jnp.where(on_conv, cb_ref[...], 0.0)

    @pl.when(r == 0)
    def _():
        odt_ref[...] = _rep_heads(wdt_ref[...].T[:, 0:N_HEADS])
        odtb_ref[...] = _rep_heads(dtb_ref[...])
        oalog_ref[...] = _rep_heads(alog_ref[...])
        d_row = jnp.concatenate([dskip_ref[...], jnp.zeros((1, LANES - N_HEADS), F32)], axis=1)
        d8 = jnp.concatenate([d_row, jnp.zeros((SUBLANES - 1, LANES), F32)], axis=0)
        odskip_ref[...] = jnp.dot(d8, exp_ref[...].astype(F32), precision=lax.Precision.HIGHEST,
                                  preferred_element_type=F32)[0:1, :]

    @pl.when(r < len(POOL_WINDOWS))
    def _():
        opool_ref[...] = jnp.dot(pw_ref[...] * ps_ref[...], wbp_ref[...], precision=lax.Precision.HIGHEST,
                                 preferred_element_type=F32).astype(BF16)


def _weight_prep(w_in_t, pool_w, pool_scale, w_bp, conv_w, conv_b, dt_bias, a_log, d_skip, expand):
    cols = 1024
    k = w_in_t.shape[1]
    n = w_bp.shape[1]
    first_after_dt = COL_POOL // cols
    last_group = len(POOL_WINDOWS) - 1
    group = lambda r: jnp.minimum(r, last_group)
    conv_steps = (COL_XS // cols, COL_POOL // cols)
    conv_blk = lambda r: (0, jnp.clip(r - conv_steps[0], 0, conv_steps[1] - conv_steps[0] - 1))
    whole = lambda r: (0, 0)

    def src_row(r):
        return (r * (cols // N_HEADS) + jnp.where(r >= first_after_dt, 1, 0)) * N_HEADS

    return pl.pallas_call(
        functools.partial(_weight_prep_kernel, conv_steps=conv_steps),
        grid=(PROJ_COLS // cols,),
        in_specs=[pl.BlockSpec((pl.Element(cols), pl.Element(k)), lambda r: (src_row(r), 0)),
                  pl.BlockSpec((LANES, k), lambda r: (COL_POOL // LANES, 0)),
                  pl.BlockSpec((None, POOL_GW, POOL_GW), lambda r: (group(r), 0, 0)),
                  pl.BlockSpec((1, POOL_GW), lambda r: (0, group(r))),
                  pl.BlockSpec((POOL_GW, n), lambda r: (group(r), 0)),
                  pl.BlockSpec((CONV_K, cols), conv_blk),
                  pl.BlockSpec((1, cols), conv_blk),
                  pl.BlockSpec((1, N_HEADS), whole),
                  pl.BlockSpec((1, N_HEADS), whole),
                  pl.BlockSpec((1, N_HEADS), whole),
                  pl.BlockSpec((LANES, D_INNER), whole)],
        out_specs=[pl.BlockSpec((None, k, cols), lambda r: (r // (PROJ_TN // cols), 0, r % (PROJ_TN // cols))),
                   pl.BlockSpec((k, LANES), whole),
                   pl.BlockSpec((POOL_GW, n), lambda r: (group(r), 0)),
                   pl.BlockSpec((CONV_K, cols), lambda r: (0, r)),
                   pl.BlockSpec((1, cols), lambda r: (0, r)),
                   pl.BlockSpec((1, LANES), whole),
                   pl.BlockSpec((1, LANES), whole),
                   pl.BlockSpec((1, D_INNER), whole)],
        out_shape=[jax.ShapeDtypeStruct((PROJ_COLS // PROJ_TN, k, PROJ_TN), BF16),
                   jax.ShapeDtypeStruct((k, LANES), F32),
                   jax.ShapeDtypeStruct((D_MODEL, n), BF16),
                   jax.ShapeDtypeStruct((CONV_K, PROJ_COLS), F32),
                   jax.ShapeDtypeStruct((1, PROJ_COLS), F32),
                   jax.ShapeDtypeStruct((1, LANES), F32),
                   jax.ShapeDtypeStruct((1, LANES), F32),
                   jax.ShapeDtypeStruct((1, D_INNER), F32)],
        name="weight_prep",
    )(w_in_t, w_in_t, pool_w, pool_scale, w_bp, conv_w, conv_b, dt_bias, a_log, d_skip, expand)


def _conv_silu_store(r, cw_ref, cb_ref, col0, halo_ref, o_ref, width):
    n_chunks = r.shape[0] // CHUNK
    pair = 2 * SUBLANES
    for c0 in range(0, width, CONV_COL_BLOCK):
        cols = slice(c0, c0 + CONV_COL_BLOCK)
        slabs = _slabs_of(r[:, cols], n_chunks)
        prev_last = {j: halo_ref[j - 1, :, cols] for j in range(1, CONV_K)}
        for j in range(1, CONV_K):
            halo_ref[j - 1, :, cols] = slabs[n_chunks - 1][SLABS - j]
        wrapped = _wrapped_slabs(slabs, prev_last, CONV_K - 1)
        pcols = slice(col0 + c0, col0 + c0 + CONV_COL_BLOCK)
        taps = [cw_ref[k:k + 1, pcols] for k in range(CONV_K)]
        bias = cb_ref[:, pcols]
        for c in range(n_chunks):
            for v0 in range(0, SLABS, 2):
                accs = []
                for v in (v0, v0 + 1):
                    acc = slabs[c][v] * taps[CONV_K - 1] + bias
                    for k in range(1, CONV_K):
                        acc = acc + _shifted(slabs, wrapped, c, v, k) * taps[CONV_K - 1 - k]
                    accs.append(acc)
                lo = (c * SLABS + v0) * SUBLANES
                o_ref[lo:lo + pair, cols] = jax.nn.silu(jnp.concatenate(accs, axis=0)).astype(BF16)


def _inproj_kernel(x_ref, nw_ref, mod_ref, w_ref, wdt_ref, cw_ref, cb_ref, perm_ref, o_ref, dt_ref,
                   h_scr, halo_scr, *, tiles_per_batch):
    i = pl.program_id(0)
    j = pl.program_id(1)

    @pl.when(j == 0)
    def _():
        mod = mod_ref[0]
        shift = mod[:, 0:D_MODEL]
        scale = mod[:, D_MODEL:2 * D_MODEL]
        h = _rms(x_ref[...]) * (nw_ref[...] * (1.0 + scale)) + shift
        hb = _permute_chunks(perm_ref[...], h.astype(BF16))
        h_scr[...] = hb
        dt_ref[...] = jnp.dot(hb, wdt_ref[...].astype(BF16), preferred_element_type=F32)

    @pl.when(jnp.logical_and(i % tiles_per_batch == 0, j == 0))
    def _():
        halo_scr[...] = jnp.zeros_like(halo_scr)

    def proj():
        return _dot_row_parts(h_scr[...], w_ref[...], PROJ_ROW_SPLIT)

    @pl.when(j == TILE_Z)
    def _():
        o_ref[...] = jax.nn.silu(proj()).astype(BF16)

    @pl.when(j == TILE_XS)
    def _():
        _conv_silu_store(proj(), cw_ref, cb_ref, TILE_XS * PROJ_TN, halo_scr.at[0], o_ref, PROJ_TN)

    @pl.when(j == TILE_BC_POOL)
    def _():
        r = proj()
        _conv_silu_store(r[:, 0:BC_WIDTH], cw_ref, cb_ref, TILE_BC_POOL * PROJ_TN, halo_scr.at[1], o_ref,
                         BC_WIDTH)
        o_ref[:, BC_WIDTH:] = r[:, BC_WIDTH:].astype(BF16)

    @pl.when(j == TILE_GATE)
    def _():
        o_ref[...] = jax.nn.sigmoid(proj()).astype(BF16)


def _in_projection(x2d, norm_w, mod3, w_main, w_dt, conv_w_cols, conv_b_cols, perm, seq):
    t = x2d.shape[0]
    tm, tn = 1024, PROJ_TN
    tiles_per_batch = seq // tm
    return pl.pallas_call(
        functools.partial(_inproj_kernel, tiles_per_batch=tiles_per_batch),
        grid=(t // tm, PROJ_COLS // tn),
        in_specs=[pl.BlockSpec((tm, D_MODEL), lambda i, j: (i, 0)),
                  pl.BlockSpec((1, D_MODEL), lambda i, j: (0, 0)),
                  pl.BlockSpec((1, 1, N_MOD * D_MODEL), lambda i, j: (i // tiles_per_batch, 0, 0)),
                  pl.BlockSpec((None, D_MODEL, tn), lambda i, j: (j, 0, 0)),
                  pl.BlockSpec((D_MODEL, LANES), lambda i, j: (0, 0)),
                  pl.BlockSpec((CONV_K, PROJ_COLS), lambda i, j: (0, 0)),
                  pl.BlockSpec((1, PROJ_COLS), lambda i, j: (0, 0)),
                  pl.BlockSpec((CHUNK, CHUNK), lambda i, j: (0, 0))],
        out_specs=[pl.BlockSpec((None, tm, tn), lambda i, j: (j, i, 0)),
                   pl.BlockSpec((tm, LANES), lambda i, j: (i, 0))],
        out_shape=[jax.ShapeDtypeStruct((PROJ_COLS // tn, t, tn), BF16),
                   jax.ShapeDtypeStruct((t, LANES), F32)],
        scratch_shapes=[pltpu.VMEM((tm, D_MODEL), BF16),
                        pltpu.VMEM((2, CONV_K - 1, SUBLANES, tn), F32)],
        compiler_params=pltpu.CompilerParams(
            dimension_semantics=("arbitrary", "arbitrary"), vmem_limit_bytes=VMEM_LIMIT),
        name="norm_in_proj",
    )(x2d, norm_w, mod3, w_main, w_dt, conv_w_cols, conv_b_cols, perm)


def _split3(v):
    hi = v.astype(BF16).astype(F32)
    r1 = v - hi
    mid = r1.astype(BF16).astype(F32)
    return hi, r1, r1 - mid


def _lane_pieces(v, lane):
    hi, r1, r2 = _split3(v)
    return jnp.where(lane < N_HEADS, hi, jnp.where(lane < 2 * N_HEADS, r1, r2)).astype(BF16)


def _ssd_prologue(xs_ref, dt_ref, dtb_ref, alog_ref, tri_ref, exp_ref, xdb, expd, acs2_scr, src_scr):
    q = CHUNK
    tm = dt_ref.shape[0]
    n_chunks = tm // q
    dt = jax.nn.softplus(dt_ref[...] + dtb_ref[...])
    dta = dt * (-jnp.exp(alog_ref[...]))
    hi, r1, r2 = _split3(dta)
    acs = []
    for c in range(n_chunks):
        rows = slice(c * q, (c + 1) * q)
        stacked = jnp.concatenate([hi[rows], r1[rows], r2[rows]], axis=0).astype(BF16)
        acs.append(jnp.dot(tri_ref[...], stacked, preferred_element_type=F32))
    acs2 = jnp.concatenate(acs, axis=0) * LOG2_E
    acs2_last = jnp.concatenate(
        [jnp.broadcast_to(acs2[(c + 1) * q - 1:(c + 1) * q, :], (q, LANES)) for c in range(n_chunks)], axis=0)
    decay_to_end = jnp.exp2(acs2_last - acs2)
    decay_from_start = jnp.exp2(acs2)
    acs2_scr[...] = acs2
    src = acs2 - jnp.log2(dt)
    for c in range(n_chunks):
        src_scr[c] = src[c * q:(c + 1) * q, :].T

    lane = lax.broadcasted_iota(jnp.int32, (tm, LANES), 1)
    to_end = jnp.dot(_lane_pieces(dt * decay_to_end, lane), exp_ref[...], preferred_element_type=F32)
    xdb[...] = (xs_ref[...].astype(F32) * to_end).astype(BF16)
    expd[...] = _lane_pieces(decay_from_start, lane)


def _ssd_chunk(ci, xs_ref, bc_ref, dskip_ref, exp_ref, o_ref, state, xdb, expd, acs2_scr, src_scr):
    q = CHUNK
    rows = slice(ci * q, (ci + 1) * q)
    acs2 = acs2_scr[rows, :]
    src_t = src_scr[ci]

    causal = (_token_of_row(lax.broadcasted_iota(jnp.int32, (q, q), 0))
              >= _token_of_row(lax.broadcasted_iota(jnp.int32, (q, q), 1)))
    half = lax.broadcasted_iota(jnp.int32, (q, LANES), 1) < HEAD_DIM
    zero_b = jnp.zeros((q, LANES), BF16)

    for g in range(N_GROUPS):
        gs = slice(g * GROUP_CH, (g + 1) * GROUP_CH)
        bg = bc_ref[rows, g * D_STATE:(g + 1) * D_STATE]
        cg = bc_ref[rows, (N_GROUPS + g) * D_STATE:(N_GROUPS + g + 1) * D_STATE]
        scores = lax.dot_general(cg, bg, (((1,), (1,)), ((), ())), preferred_element_type=F32)
        st_old = state[:, gs]
        y_off = jnp.dot(cg, st_old.astype(BF16), preferred_element_type=F32)
        new_t = lax.dot_general(bg, xdb[rows, gs], (((0,), (0,)), ((), ())),
                                preferred_element_type=F32)
        from_start = jnp.dot(expd[rows, :], exp_ref[:, gs], preferred_element_type=F32)
        state[:, gs] = st_old * from_start[q - 1:q, :] + new_t
        ys = []
        for pr in range(GROUP_CH // LANES):
            h0 = g * (GROUP_CH // HEAD_DIM) + 2 * pr
            ms = []
            for h in (h0, h0 + 1):
                a_col = jnp.broadcast_to(acs2[:, h:h + 1], (q, q))
                a_row = jnp.broadcast_to(src_t[h:h + 1, :], (q, q))
                decay_dt = jnp.exp2(jnp.where(causal, a_col - a_row, -jnp.inf))
                ms.append((scores * decay_dt).astype(BF16))
            xp = xs_ref[rows, g * GROUP_CH + pr * LANES:g * GROUP_CH + (pr + 1) * LANES]
            rhs = jnp.concatenate([jnp.where(half, xp, zero_b), jnp.where(half, zero_b, xp)], axis=0)
            ys.append(jnp.dot(jnp.concatenate(ms, axis=1), rhs, preferred_element_type=F32))
        y = jnp.concatenate(ys, axis=1) + y_off * from_start
        o_ref[rows, gs] = (y + dskip_ref[:, gs] * xs_ref[rows, gs].astype(F32)).astype(BF16)


def _ssd_kernel(xs_ref, bc_ref, dt_ref, dtb_ref, alog_ref, dskip_ref, tri_ref, exp_ref, scale0_ref, *rest,
                n_cast):
    cast_in = rest[:n_cast]
    o_ref = rest[n_cast]
    cast_out = rest[n_cast + 1:2 * n_cast + 1]
    state, xdb, expd, acs2_scr, src_scr = rest[2 * n_cast + 1:]

    @pl.when(pl.program_id(1) == 0)
    def _():
        state[...] = jnp.zeros_like(state)

    cast_out[0][...] = (cast_in[0][...] * scale0_ref[...]).astype(BF16)
    for w_ref, wb_ref in zip(cast_in[1:], cast_out[1:]):
        wb_ref[...] = w_ref[...].astype(BF16)

    _ssd_prologue(xs_ref, dt_ref, dtb_ref, alog_ref, tri_ref, exp_ref, xdb, expd, acs2_scr, src_scr)
    for ci in range(SSD_CHUNKS_PER_STEP):
        _ssd_chunk(ci, xs_ref, bc_ref, dskip_ref, exp_ref, o_ref, state, xdb, expd, acs2_scr, src_scr)


def _ssd_scan(proj, dt_raw, dt_bias3, a_log3, d_skip_ch, tri3, expand3, row_scale0, cast_weights, batch, seq):
    q = CHUNK
    tm = SSD_CHUNKS_PER_STEP * q
    steps = seq // tm
    t = batch * seq
    n_steps = batch * steps
    rowmap = lambda b, c: b * steps + c
    const = lambda b, c: (0, 0)
    slab_specs = [pl.BlockSpec((w.shape[0] // n_steps, w.shape[1]), lambda b, c: (rowmap(b, c), 0))
                  for w in cast_weights]
    outs = pl.pallas_call(
        functools.partial(_ssd_kernel, n_cast=len(cast_weights)),
        grid=(batch, steps),
        in_specs=[pl.BlockSpec((None, tm, D_INNER), lambda b, c: (TILE_XS, rowmap(b, c), 0)),
                  pl.BlockSpec((None, tm, BC_WIDTH), lambda b, c: (TILE_BC_POOL, rowmap(b, c), 0)),
                  pl.BlockSpec((tm, LANES), lambda b, c: (rowmap(b, c), 0)),
                  pl.BlockSpec((1, LANES), const),
                  pl.BlockSpec((1, LANES), const),
                  pl.BlockSpec((1, D_INNER), const),
                  pl.BlockSpec((q, DT_REP * q), const),
                  pl.BlockSpec((LANES, D_INNER), const),
                  pl.BlockSpec((cast_weights[0].shape[0] // n_steps, 1), lambda b, c: (rowmap(b, c), 0))]
        + slab_specs,
        out_specs=[pl.BlockSpec((tm, D_INNER), lambda b, c: (rowmap(b, c), 0))] + slab_specs,
        out_shape=[jax.ShapeDtypeStruct((t, D_INNER), BF16)]
        + [jax.ShapeDtypeStruct(w.shape, BF16) for w in cast_weights],
        scratch_shapes=[pltpu.VMEM((D_STATE, D_INNER), F32),
                        pltpu.VMEM((tm, D_INNER), BF16),
                        pltpu.VMEM((tm, LANES), BF16),
                        pltpu.VMEM((tm, LANES), F32),
                        pltpu.VMEM((SSD_CHUNKS_PER_STEP, LANES, q), F32)],
        compiler_params=pltpu.CompilerParams(
            dimension_semantics=("arbitrary", "arbitrary"), vmem_limit_bytes=VMEM_LIMIT),
        name="ssd_scan",
    )(proj, proj, dt_raw, dt_bias3, a_log3, d_skip_ch, tri3, expand3, row_scale0, *cast_weights)
    return outs[0], outs[1:]


def _pooled_groups(u, halo_ref, pos0):
    n_chunks = u.shape[0] // CHUNK
    max_shift = max(POOL_WINDOWS) - 1
    slabs = _slabs_of(u, n_chunks)
    prev_last = {j: halo_ref[SLABS - j] for j in range(1, max_shift + 1)}
    for v in range(SLABS):
        halo_ref[v] = slabs[n_chunks - 1][v]
    wrapped = _wrapped_slabs(slabs, prev_last, max_shift)
    sub = lax.broadcasted_iota(jnp.int32, (SUBLANES, POOL_GW), 0)
    pooled = []
    for gi, win in enumerate(POOL_WINDOWS):
        cs = slice(gi * POOL_GW, (gi + 1) * POOL_GW)
        out = []
        for c in range(n_chunks):
            for v in range(SLABS):
                cur = slabs[c][v][:, cs]
                tot = cur
                for k in range(1, win):
                    tot = tot + _shifted(slabs, wrapped, c, v, k)[:, cs]
                pos = pos0 + c * CHUNK + sub * SLABS + v
                count = jnp.minimum(pos + 1, win).astype(F32)
                out.append(tot / count - cur)
        pooled.append(jnp.concatenate(out, axis=0).astype(BF16))
    return pooled


def _tail_kernel(x_ref, y_ref, zs_ref, u_ref, g_ref, mod_ref, wbs_ref, wpool_ref,
                 wo_ref, nmlp_ref, wup_ref, wdn_ref, nfin_ref, unperm_ref, o_ref, uhalo,
                 *, tm, tiles_per_batch):
    i = pl.program_id(0)

    @pl.when(i % tiles_per_batch == 0)
    def _():
        uhalo[...] = jnp.zeros_like(uhalo)

    y_gated = []
    for g in range(N_GROUPS):
        gs = slice(g * GROUP_CH, (g + 1) * GROUP_CH)
        yg = y_ref[:, gs].astype(F32) * zs_ref[:, gs].astype(F32)
        y_gated.append(_rms(yg).astype(BF16))
    y_ssd = jnp.dot(jnp.concatenate(y_gated, axis=1), wbs_ref[...], preferred_element_type=F32)

    pooled = _pooled_groups(u_ref[...].astype(F32), uhalo, (i % tiles_per_batch) * tm)
    y_pool = jnp.dot(jnp.concatenate(pooled, axis=1), wpool_ref[...], preferred_element_type=F32)

    merged = (g_ref[:, :D_MODEL].astype(F32) * y_ssd + g_ref[:, D_MODEL:].astype(F32) * y_pool)
    merged = _permute_chunks(unperm_ref[...], merged.astype(BF16))
    mix = jnp.dot(merged, wo_ref[...], preferred_element_type=F32)

    mod = mod_ref[0]
    gate_m = mod[:, 2 * D_MODEL:3 * D_MODEL]
    shift_f = mod[:, 3 * D_MODEL:4 * D_MODEL]
    scale_f = mod[:, 4 * D_MODEL:5 * D_MODEL]
    gate_f = mod[:, 5 * D_MODEL:6 * D_MODEL]
    x1 = x_ref[...] + gate_m * mix
    h = _rms(x1) * (nmlp_ref[...] * (1.0 + scale_f)) + shift_f
    up = jnp.dot(h.astype(BF16), wup_ref[...], preferred_element_type=F32)
    act = jnp.square(jnp.maximum(up, 0.0)).astype(BF16)
    x2 = x1 + gate_f * jnp.dot(act, wdn_ref[...], preferred_element_type=F32)
    o_ref[...] = _rms(x2) * nfin_ref[...]


def _tail(x2d, y_raw, proj, mod3, w_bs, w_pool, w_out, norm_mlp_w, w_up, w_dn, norm_final_w,
          unperm, seq):
    t = x2d.shape[0]
    tm = 512
    tiles_per_batch = seq // tm
    const2 = lambda i: (0, 0)
    resident = functools.partial(pl.BlockSpec, pipeline_mode=pl.Buffered(1))
    return pl.pallas_call(
        functools.partial(_tail_kernel, tm=tm, tiles_per_batch=tiles_per_batch),
        grid=(t // tm,),
        in_specs=[pl.BlockSpec((tm, D_MODEL), lambda i: (i, 0)),
                  pl.BlockSpec((tm, D_INNER), lambda i: (i, 0)),
                  pl.BlockSpec((None, tm, D_INNER), lambda i: (TILE_Z, i, 0)),
                  pl.BlockSpec((None, tm, D_MODEL), lambda i: (TILE_BC_POOL, i, BC_WIDTH // D_MODEL)),
                  pl.BlockSpec((None, tm, 2 * D_MODEL), lambda i: (TILE_GATE, i, 0)),
                  pl.BlockSpec((1, 1, N_MOD * D_MODEL), lambda i: (i // tiles_per_batch, 0, 0)),
                  resident((D_INNER, D_MODEL), const2),
                  resident((D_MODEL, D_MODEL), const2),
                  resident((D_MODEL, D_MODEL), const2),
                  resident((1, D_MODEL), const2),
                  resident((D_MODEL, D_FF), const2),
                  resident((D_FF, D_MODEL), const2),
                  resident((1, D_MODEL), const2),
                  resident((CHUNK, CHUNK), const2)],
        out_specs=pl.BlockSpec((tm, D_MODEL), lambda i: (i, 0)),
        out_shape=jax.ShapeDtypeStruct((t, D_MODEL), F32),
        scratch_shapes=[pltpu.VMEM((SLABS, SUBLANES, D_MODEL), F32)],
        compiler_params=pltpu.CompilerParams(
            dimension_semantics=("arbitrary",), vmem_limit_bytes=VMEM_LIMIT),
        name="gate_pool_merge_mlp",
    )(x2d, y_raw, proj, proj, proj, mod3, w_bs, w_pool, w_out, norm_mlp_w, w_up, w_dn,
      norm_final_w, unperm)


def _cumsum_matrix():
    tok = _token_of_row(np.arange(CHUNK))
    tri = (tok[None, :] <= tok[:, None]).astype(np.float32)
    return jnp.asarray(np.concatenate([tri] * DT_REP, axis=1), BF16)


def _interleave_matrix():
    p = np.zeros((CHUNK, CHUNK), np.float32)
    p[np.arange(CHUNK), _token_of_row(np.arange(CHUNK))] = 1.0
    return jnp.asarray(p, BF16), jnp.asarray(p.T, BF16)


def _head_expand_matrix():
    m = np.zeros((LANES, D_INNER), np.float32)
    for k in range(DT_REP * N_HEADS):
        h = k % N_HEADS
        m[k, h * HEAD_DIM:(h + 1) * HEAD_DIM] = 1.0
    return jnp.asarray(m, BF16)


def kernel(x, c, w_ada, b_ada, norm_mix_w, w_in, conv_w, conv_b, dt_bias, a_log, d_skip, ssd_norm_w,
           w_branch_ssd, pool_w, pool_scale, w_branch_pool, w_out, norm_mlp_w, w_up, w_down,
           norm_final_w):
    batch, seq, d = x.shape
    depth = w_ada.shape[0]
    assert depth == 1, "the final norm is fused into the last kernel of a single layer"
    t = batch * seq
    tri3 = _cumsum_matrix()
    expand3 = _head_expand_matrix()
    perm, unperm = _interleave_matrix()
    x2d = x.reshape(t, d)
    i = 0
    mod3 = _modulation(c, w_ada[i], b_ada[i][None, :])[:, None, :]
    w_in_t = jnp.swapaxes(w_in[i], 0, 1)
    w_main, w_dt3, w_pool, conv_w_cols, conv_b_cols, dt_bias3, a_log3, d_skip_ch = _weight_prep(
        w_in_t, pool_w[i], pool_scale[i][None, :], w_branch_pool[i], conv_w[i], conv_b[i][None, :],
        dt_bias[i][None, :], a_log[i][None, :], d_skip[i][None, :], expand3)
    proj, dt_raw = _in_projection(x2d, norm_mix_w[i][None, :], mod3, w_main, w_dt3, conv_w_cols,
                                  conv_b_cols, perm, seq)
    tail_weights = [w_branch_ssd[i], w_out[i], w_up[i], w_down[i]]
    y_raw, (w_bs, w_ob, w_upb, w_dnb) = _ssd_scan(
        proj, dt_raw, dt_bias3, a_log3, d_skip_ch, tri3, expand3,
        ssd_norm_w[i][:, None], tail_weights, batch, seq)
    out = _tail(x2d, y_raw, proj, mod3, w_bs, w_pool, w_ob, norm_mlp_w[i][None, :], w_upb, w_dnb,
                norm_final_w[None, :], unperm, seq)
    return out.reshape(batch, seq, d)
```

```python
import functools

import jax
import jax.numpy as jnp
import numpy as np
from jax import lax
from jax.experimental import pallas as pl
from jax.experimental.pallas import tpu as pltpu

F32 = jnp.float32
BF16 = jnp.bfloat16

D_MODEL = 1024
D_INNER = 2 * D_MODEL
HEAD_DIM = 64
N_HEADS = D_INNER // HEAD_DIM
N_GROUPS = 4
GROUP_CH = D_INNER // N_GROUPS
D_STATE = 128
CONV_K = 4
CHUNK = 128
BC_WIDTH = 2 * N_GROUPS * D_STATE
POOL_WINDOWS = (2, 4, 8, 16)
POOL_GW = D_MODEL // len(POOL_WINDOWS)
D_FF = 4 * D_MODEL
N_MOD = 6
EPS = 1e-5
LOG2_E = 1.4426950408889634
LANES = 128
SUBLANES = 8
SLABS = CHUNK // SUBLANES
DT_REP = 3

PROJ_TN = 2048
COL_XS = D_INNER
COL_POOL = 2 * D_INNER + BC_WIDTH
PROJ_COLS = COL_POOL + 3 * D_MODEL
TILE_Z, TILE_XS, TILE_BC_POOL, TILE_GATE = 0, 1, 2, 3
CONV_COL_BLOCK = 512
PROJ_ROW_SPLIT = 2

VMEM_LIMIT = 56 * 1024 * 1024
SSD_CHUNKS_PER_STEP = 4


def _rms(x):
    return x * lax.rsqrt(jnp.mean(x * x, axis=-1, keepdims=True) + EPS)


def _dot_row_parts(x, w, parts):
    rows = x.shape[0] // parts
    return jnp.concatenate(
        [jnp.dot(x[m0:m0 + rows, :], w, preferred_element_type=F32) for m0 in range(0, x.shape[0], rows)],
        axis=0)


def _token_of_row(r):
    return (r % SUBLANES) * SLABS + r // SUBLANES


def _permute_chunks(perm, xb):
    n_chunks = xb.shape[0] // CHUNK
    return jnp.concatenate(
        [jnp.dot(perm, xb[c * CHUNK:(c + 1) * CHUNK, :], preferred_element_type=F32).astype(BF16)
         for c in range(n_chunks)], axis=0)


def _slabs_of(x, n_chunks):
    return [[x[(c * SLABS + v) * SUBLANES:(c * SLABS + v + 1) * SUBLANES, :] for v in range(SLABS)]
            for c in range(n_chunks)]


def _wrapped_slabs(slabs, prev_last, max_shift):
    n_chunks = len(slabs)
    width = slabs[0][0].shape[1]
    first_row = lax.broadcasted_iota(jnp.int32, (SUBLANES, width), 0) == 0
    wrapped = {}
    for j in range(1, max_shift + 1):
        stacked = jnp.concatenate([slabs[c][SLABS - j] for c in range(n_chunks)], axis=0)
        rolled = pltpu.roll(stacked, 1, 0)
        head = jnp.where(first_row, pltpu.roll(prev_last[j], 1, 0), rolled[0:SUBLANES, :])
        wrapped[j] = [head] + [rolled[c * SUBLANES:(c + 1) * SUBLANES, :] for c in range(1, n_chunks)]
    return wrapped


def _shifted(slabs, wrapped, c, v, k):
    return slabs[c][v - k] if v >= k else wrapped[k - v][c]


def _mod_kernel(c_ref, w_ref, b_ref, o_ref):
    batch = c_ref.shape[0]
    s = jax.nn.silu(c_ref[...])
    s = jnp.concatenate([s, jnp.zeros((SUBLANES - batch, s.shape[1]), F32)], axis=0)
    mod = jnp.dot(s.astype(BF16), w_ref[...].astype(BF16), preferred_element_type=F32) + b_ref[...]
    o_ref[...] = mod[0:batch, :]


def _modulation(c, w_ada, b_ada):
    batch = c.shape[0]
    n = w_ada.shape[1]
    tn = 2048
    return pl.pallas_call(
        _mod_kernel,
        grid=(n // tn,),
        in_specs=[pl.BlockSpec((batch, D_MODEL), lambda j: (0, 0)),
                  pl.BlockSpec((D_MODEL, tn), lambda j: (0, j)),
                  pl.BlockSpec((1, tn), lambda j: (0, j))],
        out_specs=pl.BlockSpec((batch, tn), lambda j: (0, j)),
        out_shape=jax.ShapeDtypeStruct((batch, n), F32),
        compiler_params=pltpu.CompilerParams(vmem_limit_bytes=VMEM_LIMIT),
        name="adaln_mod",
    )(c, w_ada, b_ada)


def _rep_heads(v):
    pad = jnp.zeros((v.shape[0], LANES - DT_REP * N_HEADS), v.dtype)
    return jnp.concatenate([v] * DT_REP + [pad], axis=1)


def _weight_prep_kernel(wt_ref, wdt_ref, pw_ref, ps_ref, wbp_ref, cw_ref, cb_ref, dtb_ref, alog_ref,
                        dskip_ref, exp_ref,
                        o_ref, odt_ref, opool_ref, ocw_ref, ocb_ref, odtb_ref, oalog_ref, odskip_ref,
                        *, conv_steps):
    r = pl.program_id(0)
    o_ref[...] = wt_ref[...].T.astype(BF16)

    on_conv = jnp.logical_and(r >= conv_steps[0], r < conv_steps[1])
    ocw_ref[...] = jnp.where(on_conv, cw_ref[...], 0.0)
    ocb_ref[...] = jnp.where(on_conv, cb_ref[...], 0.0)

    @pl.when(r == 0)
    def _():
        odt_ref[...] = _rep_heads(wdt_ref[...].T[:, 0:N_HEADS])
        odtb_ref[...] = _rep_heads(dtb_ref[...])
        oalog_ref[...] = _rep_heads(alog_ref[...])
        d_row = jnp.concatenate([dskip_ref[...], jnp.zeros((1, LANES - N_HEADS), F32)], axis=1)
        d8 = jnp.concatenate([d_row, jnp.zeros((SUBLANES - 1, LANES), F32)], axis=0)
        odskip_ref[...] = jnp.dot(d8, exp_ref[...].astype(F32), precision=lax.Precision.HIGHEST,
                                  preferred_element_type=F32)[0:1, :]

    @pl.when(r < len(POOL_WINDOWS))
    def _():
        opool_ref[...] = jnp.dot(pw_ref[...] * ps_ref[...], wbp_ref[...], precision=lax.Precision.HIGHEST,
                                 preferred_element_type=F32).astype(BF16)


def _weight_prep(w_in_t, pool_w, pool_scale, w_bp, conv_w, conv_b, dt_bias, a_log, d_skip, expand):
    cols = 1024
    k = w_in_t.shape[1]
    n = w_bp.shape[1]
    first_after_dt = COL_POOL // cols
    last_group = len(POOL_WINDOWS) - 1
    group = lambda r: jnp.minimum(r, last_group)
    conv_steps = (COL_XS // cols, COL_POOL // cols)
    conv_blk = lambda r: (0, jnp.clip(r - conv_steps[0], 0, conv_steps[1] - conv_steps[0] - 1))
    whole = lambda r: (0, 0)

    def src_row(r):
        return (r * (cols // N_HEADS) + jnp.where(r >= first_after_dt, 1, 0)) * N_HEADS

    return pl.pallas_call(
        functools.partial(_weight_prep_kernel, conv_steps=conv_steps),
        grid=(PROJ_COLS // cols,),
        in_specs=[pl.BlockSpec((pl.Element(cols), pl.Element(k)), lambda r: (src_row(r), 0)),
                  pl.BlockSpec((LANES, k), lambda r: (COL_POOL // LANES, 0)),
                  pl.BlockSpec((None, POOL_GW, POOL_GW), lambda r: (group(r), 0, 0)),
                  pl.BlockSpec((1, POOL_GW), lambda r: (0, group(r))),
                  pl.BlockSpec((POOL_GW, n), lambda r: (group(r), 0)),
                  pl.BlockSpec((CONV_K, cols), conv_blk),
                  pl.BlockSpec((1, cols), conv_blk),
                  pl.BlockSpec((1, N_HEADS), whole),
                  pl.BlockSpec((1, N_HEADS), whole),
                  pl.BlockSpec((1, N_HEADS), whole),
                  pl.BlockSpec((LANES, D_INNER), whole)],
        out_specs=[pl.BlockSpec((None, k, cols), lambda r: (r // (PROJ_TN // cols), 0, r % (PROJ_TN // cols))),
                   pl.BlockSpec((k, LANES), whole),
                   pl.BlockSpec((POOL_GW, n), lambda r: (group(r), 0)),
                   pl.BlockSpec((CONV_K, cols), lambda r: (0, r)),
                   pl.BlockSpec((1, cols), lambda r: (0, r)),
                   pl.BlockSpec((1, LANES), whole),
                   pl.BlockSpec((1, LANES), whole),
                   pl.BlockSpec((1, D_INNER), whole)],
        out_shape=[jax.ShapeDtypeStruct((PROJ_COLS // PROJ_TN, k, PROJ_TN), BF16),
                   jax.ShapeDtypeStruct((k, LANES), F32),
                   jax.ShapeDtypeStruct((D_MODEL, n), BF16),
                   jax.ShapeDtypeStruct((CONV_K, PROJ_COLS), F32),
                   jax.ShapeDtypeStruct((1, PROJ_COLS), F32),
                   jax.ShapeDtypeStruct((1, LANES), F32),
                   jax.ShapeDtypeStruct((1, LANES), F32),
                   jax.ShapeDtypeStruct((1, D_INNER), F32)],
        name="weight_prep",
    )(w_in_t, w_in_t, pool_w, pool_scale, w_bp, conv_w, conv_b, dt_bias, a_log, d_skip, expand)


def _conv_silu_store(r, cw_ref, cb_ref, col0, halo_ref, o_ref, width):
    n_chunks = r.shape[0] // CHUNK
    pair = 2 * SUBLANES
    for c0 in range(0, width, CONV_COL_BLOCK):
        cols = slice(c0, c0 + CONV_COL_BLOCK)
        slabs = _slabs_of(r[:, cols], n_chunks)
        prev_last = {j: halo_ref[j - 1, :, cols] for j in range(1, CONV_K)}
        for j in range(1, CONV_K):
            halo_ref[j - 1, :, cols] = slabs[n_chunks - 1][SLABS - j]
        wrapped = _wrapped_slabs(slabs, prev_last, CONV_K - 1)
        pcols = slice(col0 + c0, col0 + c0 + CONV_COL_BLOCK)
        taps = [cw_ref[k:k + 1, pcols] for k in range(CONV_K)]
        bias = cb_ref[:, pcols]
        for c in range(n_chunks):
            for v0 in range(0, SLABS, 2):
                accs = []
                for v in (v0, v0 + 1):
                    acc = slabs[c][v] * taps[CONV_K - 1] + bias
                    for k in range(1, CONV_K):
                        acc = acc + _shifted(slabs, wrapped, c, v, k) * taps[CONV_K - 1 - k]
                    accs.append(acc)
                lo = (c * SLABS + v0) * SUBLANES
                o_ref[lo:lo + pair, cols] = jax.nn.silu(jnp.concatenate(accs, axis=0)).astype(BF16)


def _inproj_kernel(x_ref, nw_ref, mod_ref, w_ref, wdt_ref, cw_ref, cb_ref, perm_ref, o_ref, dt_ref,
                   h_scr, halo_scr, *, tiles_per_batch):
    i = pl.program_id(0)
    j = pl.program_id(1)

    @pl.when(j == 0)
    def _():
        mod = mod_ref[0]
        shift = mod[:, 0:D_MODEL]
        scale = mod[:, D_MODEL:2 * D_MODEL]
        h = _rms(x_ref[...]) * (nw_ref[...] * (1.0 + scale)) + shift
        hb = _permute_chunks(perm_ref[...], h.astype(BF16))
        h_scr[...] = hb
        dt_ref[...] = jnp.dot(hb, wdt_ref[...].astype(BF16), preferred_element_type=F32)

    @pl.when(jnp.logical_and(i % tiles_per_batch == 0, j == 0))
    def _():
        halo_scr[...] = jnp.zeros_like(halo_scr)

    def proj():
        return _dot_row_parts(h_scr[...], w_ref[...], PROJ_ROW_SPLIT)

    @pl.when(j == TILE_Z)
    def _():
        o_ref[...] = jax.nn.silu(proj()).astype(BF16)

    @pl.when(j == TILE_XS)
    def _():
        _conv_silu_store(proj(), cw_ref, cb_ref, TILE_XS * PROJ_TN, halo_scr.at[0], o_ref, PROJ_TN)

    @pl.when(j == TILE_BC_POOL)
    def _():
        r = proj()
        _conv_silu_store(r[:, 0:BC_WIDTH], cw_ref, cb_ref, TILE_BC_POOL * PROJ_TN, halo_scr.at[1], o_ref,
                         BC_WIDTH)
        o_ref[:, BC_WIDTH:] = r[:, BC_WIDTH:].astype(BF16)

    @pl.when(j == TILE_GATE)
    def _():
        o_ref[...] = jax.nn.sigmoid(proj()).astype(BF16)


def _in_projection(x2d, norm_w, mod3, w_main, w_dt, conv_w_cols, conv_b_cols, perm, seq):
    t = x2d.shape[0]
    tm, tn = 1024, PROJ_TN
    tiles_per_batch = seq // tm
    return pl.pallas_call(
        functools.partial(_inproj_kernel, tiles_per_batch=tiles_per_batch),
        grid=(t // tm, PROJ_COLS // tn),
        in_specs=[pl.BlockSpec((tm, D_MODEL), lambda i, j: (i, 0)),
                  pl.BlockSpec((1, D_MODEL), lambda i, j: (0, 0)),
                  pl.BlockSpec((1, 1, N_MOD * D_MODEL), lambda i, j: (i // tiles_per_batch, 0, 0)),
                  pl.BlockSpec((None, D_MODEL, tn), lambda i, j: (j, 0, 0)),
                  pl.BlockSpec((D_MODEL, LANES), lambda i, j: (0, 0)),
                  pl.BlockSpec((CONV_K, PROJ_COLS), lambda i, j: (0, 0)),
                  pl.BlockSpec((1, PROJ_COLS), lambda i, j: (0, 0)),
                  pl.BlockSpec((CHUNK, CHUNK), lambda i, j: (0, 0))],
        out_specs=[pl.BlockSpec((None, tm, tn), lambda i, j: (j, i, 0)),
                   pl.BlockSpec((tm, LANES), lambda i, j: (i, 0))],
        out_shape=[jax.ShapeDtypeStruct((PROJ_COLS // tn, t, tn), BF16),
                   jax.ShapeDtypeStruct((t, LANES), F32)],
        scratch_shapes=[pltpu.VMEM((tm, D_MODEL), BF16),
                        pltpu.VMEM((2, CONV_K - 1, SUBLANES, tn), F32)],
        compiler_params=pltpu.CompilerParams(
            dimension_semantics=("arbitrary", "arbitrary"), vmem_limit_bytes=VMEM_LIMIT),
        name="norm_in_proj",
    )(x2d, norm_w, mod3, w_main, w_dt, conv_w_cols, conv_b_cols, perm)


def _split3(v):
    hi = v.astype(BF16).astype(F32)
    r1 = v - hi
    mid = r1.astype(BF16).astype(F32)
    return hi, r1, r1 - mid


def _lane_pieces(v, lane):
    hi, r1, r2 = _split3(v)
    return jnp.where(lane < N_HEADS, hi, jnp.where(lane < 2 * N_HEADS, r1, r2)).astype(BF16)


def _ssd_prologue(xs_ref, dt_ref, dtb_ref, alog_ref, tri_ref, exp_ref, xdb, expd, acs2_scr, src_scr):
    q = CHUNK
    tm = dt_ref.shape[0]
    n_chunks = tm // q
    dt = jax.nn.softplus(dt_ref[...] + dtb_ref[...])
    dta = dt * (-jnp.exp(alog_ref[...]))
    hi, r1, r2 = _split3(dta)
    acs = []
    for c in range(n_chunks):
        rows = slice(c * q, (c + 1) * q)
        stacked = jnp.concatenate([hi[rows], r1[rows], r2[rows]], axis=0).astype(BF16)
        acs.append(jnp.dot(tri_ref[...], stacked, preferred_element_type=F32))
    acs2 = jnp.concatenate(acs, axis=0) * LOG2_E
    acs2_last = jnp.concatenate(
        [jnp.broadcast_to(acs2[(c + 1) * q - 1:(c + 1) * q, :], (q, LANES)) for c in range(n_chunks)], axis=0)
    decay_to_end = jnp.exp2(acs2_last - acs2)
    decay_from_start = jnp.exp2(acs2)
    acs2_scr[...] = acs2
    src = acs2 - jnp.log2(dt)
    for c in range(n_chunks):
        src_scr[c] = src[c * q:(c + 1) * q, :].T

    lane = lax.broadcasted_iota(jnp.int32, (tm, LANES), 1)
    to_end = jnp.dot(_lane_pieces(dt * decay_to_end, lane), exp_ref[...], preferred_element_type=F32)
    xdb[...] = (xs_ref[...].astype(F32) * to_end).astype(BF16)
    expd[...] = _lane_pieces(decay_from_start, lane)


def _ssd_chunk(ci, xs_ref, bc_ref, dskip_ref, exp_ref, o_ref, state, xdb, expd, acs2_scr, src_scr):
    q = CHUNK
    rows = slice(ci * q, (ci + 1) * q)
    acs2 = acs2_scr[rows, :]
    src_t = src_scr[ci]

    causal = (_token_of_row(lax.broadcasted_iota(jnp.int32, (q, q), 0))
              >= _token_of_row(lax.broadcasted_iota(jnp.int32, (q, q), 1)))
    half = lax.broadcasted_iota(jnp.int32, (q, LANES), 1) < HEAD_DIM
    zero_b = jnp.zeros((q, LANES), BF16)

    for g in range(N_GROUPS):
        gs = slice(g * GROUP_CH, (g + 1) * GROUP_CH)
        bg = bc_ref[rows, g * D_STATE:(g + 1) * D_STATE]
        cg = bc_ref[rows, (N_GROUPS + g) * D_STATE:(N_GROUPS + g + 1) * D_STATE]
        scores = lax.dot_general(cg, bg, (((1,), (1,)), ((), ())), preferred_element_type=F32)
        st_old = state[:, gs]
        y_off = jnp.dot(cg, st_old.astype(BF16), preferred_element_type=F32)
        new_t = lax.dot_general(bg, xdb[rows, gs], (((0,), (0,)), ((), ())),
                                preferred_element_type=F32)
        from_start = jnp.dot(expd[rows, :], exp_ref[:, gs], preferred_element_type=F32)
        state[:, gs] = st_old * from_start[q - 1:q, :] + new_t
        ys = []
        for pr in range(GROUP_CH // LANES):
            h0 = g * (GROUP_CH // HEAD_DIM) + 2 * pr
            ms = []
            for h in (h0, h0 + 1):
                a_col = jnp.broadcast_to(acs2[:, h:h + 1], (q, q))
                a_row = jnp.broadcast_to(src_t[h:h + 1, :], (q, q))
                decay_dt = jnp.exp2(jnp.where(causal, a_col - a_row, -jnp.inf))
                ms.append((scores * decay_dt).astype(BF16))
            xp = xs_ref[rows, g * GROUP_CH + pr * LANES:g * GROUP_CH + (pr + 1) * LANES]
            rhs = jnp.concatenate([jnp.where(half, xp, zero_b), jnp.where(half, zero_b, xp)], axis=0)
            ys.append(jnp.dot(jnp.concatenate(ms, axis=1), rhs, preferred_element_type=F32))
        y = jnp.concatenate(ys, axis=1) + y_off * from_start
        o_ref[rows, gs] = (y + dskip_ref[:, gs] * xs_ref[rows, gs].astype(F32)).astype(BF16)


def _ssd_kernel(xs_ref, bc_ref, dt_ref, dtb_ref, alog_ref, dskip_ref, tri_ref, exp_ref, scale0_ref, *rest,
                n_cast):
    cast_in = rest[:n_cast]
    o_ref = rest[n_cast]
    cast_out = rest[n_cast + 1:2 * n_cast + 1]
    state, xdb, expd, acs2_scr, src_scr = rest[2 * n_cast + 1:]

    @pl.when(pl.program_id(1) == 0)
    def _():
        state[...] = jnp.zeros_like(state)

    cast_out[0][...] = (cast_in[0][...] * scale0_ref[...]).astype(BF16)
    for w_ref, wb_ref in zip(cast_in[1:], cast_out[1:]):
        wb_ref[...] = w_ref[...].astype(BF16)

    _ssd_prologue(xs_ref, dt_ref, dtb_ref, alog_ref, tri_ref, exp_ref, xdb, expd, acs2_scr, src_scr)
    for ci in range(SSD_CHUNKS_PER_STEP):
        _ssd_chunk(ci, xs_ref, bc_ref, dskip_ref, exp_ref, o_ref, state, xdb, expd, acs2_scr, src_scr)


def _ssd_scan(proj, dt_raw, dt_bias3, a_log3, d_skip_ch, tri3, expand3, row_scale0, cast_weights, batch, seq):
    q = CHUNK
    tm = SSD_CHUNKS_PER_STEP * q
    steps = seq // tm
    t = batch * seq
    n_steps = batch * steps
    rowmap = lambda b, c: b * steps + c
    const = lambda b, c: (0, 0)
    slab_specs = [pl.BlockSpec((w.shape[0] // n_steps, w.shape[1]), lambda b, c: (rowmap(b, c), 0))
                  for w in cast_weights]
    outs = pl.pallas_call(
        functools.partial(_ssd_kernel, n_cast=len(cast_weights)),
        grid=(batch, steps),
        in_specs=[pl.BlockSpec((None, tm, D_INNER), lambda b, c: (TILE_XS, rowmap(b, c), 0)),
                  pl.BlockSpec((None, tm, BC_WIDTH), lambda b, c: (TILE_BC_POOL, rowmap(b, c), 0)),
                  pl.BlockSpec((tm, LANES), lambda b, c: (rowmap(b, c), 0)),
                  pl.BlockSpec((1, LANES), const),
                  pl.BlockSpec((1, LANES), const),
                  pl.BlockSpec((1, D_INNER), const),
                  pl.BlockSpec((q, DT_REP * q), const),
                  pl.BlockSpec((LANES, D_INNER), const),
                  pl.BlockSpec((cast_weights[0].shape[0] // n_steps, 1), lambda b, c: (rowmap(b, c), 0))]
        + slab_specs,
        out_specs=[pl.BlockSpec((tm, D_INNER), lambda b, c: (rowmap(b, c), 0))] + slab_specs,
        out_shape=[jax.ShapeDtypeStruct((t, D_INNER), BF16)]
        + [jax.ShapeDtypeStruct(w.shape, BF16) for w in cast_weights],
        scratch_shapes=[pltpu.VMEM((D_STATE, D_INNER), F32),
                        pltpu.VMEM((tm, D_INNER), BF16),
                        pltpu.VMEM((tm, LANES), BF16),
                        pltpu.VMEM((tm, LANES), F32),
                        pltpu.VMEM((SSD_CHUNKS_PER_STEP, LANES, q), F32)],
        compiler_params=pltpu.CompilerParams(
            dimension_semantics=("arbitrary", "arbitrary"), vmem_limit_bytes=VMEM_LIMIT),
        name="ssd_scan",
    )(proj, proj, dt_raw, dt_bias3, a_log3, d_skip_ch, tri3, expand3, row_scale0, *cast_weights)
    return outs[0], outs[1:]


def _pooled_groups(u, halo_ref, pos0):
    n_chunks = u.shape[0] // CHUNK
    max_shift = max(POOL_WINDOWS) - 1
    slabs = _slabs_of(u, n_chunks)
    prev_last = {j: halo_ref[SLABS - j] for j in range(1, max_shift + 1)}
    for v in range(SLABS):
        halo_ref[v] = slabs[n_chunks - 1][v]
    wrapped = _wrapped_slabs(slabs, prev_last, max_shift)
    sub = lax.broadcasted_iota(jnp.int32, (SUBLANES, POOL_GW), 0)
    pooled = []
    for gi, win in enumerate(POOL_WINDOWS):
        cs = slice(gi * POOL_GW, (gi + 1) * POOL_GW)
        out = []
        for c in range(n_chunks):
            for v in range(SLABS):
                cur = slabs[c][v][:, cs]
                tot = cur
                for k in range(1, win):
                    tot = tot + _shifted(slabs, wrapped, c, v, k)[:, cs]
                pos = pos0 + c * CHUNK + sub * SLABS + v
                count = jnp.minimum(pos + 1, win).astype(F32)
                out.append(tot / count - cur)
        pooled.append(jnp.concatenate(out, axis=0).astype(BF16))
    return pooled


def _tail_kernel(x_ref, y_ref, zs_ref, u_ref, g_ref, mod_ref, wbs_ref, wpool_ref,
                 wo_ref, nmlp_ref, wup_ref, wdn_ref, nfin_ref, unperm_ref, o_ref, uhalo,
                 *, tm, tiles_per_batch):
    i = pl.program_id(0)

    @pl.when(i % tiles_per_batch == 0)
    def _():
        uhalo[...] = jnp.zeros_like(uhalo)

    y_gated = []
    for g in range(N_GROUPS):
        gs = slice(g * GROUP_CH, (g + 1) * GROUP_CH)
        yg = y_ref[:, gs].astype(F32) * zs_ref[:, gs].astype(F32)
        y_gated.append(_rms(yg).astype(BF16))
    y_ssd = jnp.dot(jnp.concatenate(y_gated, axis=1), wbs_ref[...], preferred_element_type=F32)

    pooled = _pooled_groups(u_ref[...].astype(F32), uhalo, (i % tiles_per_batch) * tm)
    y_pool = jnp.dot(jnp.concatenate(pooled, axis=1), wpool_ref[...], preferred_element_type=F32)

    merged = (g_ref[:, :D_MODEL].astype(F32) * y_ssd + g_ref[:, D_MODEL:].astype(F32) * y_pool)
    merged = _permute_chunks(unperm_ref[...], merged.astype(BF16))
    mix = jnp.dot(merged, wo_ref[...], preferred_element_type=F32)

    mod = mod_ref[0]
    gate_m = mod[:, 2 * D_MODEL:3 * D_MODEL]
    shift_f = mod[:, 3 * D_MODEL:4 * D_MODEL]
    scale_f = mod[:, 4 * D_MODEL:5 * D_MODEL]
    gate_f = mod[:, 5 * D_MODEL:6 * D_MODEL]
    x1 = x_ref[...] + gate_m * mix
    h = _rms(x1) * (nmlp_ref[...] * (1.0 + scale_f)) + shift_f
    up = jnp.dot(h.astype(BF16), wup_ref[...], preferred_element_type=F32)
    act = jnp.square(jnp.maximum(up, 0.0)).astype(BF16)
    x2 = x1 + gate_f * jnp.dot(act, wdn_ref[...], preferred_element_type=F32)
    o_ref[...] = _rms(x2) * nfin_ref[...]


def _tail(x2d, y_raw, proj, mod3, w_bs, w_pool, w_out, norm_mlp_w, w_up, w_dn, norm_final_w,
          unperm, seq):
    t = x2d.shape[0]
    tm = 512
    tiles_per_batch = seq // tm
    const2 = lambda i: (0, 0)
    resident = functools.partial(pl.BlockSpec, pipeline_mode=pl.Buffered(1))
    return pl.pallas_call(
        functools.partial(_tail_kernel, tm=tm, tiles_per_batch=tiles_per_batch),
        grid=(t // tm,),
        in_specs=[pl.BlockSpec((tm, D_MODEL), lambda i: (i, 0)),
                  pl.BlockSpec((tm, D_INNER), lambda i: (i, 0)),
                  pl.BlockSpec((None, tm, D_INNER), lambda i: (TILE_Z, i, 0)),
                  pl.BlockSpec((None, tm, D_MODEL), lambda i: (TILE_BC_POOL, i, BC_WIDTH // D_MODEL)),
                  pl.BlockSpec((None, tm, 2 * D_MODEL), lambda i: (TILE_GATE, i, 0)),
                  pl.BlockSpec((1, 1, N_MOD * D_MODEL), lambda i: (i // tiles_per_batch, 0, 0)),
                  resident((D_INNER, D_MODEL), const2),
                  resident((D_MODEL, D_MODEL), const2),
                  resident((D_MODEL, D_MODEL), const2),
                  resident((1, D_MODEL), const2),
                  resident((D_MODEL, D_FF), const2),
                  resident((D_FF, D_MODEL), const2),
                  resident((1, D_MODEL), const2),
                  resident((CHUNK, CHUNK), const2)],
        out_specs=pl.BlockSpec((tm, D_MODEL), lambda i: (i, 0)),
        out_shape=jax.ShapeDtypeStruct((t, D_MODEL), F32),
        scratch_shapes=[pltpu.VMEM((SLABS, SUBLANES, D_MODEL), F32)],
        compiler_params=pltpu.CompilerParams(
            dimension_semantics=("arbitrary",), vmem_limit_bytes=VMEM_LIMIT),
        name="gate_pool_merge_mlp",
    )(x2d, y_raw, proj, proj, proj, mod3, w_bs, w_pool, w_out, norm_mlp_w, w_up, w_dn,
      norm_final_w, unperm)


def _cumsum_matrix():
    tok = _token_of_row(np.arange(CHUNK))
    tri = (tok[None, :] <= tok[:, None]).astype(np.float32)
    return jnp.asarray(np.concatenate([tri] * DT_REP, axis=1), BF16)


def _interleave_matrix():
    p = np.zeros((CHUNK, CHUNK), np.float32)
    p[np.arange(CHUNK), _token_of_row(np.arange(CHUNK))] = 1.0
    return jnp.asarray(p, BF16), jnp.asarray(p.T, BF16)


def _head_expand_matrix():
    m = np.zeros((LANES, D_INNER), np.float32)
    for k in range(DT_REP * N_HEADS):
        h = k % N_HEADS
        m[k, h * HEAD_DIM:(h + 1) * HEAD_DIM] = 1.0
    return jnp.asarray(m, BF16)


def kernel(x, c, w_ada, b_ada, norm_mix_w, w_in, conv_w, conv_b, dt_bias, a_log, d_skip, ssd_norm_w,
           w_branch_ssd, pool_w, pool_scale, w_branch_pool, w_out, norm_mlp_w, w_up, w_down,
           norm_final_w):
    batch, seq, d = x.shape
    depth = w_ada.shape[0]
    assert depth == 1, "the final norm is fused into the last kernel of a single layer"
    t = batch * seq
    tri3 = _cumsum_matrix()
    expand3 = _head_expand_matrix()
    perm, unperm = _interleave_matrix()
    x2d = x.reshape(t, d)
    i = 0
    mod3 = _modulation(c, w_ada[i], b_ada[i][None, :])[:, None, :]
    w_in_t = jnp.swapaxes(w_in[i], 0, 1)
    w_main, w_dt3, w_pool, conv_w_cols, conv_b_cols, dt_bias3, a_log3, d_skip_ch = _weight_prep(
        w_in_t, pool_w[i], pool_scale[i][None, :], w_branch_pool[i], conv_w[i], conv_b[i][None, :],
        dt_bias[i][None, :], a_log[i][None, :], d_skip[i][None, :], expand3)
    proj, dt_raw = _in_projection(x2d, norm_mix_w[i][None, :], mod3, w_main, w_dt3, conv_w_cols,
                                  conv_b_cols, perm, seq)
    tail_weights = [w_branch_ssd[i], w_out[i], w_up[i], w_down[i]]
    y_raw, (w_bs, w_ob, w_upb, w_dnb) = _ssd_scan(
        proj, dt_raw, dt_bias3, a_log3, d_skip_ch, tri3, expand3,
        ssd_norm_w[i][:, None], tail_weights, batch, seq)
    out = _tail(x2d, y_raw, proj, mod3, w_bs, w_pool, w_ob, norm_mlp_w[i][None, :], w_upb, w_dnb,
                norm_final_w[None, :], unperm, seq)
    return out.reshape(batch, seq, d)
```

```python
import functools

import jax
import jax.numpy as jnp
import numpy as np
from jax import lax
from jax.experimental import pallas as pl
from jax.experimental.pallas import tpu as pltpu

F32 = jnp.float32
BF16 = jnp.bfloat16

D_MODEL = 1024
D_INNER = 2 * D_MODEL
HEAD_DIM = 64
N_HEADS = D_INNER // HEAD_DIM
N_GROUPS = 4
GROUP_CH = D_INNER // N_GROUPS
D_STATE = 128
CONV_K = 4
CHUNK = 128
BC_WIDTH = 2 * N_GROUPS * D_STATE
POOL_WINDOWS = (2, 4, 8, 16)
POOL_GW = D_MODEL // len(POOL_WINDOWS)
D_FF = 4 * D_MODEL
N_MOD = 6
EPS = 1e-5
LOG2_E = 1.4426950408889634
LANES = 128
SUBLANES = 8
SLABS = CHUNK // SUBLANES
DT_REP = 3

PROJ_TN = 2048
COL_XS = D_INNER
COL_POOL = 2 * D_INNER + BC_WIDTH
PROJ_COLS = COL_POOL + 3 * D_MODEL
TILE_Z, TILE_XS, TILE_BC_POOL, TILE_GATE = 0, 1, 2, 3
CONV_COL_BLOCK = 512
PROJ_ROW_SPLIT = 4

VMEM_LIMIT = 56 * 1024 * 1024
SSD_CHUNKS_PER_STEP = 8


def _rms(x):
    return x * lax.rsqrt(jnp.mean(x * x, axis=-1, keepdims=True) + EPS)


def _dot_row_parts(x, w, parts):
    rows = x.shape[0] // parts
    return jnp.concatenate(
        [jnp.dot(x[m0:m0 + rows, :], w, preferred_element_type=F32) for m0 in range(0, x.shape[0], rows)],
        axis=0)


def _token_of_row(r):
    return (r % SUBLANES) * SLABS + r // SUBLANES


def _permute_chunks(perm, xb):
    n_chunks = xb.shape[0] // CHUNK
    return jnp.concatenate(
        [jnp.dot(perm, xb[c * CHUNK:(c + 1) * CHUNK, :], preferred_element_type=F32).astype(BF16)
         for c in range(n_chunks)], axis=0)


def _slabs_of(x, n_chunks):
    return [[x[(c * SLABS + v) * SUBLANES:(c * SLABS + v + 1) * SUBLANES, :] for v in range(SLABS)]
            for c in range(n_chunks)]


def _wrapped_slabs(slabs, prev_last, max_shift):
    n_chunks = len(slabs)
    width = slabs[0][0].shape[1]
    first_row = lax.broadcasted_iota(jnp.int32, (SUBLANES, width), 0) == 0
    wrapped = {}
    for j in range(1, max_shift + 1):
        stacked = jnp.concatenate([slabs[c][SLABS - j] for c in range(n_chunks)], axis=0)
        rolled = pltpu.roll(stacked, 1, 0)
        head = jnp.where(first_row, pltpu.roll(prev_last[j], 1, 0), rolled[0:SUBLANES, :])
        wrapped[j] = [head] + [rolled[c * SUBLANES:(c + 1) * SUBLANES, :] for c in range(1, n_chunks)]
    return wrapped


def _shifted(slabs, wrapped, c, v, k):
    return slabs[c][v - k] if v >= k else wrapped[k - v][c]


def _mod_kernel(c_ref, w_ref, b_ref, o_ref):
    batch = c_ref.shape[0]
    s = jax.nn.silu(c_ref[...])
    s = jnp.concatenate([s, jnp.zeros((SUBLANES - batch, s.shape[1]), F32)], axis=0)
    mod = jnp.dot(s.astype(BF16), w_ref[...].astype(BF16), preferred_element_type=F32) + b_ref[...]
    o_ref[...] = mod[0:batch, :]


def _modulation(c, w_ada, b_ada):
    batch = c.shape[0]
    n = w_ada.shape[1]
    tn = 2048
    return pl.pallas_call(
        _mod_kernel,
        grid=(n // tn,),
        in_specs=[pl.BlockSpec((batch, D_MODEL), lambda j: (0, 0)),
                  pl.BlockSpec((D_MODEL, tn), lambda j: (0, j)),
                  pl.BlockSpec((1, tn), lambda j: (0, j))],
        out_specs=pl.BlockSpec((batch, tn), lambda j: (0, j)),
        out_shape=jax.ShapeDtypeStruct((batch, n), F32),
        compiler_params=pltpu.CompilerParams(vmem_limit_bytes=VMEM_LIMIT),
        name="adaln_mod",
    )(c, w_ada, b_ada)


def _rep_heads(v):
    pad = jnp.zeros((v.shape[0], LANES - DT_REP * N_HEADS), v.dtype)
    return jnp.concatenate([v] * DT_REP + [pad], axis=1)


def _weight_prep_kernel(wt_ref, wdt_ref, pw_ref, ps_ref, wbp_ref, cw_ref, cb_ref, dtb_ref, alog_ref,
                        dskip_ref, exp_ref,
                        o_ref, odt_ref, opool_ref, ocw_ref, ocb_ref, odtb_ref, oalog_ref, odskip_ref,
                        *, conv_steps):
    r = pl.program_id(0)
    o_ref[...] = wt_ref[...].T.astype(BF16)

    on_conv = jnp.logical_and(r >= conv_steps[0], r < conv_steps[1])
    ocw_ref[...] = jnp.where(on_conv, cw_ref[...], 0.0)
    ocb_ref[...] = jnp.where(on_conv, cb_ref[...], 0.0)

    @pl.when(r == 0)
    def _():
        odt_ref[...] = _rep_heads(wdt_ref[...].T[:, 0:N_HEADS])
        odtb_ref[...] = _rep_heads(dtb_ref[...])
        oalog_ref[...] = _rep_heads(alog_ref[...])
        d_row = jnp.concatenate([dskip_ref[...], jnp.zeros((1, LANES - N_HEADS), F32)], axis=1)
        d8 = jnp.concatenate([d_row, jnp.zeros((SUBLANES - 1, LANES), F32)], axis=0)
        odskip_ref[...] = jnp.dot(d8, exp_ref[...].astype(F32), precision=lax.Precision.HIGHEST,
                                  preferred_element_type=F32)[0:1, :]

    @pl.when(r < len(POOL_WINDOWS))
    def _():
        opool_ref[...] = jnp.dot(pw_ref[...] * ps_ref[...], wbp_ref[...], precision=lax.Precision.HIGHEST,
                                 preferred_element_type=F32).astype(BF16)


def _weight_prep(w_in_t, pool_w, pool_scale, w_bp, conv_w, conv_b, dt_bias, a_log, d_skip, expand):
    cols = 1024
    k = w_in_t.shape[1]
    n = w_bp.shape[1]
    first_after_dt = COL_POOL // cols
    last_group = len(POOL_WINDOWS) - 1
    group = lambda r: jnp.minimum(r, last_group)
    conv_steps = (COL_XS // cols, COL_POOL // cols)
    conv_blk = lambda r: (0, jnp.clip(r - conv_steps[0], 0, conv_steps[1] - conv_steps[0] - 1))
    whole = lambda r: (0, 0)

    def src_row(r):
        return (r * (cols // N_HEADS) + jnp.where(r >= first_after_dt, 1, 0)) * N_HEADS

    return pl.pallas_call(
        functools.partial(_weight_prep_kernel, conv_steps=conv_steps),
        grid=(PROJ_COLS // cols,),
        in_specs=[pl.BlockSpec((pl.Element(cols), pl.Element(k)), lambda r: (src_row(r), 0)),
                  pl.BlockSpec((LANES, k), lambda r: (COL_POOL // LANES, 0)),
                  pl.BlockSpec((None, POOL_GW, POOL_GW), lambda r: (group(r), 0, 0)),
                  pl.BlockSpec((1, POOL_GW), lambda r: (0, group(r))),
                  pl.BlockSpec((POOL_GW, n), lambda r: (group(r), 0)),
                  pl.BlockSpec((CONV_K, cols), conv_blk),
                  pl.BlockSpec((1, cols), conv_blk),
                  pl.BlockSpec((1, N_HEADS), whole),
                  pl.BlockSpec((1, N_HEADS), whole),
                  pl.BlockSpec((1, N_HEADS), whole),
                  pl.BlockSpec((LANES, D_INNER), whole)],
        out_specs=[pl.BlockSpec((None, k, cols), lambda r: (r // (PROJ_TN // cols), 0, r % (PROJ_TN // cols))),
                   pl.BlockSpec((k, LANES), whole),
                   pl.BlockSpec((POOL_GW, n), lambda r: (group(r), 0)),
                   pl.BlockSpec((CONV_K, cols), lambda r: (0, r)),
                   pl.BlockSpec((1, cols), lambda r: (0, r)),
                   pl.BlockSpec((1, LANES), whole),
                   pl.BlockSpec((1, LANES), whole),
                   pl.BlockSpec((1, D_INNER), whole)],
        out_shape=[jax.ShapeDtypeStruct((PROJ_COLS // PROJ_TN, k, PROJ_TN), BF16),
                   jax.ShapeDtypeStruct((k, LANES), F32),
                   jax.ShapeDtypeStruct((D_MODEL, n), BF16),
                   jax.ShapeDtypeStruct((CONV_K, PROJ_COLS), F32),
                   jax.ShapeDtypeStruct((1, PROJ_COLS), F32),
                   jax.ShapeDtypeStruct((1, LANES), F32),
                   jax.ShapeDtypeStruct((1, LANES), F32),
                   jax.ShapeDtypeStruct((1, D_INNER), F32)],
        name="weight_prep",
    )(w_in_t, w_in_t, pool_w, pool_scale, w_bp, conv_w, conv_b, dt_bias, a_log, d_skip, expand)


def _conv_silu_store(r, cw_ref, cb_ref, col0, halo_ref, o_ref, width):
    n_chunks = r.shape[0] // CHUNK
    pair = 2 * SUBLANES
    for c0 in range(0, width, CONV_COL_BLOCK):
        cols = slice(c0, c0 + CONV_COL_BLOCK)
        slabs = _slabs_of(r[:, cols], n_chunks)
        prev_last = {j: halo_ref[j - 1, :, cols] for j in range(1, CONV_K)}
        for j in range(1, CONV_K):
            halo_ref[j - 1, :, cols] = slabs[n_chunks - 1][SLABS - j]
        wrapped = _wrapped_slabs(slabs, prev_last, CONV_K - 1)
        pcols = slice(col0 + c0, col0 + c0 + CONV_COL_BLOCK)
        taps = [cw_ref[k:k + 1, pcols] for k in range(CONV_K)]
        bias = cb_ref[:, pcols]
        for c in range(n_chunks):
            for v0 in range(0, SLABS, 2):
                accs = []
                for v in (v0, v0 + 1):
                    acc = slabs[c][v] * taps[CONV_K - 1] + bias
                    for k in range(1, CONV_K):
                        acc = acc + _shifted(slabs, wrapped, c, v, k) * taps[CONV_K - 1 - k]
                    accs.append(acc)
                lo = (c * SLABS + v0) * SUBLANES
                o_ref[lo:lo + pair, cols] = jax.nn.silu(jnp.concatenate(accs, axis=0)).astype(BF16)


def _inproj_kernel(x_ref, nw_ref, mod_ref, w_ref, wdt_ref, cw_ref, cb_ref, perm_ref, o_ref, dt_ref,
                   h_scr, halo_scr, *, tiles_per_batch):
    i = pl.program_id(0)
    j = pl.program_id(1)

    @pl.when(j == 0)
    def _():
        mod = mod_ref[0]
        shift = mod[:, 0:D_MODEL]
        scale = mod[:, D_MODEL:2 * D_MODEL]
        h = _rms(x_ref[...]) * (nw_ref[...] * (1.0 + scale)) + shift
        hb = _permute_chunks(perm_ref[...], h.astype(BF16))
        h_scr[...] = hb
        dt_ref[...] = jnp.dot(hb, wdt_ref[...].astype(BF16), preferred_element_type=F32)

    @pl.when(jnp.logical_and(i % tiles_per_batch == 0, j == 0))
    def _():
        halo_scr[...] = jnp.zeros_like(halo_scr)

    def proj():
        return _dot_row_parts(h_scr[...], w_ref[...], PROJ_ROW_SPLIT)

    @pl.when(j == TILE_Z)
    def _():
        o_ref[...] = jax.nn.silu(proj()).astype(BF16)

    @pl.when(j == TILE_XS)
    def _():
        _conv_silu_store(proj(), cw_ref, cb_ref, TILE_XS * PROJ_TN, halo_scr.at[0], o_ref, PROJ_TN)

    @pl.when(j == TILE_BC_POOL)
    def _():
        r = proj()
        _conv_silu_store(r[:, 0:BC_WIDTH], cw_ref, cb_ref, TILE_BC_POOL * PROJ_TN, halo_scr.at[1], o_ref,
                         BC_WIDTH)
        o_ref[:, BC_WIDTH:] = r[:, BC_WIDTH:].astype(BF16)

    @pl.when(j == TILE_GATE)
    def _():
        o_ref[...] = jax.nn.sigmoid(proj()).astype(BF16)


def _in_projection(x2d, norm_w, mod3, w_main, w_dt, conv_w_cols, conv_b_cols, perm, seq):
    t = x2d.shape[0]
    tm, tn = 1024, PROJ_TN
    tiles_per_batch = seq // tm
    return pl.pallas_call(
        functools.partial(_inproj_kernel, tiles_per_batch=tiles_per_batch),
        grid=(t // tm, PROJ_COLS // tn),
        in_specs=[pl.BlockSpec((tm, D_MODEL), lambda i, j: (i, 0)),
                  pl.BlockSpec((1, D_MODEL), lambda i, j: (0, 0)),
                  pl.BlockSpec((1, 1, N_MOD * D_MODEL), lambda i, j: (i // tiles_per_batch, 0, 0)),
                  pl.BlockSpec((None, D_MODEL, tn), lambda i, j: (j, 0, 0)),
                  pl.BlockSpec((D_MODEL, LANES), lambda i, j: (0, 0)),
                  pl.BlockSpec((CONV_K, PROJ_COLS), lambda i, j: (0, 0)),
                  pl.BlockSpec((1, PROJ_COLS), lambda i, j: (0, 0)),
                  pl.BlockSpec((CHUNK, CHUNK), lambda i, j: (0, 0))],
        out_specs=[pl.BlockSpec((None, tm, tn), lambda i, j: (j, i, 0)),
                   pl.BlockSpec((tm, LANES), lambda i, j: (i, 0))],
        out_shape=[jax.ShapeDtypeStruct((PROJ_COLS // tn, t, tn), BF16),
                   jax.ShapeDtypeStruct((t, LANES), F32)],
        scratch_shapes=[pltpu.VMEM((tm, D_MODEL), BF16),
                        pltpu.VMEM((2, CONV_K - 1, SUBLANES, tn), F32)],
        compiler_params=pltpu.CompilerParams(
            dimension_semantics=("arbitrary", "arbitrary"), vmem_limit_bytes=VMEM_LIMIT),
        name="norm_in_proj",
    )(x2d, norm_w, mod3, w_main, w_dt, conv_w_cols, conv_b_cols, perm)


def _split3(v):
    hi = v.astype(BF16).astype(F32)
    r1 = v - hi
    mid = r1.astype(BF16).astype(F32)
    return hi, r1, r1 - mid


def _lane_pieces(v, lane):
    hi, r1, r2 = _split3(v)
    return jnp.where(lane < N_HEADS, hi, jnp.where(lane < 2 * N_HEADS, r1, r2)).astype(BF16)


def _ssd_prologue(xs_ref, dt_ref, dtb_ref, alog_ref, tri_ref, exp_ref, xdb, expd, acs2_scr, src_scr):
    q = CHUNK
    tm = dt_ref.shape[0]
    n_chunks = tm // q
    dt = jax.nn.softplus(dt_ref[...] + dtb_ref[...])
    dta = dt * (-jnp.exp(alog_ref[...]))
    hi, r1, r2 = _split3(dta)
    acs = []
    for c in range(n_chunks):
        rows = slice(c * q, (c + 1) * q)
        stacked = jnp.concatenate([hi[rows], r1[rows], r2[rows]], axis=0).astype(BF16)
        acs.append(jnp.dot(tri_ref[...], stacked, preferred_element_type=F32))
    acs2 = jnp.concatenate(acs, axis=0) * LOG2_E
    acs2_last = jnp.concatenate(
        [jnp.broadcast_to(acs2[(c + 1) * q - 1:(c + 1) * q, :], (q, LANES)) for c in range(n_chunks)], axis=0)
    decay_to_end = jnp.exp2(acs2_last - acs2)
    decay_from_start = jnp.exp2(acs2)
    acs2_scr[...] = acs2
    src = acs2 - jnp.log2(dt)
    for c in range(n_chunks):
        src_scr[c] = src[c * q:(c + 1) * q, :].T

    lane = lax.broadcasted_iota(jnp.int32, (tm, LANES), 1)
    to_end = jnp.dot(_lane_pieces(dt * decay_to_end, lane), exp_ref[...], preferred_element_type=F32)
    xdb[...] = (xs_ref[...].astype(F32) * to_end).astype(BF16)
    expd[...] = _lane_pieces(decay_from_start, lane)


def _ssd_chunk(ci, xs_ref, bc_ref, dskip_ref, exp_ref, o_ref, state, xdb, expd, acs2_scr, src_scr):
    q = CHUNK
    rows = slice(ci * q, (ci + 1) * q)
    acs2 = acs2_scr[rows, :]
    src_t = src_scr[ci]

    causal = (_token_of_row(lax.broadcasted_iota(jnp.int32, (q, q), 0))
              >= _token_of_row(lax.broadcasted_iota(jnp.int32, (q, q), 1)))
    half = lax.broadcasted_iota(jnp.int32, (q, LANES), 1) < HEAD_DIM
    zero_b = jnp.zeros((q, LANES), BF16)

    for g in range(N_GROUPS):
        gs = slice(g * GROUP_CH, (g + 1) * GROUP_CH)
        bg = bc_ref[rows, g * D_STATE:(g + 1) * D_STATE]
        cg = bc_ref[rows, (N_GROUPS + g) * D_STATE:(N_GROUPS + g + 1) * D_STATE]
        scores = lax.dot_general(cg, bg, (((1,), (1,)), ((), ())), preferred_element_type=F32)
        st_old = state[:, gs]
        y_off = jnp.dot(cg, st_old.astype(BF16), preferred_element_type=F32)
        new_t = lax.dot_general(bg, xdb[rows, gs], (((0,), (0,)), ((), ())),
                                preferred_element_type=F32)
        from_start = jnp.dot(expd[rows, :], exp_ref[:, gs], preferred_element_type=F32)
        state[:, gs] = st_old * from_start[q - 1:q, :] + new_t
        ys = []
        for pr in range(GROUP_CH // LANES):
            h0 = g * (GROUP_CH // HEAD_DIM) + 2 * pr
            ms = []
            for h in (h0, h0 + 1):
                a_col = jnp.broadcast_to(acs2[:, h:h + 1], (q, q))
                a_row = jnp.broadcast_to(src_t[h:h + 1, :], (q, q))
                decay_dt = jnp.exp2(jnp.where(causal, a_col - a_row, -jnp.inf))
                ms.append((scores * decay_dt).astype(BF16))
            xp = xs_ref[rows, g * GROUP_CH + pr * LANES:g * GROUP_CH + (pr + 1) * LANES]
            rhs = jnp.concatenate([jnp.where(half, xp, zero_b), jnp.where(half, zero_b, xp)], axis=0)
            ys.append(jnp.dot(jnp.concatenate(ms, axis=1), rhs, preferred_element_type=F32))
        y = jnp.concatenate(ys, axis=1) + y_off * from_start
        o_ref[rows, gs] = (y + dskip_ref[:, gs] * xs_ref[rows, gs].astype(F32)).astype(BF16)


def _ssd_kernel(xs_ref, bc_ref, dt_ref, dtb_ref, alog_ref, dskip_ref, tri_ref, exp_ref, scale0_ref, *rest,
                n_cast):
    cast_in = rest[:n_cast]
    o_ref = rest[n_cast]
    cast_out = rest[n_cast + 1:2 * n_cast + 1]
    state, xdb, expd, acs2_scr, src_scr = rest[2 * n_cast + 1:]

    @pl.when(pl.program_id(1) == 0)
    def _():
        state[...] = jnp.zeros_like(state)

    cast_out[0][...] = (cast_in[0][...] * scale0_ref[...]).astype(BF16)
    for w_ref, wb_ref in zip(cast_in[1:], cast_out[1:]):
        wb_ref[...] = w_ref[...].astype(BF16)

    _ssd_prologue(xs_ref, dt_ref, dtb_ref, alog_ref, tri_ref, exp_ref, xdb, expd, acs2_scr, src_scr)
    for ci in range(SSD_CHUNKS_PER_STEP):
        _ssd_chunk(ci, xs_ref, bc_ref, dskip_ref, exp_ref, o_ref, state, xdb, expd, acs2_scr, src_scr)


def _ssd_scan(proj, dt_raw, dt_bias3, a_log3, d_skip_ch, tri3, expand3, row_scale0, cast_weights, batch, seq):
    q = CHUNK
    tm = SSD_CHUNKS_PER_STEP * q
    steps = seq // tm
    t = batch * seq
    n_steps = batch * steps
    rowmap = lambda b, c: b * steps + c
    const = lambda b, c: (0, 0)
    slab_specs = [pl.BlockSpec((w.shape[0] // n_steps, w.shape[1]), lambda b, c: (rowmap(b, c), 0))
                  for w in cast_weights]
    outs = pl.pallas_call(
        functools.partial(_ssd_kernel, n_cast=len(cast_weights)),
        grid=(batch, steps),
        in_specs=[pl.BlockSpec((None, tm, D_INNER), lambda b, c: (TILE_XS, rowmap(b, c), 0)),
                  pl.BlockSpec((None, tm, BC_WIDTH), lambda b, c: (TILE_BC_POOL, rowmap(b, c), 0)),
                  pl.BlockSpec((tm, LANES), lambda b, c: (rowmap(b, c), 0)),
                  pl.BlockSpec((1, LANES), const),
                  pl.BlockSpec((1, LANES), const),
                  pl.BlockSpec((1, D_INNER), const),
                  pl.BlockSpec((q, DT_REP * q), const),
                  pl.BlockSpec((LANES, D_INNER), const),
                  pl.BlockSpec((cast_weights[0].shape[0] // n_steps, 1), lambda b, c: (rowmap(b, c), 0))]
        + slab_specs,
        out_specs=[pl.BlockSpec((tm, D_INNER), lambda b, c: (rowmap(b, c), 0))] + slab_specs,
        out_shape=[jax.ShapeDtypeStruct((t, D_INNER), BF16)]
        + [jax.ShapeDtypeStruct(w.shape, BF16) for w in cast_weights],
        scratch_shapes=[pltpu.VMEM((D_STATE, D_INNER), F32),
                        pltpu.VMEM((tm, D_INNER), BF16),
                        pltpu.VMEM((tm, LANES), BF16),
                        pltpu.VMEM((tm, LANES), F32),
                        pltpu.VMEM((SSD_CHUNKS_PER_STEP, LANES, q), F32)],
        compiler_params=pltpu.CompilerParams(
            dimension_semantics=("arbitrary", "arbitrary"), vmem_limit_bytes=VMEM_LIMIT),
        name="ssd_scan",
    )(proj, proj, dt_raw, dt_bias3, a_log3, d_skip_ch, tri3, expand3, row_scale0, *cast_weights)
    return outs[0], outs[1:]


def _pooled_groups(u, halo_ref, pos0):
    n_chunks = u.shape[0] // CHUNK
    max_shift = max(POOL_WINDOWS) - 1
    slabs = _slabs_of(u, n_chunks)
    prev_last = {j: halo_ref[SLABS - j] for j in range(1, max_shift + 1)}
    for v in range(SLABS):
        halo_ref[v] = slabs[n_chunks - 1][v]
    wrapped = _wrapped_slabs(slabs, prev_last, max_shift)
    sub = lax.broadcasted_iota(jnp.int32, (SUBLANES, POOL_GW), 0)
    pooled = []
    for gi, win in enumerate(POOL_WINDOWS):
        cs = slice(gi * POOL_GW, (gi + 1) * POOL_GW)
        out = []
        for c in range(n_chunks):
            for v in range(SLABS):
                cur = slabs[c][v][:, cs]
                tot = cur
                for k in range(1, win):
                    tot = tot + _shifted(slabs, wrapped, c, v, k)[:, cs]
                pos = pos0 + c * CHUNK + sub * SLABS + v
                count = jnp.minimum(pos + 1, win).astype(F32)
                out.append(tot / count - cur)
        pooled.append(jnp.concatenate(out, axis=0).astype(BF16))
    return pooled


def _tail_kernel(x_ref, y_ref, zs_ref, u_ref, g_ref, mod_ref, wbs_ref, wpool_ref,
                 wo_ref, nmlp_ref, wup_ref, wdn_ref, nfin_ref, unperm_ref, o_ref, uhalo,
                 *, tm, tiles_per_batch):
    i = pl.program_id(0)

    @pl.when(i % tiles_per_batch == 0)
    def _():
        uhalo[...] = jnp.zeros_like(uhalo)

    y_gated = []
    for g in range(N_GROUPS):
        gs = slice(g * GROUP_CH, (g + 1) * GROUP_CH)
        yg = y_ref[:, gs].astype(F32) * zs_ref[:, gs].astype(F32)
        y_gated.append(_rms(yg).astype(BF16))
    y_ssd = jnp.dot(jnp.concatenate(y_gated, axis=1), wbs_ref[...], preferred_element_type=F32)

    pooled = _pooled_groups(u_ref[...].astype(F32), uhalo, (i % tiles_per_batch) * tm)
    y_pool = jnp.dot(jnp.concatenate(pooled, axis=1), wpool_ref[...], preferred_element_type=F32)

    merged = (g_ref[:, :D_MODEL].astype(F32) * y_ssd + g_ref[:, D_MODEL:].astype(F32) * y_pool)
    merged = _permute_chunks(unperm_ref[...], merged.astype(BF16))
    mix = jnp.dot(merged, wo_ref[...], preferred_element_type=F32)

    mod = mod_ref[0]
    gate_m = mod[:, 2 * D_MODEL:3 * D_MODEL]
    shift_f = mod[:, 3 * D_MODEL:4 * D_MODEL]
    scale_f = mod[:, 4 * D_MODEL:5 * D_MODEL]
    gate_f = mod[:, 5 * D_MODEL:6 * D_MODEL]
    x1 = x_ref[...] + gate_m * mix
    h = _rms(x1) * (nmlp_ref[...] * (1.0 + scale_f)) + shift_f
    up = jnp.dot(h.astype(BF16), wup_ref[...], preferred_element_type=F32)
    act = jnp.square(jnp.maximum(up, 0.0)).astype(BF16)
    x2 = x1 + gate_f * jnp.dot(act, wdn_ref[...], preferred_element_type=F32)
    o_ref[...] = _rms(x2) * nfin_ref[...]


def _tail(x2d, y_raw, proj, mod3, w_bs, w_pool, w_out, norm_mlp_w, w_up, w_dn, norm_final_w,
          unperm, seq):
    t = x2d.shape[0]
    tm = 512
    tiles_per_batch = seq // tm
    const2 = lambda i: (0, 0)
    resident = functools.partial(pl.BlockSpec, pipeline_mode=pl.Buffered(1))
    return pl.pallas_call(
        functools.partial(_tail_kernel, tm=tm, tiles_per_batch=tiles_per_batch),
        grid=(t // tm,),
        in_specs=[pl.BlockSpec((tm, D_MODEL), lambda i: (i, 0)),
                  pl.BlockSpec((tm, D_INNER), lambda i: (i, 0)),
                  pl.BlockSpec((None, tm, D_INNER), lambda i: (TILE_Z, i, 0)),
                  pl.BlockSpec((None, tm, D_MODEL), lambda i: (TILE_BC_POOL, i, BC_WIDTH // D_MODEL)),
                  pl.BlockSpec((None, tm, 2 * D_MODEL), lambda i: (TILE_GATE, i, 0)),
                  pl.BlockSpec((1, 1, N_MOD * D_MODEL), lambda i: (i // tiles_per_batch, 0, 0)),
                  resident((D_INNER, D_MODEL), const2),
                  resident((D_MODEL, D_MODEL), const2),
                  resident((D_MODEL, D_MODEL), const2),
                  resident((1, D_MODEL), const2),
                  resident((D_MODEL, D_FF), const2),
                  resident((D_FF, D_MODEL), const2),
                  resident((1, D_MODEL), const2),
                  resident((CHUNK, CHUNK), const2)],
        out_specs=pl.BlockSpec((tm, D_MODEL), lambda i: (i, 0)),
        out_shape=jax.ShapeDtypeStruct((t, D_MODEL), F32),
        scratch_shapes=[pltpu.VMEM((SLABS, SUBLANES, D_MODEL), F32)],
        compiler_params=pltpu.CompilerParams(
            dimension_semantics=("arbitrary",), vmem_limit_bytes=VMEM_LIMIT),
        name="gate_pool_merge_mlp",
    )(x2d, y_raw, proj, proj, proj, mod3, w_bs, w_pool, w_out, norm_mlp_w, w_up, w_dn,
      norm_final_w, unperm)


def _cumsum_matrix():
    tok = _token_of_row(np.arange(CHUNK))
    tri = (tok[None, :] <= tok[:, None]).astype(np.float32)
    return jnp.asarray(np.concatenate([tri] * DT_REP, axis=1), BF16)


def _interleave_matrix():
    p = np.zeros((CHUNK, CHUNK), np.float32)
    p[np.arange(CHUNK), _token_of_row(np.arange(CHUNK))] = 1.0
    return jnp.asarray(p, BF16), jnp.asarray(p.T, BF16)


def _head_expand_matrix():
    m = np.zeros((LANES, D_INNER), np.float32)
    for k in range(DT_REP * N_HEADS):
        h = k % N_HEADS
        m[k, h * HEAD_DIM:(h + 1) * HEAD_DIM] = 1.0
    return jnp.asarray(m, BF16)


def kernel(x, c, w_ada, b_ada, norm_mix_w, w_in, conv_w, conv_b, dt_bias, a_log, d_skip, ssd_norm_w,
           w_branch_ssd, pool_w, pool_scale, w_branch_pool, w_out, norm_mlp_w, w_up, w_down,
           norm_final_w):
    batch, seq, d = x.shape
    depth = w_ada.shape[0]
    assert depth == 1, "the final norm is fused into the last kernel of a single layer"
    t = batch * seq
    tri3 = _cumsum_matrix()
    expand3 = _head_expand_matrix()
    perm, unperm = _interleave_matrix()
    x2d = x.reshape(t, d)
    i = 0
    mod3 = _modulation(c, w_ada[i], b_ada[i][None, :])[:, None, :]
    w_in_t = jnp.swapaxes(w_in[i], 0, 1)
    w_main, w_dt3, w_pool, conv_w_cols, conv_b_cols, dt_bias3, a_log3, d_skip_ch = _weight_prep(
        w_in_t, pool_w[i], pool_scale[i][None, :], w_branch_pool[i], conv_w[i], conv_b[i][None, :],
        dt_bias[i][None, :], a_log[i][None, :], d_skip[i][None, :], expand3)
    proj, dt_raw = _in_projection(x2d, norm_mix_w[i][None, :], mod3, w_main, w_dt3, conv_w_cols,
                                  conv_b_cols, perm, seq)
    tail_weights = [w_branch_ssd[i], w_out[i], w_up[i], w_down[i]]
    y_raw, (w_bs, w_ob, w_upb, w_dnb) = _ssd_scan(
        proj, dt_raw, dt_bias3, a_log3, d_skip_ch, tri3, expand3,
        ssd_norm_w[i][:, None], tail_weights, batch, seq)
    out = _tail(x2d, y_raw, proj, mod3, w_bs, w_pool, w_ob, norm_mlp_w[i][None, :], w_upb, w_dnb,
                norm_final_w[None, :], unperm, seq)
    return out.reshape(batch, seq, d)
```

```python
import functools

import jax
import jax.numpy as jnp
import numpy as np
from jax import lax
from jax.experimental import pallas as pl
from jax.experimental.pallas import tpu as pltpu

F32 = jnp.float32
BF16 = jnp.bfloat16

D_MODEL = 1024
D_INNER = 2 * D_MODEL
HEAD_DIM = 64
N_HEADS = D_INNER // HEAD_DIM
N_GROUPS = 4
GROUP_CH = D_INNER // N_GROUPS
D_STATE = 128
CONV_K = 4
CHUNK = 128
BC_WIDTH = 2 * N_GROUPS * D_STATE
POOL_WINDOWS = (2, 4, 8, 16)
POOL_GW = D_MODEL // len(POOL_WINDOWS)
D_FF = 4 * D_MODEL
N_MOD = 6
EPS = 1e-5
LOG2_E = 1.4426950408889634
LANES = 128
SUBLANES = 8
SLABS = CHUNK // SUBLANES
DT_REP = 3

PROJ_TN = 2048
COL_XS = D_INNER
COL_POOL = 2 * D_INNER + BC_WIDTH
PROJ_COLS = COL_POOL + 3 * D_MODEL
TILE_Z, TILE_XS, TILE_BC_POOL, TILE_GATE = 0, 1, 2, 3
CONV_COL_BLOCK = 512
PROJ_ROW_SPLIT = 4

VMEM_LIMIT = 56 * 1024 * 1024
SSD_CHUNKS_PER_STEP = 8


def _rms(x):
    return x * lax.rsqrt(jnp.mean(x * x, axis=-1, keepdims=True) + EPS)


def _dot_row_parts(x, w, parts):
    rows = x.shape[0] // parts
    return jnp.concatenate(
        [jnp.dot(x[m0:m0 + rows, :], w, preferred_element_type=F32) for m0 in range(0, x.shape[0], rows)],
        axis=0)


def _token_of_row(r):
    return (r % SUBLANES) * SLABS + r // SUBLANES


def _permute_chunks(perm, xb):
    n_chunks = xb.shape[0] // CHUNK
    return jnp.concatenate(
        [jnp.dot(perm, xb[c * CHUNK:(c + 1) * CHUNK, :], preferred_element_type=F32).astype(BF16)
         for c in range(n_chunks)], axis=0)


def _slabs_of(x, n_chunks):
    return [[x[(c * SLABS + v) * SUBLANES:(c * SLABS + v + 1) * SUBLANES, :] for v in range(SLABS)]
            for c in range(n_chunks)]


def _wrapped_slabs(slabs, prev_last, max_shift):
    n_chunks = len(slabs)
    width = slabs[0][0].shape[1]
    first_row = lax.broadcasted_iota(jnp.int32, (SUBLANES, width), 0) == 0
    wrapped = {}
    for j in range(1, max_shift + 1):
        stacked = jnp.concatenate([slabs[c][SLABS - j] for c in range(n_chunks)], axis=0)
        rolled = pltpu.roll(stacked, 1, 0)
        head = jnp.where(first_row, pltpu.roll(prev_last[j], 1, 0), rolled[0:SUBLANES, :])
        wrapped[j] = [head] + [rolled[c * SUBLANES:(c + 1) * SUBLANES, :] for c in range(1, n_chunks)]
    return wrapped


def _shifted(slabs, wrapped, c, v, k):
    return slabs[c][v - k] if v >= k else wrapped[k - v][c]


def _mod_kernel(c_ref, w_ref, b_ref, o_ref):
    batch = c_ref.shape[0]
    s = jax.nn.silu(c_ref[...])
    s = jnp.concatenate([s, jnp.zeros((SUBLANES - batch, s.shape[1]), F32)], axis=0)
    mod = jnp.dot(s.astype(BF16), w_ref[...].astype(BF16), preferred_element_type=F32) + b_ref[...]
    o_ref[...] = mod[0:batch, :]


def _rep_heads(v):
    pad = jnp.zeros((v.shape[0], LANES - DT_REP * N_HEADS), v.dtype)
    return jnp.concatenate([v] * DT_REP + [pad], axis=1)


def _weight_prep_kernel(wt_ref, wdt_ref, pw_ref, ps_ref, wbp_ref, cw_ref, cb_ref, dtb_ref, alog_ref,
                        dskip_ref, exp_ref, c_ref, wada_ref, bada_ref,
                        o_ref, odt_ref, opool_ref, ocw_ref, ocb_ref, odtb_ref, oalog_ref, odskip_ref, omod_ref,
                        *, conv_steps):
    r = pl.program_id(0)
    o_ref[...] = wt_ref[...].T.astype(BF16)
    _mod_kernel(c_ref, wada_ref, bada_ref, omod_ref)

    on_conv = jnp.logical_and(r >= conv_steps[0], r < conv_steps[1])
    ocw_ref[...] = jnp.where(on_conv, cw_ref[...], 0.0)
    ocb_ref[...] = jnp.where(on_conv, cb_ref[...], 0.0)

    @pl.when(r == 0)
    def _():
        odt_ref[...] = _rep_heads(wdt_ref[...].T[:, 0:N_HEADS])
        odtb_ref[...] = _rep_heads(dtb_ref[...])
        oalog_ref[...] = _rep_heads(alog_ref[...])
        d_row = jnp.concatenate([dskip_ref[...], jnp.zeros((1, LANES - N_HEADS), F32)], axis=1)
        d8 = jnp.concatenate([d_row, jnp.zeros((SUBLANES - 1, LANES), F32)], axis=0)
        odskip_ref[...] = jnp.dot(d8, exp_ref[...], precision=lax.Precision.HIGHEST,
                                  preferred_element_type=F32)[0:1, :]

    @pl.when(r < len(POOL_WINDOWS))
    def _():
        opool_ref[...] = jnp.dot(pw_ref[...] * ps_ref[...], wbp_ref[...], precision=lax.Precision.HIGHEST,
                                 preferred_element_type=F32).astype(BF16)


def _weight_prep(w_in_t, pool_w, pool_scale, w_bp, conv_w, conv_b, dt_bias, a_log, d_skip, expand,
                 c, w_ada, b_ada):
    cols = 1024
    k = w_in_t.shape[1]
    batch = c.shape[0]
    n_mod = w_ada.shape[1]
    mod_cols = n_mod // (PROJ_COLS // cols)
    n = w_bp.shape[1]
    first_after_dt = COL_POOL // cols
    last_group = len(POOL_WINDOWS) - 1
    group = lambda r: jnp.minimum(r, last_group)
    conv_steps = (COL_XS // cols, COL_POOL // cols)
    conv_blk = lambda r: (0, jnp.clip(r - conv_steps[0], 0, conv_steps[1] - conv_steps[0] - 1))
    whole = lambda r: (0, 0)

    def src_row(r):
        return (r * (cols // N_HEADS) + jnp.where(r >= first_after_dt, 1, 0)) * N_HEADS

    return pl.pallas_call(
        functools.partial(_weight_prep_kernel, conv_steps=conv_steps),
        grid=(PROJ_COLS // cols,),
        in_specs=[pl.BlockSpec((pl.Element(cols), pl.Element(k)), lambda r: (src_row(r), 0)),
                  pl.BlockSpec((LANES, k), lambda r: (COL_POOL // LANES, 0)),
                  pl.BlockSpec((None, POOL_GW, POOL_GW), lambda r: (group(r), 0, 0)),
                  pl.BlockSpec((1, POOL_GW), lambda r: (0, group(r))),
                  pl.BlockSpec((POOL_GW, n), lambda r: (group(r), 0)),
                  pl.BlockSpec((CONV_K, cols), conv_blk),
                  pl.BlockSpec((1, cols), conv_blk),
                  pl.BlockSpec((1, N_HEADS), whole),
                  pl.BlockSpec((1, N_HEADS), whole),
                  pl.BlockSpec((1, N_HEADS), whole),
                  pl.BlockSpec((LANES, D_INNER), whole),
                  pl.BlockSpec((batch, k), whole),
                  pl.BlockSpec((k, mod_cols), lambda r: (0, r)),
                  pl.BlockSpec((1, mod_cols), lambda r: (0, r))],
        out_specs=[pl.BlockSpec((None, k, cols), lambda r: (r // (PROJ_TN // cols), 0, r % (PROJ_TN // cols))),
                   pl.BlockSpec((k, LANES), whole),
                   pl.BlockSpec((POOL_GW, n), lambda r: (group(r), 0)),
                   pl.BlockSpec((CONV_K, cols), lambda r: (0, r)),
                   pl.BlockSpec((1, cols), lambda r: (0, r)),
                   pl.BlockSpec((1, LANES), whole),
                   pl.BlockSpec((1, LANES), whole),
                   pl.BlockSpec((1, D_INNER), whole),
                   pl.BlockSpec((batch, mod_cols), lambda r: (0, r))],
        out_shape=[jax.ShapeDtypeStruct((PROJ_COLS // PROJ_TN, k, PROJ_TN), BF16),
                   jax.ShapeDtypeStruct((k, LANES), F32),
                   jax.ShapeDtypeStruct((D_MODEL, n), BF16),
                   jax.ShapeDtypeStruct((CONV_K, PROJ_COLS), F32),
                   jax.ShapeDtypeStruct((1, PROJ_COLS), F32),
                   jax.ShapeDtypeStruct((1, LANES), F32),
                   jax.ShapeDtypeStruct((1, LANES), F32),
                   jax.ShapeDtypeStruct((1, D_INNER), F32),
                   jax.ShapeDtypeStruct((batch, n_mod), F32)],
        compiler_params=pltpu.CompilerParams(vmem_limit_bytes=VMEM_LIMIT),
        name="weight_prep",
    )(w_in_t, w_in_t, pool_w, pool_scale, w_bp, conv_w, conv_b, dt_bias, a_log, d_skip, expand,
      c, w_ada, b_ada)


def _conv_silu_store(r, cw_ref, cb_ref, col0, halo_ref, o_ref, width):
    n_chunks = r.shape[0] // CHUNK
    pair = 2 * SUBLANES
    for c0 in range(0, width, CONV_COL_BLOCK):
        cols = slice(c0, c0 + CONV_COL_BLOCK)
        slabs = _slabs_of(r[:, cols], n_chunks)
        prev_last = {j: halo_ref[j - 1, :, cols] for j in range(1, CONV_K)}
        for j in range(1, CONV_K):
            halo_ref[j - 1, :, cols] = slabs[n_chunks - 1][SLABS - j]
        wrapped = _wrapped_slabs(slabs, prev_last, CONV_K - 1)
        pcols = slice(col0 + c0, col0 + c0 + CONV_COL_BLOCK)
        taps = [cw_ref[k:k + 1, pcols] for k in range(CONV_K)]
        bias = cb_ref[:, pcols]
        for c in range(n_chunks):
            for v0 in range(0, SLABS, 2):
                accs = []
                for v in (v0, v0 + 1):
                    acc = slabs[c][v] * taps[CONV_K - 1] + bias
                    for k in range(1, CONV_K):
                        acc = acc + _shifted(slabs, wrapped, c, v, k) * taps[CONV_K - 1 - k]
                    accs.append(acc)
                lo = (c * SLABS + v0) * SUBLANES
                o_ref[lo:lo + pair, cols] = jax.nn.silu(jnp.concatenate(accs, axis=0)).astype(BF16)


def _inproj_kernel(x_ref, nw_ref, mod_ref, w_ref, wdt_ref, cw_ref, cb_ref, perm_ref, o_ref, dt_ref,
                   h_scr, halo_scr, *, tiles_per_batch):
    i = pl.program_id(0)
    j = pl.program_id(1)

    @pl.when(j == 0)
    def _():
        mod = mod_ref[0]
        shift = mod[:, 0:D_MODEL]
        scale = mod[:, D_MODEL:2 * D_MODEL]
        h = _rms(x_ref[...]) * (nw_ref[...] * (1.0 + scale)) + shift
        hb = _permute_chunks(perm_ref[...].astype(BF16), h.astype(BF16))
        h_scr[...] = hb
        dt_ref[...] = jnp.dot(hb, wdt_ref[...].astype(BF16), preferred_element_type=F32)

    @pl.when(jnp.logical_and(i % tiles_per_batch == 0, j == 0))
    def _():
        halo_scr[...] = jnp.zeros_like(halo_scr)

    def proj():
        return _dot_row_parts(h_scr[...], w_ref[...], PROJ_ROW_SPLIT)

    @pl.when(j == TILE_Z)
    def _():
        o_ref[...] = jax.nn.silu(proj()).astype(BF16)

    @pl.when(j == TILE_XS)
    def _():
        _conv_silu_store(proj(), cw_ref, cb_ref, TILE_XS * PROJ_TN, halo_scr.at[0], o_ref, PROJ_TN)

    @pl.when(j == TILE_BC_POOL)
    def _():
        r = proj()
        _conv_silu_store(r[:, 0:BC_WIDTH], cw_ref, cb_ref, TILE_BC_POOL * PROJ_TN, halo_scr.at[1], o_ref,
                         BC_WIDTH)
        o_ref[:, BC_WIDTH:] = r[:, BC_WIDTH:].astype(BF16)

    @pl.when(j == TILE_GATE)
    def _():
        o_ref[...] = jax.nn.sigmoid(proj()).astype(BF16)


def _in_projection(x2d, norm_w, mod3, w_main, w_dt, conv_w_cols, conv_b_cols, perm, seq):
    t = x2d.shape[0]
    tm, tn = 1024, PROJ_TN
    tiles_per_batch = seq // tm
    return pl.pallas_call(
        functools.partial(_inproj_kernel, tiles_per_batch=tiles_per_batch),
        grid=(t // tm, PROJ_COLS // tn),
        in_specs=[pl.BlockSpec((tm, D_MODEL), lambda i, j: (i, 0)),
                  pl.BlockSpec((1, D_MODEL), lambda i, j: (0, 0)),
                  pl.BlockSpec((1, 1, N_MOD * D_MODEL), lambda i, j: (i // tiles_per_batch, 0, 0)),
                  pl.BlockSpec((None, D_MODEL, tn), lambda i, j: (j, 0, 0)),
                  pl.BlockSpec((D_MODEL, LANES), lambda i, j: (0, 0)),
                  pl.BlockSpec((CONV_K, PROJ_COLS), lambda i, j: (0, 0)),
                  pl.BlockSpec((1, PROJ_COLS), lambda i, j: (0, 0)),
                  pl.BlockSpec((CHUNK, CHUNK), lambda i, j: (0, 0))],
        out_specs=[pl.BlockSpec((None, tm, tn), lambda i, j: (j, i, 0)),
                   pl.BlockSpec((tm, LANES), lambda i, j: (i, 0))],
        out_shape=[jax.ShapeDtypeStruct((PROJ_COLS // tn, t, tn), BF16),
                   jax.ShapeDtypeStruct((t, LANES), F32)],
        scratch_shapes=[pltpu.VMEM((tm, D_MODEL), BF16),
                        pltpu.VMEM((2, CONV_K - 1, SUBLANES, tn), F32)],
        compiler_params=pltpu.CompilerParams(
            dimension_semantics=("arbitrary", "arbitrary"), vmem_limit_bytes=VMEM_LIMIT),
        name="norm_in_proj",
    )(x2d, norm_w, mod3, w_main, w_dt, conv_w_cols, conv_b_cols, perm)


def _split3(v):
    hi = v.astype(BF16).astype(F32)
    r1 = v - hi
    mid = r1.astype(BF16).astype(F32)
    return hi, r1, r1 - mid


def _lane_pieces(v, lane):
    hi, r1, r2 = _split3(v)
    return jnp.where(lane < N_HEADS, hi, jnp.where(lane < 2 * N_HEADS, r1, r2)).astype(BF16)


def _ssd_prologue(xs_ref, dt_ref, dtb_ref, alog_ref, tri, expand, xdb, expd, acs2_scr, src_scr):
    q = CHUNK
    tm = dt_ref.shape[0]
    n_chunks = tm // q
    dt = jax.nn.softplus(dt_ref[...] + dtb_ref[...])
    dta = dt * (-jnp.exp(alog_ref[...]))
    hi, r1, r2 = _split3(dta)
    acs = []
    for c in range(n_chunks):
        rows = slice(c * q, (c + 1) * q)
        stacked = jnp.concatenate([hi[rows], r1[rows], r2[rows]], axis=0).astype(BF16)
        acs.append(jnp.dot(tri, stacked, preferred_element_type=F32))
    acs2 = jnp.concatenate(acs, axis=0) * LOG2_E
    acs2_last = jnp.concatenate(
        [jnp.broadcast_to(acs2[(c + 1) * q - 1:(c + 1) * q, :], (q, LANES)) for c in range(n_chunks)], axis=0)
    decay_to_end = jnp.exp2(acs2_last - acs2)
    decay_from_start = jnp.exp2(acs2)
    acs2_scr[...] = acs2
    src = acs2 - jnp.log2(dt)
    for c in range(n_chunks):
        src_scr[c] = src[c * q:(c + 1) * q, :].T

    lane = lax.broadcasted_iota(jnp.int32, (tm, LANES), 1)
    to_end = jnp.dot(_lane_pieces(dt * decay_to_end, lane), expand, preferred_element_type=F32)
    xdb[...] = (xs_ref[...].astype(F32) * to_end).astype(BF16)
    expd[...] = _lane_pieces(decay_from_start, lane)


def _ssd_chunk(ci, xs_ref, bc_ref, dskip_ref, expand, o_ref, state, xdb, expd, acs2_scr, src_scr):
    q = CHUNK
    rows = slice(ci * q, (ci + 1) * q)
    acs2 = acs2_scr[rows, :]
    src_t = src_scr[ci]

    causal = (_token_of_row(lax.broadcasted_iota(jnp.int32, (q, q), 0))
              >= _token_of_row(lax.broadcasted_iota(jnp.int32, (q, q), 1)))
    half = lax.broadcasted_iota(jnp.int32, (q, LANES), 1) < HEAD_DIM
    zero_b = jnp.zeros((q, LANES), BF16)

    for g in range(N_GROUPS):
        gs = slice(g * GROUP_CH, (g + 1) * GROUP_CH)
        bg = bc_ref[rows, g * D_STATE:(g + 1) * D_STATE]
        cg = bc_ref[rows, (N_GROUPS + g) * D_STATE:(N_GROUPS + g + 1) * D_STATE]
        scores = lax.dot_general(cg, bg, (((1,), (1,)), ((), ())), preferred_element_type=F32)
        st_old = state[:, gs]
        y_off = jnp.dot(cg, st_old.astype(BF16), preferred_element_type=F32)
        new_t = lax.dot_general(bg, xdb[rows, gs], (((0,), (0,)), ((), ())),
                                preferred_element_type=F32)
        from_start = jnp.dot(expd[rows, :], expand[:, gs], preferred_element_type=F32)
        state[:, gs] = st_old * from_start[q - 1:q, :] + new_t
        ys = []
        for pr in range(GROUP_CH // LANES):
            h0 = g * (GROUP_CH // HEAD_DIM) + 2 * pr
            ms = []
            for h in (h0, h0 + 1):
                a_col = jnp.broadcast_to(acs2[:, h:h + 1], (q, q))
                a_row = jnp.broadcast_to(src_t[h:h + 1, :], (q, q))
                decay_dt = jnp.exp2(jnp.where(causal, a_col - a_row, -jnp.inf))
                ms.append((scores * decay_dt).astype(BF16))
            xp = xs_ref[rows, g * GROUP_CH + pr * LANES:g * GROUP_CH + (pr + 1) * LANES]
            rhs = jnp.concatenate([jnp.where(half, xp, zero_b), jnp.where(half, zero_b, xp)], axis=0)
            ys.append(jnp.dot(jnp.concatenate(ms, axis=1), rhs, preferred_element_type=F32))
        y = jnp.concatenate(ys, axis=1) + y_off * from_start
        o_ref[rows, gs] = (y + dskip_ref[:, gs] * xs_ref[rows, gs].astype(F32)).astype(BF16)


def _ssd_kernel(xs_ref, bc_ref, dt_ref, dtb_ref, alog_ref, dskip_ref, tri_ref, exp_ref, scale0_ref, *rest,
                n_cast):
    cast_in = rest[:n_cast]
    o_ref = rest[n_cast]
    cast_out = rest[n_cast + 1:2 * n_cast + 1]
    state, xdb, expd, acs2_scr, src_scr = rest[2 * n_cast + 1:]

    @pl.when(pl.program_id(1) == 0)
    def _():
        state[...] = jnp.zeros_like(state)

    cast_out[0][...] = (cast_in[0][...] * scale0_ref[...]).astype(BF16)
    for w_ref, wb_ref in zip(cast_in[1:], cast_out[1:]):
        wb_ref[...] = w_ref[...].astype(BF16)

    tri = tri_ref[...].astype(BF16)
    expand = exp_ref[...].astype(BF16)
    _ssd_prologue(xs_ref, dt_ref, dtb_ref, alog_ref, tri, expand, xdb, expd, acs2_scr, src_scr)
    for ci in range(SSD_CHUNKS_PER_STEP):
        _ssd_chunk(ci, xs_ref, bc_ref, dskip_ref, expand, o_ref, state, xdb, expd, acs2_scr, src_scr)


def _ssd_scan(proj, dt_raw, dt_bias3, a_log3, d_skip_ch, tri3, expand3, row_scale0, cast_weights, batch, seq):
    q = CHUNK
    tm = SSD_CHUNKS_PER_STEP * q
    steps = seq // tm
    t = batch * seq
    n_steps = batch * steps
    rowmap = lambda b, c: b * steps + c
    const = lambda b, c: (0, 0)
    slab_specs = [pl.BlockSpec((w.shape[0] // n_steps, w.shape[1]), lambda b, c: (rowmap(b, c), 0))
                  for w in cast_weights]
    outs = pl.pallas_call(
        functools.partial(_ssd_kernel, n_cast=len(cast_weights)),
        grid=(batch, steps),
        in_specs=[pl.BlockSpec((None, tm, D_INNER), lambda b, c: (TILE_XS, rowmap(b, c), 0)),
                  pl.BlockSpec((None, tm, BC_WIDTH), lambda b, c: (TILE_BC_POOL, rowmap(b, c), 0)),
                  pl.BlockSpec((tm, LANES), lambda b, c: (rowmap(b, c), 0)),
                  pl.BlockSpec((1, LANES), const),
                  pl.BlockSpec((1, LANES), const),
                  pl.BlockSpec((1, D_INNER), const),
                  pl.BlockSpec((q, DT_REP * q), const),
                  pl.BlockSpec((LANES, D_INNER), const),
                  pl.BlockSpec((cast_weights[0].shape[0] // n_steps, 1), lambda b, c: (rowmap(b, c), 0))]
        + slab_specs,
        out_specs=[pl.BlockSpec((tm, D_INNER), lambda b, c: (rowmap(b, c), 0))] + slab_specs,
        out_shape=[jax.ShapeDtypeStruct((t, D_INNER), BF16)]
        + [jax.ShapeDtypeStruct(w.shape, BF16) for w in cast_weights],
        scratch_shapes=[pltpu.VMEM((D_STATE, D_INNER), F32),
                        pltpu.VMEM((tm, D_INNER), BF16),
                        pltpu.VMEM((tm, LANES), BF16),
                        pltpu.VMEM((tm, LANES), F32),
                        pltpu.VMEM((SSD_CHUNKS_PER_STEP, LANES, q), F32)],
        compiler_params=pltpu.CompilerParams(
            dimension_semantics=("arbitrary", "arbitrary"), vmem_limit_bytes=VMEM_LIMIT),
        name="ssd_scan",
    )(proj, proj, dt_raw, dt_bias3, a_log3, d_skip_ch, tri3, expand3, row_scale0, *cast_weights)
    return outs[0], outs[1:]


def _pooled_groups(u, halo_ref, pos0):
    n_chunks = u.shape[0] // CHUNK
    max_shift = max(POOL_WINDOWS) - 1
    slabs = _slabs_of(u, n_chunks)
    prev_last = {j: halo_ref[SLABS - j] for j in range(1, max_shift + 1)}
    for v in range(SLABS):
        halo_ref[v] = slabs[n_chunks - 1][v]
    wrapped = _wrapped_slabs(slabs, prev_last, max_shift)
    sub = lax.broadcasted_iota(jnp.int32, (SUBLANES, POOL_GW), 0)
    pooled = []
    for gi, win in enumerate(POOL_WINDOWS):
        cs = slice(gi * POOL_GW, (gi + 1) * POOL_GW)
        out = []
        for c in range(n_chunks):
            for v in range(SLABS):
                cur = slabs[c][v][:, cs]
                tot = cur
                for k in range(1, win):
                    tot = tot + _shifted(slabs, wrapped, c, v, k)[:, cs]
                pos = pos0 + c * CHUNK + sub * SLABS + v
                count = jnp.minimum(pos + 1, win).astype(F32)
                out.append(tot / count - cur)
        pooled.append(jnp.concatenate(out, axis=0).astype(BF16))
    return pooled


def _tail_kernel(x_ref, y_ref, zs_ref, u_ref, g_ref, mod_ref, wbs_ref, wpool_ref,
                 wo_ref, nmlp_ref, wup_ref, wdn_ref, nfin_ref, unperm_ref, o_ref, uhalo,
                 *, tm, tiles_per_batch):
    i = pl.program_id(0)

    @pl.when(i % tiles_per_batch == 0)
    def _():
        uhalo[...] = jnp.zeros_like(uhalo)

    y_gated = []
    for g in range(N_GROUPS):
        gs = slice(g * GROUP_CH, (g + 1) * GROUP_CH)
        yg = y_ref[:, gs].astype(F32) * zs_ref[:, gs].astype(F32)
        y_gated.append(_rms(yg).astype(BF16))
    y_ssd = jnp.dot(jnp.concatenate(y_gated, axis=1), wbs_ref[...], preferred_element_type=F32)

    pooled = _pooled_groups(u_ref[...].astype(F32), uhalo, (i % tiles_per_batch) * tm)
    y_pool = jnp.dot(jnp.concatenate(pooled, axis=1), wpool_ref[...], preferred_element_type=F32)

    merged = (g_ref[:, :D_MODEL].astype(F32) * y_ssd + g_ref[:, D_MODEL:].astype(F32) * y_pool)
    merged = _permute_chunks(unperm_ref[...].astype(BF16), merged.astype(BF16))
    mix = jnp.dot(merged, wo_ref[...], preferred_element_type=F32)

    mod = mod_ref[0]
    gate_m = mod[:, 2 * D_MODEL:3 * D_MODEL]
    shift_f = mod[:, 3 * D_MODEL:4 * D_MODEL]
    scale_f = mod[:, 4 * D_MODEL:5 * D_MODEL]
    gate_f = mod[:, 5 * D_MODEL:6 * D_MODEL]
    x1 = x_ref[...] + gate_m * mix
    h = _rms(x1) * (nmlp_ref[...] * (1.0 + scale_f)) + shift_f
    up = jnp.dot(h.astype(BF16), wup_ref[...], preferred_element_type=F32)
    act = jnp.square(jnp.maximum(up, 0.0)).astype(BF16)
    x2 = x1 + gate_f * jnp.dot(act, wdn_ref[...], preferred_element_type=F32)
    o_ref[...] = _rms(x2) * nfin_ref[...]


def _tail(x2d, y_raw, proj, mod3, w_bs, w_pool, w_out, norm_mlp_w, w_up, w_dn, norm_final_w,
          unperm, seq):
    t = x2d.shape[0]
    tm = 512
    tiles_per_batch = seq // tm
    const2 = lambda i: (0, 0)
    resident = functools.partial(pl.BlockSpec, pipeline_mode=pl.Buffered(1))
    return pl.pallas_call(
        functools.partial(_tail_kernel, tm=tm, tiles_per_batch=tiles_per_batch),
        grid=(t // tm,),
        in_specs=[pl.BlockSpec((tm, D_MODEL), lambda i: (i, 0)),
                  pl.BlockSpec((tm, D_INNER), lambda i: (i, 0)),
                  pl.BlockSpec((None, tm, D_INNER), lambda i: (TILE_Z, i, 0)),
                  pl.BlockSpec((None, tm, D_MODEL), lambda i: (TILE_BC_POOL, i, BC_WIDTH // D_MODEL)),
                  pl.BlockSpec((None, tm, 2 * D_MODEL), lambda i: (TILE_GATE, i, 0)),
                  pl.BlockSpec((1, 1, N_MOD * D_MODEL), lambda i: (i // tiles_per_batch, 0, 0)),
                  resident((D_INNER, D_MODEL), const2),
                  resident((D_MODEL, D_MODEL), const2),
                  resident((D_MODEL, D_MODEL), const2),
                  resident((1, D_MODEL), const2),
                  resident((D_MODEL, D_FF), const2),
                  resident((D_FF, D_MODEL), const2),
                  resident((1, D_MODEL), const2),
                  resident((CHUNK, CHUNK), const2)],
        out_specs=pl.BlockSpec((tm, D_MODEL), lambda i: (i, 0)),
        out_shape=jax.ShapeDtypeStruct((t, D_MODEL), F32),
        scratch_shapes=[pltpu.VMEM((SLABS, SUBLANES, D_MODEL), F32)],
        compiler_params=pltpu.CompilerParams(
            dimension_semantics=("arbitrary",), vmem_limit_bytes=VMEM_LIMIT),
        name="gate_pool_merge_mlp",
    )(x2d, y_raw, proj, proj, proj, mod3, w_bs, w_pool, w_out, norm_mlp_w, w_up, w_dn,
      norm_final_w, unperm)


def _cumsum_matrix():
    tok = _token_of_row(np.arange(CHUNK))
    tri = (tok[None, :] <= tok[:, None]).astype(np.float32)
    return jnp.asarray(np.concatenate([tri] * DT_REP, axis=1), F32)


def _interleave_matrix():
    p = np.zeros((CHUNK, CHUNK), np.float32)
    p[np.arange(CHUNK), _token_of_row(np.arange(CHUNK))] = 1.0
    return jnp.asarray(p, F32), jnp.asarray(p.T, F32)


def _head_expand_matrix():
    m = np.zeros((LANES, D_INNER), np.float32)
    for k in range(DT_REP * N_HEADS):
        h = k % N_HEADS
        m[k, h * HEAD_DIM:(h + 1) * HEAD_DIM] = 1.0
    return jnp.asarray(m, F32)


def kernel(x, c, w_ada, b_ada, norm_mix_w, w_in, conv_w, conv_b, dt_bias, a_log, d_skip, ssd_norm_w,
           w_branch_ssd, pool_w, pool_scale, w_branch_pool, w_out, norm_mlp_w, w_up, w_down,
           norm_final_w):
    batch, seq, d = x.shape
    depth = w_ada.shape[0]
    assert depth == 1, "the final norm is fused into the last kernel of a single layer"
    t = batch * seq
    tri3 = _cumsum_matrix()
    expand3 = _head_expand_matrix()
    perm, unperm = _interleave_matrix()
    x2d = x.reshape(t, d)
    i = 0
    w_in_t = jnp.swapaxes(w_in[i], 0, 1)
    w_main, w_dt3, w_pool, conv_w_cols, conv_b_cols, dt_bias3, a_log3, d_skip_ch, mod = _weight_prep(
        w_in_t, pool_w[i], pool_scale[i][None, :], w_branch_pool[i], conv_w[i], conv_b[i][None, :],
        dt_bias[i][None, :], a_log[i][None, :], d_skip[i][None, :], expand3, c, w_ada[i], b_ada[i][None, :])
    mod3 = mod[:, None, :]
    proj, dt_raw = _in_projection(x2d, norm_mix_w[i][None, :], mod3, w_main, w_dt3, conv_w_cols,
                                  conv_b_cols, perm, seq)
    tail_weights = [w_branch_ssd[i], w_out[i], w_up[i], w_down[i]]
    y_raw, (w_bs, w_ob, w_upb, w_dnb) = _ssd_scan(
        proj, dt_raw, dt_bias3, a_log3, d_skip_ch, tri3, expand3,
        ssd_norm_w[i][:, None], tail_weights, batch, seq)
    out = _tail(x2d, y_raw, proj, mod3, w_bs, w_pool, w_ob, norm_mlp_w[i][None, :], w_upb, w_dnb,
                norm_final_w[None, :], unperm, seq)
    return out.reshape(batch, seq, d)
```

```python
import functools

import jax
import jax.numpy as jnp
import numpy as np
from jax import lax
from jax.experimental import pallas as pl
from jax.experimental.pallas import tpu as pltpu

F32 = jnp.float32
BF16 = jnp.bfloat16

D_MODEL = 1024
D_INNER = 2 * D_MODEL
HEAD_DIM = 64
N_HEADS = D_INNER // HEAD_DIM
N_GROUPS = 4
GROUP_CH = D_INNER // N_GROUPS
D_STATE = 128
CONV_K = 4
CHUNK = 128
BC_WIDTH = 2 * N_GROUPS * D_STATE
POOL_WINDOWS = (2, 4, 8, 16)
POOL_GW = D_MODEL // len(POOL_WINDOWS)
D_FF = 4 * D_MODEL
N_MOD = 6
EPS = 1e-5
LOG2_E = 1.4426950408889634
LANES = 128
SUBLANES = 8
SLABS = CHUNK // SUBLANES
DT_REP = 3

PROJ_TN = 2048
COL_XS = D_INNER
COL_POOL = 2 * D_INNER + BC_WIDTH
PROJ_COLS = COL_POOL + 3 * D_MODEL
TILE_Z, TILE_XS, TILE_BC_POOL, TILE_GATE = 0, 1, 2, 3
CONV_COL_BLOCK = 512
PROJ_ROW_SPLIT = 4

VMEM_LIMIT = 56 * 1024 * 1024
SSD_CHUNKS_PER_STEP = 8


def _rms(x):
    return x * lax.rsqrt(jnp.mean(x * x, axis=-1, keepdims=True) + EPS)


def _dot_row_parts(x, w, parts):
    rows = x.shape[0] // parts
    return jnp.concatenate(
        [jnp.dot(x[m0:m0 + rows, :], w, preferred_element_type=F32) for m0 in range(0, x.shape[0], rows)],
        axis=0)


def _token_of_row(r):
    return (r % SUBLANES) * SLABS + r // SUBLANES


def _permute_chunks(perm, xb):
    n_chunks = xb.shape[0] // CHUNK
    return jnp.concatenate(
        [jnp.dot(perm, xb[c * CHUNK:(c + 1) * CHUNK, :], preferred_element_type=F32).astype(BF16)
         for c in range(n_chunks)], axis=0)


def _slabs_of(x, n_chunks):
    return [[x[(c * SLABS + v) * SUBLANES:(c * SLABS + v + 1) * SUBLANES, :] for v in range(SLABS)]
            for c in range(n_chunks)]


def _wrapped_slabs(slabs, prev_last, max_shift):
    n_chunks = len(slabs)
    width = slabs[0][0].shape[1]
    first_row = lax.broadcasted_iota(jnp.int32, (SUBLANES, width), 0) == 0
    wrapped = {}
    for j in range(1, max_shift + 1):
        stacked = jnp.concatenate([slabs[c][SLABS - j] for c in range(n_chunks)], axis=0)
        rolled = pltpu.roll(stacked, 1, 0)
        head = jnp.where(first_row, pltpu.roll(prev_last[j], 1, 0), rolled[0:SUBLANES, :])
        wrapped[j] = [head] + [rolled[c * SUBLANES:(c + 1) * SUBLANES, :] for c in range(1, n_chunks)]
    return wrapped


def _shifted(slabs, wrapped, c, v, k):
    return slabs[c][v - k] if v >= k else wrapped[k - v][c]


def _mod_kernel(c_ref, w_ref, b_ref, o_ref):
    batch = c_ref.shape[0]
    s = jax.nn.silu(c_ref[...])
    s = jnp.concatenate([s, jnp.zeros((SUBLANES - batch, s.shape[1]), F32)], axis=0)
    mod = jnp.dot(s.astype(BF16), w_ref[...].astype(BF16), preferred_element_type=F32) + b_ref[...]
    o_ref[:, 0, :] = mod[0:batch, :]


def _rep_heads(v):
    pad = jnp.zeros((v.shape[0], LANES - DT_REP * N_HEADS), v.dtype)
    return jnp.concatenate([v] * DT_REP + [pad], axis=1)


def _weight_prep_kernel(wt_ref, wdt_ref, pw_ref, ps_ref, wbp_ref, cw_ref, cb_ref, dtb_ref, alog_ref,
                        dskip_ref, exp_ref, c_ref, wada_ref, bada_ref,
                        o_ref, odt_ref, opool_ref, ocw_ref, ocb_ref, odtb_ref, oalog_ref, odskip_ref, omod_ref,
                        *, conv_steps):
    r = pl.program_id(0)
    o_ref[...] = wt_ref[...].T.astype(BF16)
    _mod_kernel(c_ref, wada_ref, bada_ref, omod_ref)

    on_conv = jnp.logical_and(r >= conv_steps[0], r < conv_steps[1])
    ocw_ref[...] = jnp.where(on_conv, cw_ref[...], 0.0)
    ocb_ref[...] = jnp.where(on_conv, cb_ref[...], 0.0)

    @pl.when(r == 0)
    def _():
        odt_ref[...] = _rep_heads(wdt_ref[...].T[:, 0:N_HEADS])
        odtb_ref[...] = _rep_heads(dtb_ref[...])
        oalog_ref[...] = _rep_heads(alog_ref[...])
        d_row = jnp.concatenate([dskip_ref[...], jnp.zeros((1, LANES - N_HEADS), F32)], axis=1)
        d8 = jnp.concatenate([d_row, jnp.zeros((SUBLANES - 1, LANES), F32)], axis=0)
        odskip_ref[...] = jnp.dot(d8, exp_ref[...], precision=lax.Precision.HIGHEST,
                                  preferred_element_type=F32)[0:1, :]

    @pl.when(r < len(POOL_WINDOWS))
    def _():
        opool_ref[...] = jnp.dot(pw_ref[...] * ps_ref[...], wbp_ref[...], precision=lax.Precision.HIGHEST,
                                 preferred_element_type=F32).astype(BF16)


def _weight_prep(w_in_t, pool_w, pool_scale, w_bp, conv_w, conv_b, dt_bias, a_log, d_skip, expand,
                 c, w_ada, b_ada):
    cols = 1024
    k = w_in_t.shape[1]
    batch = c.shape[0]
    n_mod = w_ada.shape[1]
    mod_cols = n_mod // (PROJ_COLS // cols)
    n = w_bp.shape[1]
    first_after_dt = COL_POOL // cols
    last_group = len(POOL_WINDOWS) - 1
    group = lambda r: jnp.minimum(r, last_group)
    conv_steps = (COL_XS // cols, COL_POOL // cols)
    conv_blk = lambda r: (0, jnp.clip(r - conv_steps[0], 0, conv_steps[1] - conv_steps[0] - 1))
    whole = lambda r: (0, 0)

    def src_row(r):
        return (r * (cols // N_HEADS) + jnp.where(r >= first_after_dt, 1, 0)) * N_HEADS

    return pl.pallas_call(
        functools.partial(_weight_prep_kernel, conv_steps=conv_steps),
        grid=(PROJ_COLS // cols,),
        in_specs=[pl.BlockSpec((pl.Element(cols), pl.Element(k)), lambda r: (src_row(r), 0)),
                  pl.BlockSpec((LANES, k), lambda r: (COL_POOL // LANES, 0)),
                  pl.BlockSpec((None, POOL_GW, POOL_GW), lambda r: (group(r), 0, 0)),
                  pl.BlockSpec((1, POOL_GW), lambda r: (0, group(r))),
                  pl.BlockSpec((POOL_GW, n), lambda r: (group(r), 0)),
                  pl.BlockSpec((CONV_K, cols), conv_blk),
                  pl.BlockSpec((1, cols), conv_blk),
                  pl.BlockSpec((1, N_HEADS), whole),
                  pl.BlockSpec((1, N_HEADS), whole),
                  pl.BlockSpec((1, N_HEADS), whole),
                  pl.BlockSpec((LANES, D_INNER), whole),
                  pl.BlockSpec((batch, k), whole),
                  pl.BlockSpec((k, mod_cols), lambda r: (0, r)),
                  pl.BlockSpec((1, mod_cols), lambda r: (0, r))],
        out_specs=[pl.BlockSpec((None, k, cols), lambda r: (r // (PROJ_TN // cols), 0, r % (PROJ_TN // cols))),
                   pl.BlockSpec((k, LANES), whole),
                   pl.BlockSpec((POOL_GW, n), lambda r: (group(r), 0)),
                   pl.BlockSpec((CONV_K, cols), lambda r: (0, r)),
                   pl.BlockSpec((1, cols), lambda r: (0, r)),
                   pl.BlockSpec((1, LANES), whole),
                   pl.BlockSpec((1, LANES), whole),
                   pl.BlockSpec((1, D_INNER), whole),
                   pl.BlockSpec((batch, 1, mod_cols), lambda r: (0, 0, r))],
        out_shape=[jax.ShapeDtypeStruct((PROJ_COLS // PROJ_TN, k, PROJ_TN), BF16),
                   jax.ShapeDtypeStruct((k, LANES), F32),
                   jax.ShapeDtypeStruct((D_MODEL, n), BF16),
                   jax.ShapeDtypeStruct((CONV_K, PROJ_COLS), F32),
                   jax.ShapeDtypeStruct((1, PROJ_COLS), F32),
                   jax.ShapeDtypeStruct((1, LANES), F32),
                   jax.ShapeDtypeStruct((1, LANES), F32),
                   jax.ShapeDtypeStruct((1, D_INNER), F32),
                   jax.ShapeDtypeStruct((batch, 1, n_mod), F32)],
        compiler_params=pltpu.CompilerParams(vmem_limit_bytes=VMEM_LIMIT),
        name="weight_prep",
    )(w_in_t, w_in_t, pool_w, pool_scale, w_bp, conv_w, conv_b, dt_bias, a_log, d_skip, expand,
      c, w_ada, b_ada)


def _conv_silu_store(r, cw_ref, cb_ref, col0, halo_ref, o_ref, width):
    n_chunks = r.shape[0] // CHUNK
    pair = 2 * SUBLANES
    for c0 in range(0, width, CONV_COL_BLOCK):
        cols = slice(c0, c0 + CONV_COL_BLOCK)
        slabs = _slabs_of(r[:, cols], n_chunks)
        prev_last = {j: halo_ref[j - 1, :, cols] for j in range(1, CONV_K)}
        for j in range(1, CONV_K):
            halo_ref[j - 1, :, cols] = slabs[n_chunks - 1][SLABS - j]
        wrapped = _wrapped_slabs(slabs, prev_last, CONV_K - 1)
        pcols = slice(col0 + c0, col0 + c0 + CONV_COL_BLOCK)
        taps = [cw_ref[k:k + 1, pcols] for k in range(CONV_K)]
        bias = cb_ref[:, pcols]
        for c in range(n_chunks):
            for v0 in range(0, SLABS, 2):
                accs = []
                for v in (v0, v0 + 1):
                    acc = slabs[c][v] * taps[CONV_K - 1] + bias
                    for k in range(1, CONV_K):
                        acc = acc + _shifted(slabs, wrapped, c, v, k) * taps[CONV_K - 1 - k]
                    accs.append(acc)
                lo = (c * SLABS + v0) * SUBLANES
                o_ref[lo:lo + pair, cols] = jax.nn.silu(jnp.concatenate(accs, axis=0)).astype(BF16)


def _inproj_kernel(x_ref, nw_ref, mod_ref, w_ref, wdt_ref, cw_ref, cb_ref, perm_ref, o_ref, dt_ref,
                   h_scr, halo_scr, *, tiles_per_batch):
    i = pl.program_id(0)
    j = pl.program_id(1)

    @pl.when(j == 0)
    def _():
        mod = mod_ref[0]
        shift = mod[:, 0:D_MODEL]
        scale = mod[:, D_MODEL:2 * D_MODEL]
        h = _rms(x_ref[...]) * (nw_ref[...] * (1.0 + scale)) + shift
        hb = _permute_chunks(perm_ref[...].astype(BF16), h.astype(BF16))
        h_scr[...] = hb
        dt_ref[...] = jnp.dot(hb, wdt_ref[...].astype(BF16), preferred_element_type=F32)

    @pl.when(jnp.logical_and(i % tiles_per_batch == 0, j == 0))
    def _():
        halo_scr[...] = jnp.zeros_like(halo_scr)

    def proj():
        return _dot_row_parts(h_scr[...], w_ref[...], PROJ_ROW_SPLIT)

    @pl.when(j == TILE_Z)
    def _():
        o_ref[...] = jax.nn.silu(proj()).astype(BF16)

    @pl.when(j == TILE_XS)
    def _():
        _conv_silu_store(proj(), cw_ref, cb_ref, TILE_XS * PROJ_TN, halo_scr.at[0], o_ref, PROJ_TN)

    @pl.when(j == TILE_BC_POOL)
    def _():
        r = proj()
        _conv_silu_store(r[:, 0:BC_WIDTH], cw_ref, cb_ref, TILE_BC_POOL * PROJ_TN, halo_scr.at[1], o_ref,
                         BC_WIDTH)
        o_ref[:, BC_WIDTH:] = r[:, BC_WIDTH:].astype(BF16)

    @pl.when(j == TILE_GATE)
    def _():
        o_ref[...] = jax.nn.sigmoid(proj()).astype(BF16)


def _in_projection(x2d, norm_w, mod3, w_main, w_dt, conv_w_cols, conv_b_cols, perm, seq):
    t = x2d.shape[0]
    tm, tn = 1024, PROJ_TN
    tiles_per_batch = seq // tm
    return pl.pallas_call(
        functools.partial(_inproj_kernel, tiles_per_batch=tiles_per_batch),
        grid=(t // tm, PROJ_COLS // tn),
        in_specs=[pl.BlockSpec((tm, D_MODEL), lambda i, j: (i, 0)),
                  pl.BlockSpec((1, D_MODEL), lambda i, j: (0, 0)),
                  pl.BlockSpec((1, 1, N_MOD * D_MODEL), lambda i, j: (i // tiles_per_batch, 0, 0)),
                  pl.BlockSpec((None, D_MODEL, tn), lambda i, j: (j, 0, 0)),
                  pl.BlockSpec((D_MODEL, LANES), lambda i, j: (0, 0)),
                  pl.BlockSpec((CONV_K, PROJ_COLS), lambda i, j: (0, 0)),
                  pl.BlockSpec((1, PROJ_COLS), lambda i, j: (0, 0)),
                  pl.BlockSpec((CHUNK, CHUNK), lambda i, j: (0, 0))],
        out_specs=[pl.BlockSpec((None, tm, tn), lambda i, j: (j, i, 0)),
                   pl.BlockSpec((tm, LANES), lambda i, j: (i, 0))],
        out_shape=[jax.ShapeDtypeStruct((PROJ_COLS // tn, t, tn), BF16),
                   jax.ShapeDtypeStruct((t, LANES), F32)],
        scratch_shapes=[pltpu.VMEM((tm, D_MODEL), BF16),
                        pltpu.VMEM((2, CONV_K - 1, SUBLANES, tn), F32)],
        compiler_params=pltpu.CompilerParams(
            dimension_semantics=("arbitrary", "arbitrary"), vmem_limit_bytes=VMEM_LIMIT),
        name="norm_in_proj",
    )(x2d, norm_w, mod3, w_main, w_dt, conv_w_cols, conv_b_cols, perm)


def _split3(v):
    hi = v.astype(BF16).astype(F32)
    r1 = v - hi
    mid = r1.astype(BF16).astype(F32)
    return hi, r1, r1 - mid


def _lane_pieces(v, lane):
    hi, r1, r2 = _split3(v)
    return jnp.where(lane < N_HEADS, hi, jnp.where(lane < 2 * N_HEADS, r1, r2)).astype(BF16)


def _ssd_prologue(xs_ref, dt_ref, dtb_ref, alog_ref, tri, expand, xdb, expd, acs2_scr, src_scr):
    q = CHUNK
    tm = dt_ref.shape[0]
    n_chunks = tm // q
    dt = jax.nn.softplus(dt_ref[...] + dtb_ref[...])
    dta = dt * (-jnp.exp(alog_ref[...]))
    hi, r1, r2 = _split3(dta)
    acs = []
    for c in range(n_chunks):
        rows = slice(c * q, (c + 1) * q)
        stacked = jnp.concatenate([hi[rows], r1[rows], r2[rows]], axis=0).astype(BF16)
        acs.append(jnp.dot(tri, stacked, preferred_element_type=F32))
    acs2 = jnp.concatenate(acs, axis=0) * LOG2_E
    acs2_last = jnp.concatenate(
        [jnp.broadcast_to(acs2[(c + 1) * q - 1:(c + 1) * q, :], (q, LANES)) for c in range(n_chunks)], axis=0)
    decay_to_end = jnp.exp2(acs2_last - acs2)
    decay_from_start = jnp.exp2(acs2)
    acs2_scr[...] = acs2
    src = acs2 - jnp.log2(dt)
    for c in range(n_chunks):
        src_scr[c] = src[c * q:(c + 1) * q, :].T

    lane = lax.broadcasted_iota(jnp.int32, (tm, LANES), 1)
    to_end = jnp.dot(_lane_pieces(dt * decay_to_end, lane), expand, preferred_element_type=F32)
    xdb[...] = (xs_ref[...].astype(F32) * to_end).astype(BF16)
    expd[...] = _lane_pieces(decay_from_start, lane)


def _ssd_chunk(ci, xs_ref, bc_ref, dskip_ref, expand, o_ref, state, xdb, expd, acs2_scr, src_scr):
    q = CHUNK
    rows = slice(ci * q, (ci + 1) * q)
    acs2 = acs2_scr[rows, :]
    src_t = src_scr[ci]

    causal = (_token_of_row(lax.broadcasted_iota(jnp.int32, (q, q), 0))
              >= _token_of_row(lax.broadcasted_iota(jnp.int32, (q, q), 1)))
    half = lax.broadcasted_iota(jnp.int32, (q, LANES), 1) < HEAD_DIM
    zero_b = jnp.zeros((q, LANES), BF16)

    for g in range(N_GROUPS):
        gs = slice(g * GROUP_CH, (g + 1) * GROUP_CH)
        bg = bc_ref[rows, g * D_STATE:(g + 1) * D_STATE]
        cg = bc_ref[rows, (N_GROUPS + g) * D_STATE:(N_GROUPS + g + 1) * D_STATE]
        scores = lax.dot_general(cg, bg, (((1,), (1,)), ((), ())), preferred_element_type=F32)
        st_old = state[:, gs]
        y_off = jnp.dot(cg, st_old.astype(BF16), preferred_element_type=F32)
        new_t = lax.dot_general(bg, xdb[rows, gs], (((0,), (0,)), ((), ())),
                                preferred_element_type=F32)
        from_start = jnp.dot(expd[rows, :], expand[:, gs], preferred_element_type=F32)
        state[:, gs] = st_old * from_start[q - 1:q, :] + new_t
        ys = []
        for pr in range(GROUP_CH // LANES):
            h0 = g * (GROUP_CH // HEAD_DIM) + 2 * pr
            ms = []
            for h in (h0, h0 + 1):
                a_col = jnp.broadcast_to(acs2[:, h:h + 1], (q, q))
                a_row = jnp.broadcast_to(src_t[h:h + 1, :], (q, q))
                decay_dt = jnp.exp2(jnp.where(causal, a_col - a_row, -jnp.inf))
                ms.append((scores * decay_dt).astype(BF16))
            xp = xs_ref[rows, g * GROUP_CH + pr * LANES:g * GROUP_CH + (pr + 1) * LANES]
            rhs = jnp.concatenate([jnp.where(half, xp, zero_b), jnp.where(half, zero_b, xp)], axis=0)
            ys.append(jnp.dot(jnp.concatenate(ms, axis=1), rhs, preferred_element_type=F32))
        y = jnp.concatenate(ys, axis=1) + y_off * from_start
        o_ref[rows, gs] = (y + dskip_ref[:, gs] * xs_ref[rows, gs].astype(F32)).astype(BF16)


def _ssd_kernel(xs_ref, bc_ref, dt_ref, dtb_ref, alog_ref, dskip_ref, tri_ref, exp_ref, scale0_ref, *rest,
                n_cast):
    cast_in = rest[:n_cast]
    o_ref = rest[n_cast]
    cast_out = rest[n_cast + 1:2 * n_cast + 1]
    state, xdb, expd, acs2_scr, src_scr = rest[2 * n_cast + 1:]

    @pl.when(pl.program_id(1) == 0)
    def _():
        state[...] = jnp.zeros_like(state)

    rows0 = cast_in[0].shape[0]
    eye = (lax.broadcasted_iota(jnp.int32, (rows0, rows0), 0)
           == lax.broadcasted_iota(jnp.int32, (rows0, rows0), 1))
    scale_col = jnp.sum(jnp.where(eye, scale0_ref[...], 0.0), axis=1, keepdims=True)
    cast_out[0][...] = (cast_in[0][...] * scale_col).astype(BF16)
    for w_ref, wb_ref in zip(cast_in[1:], cast_out[1:]):
        wb_ref[...] = w_ref[...].astype(BF16)

    tri = tri_ref[...].astype(BF16)
    expand = exp_ref[...].astype(BF16)
    _ssd_prologue(xs_ref, dt_ref, dtb_ref, alog_ref, tri, expand, xdb, expd, acs2_scr, src_scr)
    for ci in range(SSD_CHUNKS_PER_STEP):
        _ssd_chunk(ci, xs_ref, bc_ref, dskip_ref, expand, o_ref, state, xdb, expd, acs2_scr, src_scr)


def _ssd_scan(proj, dt_raw, dt_bias3, a_log3, d_skip_ch, tri3, expand3, row_scale0, cast_weights, batch, seq):
    q = CHUNK
    tm = SSD_CHUNKS_PER_STEP * q
    steps = seq // tm
    t = batch * seq
    n_steps = batch * steps
    row_scale0 = row_scale0.reshape(n_steps, 1, -1)
    rowmap = lambda b, c: b * steps + c
    const = lambda b, c: (0, 0)
    slab_specs = [pl.BlockSpec((w.shape[0] // n_steps, w.shape[1]), lambda b, c: (rowmap(b, c), 0))
                  for w in cast_weights]
    outs = pl.pallas_call(
        functools.partial(_ssd_kernel, n_cast=len(cast_weights)),
        grid=(batch, steps),
        in_specs=[pl.BlockSpec((None, tm, D_INNER), lambda b, c: (TILE_XS, rowmap(b, c), 0)),
                  pl.BlockSpec((None, tm, BC_WIDTH), lambda b, c: (TILE_BC_POOL, rowmap(b, c), 0)),
                  pl.BlockSpec((tm, LANES), lambda b, c: (rowmap(b, c), 0)),
                  pl.BlockSpec((1, LANES), const),
                  pl.BlockSpec((1, LANES), const),
                  pl.BlockSpec((1, D_INNER), const),
                  pl.BlockSpec((q, DT_REP * q), const),
                  pl.BlockSpec((LANES, D_INNER), const),
                  pl.BlockSpec((None, 1, cast_weights[0].shape[0] // n_steps),
                               lambda b, c: (rowmap(b, c), 0, 0))]
        + slab_specs,
        out_specs=[pl.BlockSpec((tm, D_INNER), lambda b, c: (rowmap(b, c), 0))] + slab_specs,
        out_shape=[jax.ShapeDtypeStruct((t, D_INNER), BF16)]
        + [jax.ShapeDtypeStruct(w.shape, BF16) for w in cast_weights],
        scratch_shapes=[pltpu.VMEM((D_STATE, D_INNER), F32),
                        pltpu.VMEM((tm, D_INNER), BF16),
                        pltpu.VMEM((tm, LANES), BF16),
                        pltpu.VMEM((tm, LANES), F32),
                        pltpu.VMEM((SSD_CHUNKS_PER_STEP, LANES, q), F32)],
        compiler_params=pltpu.CompilerParams(
            dimension_semantics=("arbitrary", "arbitrary"), vmem_limit_bytes=VMEM_LIMIT),
        name="ssd_scan",
    )(proj, proj, dt_raw, dt_bias3, a_log3, d_skip_ch, tri3, expand3, row_scale0, *cast_weights)
    return outs[0], outs[1:]


def _pooled_groups(u, halo_ref, pos0):
    n_chunks = u.shape[0] // CHUNK
    max_shift = max(POOL_WINDOWS) - 1
    slabs = _slabs_of(u, n_chunks)
    prev_last = {j: halo_ref[SLABS - j] for j in range(1, max_shift + 1)}
    for v in range(SLABS):
        halo_ref[v] = slabs[n_chunks - 1][v]
    wrapped = _wrapped_slabs(slabs, prev_last, max_shift)
    sub = lax.broadcasted_iota(jnp.int32, (SUBLANES, POOL_GW), 0)
    pooled = []
    for gi, win in enumerate(POOL_WINDOWS):
        cs = slice(gi * POOL_GW, (gi + 1) * POOL_GW)
        out = []
        for c in range(n_chunks):
            for v in range(SLABS):
                cur = slabs[c][v][:, cs]
                tot = cur
                for k in range(1, win):
                    tot = tot + _shifted(slabs, wrapped, c, v, k)[:, cs]
                pos = pos0 + c * CHUNK + sub * SLABS + v
                count = jnp.minimum(pos + 1, win).astype(F32)
                out.append(tot / count - cur)
        pooled.append(jnp.concatenate(out, axis=0).astype(BF16))
    return pooled


def _tail_kernel(x_ref, y_ref, zs_ref, u_ref, g_ref, mod_ref, wbs_ref, wpool_ref,
                 wo_ref, nmlp_ref, wup_ref, wdn_ref, nfin_ref, unperm_ref, o_ref, uhalo,
                 *, tm, tiles_per_batch):
    i = pl.program_id(0)

    @pl.when(i % tiles_per_batch == 0)
    def _():
        uhalo[...] = jnp.zeros_like(uhalo)

    y_gated = []
    for g in range(N_GROUPS):
        gs = slice(g * GROUP_CH, (g + 1) * GROUP_CH)
        yg = y_ref[:, gs].astype(F32) * zs_ref[:, gs].astype(F32)
        y_gated.append(_rms(yg).astype(BF16))
    y_ssd = jnp.dot(jnp.concatenate(y_gated, axis=1), wbs_ref[...], preferred_element_type=F32)

    pooled = _pooled_groups(u_ref[...].astype(F32), uhalo, (i % tiles_per_batch) * tm)
    y_pool = jnp.dot(jnp.concatenate(pooled, axis=1), wpool_ref[...], preferred_element_type=F32)

    merged = (g_ref[:, :D_MODEL].astype(F32) * y_ssd + g_ref[:, D_MODEL:].astype(F32) * y_pool)
    merged = _permute_chunks(unperm_ref[...].astype(BF16), merged.astype(BF16))
    mix = jnp.dot(merged, wo_ref[...], preferred_element_type=F32)

    mod = mod_ref[0]
    gate_m = mod[:, 2 * D_MODEL:3 * D_MODEL]
    shift_f = mod[:, 3 * D_MODEL:4 * D_MODEL]
    scale_f = mod[:, 4 * D_MODEL:5 * D_MODEL]
    gate_f = mod[:, 5 * D_MODEL:6 * D_MODEL]
    x1 = x_ref[...] + gate_m * mix
    h = _rms(x1) * (nmlp_ref[...] * (1.0 + scale_f)) + shift_f
    up = jnp.dot(h.astype(BF16), wup_ref[...], preferred_element_type=F32)
    act = jnp.square(jnp.maximum(up, 0.0)).astype(BF16)
    x2 = x1 + gate_f * jnp.dot(act, wdn_ref[...], preferred_element_type=F32)
    o_ref[...] = _rms(x2) * nfin_ref[...]


def _tail(x2d, y_raw, proj, mod3, w_bs, w_pool, w_out, norm_mlp_w, w_up, w_dn, norm_final_w,
          unperm, seq):
    t = x2d.shape[0]
    tm = 512
    tiles_per_batch = seq // tm
    const2 = lambda i: (0, 0)
    resident = functools.partial(pl.BlockSpec, pipeline_mode=pl.Buffered(1))
    return pl.pallas_call(
        functools.partial(_tail_kernel, tm=tm, tiles_per_batch=tiles_per_batch),
        grid=(t // tm,),
        in_specs=[pl.BlockSpec((tm, D_MODEL), lambda i: (i, 0)),
                  pl.BlockSpec((tm, D_INNER), lambda i: (i, 0)),
                  pl.BlockSpec((None, tm, D_INNER), lambda i: (TILE_Z, i, 0)),
                  pl.BlockSpec((None, tm, D_MODEL), lambda i: (TILE_BC_POOL, i, BC_WIDTH // D_MODEL)),
                  pl.BlockSpec((None, tm, 2 * D_MODEL), lambda i: (TILE_GATE, i, 0)),
                  pl.BlockSpec((1, 1, N_MOD * D_MODEL), lambda i: (i // tiles_per_batch, 0, 0)),
                  resident((D_INNER, D_MODEL), const2),
                  resident((D_MODEL, D_MODEL), const2),
                  resident((D_MODEL, D_MODEL), const2),
                  resident((1, D_MODEL), const2),
                  resident((D_MODEL, D_FF), const2),
                  resident((D_FF, D_MODEL), const2),
                  resident((1, D_MODEL), const2),
                  resident((CHUNK, CHUNK), const2)],
        out_specs=pl.BlockSpec((tm, D_MODEL), lambda i: (i, 0)),
        out_shape=jax.ShapeDtypeStruct((t, D_MODEL), F32),
        scratch_shapes=[pltpu.VMEM((SLABS, SUBLANES, D_MODEL), F32)],
        compiler_params=pltpu.CompilerParams(
            dimension_semantics=("arbitrary",), vmem_limit_bytes=VMEM_LIMIT),
        name="gate_pool_merge_mlp",
    )(x2d, y_raw, proj, proj, proj, mod3, w_bs, w_pool, w_out, norm_mlp_w, w_up, w_dn,
      norm_final_w, unperm)


def _cumsum_matrix():
    tok = _token_of_row(np.arange(CHUNK))
    tri = (tok[None, :] <= tok[:, None]).astype(np.float32)
    return jnp.asarray(np.concatenate([tri] * DT_REP, axis=1), F32)


def _interleave_matrix():
    p = np.zeros((CHUNK, CHUNK), np.float32)
    p[np.arange(CHUNK), _token_of_row(np.arange(CHUNK))] = 1.0
    return jnp.asarray(p, F32), jnp.asarray(p.T, F32)


def _head_expand_matrix():
    m = np.zeros((LANES, D_INNER), np.float32)
    for k in range(DT_REP * N_HEADS):
        h = k % N_HEADS
        m[k, h * HEAD_DIM:(h + 1) * HEAD_DIM] = 1.0
    return jnp.asarray(m, F32)


def kernel(x, c, w_ada, b_ada, norm_mix_w, w_in, conv_w, conv_b, dt_bias, a_log, d_skip, ssd_norm_w,
           w_branch_ssd, pool_w, pool_scale, w_branch_pool, w_out, norm_mlp_w, w_up, w_down,
           norm_final_w):
    batch, seq, d = x.shape
    depth = w_ada.shape[0]
    assert depth == 1, "the final norm is fused into the last kernel of a single layer"
    t = batch * seq
    tri3 = _cumsum_matrix()
    expand3 = _head_expand_matrix()
    perm, unperm = _interleave_matrix()
    x2d = x.reshape(t, d)
    i = 0
    w_in_t = jnp.swapaxes(w_in[i], 0, 1)
    w_main, w_dt3, w_pool, conv_w_cols, conv_b_cols, dt_bias3, a_log3, d_skip_ch, mod3 = _weight_prep(
        w_in_t, pool_w[i], pool_scale[i][None, :], w_branch_pool[i], conv_w[i], conv_b[i][None, :],
        dt_bias[i][None, :], a_log[i][None, :], d_skip[i][None, :], expand3, c, w_ada[i], b_ada[i][None, :])
    proj, dt_raw = _in_projection(x2d, norm_mix_w[i][None, :], mod3, w_main, w_dt3, conv_w_cols,
                                  conv_b_cols, perm, seq)
    tail_weights = [w_branch_ssd[i], w_out[i], w_up[i], w_down[i]]
    y_raw, (w_bs, w_ob, w_upb, w_dnb) = _ssd_scan(
        proj, dt_raw, dt_bias3, a_log3, d_skip_ch, tri3, expand3,
        ssd_norm_w[i], tail_weights, batch, seq)
    out = _tail(x2d, y_raw, proj, mod3, w_bs, w_pool, w_ob, norm_mlp_w[i][None, :], w_upb, w_dnb,
                norm_final_w[None, :], unperm, seq)
    return out.reshape(batch, seq, d)
```

```python
import functools

import jax
import jax.numpy as jnp
import numpy as np
from jax import lax
from jax.experimental import pallas as pl
from jax.experimental.pallas import tpu as pltpu

F32 = jnp.float32
BF16 = jnp.bfloat16

D_MODEL = 1024
D_INNER = 2 * D_MODEL
HEAD_DIM = 64
N_HEADS = D_INNER // HEAD_DIM
N_GROUPS = 4
GROUP_CH = D_INNER // N_GROUPS
D_STATE = 128
CONV_K = 4
CHUNK = 128
BC_WIDTH = 2 * N_GROUPS * D_STATE
POOL_WINDOWS = (2, 4, 8, 16)
POOL_GW = D_MODEL // len(POOL_WINDOWS)
D_FF = 4 * D_MODEL
N_MOD = 6
EPS = 1e-5
LOG2_E = 1.4426950408889634
LANES = 128
SUBLANES = 8
SLABS = CHUNK // SUBLANES
DT_REP = 3

PROJ_TN = 2048
COL_XS = D_INNER
COL_POOL = 2 * D_INNER + BC_WIDTH
PROJ_COLS = COL_POOL + 3 * D_MODEL
TILE_Z, TILE_XS, TILE_BC_POOL, TILE_GATE = 0, 1, 2, 3
CONV_COL_BLOCK = 512
PROJ_ROW_SPLIT = 4

VMEM_LIMIT = 56 * 1024 * 1024
SSD_CHUNKS_PER_STEP = 8
MLP_HIDDEN_PARTS = 2


def _rms(x):
    return x * lax.rsqrt(jnp.mean(x * x, axis=-1, keepdims=True) + EPS)


def _dot_row_parts(x, w, parts):
    rows = x.shape[0] // parts
    return jnp.concatenate(
        [jnp.dot(x[m0:m0 + rows, :], w, preferred_element_type=F32) for m0 in range(0, x.shape[0], rows)],
        axis=0)


def _token_of_row(r):
    return (r % SUBLANES) * SLABS + r // SUBLANES


def _permute_chunks(perm, xb):
    n_chunks = xb.shape[0] // CHUNK
    return jnp.concatenate(
        [jnp.dot(perm, xb[c * CHUNK:(c + 1) * CHUNK, :], preferred_element_type=F32).astype(BF16)
         for c in range(n_chunks)], axis=0)


def _slabs_of(x, n_chunks):
    return [[x[(c * SLABS + v) * SUBLANES:(c * SLABS + v + 1) * SUBLANES, :] for v in range(SLABS)]
            for c in range(n_chunks)]


def _wrapped_slabs(slabs, prev_last, max_shift):
    n_chunks = len(slabs)
    width = slabs[0][0].shape[1]
    first_row = lax.broadcasted_iota(jnp.int32, (SUBLANES, width), 0) == 0
    wrapped = {}
    for j in range(1, max_shift + 1):
        stacked = jnp.concatenate([slabs[c][SLABS - j] for c in range(n_chunks)], axis=0)
        rolled = pltpu.roll(stacked, 1, 0)
        head = jnp.where(first_row, pltpu.roll(prev_last[j], 1, 0), rolled[0:SUBLANES, :])
        wrapped[j] = [head] + [rolled[c * SUBLANES:(c + 1) * SUBLANES, :] for c in range(1, n_chunks)]
    return wrapped


def _shifted(slabs, wrapped, c, v, k):
    return slabs[c][v - k] if v >= k else wrapped[k - v][c]


def _mod_kernel(c_ref, w_ref, b_ref, o_ref):
    batch = c_ref.shape[0]
    s = jax.nn.silu(c_ref[...])
    s = jnp.concatenate([s, jnp.zeros((SUBLANES - batch, s.shape[1]), F32)], axis=0)
    mod = jnp.dot(s.astype(BF16), w_ref[...].astype(BF16), preferred_element_type=F32) + b_ref[...]
    o_ref[:, 0, :] = mod[0:batch, :]


def _rep_heads(v):
    pad = jnp.zeros((v.shape[0], LANES - DT_REP * N_HEADS), v.dtype)
    return jnp.concatenate([v] * DT_REP + [pad], axis=1)


def _weight_prep_kernel(wt_ref, wdt_ref, pw_ref, ps_ref, wbp_ref, cw_ref, cb_ref, dtb_ref, alog_ref,
                        dskip_ref, exp_ref, c_ref, wada_ref, bada_ref,
                        o_ref, odt_ref, opool_ref, ocw_ref, ocb_ref, odtb_ref, oalog_ref, odskip_ref, omod_ref,
                        *, conv_steps):
    r = pl.program_id(0)
    o_ref[...] = wt_ref[...].T.astype(BF16)
    _mod_kernel(c_ref, wada_ref, bada_ref, omod_ref)

    on_conv = jnp.logical_and(r >= conv_steps[0], r < conv_steps[1])
    ocw_ref[...] = jnp.where(on_conv, cw_ref[...], 0.0)
    ocb_ref[...] = jnp.where(on_conv, cb_ref[...], 0.0)

    @pl.when(r == 0)
    def _():
        odt_ref[...] = _rep_heads(wdt_ref[...].T[:, 0:N_HEADS])
        odtb_ref[...] = _rep_heads(dtb_ref[...])
        oalog_ref[...] = _rep_heads(alog_ref[...])
        d_row = jnp.concatenate([dskip_ref[...], jnp.zeros((1, LANES - N_HEADS), F32)], axis=1)
        d8 = jnp.concatenate([d_row, jnp.zeros((SUBLANES - 1, LANES), F32)], axis=0)
        odskip_ref[...] = jnp.dot(d8, exp_ref[...], precision=lax.Precision.HIGHEST,
                                  preferred_element_type=F32)[0:1, :]

    @pl.when(r < len(POOL_WINDOWS))
    def _():
        opool_ref[...] = jnp.dot(pw_ref[...] * ps_ref[...], wbp_ref[...], precision=lax.Precision.HIGHEST,
                                 preferred_element_type=F32).astype(BF16)


def _weight_prep(w_in_t, pool_w, pool_scale, w_bp, conv_w, conv_b, dt_bias, a_log, d_skip, expand,
                 c, w_ada, b_ada):
    cols = 1024
    k = w_in_t.shape[1]
    batch = c.shape[0]
    n_mod = w_ada.shape[1]
    mod_cols = n_mod // (PROJ_COLS // cols)
    n = w_bp.shape[1]
    first_after_dt = COL_POOL // cols
    last_group = len(POOL_WINDOWS) - 1
    group = lambda r: jnp.minimum(r, last_group)
    conv_steps = (COL_XS // cols, COL_POOL // cols)
    conv_blk = lambda r: (0, jnp.clip(r - conv_steps[0], 0, conv_steps[1] - conv_steps[0] - 1))
    whole = lambda r: (0, 0)

    def src_row(r):
        return (r * (cols // N_HEADS) + jnp.where(r >= first_after_dt, 1, 0)) * N_HEADS

    return pl.pallas_call(
        functools.partial(_weight_prep_kernel, conv_steps=conv_steps),
        grid=(PROJ_COLS // cols,),
        in_specs=[pl.BlockSpec((pl.Element(cols), pl.Element(k)), lambda r: (src_row(r), 0)),
                  pl.BlockSpec((LANES, k), lambda r: (COL_POOL // LANES, 0)),
                  pl.BlockSpec((None, POOL_GW, POOL_GW), lambda r: (group(r), 0, 0)),
                  pl.BlockSpec((1, POOL_GW), lambda r: (0, group(r))),
                  pl.BlockSpec((POOL_GW, n), lambda r: (group(r), 0)),
                  pl.BlockSpec((CONV_K, cols), conv_blk),
                  pl.BlockSpec((1, cols), conv_blk),
                  pl.BlockSpec((1, N_HEADS), whole),
                  pl.BlockSpec((1, N_HEADS), whole),
                  pl.BlockSpec((1, N_HEADS), whole),
                  pl.BlockSpec((LANES, D_INNER), whole),
                  pl.BlockSpec((batch, k), whole),
                  pl.BlockSpec((k, mod_cols), lambda r: (0, r)),
                  pl.BlockSpec((1, mod_cols), lambda r: (0, r))],
        out_specs=[pl.BlockSpec((None, k, cols), lambda r: (r // (PROJ_TN // cols), 0, r % (PROJ_TN // cols))),
                   pl.BlockSpec((k, LANES), whole),
                   pl.BlockSpec((POOL_GW, n), lambda r: (group(r), 0)),
                   pl.BlockSpec((CONV_K, cols), lambda r: (0, r)),
                   pl.BlockSpec((1, cols), lambda r: (0, r)),
                   pl.BlockSpec((1, LANES), whole),
                   pl.BlockSpec((1, LANES), whole),
                   pl.BlockSpec((1, D_INNER), whole),
                   pl.BlockSpec((batch, 1, mod_cols), lambda r: (0, 0, r))],
        out_shape=[jax.ShapeDtypeStruct((PROJ_COLS // PROJ_TN, k, PROJ_TN), BF16),
                   jax.ShapeDtypeStruct((k, LANES), F32),
                   jax.ShapeDtypeStruct((D_MODEL, n), BF16),
                   jax.ShapeDtypeStruct((CONV_K, PROJ_COLS), F32),
                   jax.ShapeDtypeStruct((1, PROJ_COLS), F32),
                   jax.ShapeDtypeStruct((1, LANES), F32),
                   jax.ShapeDtypeStruct((1, LANES), F32),
                   jax.ShapeDtypeStruct((1, D_INNER), F32),
                   jax.ShapeDtypeStruct((batch, 1, n_mod), F32)],
        compiler_params=pltpu.CompilerParams(vmem_limit_bytes=VMEM_LIMIT),
        name="weight_prep",
    )(w_in_t, w_in_t, pool_w, pool_scale, w_bp, conv_w, conv_b, dt_bias, a_log, d_skip, expand,
      c, w_ada, b_ada)


def _conv_silu_store(r, cw_ref, cb_ref, col0, halo_ref, o_ref, width):
    n_chunks = r.shape[0] // CHUNK
    pair = 2 * SUBLANES
    for c0 in range(0, width, CONV_COL_BLOCK):
        cols = slice(c0, c0 + CONV_COL_BLOCK)
        slabs = _slabs_of(r[:, cols], n_chunks)
        prev_last = {j: halo_ref[j - 1, :, cols] for j in range(1, CONV_K)}
        for j in range(1, CONV_K):
            halo_ref[j - 1, :, cols] = slabs[n_chunks - 1][SLABS - j]
        wrapped = _wrapped_slabs(slabs, prev_last, CONV_K - 1)
        pcols = slice(col0 + c0, col0 + c0 + CONV_COL_BLOCK)
        taps = [cw_ref[k:k + 1, pcols] for k in range(CONV_K)]
        bias = cb_ref[:, pcols]
        for c in range(n_chunks):
            for v0 in range(0, SLABS, 2):
                accs = []
                for v in (v0, v0 + 1):
                    acc = slabs[c][v] * taps[CONV_K - 1] + bias
                    for k in range(1, CONV_K):
                        acc = acc + _shifted(slabs, wrapped, c, v, k) * taps[CONV_K - 1 - k]
                    accs.append(acc)
                lo = (c * SLABS + v0) * SUBLANES
                o_ref[lo:lo + pair, cols] = jax.nn.silu(jnp.concatenate(accs, axis=0)).astype(BF16)


def _inproj_kernel(x_ref, nw_ref, mod_ref, w_ref, wdt_ref, cw_ref, cb_ref, perm_ref, o_ref, dt_ref,
                   h_scr, halo_scr, *, tiles_per_batch):
    i = pl.program_id(0)
    j = pl.program_id(1)

    @pl.when(j == 0)
    def _():
        mod = mod_ref[0]
        shift = mod[:, 0:D_MODEL]
        scale = mod[:, D_MODEL:2 * D_MODEL]
        h = _rms(x_ref[...]) * (nw_ref[...] * (1.0 + scale)) + shift
        hb = _permute_chunks(perm_ref[...].astype(BF16), h.astype(BF16))
        h_scr[...] = hb
        dt_ref[...] = jnp.dot(hb, wdt_ref[...].astype(BF16), preferred_element_type=F32)

    @pl.when(jnp.logical_and(i % tiles_per_batch == 0, j == 0))
    def _():
        halo_scr[...] = jnp.zeros_like(halo_scr)

    def proj():
        return _dot_row_parts(h_scr[...], w_ref[...], PROJ_ROW_SPLIT)

    @pl.when(j == TILE_Z)
    def _():
        o_ref[...] = jax.nn.silu(proj()).astype(BF16)

    @pl.when(j == TILE_XS)
    def _():
        _conv_silu_store(proj(), cw_ref, cb_ref, TILE_XS * PROJ_TN, halo_scr.at[0], o_ref, PROJ_TN)

    @pl.when(j == TILE_BC_POOL)
    def _():
        r = proj()
        _conv_silu_store(r[:, 0:BC_WIDTH], cw_ref, cb_ref, TILE_BC_POOL * PROJ_TN, halo_scr.at[1], o_ref,
                         BC_WIDTH)
        o_ref[:, BC_WIDTH:] = r[:, BC_WIDTH:].astype(BF16)

    @pl.when(j == TILE_GATE)
    def _():
        o_ref[...] = jax.nn.sigmoid(proj()).astype(BF16)


def _in_projection(x2d, norm_w, mod3, w_main, w_dt, conv_w_cols, conv_b_cols, perm, seq):
    t = x2d.shape[0]
    tm, tn = 1024, PROJ_TN
    tiles_per_batch = seq // tm
    return pl.pallas_call(
        functools.partial(_inproj_kernel, tiles_per_batch=tiles_per_batch),
        grid=(t // tm, PROJ_COLS // tn),
        in_specs=[pl.BlockSpec((tm, D_MODEL), lambda i, j: (i, 0)),
                  pl.BlockSpec((1, D_MODEL), lambda i, j: (0, 0)),
                  pl.BlockSpec((1, 1, N_MOD * D_MODEL), lambda i, j: (i // tiles_per_batch, 0, 0)),
                  pl.BlockSpec((None, D_MODEL, tn), lambda i, j: (j, 0, 0)),
                  pl.BlockSpec((D_MODEL, LANES), lambda i, j: (0, 0)),
                  pl.BlockSpec((CONV_K, PROJ_COLS), lambda i, j: (0, 0)),
                  pl.BlockSpec((1, PROJ_COLS), lambda i, j: (0, 0)),
                  pl.BlockSpec((CHUNK, CHUNK), lambda i, j: (0, 0))],
        out_specs=[pl.BlockSpec((None, tm, tn), lambda i, j: (j, i, 0)),
                   pl.BlockSpec((tm, LANES), lambda i, j: (i, 0))],
        out_shape=[jax.ShapeDtypeStruct((PROJ_COLS // tn, t, tn), BF16),
                   jax.ShapeDtypeStruct((t, LANES), F32)],
        scratch_shapes=[pltpu.VMEM((tm, D_MODEL), BF16),
                        pltpu.VMEM((2, CONV_K - 1, SUBLANES, tn), F32)],
        compiler_params=pltpu.CompilerParams(
            dimension_semantics=("arbitrary", "arbitrary"), vmem_limit_bytes=VMEM_LIMIT),
        name="norm_in_proj",
    )(x2d, norm_w, mod3, w_main, w_dt, conv_w_cols, conv_b_cols, perm)


def _split3(v):
    hi = v.astype(BF16).astype(F32)
    r1 = v - hi
    mid = r1.astype(BF16).astype(F32)
    return hi, r1, r1 - mid


def _lane_pieces(v, lane):
    hi, r1, r2 = _split3(v)
    return jnp.where(lane < N_HEADS, hi, jnp.where(lane < 2 * N_HEADS, r1, r2)).astype(BF16)


def _ssd_prologue(xs_ref, dt_ref, dtb_ref, alog_ref, tri, expand, xdb, expd, acs2_scr, src_scr):
    q = CHUNK
    tm = dt_ref.shape[0]
    n_chunks = tm // q
    dt = jax.nn.softplus(dt_ref[...] + dtb_ref[...])
    dta = dt * (-jnp.exp(alog_ref[...]))
    hi, r1, r2 = _split3(dta)
    acs = []
    for c in range(n_chunks):
        rows = slice(c * q, (c + 1) * q)
        stacked = jnp.concatenate([hi[rows], r1[rows], r2[rows]], axis=0).astype(BF16)
        acs.append(jnp.dot(tri, stacked, preferred_element_type=F32))
    acs2 = jnp.concatenate(acs, axis=0) * LOG2_E
    acs2_last = jnp.concatenate(
        [jnp.broadcast_to(acs2[(c + 1) * q - 1:(c + 1) * q, :], (q, LANES)) for c in range(n_chunks)], axis=0)
    decay_to_end = jnp.exp2(acs2_last - acs2)
    decay_from_start = jnp.exp2(acs2)
    acs2_scr[...] = acs2
    src = acs2 - jnp.log2(dt)
    for c in range(n_chunks):
        src_scr[c] = src[c * q:(c + 1) * q, :].T

    lane = lax.broadcasted_iota(jnp.int32, (tm, LANES), 1)
    to_end = jnp.dot(_lane_pieces(dt * decay_to_end, lane), expand, preferred_element_type=F32)
    xdb[...] = (xs_ref[...].astype(F32) * to_end).astype(BF16)
    expd[...] = _lane_pieces(decay_from_start, lane)


def _ssd_chunk(ci, xs_ref, bc_ref, dskip_ref, expand, o_ref, state, xdb, expd, acs2_scr, src_scr):
    q = CHUNK
    rows = slice(ci * q, (ci + 1) * q)
    acs2 = acs2_scr[rows, :]
    src_t = src_scr[ci]

    causal = (_token_of_row(lax.broadcasted_iota(jnp.int32, (q, q), 0))
              >= _token_of_row(lax.broadcasted_iota(jnp.int32, (q, q), 1)))
    half = lax.broadcasted_iota(jnp.int32, (q, LANES), 1) < HEAD_DIM
    zero_b = jnp.zeros((q, LANES), BF16)

    for g in range(N_GROUPS):
        gs = slice(g * GROUP_CH, (g + 1) * GROUP_CH)
        bg = bc_ref[rows, g * D_STATE:(g + 1) * D_STATE]
        cg = bc_ref[rows, (N_GROUPS + g) * D_STATE:(N_GROUPS + g + 1) * D_STATE]
        scores = lax.dot_general(cg, bg, (((1,), (1,)), ((), ())), preferred_element_type=F32)
        st_old = state[:, gs]
        y_off = jnp.dot(cg, st_old.astype(BF16), preferred_element_type=F32)
        new_t = lax.dot_general(bg, xdb[rows, gs], (((0,), (0,)), ((), ())),
                                preferred_element_type=F32)
        from_start = jnp.dot(expd[rows, :], expand[:, gs], preferred_element_type=F32)
        state[:, gs] = st_old * from_start[q - 1:q, :] + new_t
        ys = []
        for pr in range(GROUP_CH // LANES):
            h0 = g * (GROUP_CH // HEAD_DIM) + 2 * pr
            ms = []
            for h in (h0, h0 + 1):
                a_col = jnp.broadcast_to(acs2[:, h:h + 1], (q, q))
                a_row = jnp.broadcast_to(src_t[h:h + 1, :], (q, q))
                decay_dt = jnp.exp2(jnp.where(causal, a_col - a_row, -jnp.inf))
                ms.append((scores * decay_dt).astype(BF16))
            xp = xs_ref[rows, g * GROUP_CH + pr * LANES:g * GROUP_CH + (pr + 1) * LANES]
            rhs = jnp.concatenate([jnp.where(half, xp, zero_b), jnp.where(half, zero_b, xp)], axis=0)
            ys.append(jnp.dot(jnp.concatenate(ms, axis=1), rhs, preferred_element_type=F32))
        y = jnp.concatenate(ys, axis=1) + y_off * from_start
        o_ref[rows, gs] = (y + dskip_ref[:, gs] * xs_ref[rows, gs].astype(F32)).astype(BF16)


def _ssd_kernel(xs_ref, bc_ref, dt_ref, dtb_ref, alog_ref, dskip_ref, tri_ref, exp_ref, scale0_ref, *rest,
                n_cast):
    cast_in = rest[:n_cast]
    o_ref = rest[n_cast]
    cast_out = rest[n_cast + 1:2 * n_cast + 1]
    state, xdb, expd, acs2_scr, src_scr = rest[2 * n_cast + 1:]

    @pl.when(pl.program_id(1) == 0)
    def _():
        state[...] = jnp.zeros_like(state)

    rows0 = cast_in[0].shape[0]
    eye = (lax.broadcasted_iota(jnp.int32, (rows0, rows0), 0)
           == lax.broadcasted_iota(jnp.int32, (rows0, rows0), 1))
    scale_col = jnp.sum(jnp.where(eye, scale0_ref[...], 0.0), axis=1, keepdims=True)
    cast_out[0][...] = (cast_in[0][...] * scale_col).astype(BF16)
    for w_ref, wb_ref in zip(cast_in[1:], cast_out[1:]):
        wb_ref[...] = w_ref[...].astype(BF16)

    tri = tri_ref[...].astype(BF16)
    expand = exp_ref[...].astype(BF16)
    _ssd_prologue(xs_ref, dt_ref, dtb_ref, alog_ref, tri, expand, xdb, expd, acs2_scr, src_scr)
    for ci in range(SSD_CHUNKS_PER_STEP):
        _ssd_chunk(ci, xs_ref, bc_ref, dskip_ref, expand, o_ref, state, xdb, expd, acs2_scr, src_scr)


def _ssd_scan(proj, dt_raw, dt_bias3, a_log3, d_skip_ch, tri3, expand3, row_scale0, cast_weights, batch, seq):
    q = CHUNK
    tm = SSD_CHUNKS_PER_STEP * q
    steps = seq // tm
    t = batch * seq
    n_steps = batch * steps
    row_scale0 = row_scale0.reshape(n_steps, 1, -1)
    rowmap = lambda b, c: b * steps + c
    const = lambda b, c: (0, 0)
    slab_specs = [pl.BlockSpec((w.shape[0] // n_steps, w.shape[1]), lambda b, c: (rowmap(b, c), 0))
                  for w in cast_weights]
    outs = pl.pallas_call(
        functools.partial(_ssd_kernel, n_cast=len(cast_weights)),
        grid=(batch, steps),
        in_specs=[pl.BlockSpec((None, tm, D_INNER), lambda b, c: (TILE_XS, rowmap(b, c), 0)),
                  pl.BlockSpec((None, tm, BC_WIDTH), lambda b, c: (TILE_BC_POOL, rowmap(b, c), 0)),
                  pl.BlockSpec((tm, LANES), lambda b, c: (rowmap(b, c), 0)),
                  pl.BlockSpec((1, LANES), const),
                  pl.BlockSpec((1, LANES), const),
                  pl.BlockSpec((1, D_INNER), const),
                  pl.BlockSpec((q, DT_REP * q), const),
                  pl.BlockSpec((LANES, D_INNER), const),
                  pl.BlockSpec((None, 1, cast_weights[0].shape[0] // n_steps),
                               lambda b, c: (rowmap(b, c), 0, 0))]
        + slab_specs,
        out_specs=[pl.BlockSpec((tm, D_INNER), lambda b, c: (rowmap(b, c), 0))] + slab_specs,
        out_shape=[jax.ShapeDtypeStruct((t, D_INNER), BF16)]
        + [jax.ShapeDtypeStruct(w.shape, BF16) for w in cast_weights],
        scratch_shapes=[pltpu.VMEM((D_STATE, D_INNER), F32),
                        pltpu.VMEM((tm, D_INNER), BF16),
                        pltpu.VMEM((tm, LANES), BF16),
                        pltpu.VMEM((tm, LANES), F32),
                        pltpu.VMEM((SSD_CHUNKS_PER_STEP, LANES, q), F32)],
        compiler_params=pltpu.CompilerParams(
            dimension_semantics=("arbitrary", "arbitrary"), vmem_limit_bytes=VMEM_LIMIT),
        name="ssd_scan",
    )(proj, proj, dt_raw, dt_bias3, a_log3, d_skip_ch, tri3, expand3, row_scale0, *cast_weights)
    return outs[0], outs[1:]


def _pooled_groups(u, halo_ref, pos0):
    n_chunks = u.shape[0] // CHUNK
    max_shift = max(POOL_WINDOWS) - 1
    slabs = _slabs_of(u, n_chunks)
    prev_last = {j: halo_ref[SLABS - j] for j in range(1, max_shift + 1)}
    for v in range(SLABS):
        halo_ref[v] = slabs[n_chunks - 1][v]
    wrapped = _wrapped_slabs(slabs, prev_last, max_shift)
    sub = lax.broadcasted_iota(jnp.int32, (SUBLANES, POOL_GW), 0)
    pooled = []
    for gi, win in enumerate(POOL_WINDOWS):
        cs = slice(gi * POOL_GW, (gi + 1) * POOL_GW)
        out = []
        for c in range(n_chunks):
            for v in range(SLABS):
                cur = slabs[c][v][:, cs]
                tot = cur
                for k in range(1, win):
                    tot = tot + _shifted(slabs, wrapped, c, v, k)[:, cs]
                pos = pos0 + c * CHUNK + sub * SLABS + v
                count = jnp.minimum(pos + 1, win).astype(F32)
                out.append(tot / count - cur)
        pooled.append(jnp.concatenate(out, axis=0).astype(BF16))
    return pooled


def _tail_kernel(x_ref, y_ref, zs_ref, u_ref, g_ref, mod_ref, wbs_ref, wpool_ref,
                 wo_ref, nmlp_ref, wup_ref, wdn_ref, nfin_ref, unperm_ref, o_ref, uhalo,
                 *, tm, tiles_per_batch):
    i = pl.program_id(0)

    @pl.when(i % tiles_per_batch == 0)
    def _():
        uhalo[...] = jnp.zeros_like(uhalo)

    y_gated = []
    for g in range(N_GROUPS):
        gs = slice(g * GROUP_CH, (g + 1) * GROUP_CH)
        yg = y_ref[:, gs].astype(F32) * zs_ref[:, gs].astype(F32)
        y_gated.append(_rms(yg).astype(BF16))
    y_ssd = jnp.dot(jnp.concatenate(y_gated, axis=1), wbs_ref[...], preferred_element_type=F32)

    pooled = _pooled_groups(u_ref[...].astype(F32), uhalo, (i % tiles_per_batch) * tm)
    y_pool = jnp.dot(jnp.concatenate(pooled, axis=1), wpool_ref[...], preferred_element_type=F32)

    merged = (g_ref[:, :D_MODEL].astype(F32) * y_ssd + g_ref[:, D_MODEL:].astype(F32) * y_pool)
    merged = _permute_chunks(unperm_ref[...].astype(BF16), merged.astype(BF16))
    mix = jnp.dot(merged, wo_ref[...], preferred_element_type=F32)

    mod = mod_ref[0]
    gate_m = mod[:, 2 * D_MODEL:3 * D_MODEL]
    shift_f = mod[:, 3 * D_MODEL:4 * D_MODEL]
    scale_f = mod[:, 4 * D_MODEL:5 * D_MODEL]
    gate_f = mod[:, 5 * D_MODEL:6 * D_MODEL]
    x1 = x_ref[...] + gate_m * mix
    h = _rms(x1) * (nmlp_ref[...] * (1.0 + scale_f)) + shift_f
    hb = h.astype(BF16)
    part = D_FF // MLP_HIDDEN_PARTS
    down = None
    for f0 in range(0, D_FF, part):
        up = jnp.dot(hb, wup_ref[:, f0:f0 + part], preferred_element_type=F32)
        act = jnp.square(jnp.maximum(up, 0.0)).astype(BF16)
        d = jnp.dot(act, wdn_ref[f0:f0 + part, :], preferred_element_type=F32)
        down = d if down is None else down + d
    x2 = x1 + gate_f * down
    o_ref[...] = _rms(x2) * nfin_ref[...]


def _tail(x2d, y_raw, proj, mod3, w_bs, w_pool, w_out, norm_mlp_w, w_up, w_dn, norm_final_w,
          unperm, seq):
    t = x2d.shape[0]
    tm = 512
    tiles_per_batch = seq // tm
    const2 = lambda i: (0, 0)
    resident = functools.partial(pl.BlockSpec, pipeline_mode=pl.Buffered(1))
    return pl.pallas_call(
        functools.partial(_tail_kernel, tm=tm, tiles_per_batch=tiles_per_batch),
        grid=(t // tm,),
        in_specs=[pl.BlockSpec((tm, D_MODEL), lambda i: (i, 0)),
                  pl.BlockSpec((tm, D_INNER), lambda i: (i, 0)),
                  pl.BlockSpec((None, tm, D_INNER), lambda i: (TILE_Z, i, 0)),
                  pl.BlockSpec((None, tm, D_MODEL), lambda i: (TILE_BC_POOL, i, BC_WIDTH // D_MODEL)),
                  pl.BlockSpec((None, tm, 2 * D_MODEL), lambda i: (TILE_GATE, i, 0)),
                  pl.BlockSpec((1, 1, N_MOD * D_MODEL), lambda i: (i // tiles_per_batch, 0, 0)),
                  resident((D_INNER, D_MODEL), const2),
                  resident((D_MODEL, D_MODEL), const2),
                  resident((D_MODEL, D_MODEL), const2),
                  resident((1, D_MODEL), const2),
                  resident((D_MODEL, D_FF), const2),
                  resident((D_FF, D_MODEL), const2),
                  resident((1, D_MODEL), const2),
                  resident((CHUNK, CHUNK), const2)],
        out_specs=pl.BlockSpec((tm, D_MODEL), lambda i: (i, 0)),
        out_shape=jax.ShapeDtypeStruct((t, D_MODEL), F32),
        scratch_shapes=[pltpu.VMEM((SLABS, SUBLANES, D_MODEL), F32)],
        compiler_params=pltpu.CompilerParams(
            dimension_semantics=("arbitrary",), vmem_limit_bytes=VMEM_LIMIT),
        name="gate_pool_merge_mlp",
    )(x2d, y_raw, proj, proj, proj, mod3, w_bs, w_pool, w_out, norm_mlp_w, w_up, w_dn,
      norm_final_w, unperm)


def _cumsum_matrix():
    tok = _token_of_row(np.arange(CHUNK))
    tri = (tok[None, :] <= tok[:, None]).astype(np.float32)
    return jnp.asarray(np.concatenate([tri] * DT_REP, axis=1), F32)


def _interleave_matrix():
    p = np.zeros((CHUNK, CHUNK), np.float32)
    p[np.arange(CHUNK), _token_of_row(np.arange(CHUNK))] = 1.0
    return jnp.asarray(p, F32), jnp.asarray(p.T, F32)


def _head_expand_matrix():
    m = np.zeros((LANES, D_INNER), np.float32)
    for k in range(DT_REP * N_HEADS):
        h = k % N_HEADS
        m[k, h * HEAD_DIM:(h + 1) * HEAD_DIM] = 1.0
    return jnp.asarray(m, F32)


def kernel(x, c, w_ada, b_ada, norm_mix_w, w_in, conv_w, conv_b, dt_bias, a_log, d_skip, ssd_norm_w,
           w_branch_ssd, pool_w, pool_scale, w_branch_pool, w_out, norm_mlp_w, w_up, w_down,
           norm_final_w):
    batch, seq, d = x.shape
    depth = w_ada.shape[0]
    assert depth == 1, "the final norm is fused into the last kernel of a single layer"
    t = batch * seq
    tri3 = _cumsum_matrix()
    expand3 = _head_expand_matrix()
    perm, unperm = _interleave_matrix()
    x2d = x.reshape(t, d)
    i = 0
    w_in_t = jnp.swapaxes(w_in[i], 0, 1)
    w_main, w_dt3, w_pool, conv_w_cols, conv_b_cols, dt_bias3, a_log3, d_skip_ch, mod3 = _weight_prep(
        w_in_t, pool_w[i], pool_scale[i][None, :], w_branch_pool[i], conv_w[i], conv_b[i][None, :],
        dt_bias[i][None, :], a_log[i][None, :], d_skip[i][None, :], expand3, c, w_ada[i], b_ada[i][None, :])
    proj, dt_raw = _in_projection(x2d, norm_mix_w[i][None, :], mod3, w_main, w_dt3, conv_w_cols,
                                  conv_b_cols, perm, seq)
    tail_weights = [w_branch_ssd[i], w_out[i], w_up[i], w_down[i]]
    y_raw, (w_bs, w_ob, w_upb, w_dnb) = _ssd_scan(
        proj, dt_raw, dt_bias3, a_log3, d_skip_ch, tri3, expand3,
        ssd_norm_w[i], tail_weights, batch, seq)
    out = _tail(x2d, y_raw, proj, mod3, w_bs, w_pool, w_ob, norm_mlp_w[i][None, :], w_upb, w_dnb,
                norm_final_w[None, :], unperm, seq)
    return out.reshape(batch, seq, d)
```
